```python
import math
import jax
import jax.numpy as jnp
from jax import lax
import numpy as np

D_MODEL = 1024
BATCH = 16
SEQ = 4096
DEPTH = 2

CTX_LEN = 256
GRID_W = 64

RG_WIDTH = D_MODEL
RG_HEADS = 8
RG_HEAD_DIM = RG_WIDTH // RG_HEADS
RG_C = 8.0
CONV_WIDTH = 4
CONV_PAD_LO = 1

S5_WIDTH = D_MODEL
S5_GROUP = 16
S5_GROUPS = S5_WIDTH // S5_GROUP
S5_STATE = 64
DT_MIN = 1e-3
DT_MAX = 1e-1

MIX_WIDTH = RG_WIDTH + S5_WIDTH
IN_WIDTH = 2 * RG_WIDTH + S5_WIDTH
D_FF = 4 * D_MODEL
N_MOD = 6
DEEPNORM_ALPHA = (2.0 * DEPTH) ** 0.25
DEEPNORM_BETA = (8.0 * DEPTH) ** -0.25
LN_EPS = 1e-5

kernel_name = 'hybrid_rglru_s5_deepnorm_prefix_dit'


def layer_norm(x, g, b):
    xf = x.astype(jnp.float32)
    mu = jnp.mean(xf, axis=-1, keepdims=True)
    var = jnp.mean(jnp.square(xf - mu), axis=-1, keepdims=True)
    return ((xf - mu) * lax.rsqrt(var + LN_EPS) * g + b).astype(x.dtype)


def modulate(x, shift, scale):
    return x * (1.0 + scale) + shift


def to_col_major(t, rows):
    b, n, ch = t.shape
    return t.reshape(b, rows, GRID_W, ch).swapaxes(1, 2).reshape(b, n, ch)


def from_col_major(t, rows):
    b, n, ch = t.shape
    return t.reshape(b, GRID_W, rows, ch).swapaxes(1, 2).reshape(b, n, ch)


def centred_dwconv(x, w, b):
    length = x.shape[1]
    xp = jnp.pad(x, ((0, 0), (CONV_PAD_LO, CONV_WIDTH - 1 - CONV_PAD_LO), (0, 0)))
    out = b
    for k in range(CONV_WIDTH):
        out = out + xp[:, k:k + length] * w[k]
    return out


def block_diag_linear(x, w, b):
    bsz, length, _ = x.shape
    xh = x.reshape(bsz, length, RG_HEADS, RG_HEAD_DIM)
    y = jnp.einsum('blhi,hij->blhj', xh, w)
    return y.reshape(bsz, length, RG_WIDTH) + b


def _combine(left, right):
    a_l, b_l = left
    a_r, b_r = right
    return a_l * a_r, a_r * b_l + b_r


def linear_scan(a, b, h0, reverse):
    if reverse:
        a = jnp.flip(a, axis=1)
        b = jnp.flip(b, axis=1)
    a_cum, h = lax.associative_scan(_combine, (a, b), axis=1)
    h = h + a_cum * h0[:, None]
    h_final = h[:, -1]
    if reverse:
        h = jnp.flip(h, axis=1)
    return h, h_final


def rglru_direction(xc_ctx, xc_lat, lam, wa, ba, wi, bi, reverse):
    def coeffs(xc):
        xf = xc.astype(jnp.float32)
        r = jax.nn.sigmoid(block_diag_linear(xf, wa, ba))
        i = jax.nn.sigmoid(block_diag_linear(xf, wi, bi))
        log_a = -RG_C * jax.nn.softplus(-lam) * r
        return jnp.exp(log_a), jnp.sqrt(-jnp.expm1(2.0 * log_a)) * (i * xf)
    a_c, b_c = coeffs(xc_ctx)
    h0 = jnp.zeros((xc_ctx.shape[0], RG_WIDTH), jnp.float32)
    h_ctx, h_fin = linear_scan(a_c, b_c, h0, reverse)
    a_l, b_l = coeffs(xc_lat)
    h_lat, _ = linear_scan(a_l, b_l, h_fin, reverse)
    return h_ctx, h_lat


def s5_direction(u_ctx, u_lat, a_re, a_im, log_dt, b_re, b_im, c_re, c_im, reverse):
    f32 = jnp.float32
    lam = lax.complex(a_re.astype(f32), a_im.astype(f32))
    dt = jnp.exp(log_dt.astype(f32))[:, None]
    lam_bar = jnp.exp(lam * dt)
    b_bar = ((lam_bar - 1.0) / lam)[..., None] * lax.complex(b_re.astype(f32), b_im.astype(f32))
    c_mat = lax.complex(c_re.astype(f32), c_im.astype(f32))

    def run(u, h0):
        bu = jnp.einsum('blgh,gph->blgp', u.astype(jnp.complex64), b_bar)
        a = jnp.broadcast_to(lam_bar, (1, u.shape[1]) + lam_bar.shape)
        h, h_fin = linear_scan(a, bu, h0, reverse)
        return jnp.einsum('blgp,ghp->blgh', h, c_mat).real, h_fin

    h0 = jnp.zeros((u_ctx.shape[0], S5_GROUPS, S5_STATE), jnp.complex64)
    y_ctx, h_fin = run(u_ctx, h0)
    y_lat, _ = run(u_lat, h_fin)
    return y_ctx, y_lat


def hybrid_mixer(u_lat, u_ctx, rows, w_in, conv_w, conv_b, rg_lambda, rg_wa, rg_ba, rg_wi, rg_bi,
                 s5_a_re, s5_a_im, s5_log_dt, s5_b_re, s5_b_im, s5_c_re, s5_c_im, s5_d,
                 s5_glu_w, s5_glu_b, w_out, b_out, ctx_out):
    dtype = u_lat.dtype
    rgx_lat, gate_lat, s5u_lat = jnp.split(u_lat @ w_in, [RG_WIDTH, 2 * RG_WIDTH], axis=-1)
    rgx_ctx, gate_ctx, s5u_ctx = jnp.split(u_ctx @ w_in, [RG_WIDTH, 2 * RG_WIDTH], axis=-1)

    xc_lat = centred_dwconv(rgx_lat, conv_w, conv_b)
    xc_ctx = centred_dwconv(rgx_ctx, conv_w, conv_b)
    hc_f, hl_f = rglru_direction(xc_ctx, xc_lat, rg_lambda[0], rg_wa[0], rg_ba[0], rg_wi[0], rg_bi[0], False)
    hc_b, hl_b = rglru_direction(xc_ctx, xc_lat, rg_lambda[1], rg_wa[1], rg_ba[1], rg_wi[1], rg_bi[1], True)
    rg_lat = (hl_f + hl_b).astype(dtype) * jax.nn.gelu(gate_lat)

    def grouped(t):
        return t.astype(jnp.float32).reshape(t.shape[0], t.shape[1], S5_GROUPS, S5_GROUP)
    s_lat = grouped(to_col_major(s5u_lat, rows))
    s_ctx = grouped(s5u_ctx)
    yc_f, yl_f = s5_direction(s_ctx, s_lat, s5_a_re[0], s5_a_im[0], s5_log_dt[0], s5_b_re[0], s5_b_im[0],
                              s5_c_re[0], s5_c_im[0], False)
    yc_b, yl_b = s5_direction(s_ctx, s_lat, s5_a_re[1], s5_a_im[1], s5_log_dt[1], s5_b_re[1], s5_b_im[1],
                              s5_c_re[1], s5_c_im[1], True)
    d_skip = s5_d.astype(jnp.float32).reshape(S5_GROUPS, S5_GROUP)

    def s5_glu(y):
        g = jax.nn.gelu(y.reshape(y.shape[0], y.shape[1], S5_WIDTH).astype(dtype))
        return g * jax.nn.sigmoid(g @ s5_glu_w + s5_glu_b)

    s5_lat = from_col_major(s5_glu(yl_f + yl_b + d_skip * s_lat), rows)
    out_lat = jnp.concatenate([rg_lat, s5_lat], axis=-1) @ w_out + b_out
    if not ctx_out:
        return out_lat, None
    rg_ctx = (hc_f + hc_b).astype(dtype) * jax.nn.gelu(gate_ctx)
    s5_ctx = s5_glu(yc_f + yc_b + d_skip * s_ctx)
    out_ctx = jnp.concatenate([rg_ctx, s5_ctx], axis=-1) @ w_out + b_out
    return out_lat, out_ctx


def sq_relu_mlp(x, w1, b1, w2, b2):
    return jnp.square(jax.nn.relu(x @ w1 + b1)) @ w2 + b2


def _fwd_setup_inputs(seed: int = 0) -> dict:
    key = jax.random.key(seed)
    keys = iter(jax.random.split(key, 48))
    f32 = jnp.float32

    def normal(shape, scale):
        return scale * jax.random.normal(next(keys), shape, f32)

    def uniform(shape, lo, hi):
        return jax.random.uniform(next(keys), shape, f32, lo, hi)

    nl = DEPTH
    x = normal((BATCH, SEQ, D_MODEL), 1.0)
    c = normal((BATCH, D_MODEL), 1.0)
    ctx = normal((BATCH, CTX_LEN, D_MODEL), 1.0)
    c_ctx = normal((D_MODEL,), 1.0)
    ada_w = normal((nl, D_MODEL, N_MOD * D_MODEL), 0.5 * D_MODEL ** -0.5)
    ada_b = normal((nl, N_MOD * D_MODEL), 0.02)
    ln1_g = 1.0 + normal((nl, D_MODEL), 0.02)
    ln1_b = normal((nl, D_MODEL), 0.02)
    w_in = normal((nl, D_MODEL, IN_WIDTH), D_MODEL ** -0.5)
    conv_w = normal((nl, CONV_WIDTH, RG_WIDTH), CONV_WIDTH ** -0.5)
    conv_b = normal((nl, RG_WIDTH), 0.02)
    a_c = uniform((nl, 2, RG_WIDTH), 0.9, 0.999)
    a0 = a_c ** (1.0 / RG_C)
    rg_lambda = jnp.log(a0) - jnp.log1p(-a0)
    rg_wa = normal((nl, 2, RG_HEADS, RG_HEAD_DIM, RG_HEAD_DIM), RG_HEAD_DIM ** -0.5)
    rg_ba = normal((nl, 2, RG_WIDTH), 0.02)
    rg_wi = normal((nl, 2, RG_HEADS, RG_HEAD_DIM, RG_HEAD_DIM), RG_HEAD_DIM ** -0.5)
    rg_bi = normal((nl, 2, RG_WIDTH), 0.02)
    n_idx = jnp.arange(S5_STATE, dtype=f32)
    s5_a_re = -0.5 * jnp.exp(normal((nl, 2, S5_GROUPS, S5_STATE), 0.02))
    s5_a_im = np.pi * n_idx + normal((nl, 2, S5_GROUPS, S5_STATE), 0.02)
    s5_log_dt = uniform((nl, 2, S5_GROUPS), math.log(DT_MIN), math.log(DT_MAX))
    s5_b_re = normal((nl, 2, S5_GROUPS, S5_STATE, S5_GROUP), (2.0 * S5_GROUP) ** -0.5)
    s5_b_im = normal((nl, 2, S5_GROUPS, S5_STATE, S5_GROUP), (2.0 * S5_GROUP) ** -0.5)
    s5_c_re = normal((nl, 2, S5_GROUPS, S5_GROUP, S5_STATE), 0.5 ** 0.5)
    s5_c_im = normal((nl, 2, S5_GROUPS, S5_GROUP, S5_STATE), 0.5 ** 0.5)
    s5_d = normal((nl, S5_WIDTH), 1.0)
    s5_glu_w = normal((nl, S5_WIDTH, S5_WIDTH), S5_WIDTH ** -0.5)
    s5_glu_b = normal((nl, S5_WIDTH), 0.02)
    w_out = normal((nl, MIX_WIDTH, D_MODEL), DEEPNORM_BETA * MIX_WIDTH ** -0.5)
    b_out = normal((nl, D_MODEL), 0.02)
    ln2_g = 1.0 + normal((nl, D_MODEL), 0.02)
    ln2_b = normal((nl, D_MODEL), 0.02)
    mlp_w1 = normal((nl, D_MODEL, D_FF), D_MODEL ** -0.5)
    mlp_b1 = normal((nl, D_FF), 0.02)
    mlp_w2 = normal((nl, D_FF, D_MODEL), DEEPNORM_BETA * D_FF ** -0.5)
    mlp_b2 = normal((nl, D_MODEL), 0.02)
    return {'x': x, 'c': c, 'ctx': ctx, 'c_ctx': c_ctx, 'ada_w': ada_w, 'ada_b': ada_b,
            'ln1_g': ln1_g, 'ln1_b': ln1_b, 'w_in': w_in, 'conv_w': conv_w, 'conv_b': conv_b,
            'rg_lambda': rg_lambda, 'rg_wa': rg_wa, 'rg_ba': rg_ba, 'rg_wi': rg_wi, 'rg_bi': rg_bi,
            's5_a_re': s5_a_re, 's5_a_im': s5_a_im, 's5_log_dt': s5_log_dt, 's5_b_re': s5_b_re,
            's5_b_im': s5_b_im, 's5_c_re': s5_c_re, 's5_c_im': s5_c_im, 's5_d': s5_d,
            's5_glu_w': s5_glu_w, 's5_glu_b': s5_glu_b, 'w_out': w_out, 'b_out': b_out,
            'ln2_g': ln2_g, 'ln2_b': ln2_b, 'mlp_w1': mlp_w1, 'mlp_b1': mlp_b1,
            'mlp_w2': mlp_w2, 'mlp_b2': mlp_b2}


def _fwd_reference(x, c, ctx, c_ctx, ada_w, ada_b, ln1_g, ln1_b, w_in, conv_w, conv_b,
              rg_lambda, rg_wa, rg_ba, rg_wi, rg_bi, s5_a_re, s5_a_im, s5_log_dt, s5_b_re,
              s5_b_im, s5_c_re, s5_c_im, s5_d, s5_glu_w, s5_glu_b, w_out, b_out,
              ln2_g, ln2_b, mlp_w1, mlp_b1, mlp_w2, mlp_b2):
    rows = x.shape[1] // GRID_W
    for l in range(DEPTH):
        last = l == DEPTH - 1
        mod = jax.nn.silu(c) @ ada_w[l] + ada_b[l]
        mod_c = jax.nn.silu(c_ctx) @ ada_w[l] + ada_b[l]
        sh1, sc1, g1, sh2, sc2, g2 = jnp.split(mod[:, None, :], N_MOD, axis=-1)
        sh1c, sc1c, g1c, sh2c, sc2c, g2c = jnp.split(mod_c, N_MOD, axis=-1)

        m_lat, m_ctx = hybrid_mixer(
            modulate(x, sh1, sc1), modulate(ctx, sh1c, sc1c), rows,
            w_in[l], conv_w[l], conv_b[l], rg_lambda[l], rg_wa[l], rg_ba[l], rg_wi[l], rg_bi[l],
            s5_a_re[l], s5_a_im[l], s5_log_dt[l], s5_b_re[l], s5_b_im[l], s5_c_re[l], s5_c_im[l],
            s5_d[l], s5_glu_w[l], s5_glu_b[l], w_out[l], b_out[l], not last)

        x = layer_norm(DEEPNORM_ALPHA * x + g1 * m_lat, ln1_g[l], ln1_b[l])
        f_lat = sq_relu_mlp(modulate(x, sh2, sc2), mlp_w1[l], mlp_b1[l], mlp_w2[l], mlp_b2[l])
        x = layer_norm(DEEPNORM_ALPHA * x + g2 * f_lat, ln2_g[l], ln2_b[l])

        if not last:
            ctx = layer_norm(DEEPNORM_ALPHA * ctx + g1c * m_ctx, ln1_g[l], ln1_b[l])
            f_ctx = sq_relu_mlp(modulate(ctx, sh2c, sc2c), mlp_w1[l], mlp_b1[l], mlp_w2[l], mlp_b2[l])
            ctx = layer_norm(DEEPNORM_ALPHA * ctx + g2c * f_ctx, ln2_g[l], ln2_b[l])
    return x


import jax as _jax
import jax.numpy as _jnp

TWIN_FORMAT = 'train_step'
FWD_PARAMS = ['x', 'c', 'ctx', 'c_ctx', 'ada_w', 'ada_b', 'ln1_g', 'ln1_b', 'w_in', 'conv_w', 'conv_b', 'rg_lambda', 'rg_wa', 'rg_ba', 'rg_wi', 'rg_bi', 's5_a_re', 's5_a_im', 's5_log_dt', 's5_b_re', 's5_b_im', 's5_c_re', 's5_c_im', 's5_d', 's5_glu_w', 's5_glu_b', 'w_out', 'b_out', 'ln2_g', 'ln2_b', 'mlp_w1', 'mlp_b1', 'mlp_w2', 'mlp_b2']
TWIN_WEIGHTS = ['c_ctx', 'ada_w', 'ada_b', 'ln1_g', 'ln1_b', 'w_in', 'conv_w', 'conv_b', 'rg_lambda', 'rg_wa', 'rg_ba', 'rg_wi', 'rg_bi', 's5_a_re', 's5_a_im', 's5_log_dt', 's5_b_re', 's5_b_im', 's5_c_re', 's5_c_im', 's5_d', 's5_glu_w', 's5_glu_b', 'w_out', 'b_out', 'ln2_g', 'ln2_b', 'mlp_w1', 'mlp_b1', 'mlp_w2', 'mlp_b2']
TWIN_DIFF_INPUT = 'x'
TWIN_INPUTS = ['x', 'c', 'ctx', 'c_ctx', 'ada_w', 'ada_b', 'ln1_g', 'ln1_b', 'w_in', 'conv_w', 'conv_b', 'rg_lambda', 'rg_wa', 'rg_ba', 'rg_wi', 'rg_bi', 's5_a_re', 's5_a_im', 's5_log_dt', 's5_b_re', 's5_b_im', 's5_c_re', 's5_c_im', 's5_d', 's5_glu_w', 's5_glu_b', 'w_out', 'b_out', 'ln2_g', 'ln2_b', 'mlp_w1', 'mlp_b1', 'mlp_w2', 'mlp_b2', 'loss_target', 'm_c_ctx', 'm_ada_w', 'm_ada_b', 'm_ln1_g', 'm_ln1_b', 'm_w_in', 'm_conv_w', 'm_conv_b', 'm_rg_lambda', 'm_rg_wa', 'm_rg_ba', 'm_rg_wi', 'm_rg_bi', 'm_s5_a_re', 'm_s5_a_im', 'm_s5_log_dt', 'm_s5_b_re', 'm_s5_b_im', 'm_s5_c_re', 'm_s5_c_im', 'm_s5_d', 'm_s5_glu_w', 'm_s5_glu_b', 'm_w_out', 'm_b_out', 'm_ln2_g', 'm_ln2_b', 'm_mlp_w1', 'm_mlp_b1', 'm_mlp_w2', 'm_mlp_b2', 'v_c_ctx', 'v_ada_w', 'v_ada_b', 'v_ln1_g', 'v_ln1_b', 'v_w_in', 'v_conv_w', 'v_conv_b', 'v_rg_lambda', 'v_rg_wa', 'v_rg_ba', 'v_rg_wi', 'v_rg_bi', 'v_s5_a_re', 'v_s5_a_im', 'v_s5_log_dt', 'v_s5_b_re', 'v_s5_b_im', 'v_s5_c_re', 'v_s5_c_im', 'v_s5_d', 'v_s5_glu_w', 'v_s5_glu_b', 'v_w_out', 'v_b_out', 'v_ln2_g', 'v_ln2_b', 'v_mlp_w1', 'v_mlp_b1', 'v_mlp_w2', 'v_mlp_b2']
TWIN_OUTPUTS = ['loss', 'grad_x', 'grad_c_ctx', 'grad_ada_w', 'grad_ada_b', 'grad_ln1_g', 'grad_ln1_b', 'grad_w_in', 'grad_conv_w', 'grad_conv_b', 'grad_rg_lambda', 'grad_rg_wa', 'grad_rg_ba', 'grad_rg_wi', 'grad_rg_bi', 'grad_s5_a_re', 'grad_s5_a_im', 'grad_s5_log_dt', 'grad_s5_b_re', 'grad_s5_b_im', 'grad_s5_c_re', 'grad_s5_c_im', 'grad_s5_d', 'grad_s5_glu_w', 'grad_s5_glu_b', 'grad_w_out', 'grad_b_out', 'grad_ln2_g', 'grad_ln2_b', 'grad_mlp_w1', 'grad_mlp_b1', 'grad_mlp_w2', 'grad_mlp_b2', 'delta_c_ctx', 'delta_ada_w', 'delta_ada_b', 'delta_ln1_g', 'delta_ln1_b', 'delta_w_in', 'delta_conv_w', 'delta_conv_b', 'delta_rg_lambda', 'delta_rg_wa', 'delta_rg_ba', 'delta_rg_wi', 'delta_rg_bi', 'delta_s5_a_re', 'delta_s5_a_im', 'delta_s5_log_dt', 'delta_s5_b_re', 'delta_s5_b_im', 'delta_s5_c_re', 'delta_s5_c_im', 'delta_s5_d', 'delta_s5_glu_w', 'delta_s5_glu_b', 'delta_w_out', 'delta_b_out', 'delta_ln2_g', 'delta_ln2_b', 'delta_mlp_w1', 'delta_mlp_b1', 'delta_mlp_w2', 'delta_mlp_b2', 'new_m_c_ctx', 'new_m_ada_w', 'new_m_ada_b', 'new_m_ln1_g', 'new_m_ln1_b', 'new_m_w_in', 'new_m_conv_w', 'new_m_conv_b', 'new_m_rg_lambda', 'new_m_rg_wa', 'new_m_rg_ba', 'new_m_rg_wi', 'new_m_rg_bi', 'new_m_s5_a_re', 'new_m_s5_a_im', 'new_m_s5_log_dt', 'new_m_s5_b_re', 'new_m_s5_b_im', 'new_m_s5_c_re', 'new_m_s5_c_im', 'new_m_s5_d', 'new_m_s5_glu_w', 'new_m_s5_glu_b', 'new_m_w_out', 'new_m_b_out', 'new_m_ln2_g', 'new_m_ln2_b', 'new_m_mlp_w1', 'new_m_mlp_b1', 'new_m_mlp_w2', 'new_m_mlp_b2', 'new_v_c_ctx', 'new_v_ada_w', 'new_v_ada_b', 'new_v_ln1_g', 'new_v_ln1_b', 'new_v_w_in', 'new_v_conv_w', 'new_v_conv_b', 'new_v_rg_lambda', 'new_v_rg_wa', 'new_v_rg_ba', 'new_v_rg_wi', 'new_v_rg_bi', 'new_v_s5_a_re', 'new_v_s5_a_im', 'new_v_s5_log_dt', 'new_v_s5_b_re', 'new_v_s5_b_im', 'new_v_s5_c_re', 'new_v_s5_c_im', 'new_v_s5_d', 'new_v_s5_glu_w', 'new_v_s5_glu_b', 'new_v_w_out', 'new_v_b_out', 'new_v_ln2_g', 'new_v_ln2_b', 'new_v_mlp_w1', 'new_v_mlp_b1', 'new_v_mlp_w2', 'new_v_mlp_b2']
TWIN_LEAF_KINDS = {'loss': 'loss', 'grad_x': 'grad_x', 'grad_c_ctx': 'grad_w', 'grad_ada_w': 'grad_w', 'grad_ada_b': 'grad_w', 'grad_ln1_g': 'grad_w', 'grad_ln1_b': 'grad_w', 'grad_w_in': 'grad_w', 'grad_conv_w': 'grad_w', 'grad_conv_b': 'grad_w', 'grad_rg_lambda': 'grad_w', 'grad_rg_wa': 'grad_w', 'grad_rg_ba': 'grad_w', 'grad_rg_wi': 'grad_w', 'grad_rg_bi': 'grad_w', 'grad_s5_a_re': 'grad_w', 'grad_s5_a_im': 'grad_w', 'grad_s5_log_dt': 'grad_w', 'grad_s5_b_re': 'grad_w', 'grad_s5_b_im': 'grad_w', 'grad_s5_c_re': 'grad_w', 'grad_s5_c_im': 'grad_w', 'grad_s5_d': 'grad_w', 'grad_s5_glu_w': 'grad_w', 'grad_s5_glu_b': 'grad_w', 'grad_w_out': 'grad_w', 'grad_b_out': 'grad_w', 'grad_ln2_g': 'grad_w', 'grad_ln2_b': 'grad_w', 'grad_mlp_w1': 'grad_w', 'grad_mlp_b1': 'grad_w', 'grad_mlp_w2': 'grad_w', 'grad_mlp_b2': 'grad_w', 'delta_c_ctx': 'delta_w', 'delta_ada_w': 'delta_w', 'delta_ada_b': 'delta_w', 'delta_ln1_g': 'delta_w', 'delta_ln1_b': 'delta_w', 'delta_w_in': 'delta_w', 'delta_conv_w': 'delta_w', 'delta_conv_b': 'delta_w', 'delta_rg_lambda': 'delta_w', 'delta_rg_wa': 'delta_w', 'delta_rg_ba': 'delta_w', 'delta_rg_wi': 'delta_w', 'delta_rg_bi': 'delta_w', 'delta_s5_a_re': 'delta_w', 'delta_s5_a_im': 'delta_w', 'delta_s5_log_dt': 'delta_w', 'delta_s5_b_re': 'delta_w', 'delta_s5_b_im': 'delta_w', 'delta_s5_c_re': 'delta_w', 'delta_s5_c_im': 'delta_w', 'delta_s5_d': 'delta_w', 'delta_s5_glu_w': 'delta_w', 'delta_s5_glu_b': 'delta_w', 'delta_w_out': 'delta_w', 'delta_b_out': 'delta_w', 'delta_ln2_g': 'delta_w', 'delta_ln2_b': 'delta_w', 'delta_mlp_w1': 'delta_w', 'delta_mlp_b1': 'delta_w', 'delta_mlp_w2': 'delta_w', 'delta_mlp_b2': 'delta_w', 'new_m_c_ctx': 'new_m', 'new_m_ada_w': 'new_m', 'new_m_ada_b': 'new_m', 'new_m_ln1_g': 'new_m', 'new_m_ln1_b': 'new_m', 'new_m_w_in': 'new_m', 'new_m_conv_w': 'new_m', 'new_m_conv_b': 'new_m', 'new_m_rg_lambda': 'new_m', 'new_m_rg_wa': 'new_m', 'new_m_rg_ba': 'new_m', 'new_m_rg_wi': 'new_m', 'new_m_rg_bi': 'new_m', 'new_m_s5_a_re': 'new_m', 'new_m_s5_a_im': 'new_m', 'new_m_s5_log_dt': 'new_m', 'new_m_s5_b_re': 'new_m', 'new_m_s5_b_im': 'new_m', 'new_m_s5_c_re': 'new_m', 'new_m_s5_c_im': 'new_m', 'new_m_s5_d': 'new_m', 'new_m_s5_glu_w': 'new_m', 'new_m_s5_glu_b': 'new_m', 'new_m_w_out': 'new_m', 'new_m_b_out': 'new_m', 'new_m_ln2_g': 'new_m', 'new_m_ln2_b': 'new_m', 'new_m_mlp_w1': 'new_m', 'new_m_mlp_b1': 'new_m', 'new_m_mlp_w2': 'new_m', 'new_m_mlp_b2': 'new_m', 'new_v_c_ctx': 'new_v', 'new_v_ada_w': 'new_v', 'new_v_ada_b': 'new_v', 'new_v_ln1_g': 'new_v', 'new_v_ln1_b': 'new_v', 'new_v_w_in': 'new_v', 'new_v_conv_w': 'new_v', 'new_v_conv_b': 'new_v', 'new_v_rg_lambda': 'new_v', 'new_v_rg_wa': 'new_v', 'new_v_rg_ba': 'new_v', 'new_v_rg_wi': 'new_v', 'new_v_rg_bi': 'new_v', 'new_v_s5_a_re': 'new_v', 'new_v_s5_a_im': 'new_v', 'new_v_s5_log_dt': 'new_v', 'new_v_s5_b_re': 'new_v', 'new_v_s5_b_im': 'new_v', 'new_v_s5_c_re': 'new_v', 'new_v_s5_c_im': 'new_v', 'new_v_s5_d': 'new_v', 'new_v_s5_glu_w': 'new_v', 'new_v_s5_glu_b': 'new_v', 'new_v_w_out': 'new_v', 'new_v_b_out': 'new_v', 'new_v_ln2_g': 'new_v', 'new_v_ln2_b': 'new_v', 'new_v_mlp_w1': 'new_v', 'new_v_mlp_b1': 'new_v', 'new_v_mlp_w2': 'new_v', 'new_v_mlp_b2': 'new_v'}


def _forward(args):
    return _fwd_reference(*[args[k] for k in FWD_PARAMS])


def _output_shape():
    out = _jax.eval_shape(lambda: _forward(_fwd_setup_inputs(0)))
    return out.shape, out.dtype

N_MICROBATCH = 1
ADAM_LR = 0.001
ADAM_B1 = 0.9
ADAM_B2 = 0.999
ADAM_EPS = 1e-08
ADAM_WD = 0.01
ADAM_STEP = 10
PER_EXAMPLE_BATCH_AXIS = {'x': 0, 'c': 0, 'ctx': 0, 'loss_target': 0}
SHARED_INPUTS = []
_WEIGHT_DTYPES = {'c_ctx': _jnp.float32, 'ada_w': _jnp.float32, 'ada_b': _jnp.float32, 'ln1_g': _jnp.float32, 'ln1_b': _jnp.float32, 'w_in': _jnp.float32, 'conv_w': _jnp.float32, 'conv_b': _jnp.float32, 'rg_lambda': _jnp.float32, 'rg_wa': _jnp.float32, 'rg_ba': _jnp.float32, 'rg_wi': _jnp.float32, 'rg_bi': _jnp.float32, 's5_a_re': _jnp.float32, 's5_a_im': _jnp.float32, 's5_log_dt': _jnp.float32, 's5_b_re': _jnp.float32, 's5_b_im': _jnp.float32, 's5_c_re': _jnp.float32, 's5_c_im': _jnp.float32, 's5_d': _jnp.float32, 's5_glu_w': _jnp.float32, 's5_glu_b': _jnp.float32, 'w_out': _jnp.float32, 'b_out': _jnp.float32, 'ln2_g': _jnp.float32, 'ln2_b': _jnp.float32, 'mlp_w1': _jnp.float32, 'mlp_b1': _jnp.float32, 'mlp_w2': _jnp.float32, 'mlp_b2': _jnp.float32}
MOMENT_SCALE = {'c_ctx': 2.162104e-02, 'ada_w': 1.004394e-01, 'ada_b': 1.712682e-01, 'ln1_g': 1.844804e+00, 'ln1_b': 7.835340e-01, 'w_in': 6.102135e-02, 'conv_w': 7.750302e-02, 'conv_b': 2.062010e-01, 'rg_lambda': 1.908309e-02, 'rg_wa': 4.020462e-03, 'rg_ba': 7.179840e-03, 'rg_wi': 7.738890e-03, 'rg_bi': 1.507214e-02, 's5_a_re': 5.476268e-03, 's5_a_im': 5.626413e-03, 's5_log_dt': 2.383494e+00, 's5_b_re': 3.190554e-03, 's5_b_im': 3.237419e-03, 's5_c_re': 7.714381e-04, 's5_c_im': 8.311453e-04, 's5_d': 9.572481e-03, 's5_glu_w': 4.395857e-03, 's5_glu_b': 4.636156e-03, 'w_out': 1.625420e-01, 'b_out': 1.004027e-01, 'ln2_g': 4.528350e+01, 'ln2_b': 3.876294e+00, 'mlp_w1': 2.384027e-02, 'mlp_b1': 2.936979e-02, 'mlp_w2': 9.497632e-02, 'mlp_b2': 1.016130e-01}


def _to_microbatches(a, axis):
    t = _jnp.moveaxis(a, axis, 0)
    t = t.reshape((N_MICROBATCH, t.shape[0] // N_MICROBATCH) + t.shape[1:])
    return _jnp.moveaxis(t, 1, axis + 1)


def setup_inputs(seed: int = 0) -> dict:
    inp = _fwd_setup_inputs(seed)
    key = _jax.random.fold_in(_jax.random.key(seed), 7919)
    shape, _ = _output_shape()
    out = dict(inp)
    out["loss_target"] = _jax.random.normal(_jax.random.fold_in(key, 0), shape, _jnp.float32)
    for i, name in enumerate(TWIN_WEIGHTS):
        w = inp[name].astype(_jnp.float32)
        if MOMENT_SCALE is None:
            s = _jnp.sqrt(_jnp.mean(_jnp.square(w)) + 1e-30)
        else:
            s = MOMENT_SCALE[name]
        km, kv = _jax.random.split(_jax.random.fold_in(key, i + 1))
        out[name] = w
        out["m_" + name] = s * _jax.random.normal(km, w.shape, _jnp.float32)
        out["v_" + name] = (s * s) * _jax.random.uniform(kv, w.shape, _jnp.float32, 0.5, 1.5)
    if N_MICROBATCH > 1:
        for name, axis in PER_EXAMPLE_BATCH_AXIS.items():
            out[name] = _to_microbatches(out[name], axis)
    return {'x': out['x'], 'c': out['c'], 'ctx': out['ctx'], 'c_ctx': out['c_ctx'], 'ada_w': out['ada_w'], 'ada_b': out['ada_b'], 'ln1_g': out['ln1_g'], 'ln1_b': out['ln1_b'], 'w_in': out['w_in'], 'conv_w': out['conv_w'], 'conv_b': out['conv_b'], 'rg_lambda': out['rg_lambda'], 'rg_wa': out['rg_wa'], 'rg_ba': out['rg_ba'], 'rg_wi': out['rg_wi'], 'rg_bi': out['rg_bi'], 's5_a_re': out['s5_a_re'], 's5_a_im': out['s5_a_im'], 's5_log_dt': out['s5_log_dt'], 's5_b_re': out['s5_b_re'], 's5_b_im': out['s5_b_im'], 's5_c_re': out['s5_c_re'], 's5_c_im': out['s5_c_im'], 's5_d': out['s5_d'], 's5_glu_w': out['s5_glu_w'], 's5_glu_b': out['s5_glu_b'], 'w_out': out['w_out'], 'b_out': out['b_out'], 'ln2_g': out['ln2_g'], 'ln2_b': out['ln2_b'], 'mlp_w1': out['mlp_w1'], 'mlp_b1': out['mlp_b1'], 'mlp_w2': out['mlp_w2'], 'mlp_b2': out['mlp_b2'], 'loss_target': out['loss_target'], 'm_c_ctx': out['m_c_ctx'], 'm_ada_w': out['m_ada_w'], 'm_ada_b': out['m_ada_b'], 'm_ln1_g': out['m_ln1_g'], 'm_ln1_b': out['m_ln1_b'], 'm_w_in': out['m_w_in'], 'm_conv_w': out['m_conv_w'], 'm_conv_b': out['m_conv_b'], 'm_rg_lambda': out['m_rg_lambda'], 'm_rg_wa': out['m_rg_wa'], 'm_rg_ba': out['m_rg_ba'], 'm_rg_wi': out['m_rg_wi'], 'm_rg_bi': out['m_rg_bi'], 'm_s5_a_re': out['m_s5_a_re'], 'm_s5_a_im': out['m_s5_a_im'], 'm_s5_log_dt': out['m_s5_log_dt'], 'm_s5_b_re': out['m_s5_b_re'], 'm_s5_b_im': out['m_s5_b_im'], 'm_s5_c_re': out['m_s5_c_re'], 'm_s5_c_im': out['m_s5_c_im'], 'm_s5_d': out['m_s5_d'], 'm_s5_glu_w': out['m_s5_glu_w'], 'm_s5_glu_b': out['m_s5_glu_b'], 'm_w_out': out['m_w_out'], 'm_b_out': out['m_b_out'], 'm_ln2_g': out['m_ln2_g'], 'm_ln2_b': out['m_ln2_b'], 'm_mlp_w1': out['m_mlp_w1'], 'm_mlp_b1': out['m_mlp_b1'], 'm_mlp_w2': out['m_mlp_w2'], 'm_mlp_b2': out['m_mlp_b2'], 'v_c_ctx': out['v_c_ctx'], 'v_ada_w': out['v_ada_w'], 'v_ada_b': out['v_ada_b'], 'v_ln1_g': out['v_ln1_g'], 'v_ln1_b': out['v_ln1_b'], 'v_w_in': out['v_w_in'], 'v_conv_w': out['v_conv_w'], 'v_conv_b': out['v_conv_b'], 'v_rg_lambda': out['v_rg_lambda'], 'v_rg_wa': out['v_rg_wa'], 'v_rg_ba': out['v_rg_ba'], 'v_rg_wi': out['v_rg_wi'], 'v_rg_bi': out['v_rg_bi'], 'v_s5_a_re': out['v_s5_a_re'], 'v_s5_a_im': out['v_s5_a_im'], 'v_s5_log_dt': out['v_s5_log_dt'], 'v_s5_b_re': out['v_s5_b_re'], 'v_s5_b_im': out['v_s5_b_im'], 'v_s5_c_re': out['v_s5_c_re'], 'v_s5_c_im': out['v_s5_c_im'], 'v_s5_d': out['v_s5_d'], 'v_s5_glu_w': out['v_s5_glu_w'], 'v_s5_glu_b': out['v_s5_glu_b'], 'v_w_out': out['v_w_out'], 'v_b_out': out['v_b_out'], 'v_ln2_g': out['v_ln2_g'], 'v_ln2_b': out['v_ln2_b'], 'v_mlp_w1': out['v_mlp_w1'], 'v_mlp_b1': out['v_mlp_b1'], 'v_mlp_w2': out['v_mlp_w2'], 'v_mlp_b2': out['v_mlp_b2']}


def _loss(weights, diff, rest, loss_target):
    with _jax.named_scope("forward"):
        args = {**rest, TWIN_DIFF_INPUT: diff, **{k: w.astype(_WEIGHT_DTYPES[k]) for k, w in weights.items()}}
        y = _forward(args)
    with _jax.named_scope("loss_head"):
        err = _jnp.square(y.astype(_jnp.float32) - loss_target)
        return 0.5 * _jnp.sum(_jnp.mean(err, axis=-1)) if err.ndim else 0.5 * err


def _adamw(w, g, m, v):
    m = ADAM_B1 * m + (1.0 - ADAM_B1) * g
    v = ADAM_B2 * v + (1.0 - ADAM_B2) * _jnp.square(g)
    m_hat = m / (1.0 - ADAM_B1 ** ADAM_STEP)
    v_hat = v / (1.0 - ADAM_B2 ** ADAM_STEP)
    delta = -ADAM_LR * (m_hat / (_jnp.sqrt(v_hat) + ADAM_EPS) + ADAM_WD * w)
    return delta, m, v


def reference(x, c, ctx, c_ctx, ada_w, ada_b, ln1_g, ln1_b, w_in, conv_w, conv_b, rg_lambda, rg_wa, rg_ba, rg_wi, rg_bi, s5_a_re, s5_a_im, s5_log_dt, s5_b_re, s5_b_im, s5_c_re, s5_c_im, s5_d, s5_glu_w, s5_glu_b, w_out, b_out, ln2_g, ln2_b, mlp_w1, mlp_b1, mlp_w2, mlp_b2, loss_target, m_c_ctx, m_ada_w, m_ada_b, m_ln1_g, m_ln1_b, m_w_in, m_conv_w, m_conv_b, m_rg_lambda, m_rg_wa, m_rg_ba, m_rg_wi, m_rg_bi, m_s5_a_re, m_s5_a_im, m_s5_log_dt, m_s5_b_re, m_s5_b_im, m_s5_c_re, m_s5_c_im, m_s5_d, m_s5_glu_w, m_s5_glu_b, m_w_out, m_b_out, m_ln2_g, m_ln2_b, m_mlp_w1, m_mlp_b1, m_mlp_w2, m_mlp_b2, v_c_ctx, v_ada_w, v_ada_b, v_ln1_g, v_ln1_b, v_w_in, v_conv_w, v_conv_b, v_rg_lambda, v_rg_wa, v_rg_ba, v_rg_wi, v_rg_bi, v_s5_a_re, v_s5_a_im, v_s5_log_dt, v_s5_b_re, v_s5_b_im, v_s5_c_re, v_s5_c_im, v_s5_d, v_s5_glu_w, v_s5_glu_b, v_w_out, v_b_out, v_ln2_g, v_ln2_b, v_mlp_w1, v_mlp_b1, v_mlp_w2, v_mlp_b2):
    given = dict(x=x, c=c, ctx=ctx, c_ctx=c_ctx, ada_w=ada_w, ada_b=ada_b, ln1_g=ln1_g, ln1_b=ln1_b, w_in=w_in, conv_w=conv_w, conv_b=conv_b, rg_lambda=rg_lambda, rg_wa=rg_wa, rg_ba=rg_ba, rg_wi=rg_wi, rg_bi=rg_bi, s5_a_re=s5_a_re, s5_a_im=s5_a_im, s5_log_dt=s5_log_dt, s5_b_re=s5_b_re, s5_b_im=s5_b_im, s5_c_re=s5_c_re, s5_c_im=s5_c_im, s5_d=s5_d, s5_glu_w=s5_glu_w, s5_glu_b=s5_glu_b, w_out=w_out, b_out=b_out, ln2_g=ln2_g, ln2_b=ln2_b, mlp_w1=mlp_w1, mlp_b1=mlp_b1, mlp_w2=mlp_w2, mlp_b2=mlp_b2, loss_target=loss_target, m_c_ctx=m_c_ctx, m_ada_w=m_ada_w, m_ada_b=m_ada_b, m_ln1_g=m_ln1_g, m_ln1_b=m_ln1_b, m_w_in=m_w_in, m_conv_w=m_conv_w, m_conv_b=m_conv_b, m_rg_lambda=m_rg_lambda, m_rg_wa=m_rg_wa, m_rg_ba=m_rg_ba, m_rg_wi=m_rg_wi, m_rg_bi=m_rg_bi, m_s5_a_re=m_s5_a_re, m_s5_a_im=m_s5_a_im, m_s5_log_dt=m_s5_log_dt, m_s5_b_re=m_s5_b_re, m_s5_b_im=m_s5_b_im, m_s5_c_re=m_s5_c_re, m_s5_c_im=m_s5_c_im, m_s5_d=m_s5_d, m_s5_glu_w=m_s5_glu_w, m_s5_glu_b=m_s5_glu_b, m_w_out=m_w_out, m_b_out=m_b_out, m_ln2_g=m_ln2_g, m_ln2_b=m_ln2_b, m_mlp_w1=m_mlp_w1, m_mlp_b1=m_mlp_b1, m_mlp_w2=m_mlp_w2, m_mlp_b2=m_mlp_b2, v_c_ctx=v_c_ctx, v_ada_w=v_ada_w, v_ada_b=v_ada_b, v_ln1_g=v_ln1_g, v_ln1_b=v_ln1_b, v_w_in=v_w_in, v_conv_w=v_conv_w, v_conv_b=v_conv_b, v_rg_lambda=v_rg_lambda, v_rg_wa=v_rg_wa, v_rg_ba=v_rg_ba, v_rg_wi=v_rg_wi, v_rg_bi=v_rg_bi, v_s5_a_re=v_s5_a_re, v_s5_a_im=v_s5_a_im, v_s5_log_dt=v_s5_log_dt, v_s5_b_re=v_s5_b_re, v_s5_b_im=v_s5_b_im, v_s5_c_re=v_s5_c_re, v_s5_c_im=v_s5_c_im, v_s5_d=v_s5_d, v_s5_glu_w=v_s5_glu_w, v_s5_glu_b=v_s5_glu_b, v_w_out=v_w_out, v_b_out=v_b_out, v_ln2_g=v_ln2_g, v_ln2_b=v_ln2_b, v_mlp_w1=v_mlp_w1, v_mlp_b1=v_mlp_b1, v_mlp_w2=v_mlp_w2, v_mlp_b2=v_mlp_b2)
    weights = {n: given[n] for n in TWIN_WEIGHTS}
    shared = {n: given[n] for n in SHARED_INPUTS}
    per_example = {n: given[n] for n in ['x', 'c', 'ctx']}
    grad_fn = _jax.value_and_grad(_loss, argnums=(0, 1))

    def one_microbatch(ex, loss_target):
        ex = dict(ex)
        diff = ex.pop(TWIN_DIFF_INPUT)
        return grad_fn(weights, diff, {**shared, **ex}, loss_target)

    if N_MICROBATCH == 1:
        loss, (grad_w, grad_x) = one_microbatch(per_example, given["loss_target"])
    else:
        def body(carry, xs):
            loss_sum, grad_sum = carry
            l_k, (gw_k, gx_k) = one_microbatch(xs[0], xs[1])
            with _jax.named_scope("update"):
                return (loss_sum + l_k, _jax.tree.map(_jnp.add, grad_sum, gw_k)), gx_k

        init = (_jnp.zeros((), _jnp.float32), _jax.tree.map(_jnp.zeros_like, weights))
        (loss, grad_w), grad_x = _jax.lax.scan(body, init, (per_example, given["loss_target"]))
    with _jax.named_scope("update"):
        delta_w, new_m, new_v = {}, {}, {}
        for n in TWIN_WEIGHTS:
            delta_w[n], new_m[n], new_v[n] = _adamw(weights[n], grad_w[n], given["m_" + n], given["v_" + n])
    return (loss, grad_x, *[grad_w[n] for n in TWIN_WEIGHTS], *[delta_w[n] for n in TWIN_WEIGHTS],
            *[new_m[n] for n in TWIN_WEIGHTS], *[new_v[n] for n in TWIN_WEIGHTS])
```

```python
import functools
import math

import jax
import jax.numpy as jnp
from jax import lax
from jax.experimental import pallas as pl
from jax.experimental.pallas import tpu as pltpu

F32 = jnp.float32
BF16 = jnp.bfloat16
MXU_DT = jnp.bfloat16
ACT_DT = jnp.bfloat16
WIRE_DT = jnp.bfloat16

N_DEV = 8
GRID_W = 64
RG_C = 8.0
LN_EPS = 1e-5
S5_T = 16
ROW_TILE = 256
VMEM_LIMIT = 56 * 1024 * 1024
PACK_W = 1024

ADAM_LR = 0.001
ADAM_B1 = 0.9
ADAM_B2 = 0.999
ADAM_EPS = 1e-08
ADAM_WD = 0.01
ADAM_STEP = 10

WEIGHTS = ['c_ctx', 'ada_w', 'ada_b', 'ln1_g', 'ln1_b', 'w_in', 'conv_w', 'conv_b', 'rg_lambda', 'rg_wa', 'rg_ba',
           'rg_wi', 'rg_bi', 's5_a_re', 's5_a_im', 's5_log_dt', 's5_b_re', 's5_b_im', 's5_c_re', 's5_c_im', 's5_d',
           's5_glu_w', 's5_glu_b', 'w_out', 'b_out', 'ln2_g', 'ln2_b', 'mlp_w1', 'mlp_b1', 'mlp_w2', 'mlp_b2']
COL_SHARDED = ('w_in', 'mlp_w1')
ROW_SHARDED = ('s5_glu_w', 'w_out', 'mlp_w2')
BIG = ('w_in', 's5_glu_w', 'w_out', 'mlp_w1', 'mlp_w2')
CHAN_SHARDED = ('conv_w', 'rg_lambda', 'rg_ba', 'rg_bi')
S5_NAMES = ('s5_a_re', 's5_a_im', 's5_log_dt', 's5_b_re', 's5_b_im', 's5_c_re', 's5_c_im', 's5_d')
MESH = pl.DeviceIdType.MESH


def _tile(n, pref, align):
    t = (min(pref, n) // align) * align
    while t >= align:
        if n % t == 0:
            return t
        t -= align
    return n


def _params(sem):
    return pltpu.CompilerParams(dimension_semantics=sem, vmem_limit_bytes=VMEM_LIMIT)


def _sigmoid(v):
    return 1.0 / (1.0 + jnp.exp(-v))


def _silu(v):
    return v * _sigmoid(v)


_GELU_K = math.sqrt(2.0 / math.pi)


def _gelu(v):
    return 0.5 * v * (1.0 + jnp.tanh(_GELU_K * (v + 0.044715 * v * v * v)))


def _gelu_grad(v):
    th = jnp.tanh(_GELU_K * (v + 0.044715 * v * v * v))
    return 0.5 * (1.0 + th) + 0.5 * v * (1.0 - th * th) * _GELU_K * (1.0 + 3.0 * 0.044715 * v * v)


def _expm1(v):
    series = v * (1.0 + v * (0.5 + v * (1.0 / 6.0 + v * (1.0 / 24.0 + v * (1.0 / 120.0)))))
    return jnp.where(jnp.abs(v) < 0.1, series, jnp.exp(v) - 1.0)


def _softplus(v):
    return jnp.maximum(v, 0.0) + jnp.log(1.0 + jnp.exp(-jnp.abs(v)))


def _mm(a, b, *, ta=False, tb=False, bias=None, a_fn=None, epi=None, extras=(), out_dtype=F32, name,
        tm=512, tn=512, tk=512):
    if ta:
        kdim, m = a.shape
    else:
        m, kdim = a.shape
    if tb:
        n, kb = b.shape
    else:
        kb, n = b.shape
    assert kdim == kb, (a.shape, b.shape, ta, tb)
    tm = _tile(m, tm, 128 if ta else 8)
    tn = _tile(n, tn, 128)
    tk = _tile(kdim, tk, 128)
    nk = kdim // tk
    a_spec = pl.BlockSpec((tk, tm), lambda i, j, k: (k, i)) if ta else pl.BlockSpec((tm, tk), lambda i, j, k: (i, k))
    b_spec = pl.BlockSpec((tn, tk), lambda i, j, k: (j, k)) if tb else pl.BlockSpec((tk, tn), lambda i, j, k: (k, j))
    in_specs = [a_spec, b_spec]
    args = [a, b]
    has_bias = bias is not None
    if has_bias:
        in_specs.append(pl.BlockSpec((1, tn), lambda i, j, k: (0, j)))
        args.append(bias.reshape(1, n).astype(F32))
    for e in extras:
        assert e.shape == (m, n), (e.shape, m, n)
        in_specs.append(pl.BlockSpec((tm, tn), lambda i, j, k: (i, j)))
        args.append(e)
    nex = len(extras)
    dn = (((0 if ta else 1,), (1 if tb else 0,)), ((), ()))

    def body(*refs):
        a_ref, b_ref = refs[0], refs[1]
        pos = 2
        bias_ref = refs[pos] if has_bias else None
        pos += int(has_bias)
        ex_refs = refs[pos:pos + nex]
        o_ref, acc_ref = refs[pos + nex], refs[pos + nex + 1]
        k = pl.program_id(2)

        @pl.when(k == 0)
        def _():
            acc_ref[...] = jnp.zeros_like(acc_ref)

        av = a_ref[...]
        if a_fn is not None:
            av = a_fn(av.astype(F32))
        acc_ref[...] += lax.dot_general(av.astype(MXU_DT), b_ref[...].astype(MXU_DT), dn, preferred_element_type=F32)

        @pl.when(k == nk - 1)
        def _():
            r = acc_ref[...]
            if has_bias:
                r = r + bias_ref[...]
            if epi is not None:
                r = epi(r, *[e[...] for e in ex_refs])
            o_ref[...] = r.astype(out_dtype)

    return pl.pallas_call(
        body, out_shape=jax.ShapeDtypeStruct((m, n), out_dtype), grid=(m // tm, n // tn, nk),
        in_specs=in_specs, out_specs=pl.BlockSpec((tm, tn), lambda i, j, k: (i, j)),
        scratch_shapes=[pltpu.VMEM((tm, tn), F32)],
        compiler_params=_params(("parallel", "parallel", "arbitrary")), name=name)(*args)


def _ew(fn, tiles, rows=(), mods=(), outs=(), accs=(), *, name, nctx_tiles=0, tr=None):
    bsz, tlen = tiles[0][0].shape[0], tiles[0][0].shape[1]
    if tr is None:
        tr = _tile(tlen, ROW_TILE, 8)
    nt = tlen // tr
    in_specs, args = [], []
    for arr, cb, width, toff in tiles:
        in_specs.append(pl.BlockSpec((1, tr, width), functools.partial(
            lambda b, t, cb, toff: (b, jnp.maximum(t - toff, 0), cb), cb=cb, toff=toff)))
        args.append(arr)
    for r in rows:
        in_specs.append(pl.BlockSpec(r.shape, lambda b, t: (0, 0)))
        args.append(r)

    def seg_of(t):
        return jnp.where(t >= nctx_tiles, 1, 0)

    for mo in mods:
        in_specs.append(pl.BlockSpec((1, 1) + mo.shape[2:], lambda b, t: (b, seg_of(t), 0, 0)))
        args.append(mo)
    out_shape, out_specs = [], []
    for width, dt in outs:
        out_shape.append(jax.ShapeDtypeStruct((bsz, tlen, width), dt))
        out_specs.append(pl.BlockSpec((1, tr, width), lambda b, t: (b, t, 0)))
    for kk, cc in accs:
        out_shape.append(jax.ShapeDtypeStruct((bsz, 2, kk, cc), F32))
        out_specs.append(pl.BlockSpec((1, 1, kk, cc), lambda b, t: (b, seg_of(t), 0, 0)))
    nti, nr, nm, no, na = len(tiles), len(rows), len(mods), len(outs), len(accs)

    def body(*refs):
        t = pl.program_id(1)
        tv = [r[0] for r in refs[:nti]]
        rv = [r[...] for r in refs[nti:nti + nr]]
        mv = [r[0, 0] for r in refs[nti + nr:nti + nr + nm]]
        o_refs = refs[nti + nr + nm:nti + nr + nm + no]
        a_refs = refs[nti + nr + nm + no:]
        seg = seg_of(t)
        ov, av = fn(tv, rv, mv, seg)
        for r, v in zip(o_refs, ov):
            r[0] = v.astype(r.dtype)
        if na:
            @pl.when((t == 0) | (t == nctx_tiles))
            def _():
                for r in a_refs:
                    r[...] = jnp.zeros_like(r)

            for r, v in zip(a_refs, av):
                r[0, 0] += v

    res = pl.pallas_call(
        body, out_shape=tuple(out_shape), grid=(bsz, nt), in_specs=in_specs, out_specs=tuple(out_specs),
        compiler_params=_params(("arbitrary", "arbitrary")), name=name)(*args)
    return res[:no], res[no:]


def _full(arr):
    return (arr, 0, arr.shape[-1], 0)


def _colsum(v):
    return jnp.sum(v, axis=0, keepdims=True)


def _shifted(x, prev8, next8, first, last, k):
    tr = x.shape[0]
    rid = lax.broadcasted_iota(jnp.int32, x.shape, 0)
    keep_prev = jnp.where(first, 0.0, 1.0)
    keep_next = jnp.where(last, 0.0, 1.0)
    if k == -1:
        return jnp.where(rid == 0, prev8[7:8] * keep_prev, pltpu.roll(x, 1, 0))
    if k == -2:
        r = pltpu.roll(x, 2, 0)
        r = jnp.where(rid == 1, prev8[7:8] * keep_prev, r)
        return jnp.where(rid == 0, prev8[6:7] * keep_prev, r)
    if k == 1:
        return jnp.where(rid == tr - 1, next8[0:1] * keep_next, pltpu.roll(x, tr - 1, 0))
    if k == 2:
        r = pltpu.roll(x, tr - 2, 0)
        r = jnp.where(rid == tr - 2, next8[0:1] * keep_next, r)
        return jnp.where(rid == tr - 1, next8[1:2] * keep_next, r)
    raise ValueError(k)


def _halo_specs(tr, width, cb, n8):
    cur = pl.BlockSpec((1, tr, width), lambda b, t: (b, t, cb))
    prev = pl.BlockSpec((1, 8, width), lambda b, t: (b, jnp.maximum(t * (tr // 8) - 1, 0), cb))
    nxt = pl.BlockSpec((1, 8, width), lambda b, t: (b, jnp.minimum((t + 1) * (tr // 8), n8 - 1), cb))
    return [cur, prev, nxt]


def _conv_fwd(z3, conv_w, conv_b, d, nctx_tiles, tr, name):
    bsz, tlen, _ = z3.shape
    nt = tlen // tr

    def body(x_ref, xp_ref, xn_ref, w_ref, b_ref, o_ref):
        t = pl.program_id(1)
        first = (t == 0) | (t == nctx_tiles)
        last = (t == nctx_tiles - 1) | (t == nt - 1)
        x, p8, n8 = x_ref[0], xp_ref[0], xn_ref[0]
        w = w_ref[...]
        o_ref[0] = (b_ref[...] + w[0:1] * _shifted(x, p8, n8, first, last, -1) + w[1:2] * x
                    + w[2:3] * _shifted(x, p8, n8, first, last, 1) + w[3:4] * _shifted(x, p8, n8, first, last, 2))

    return pl.pallas_call(
        body, out_shape=jax.ShapeDtypeStruct((bsz, tlen, d), F32), grid=(bsz, nt),
        in_specs=_halo_specs(tr, d, 0, tlen // 8) + [pl.BlockSpec((4, d), lambda b, t: (0, 0)),
                                                      pl.BlockSpec((1, d), lambda b, t: (0, 0))],
        out_specs=pl.BlockSpec((1, tr, d), lambda b, t: (b, t, 0)),
        compiler_params=_params(("parallel", "parallel")), name=name)(z3, z3, z3, conv_w, conv_b.reshape(1, d))


def _conv_bwd(dxc, z3, conv_w, d, nctx_tiles, tr, name):
    bsz, tlen, _ = dxc.shape
    nt = tlen // tr

    def body(g_ref, gp_ref, gn_ref, x_ref, xp_ref, xn_ref, w_ref, o_ref, acc_ref):
        t = pl.program_id(1)
        first = (t == 0) | (t == nctx_tiles)
        last = (t == nctx_tiles - 1) | (t == nt - 1)
        g, gp, gn = g_ref[0], gp_ref[0], gn_ref[0]
        x, xp, xn = x_ref[0], xp_ref[0], xn_ref[0]
        w = w_ref[...]
        o_ref[0] = (w[0:1] * _shifted(g, gp, gn, first, last, 1) + w[1:2] * g
                    + w[2:3] * _shifted(g, gp, gn, first, last, -1) + w[3:4] * _shifted(g, gp, gn, first, last, -2))

        @pl.when(t == 0)
        def _():
            acc_ref[...] = jnp.zeros_like(acc_ref)

        acc_ref[0, 0:1] += _colsum(g * _shifted(x, xp, xn, first, last, -1))
        acc_ref[0, 1:2] += _colsum(g * x)
        acc_ref[0, 2:3] += _colsum(g * _shifted(x, xp, xn, first, last, 1))
        acc_ref[0, 3:4] += _colsum(g * _shifted(x, xp, xn, first, last, 2))
        acc_ref[0, 4:5] += _colsum(g)

    return pl.pallas_call(
        body, out_shape=(jax.ShapeDtypeStruct((bsz, tlen, d), F32), jax.ShapeDtypeStruct((bsz, 8, d), F32)),
        grid=(bsz, nt),
        in_specs=_halo_specs(tr, d, 0, tlen // 8) + _halo_specs(tr, d, 0, tlen // 8)
        + [pl.BlockSpec((4, d), lambda b, t: (0, 0))],
        out_specs=(pl.BlockSpec((1, tr, d), lambda b, t: (b, t, 0)), pl.BlockSpec((1, 8, d), lambda b, t: (b, 0, 0))),
        compiler_params=_params(("arbitrary", "arbitrary")), name=name)(dxc, dxc, dxc, z3, z3, z3, conv_w)


def _rg_gates(x, lam, wa_ref, ba, wi_ref, bi, nh, hd):
    sp = _softplus(-lam)
    prs, pis = [], []
    for h in range(nh):
        xh = x[:, h * hd:(h + 1) * hd].astype(MXU_DT)
        prs.append(jnp.dot(xh, wa_ref[h].astype(MXU_DT), preferred_element_type=F32))
        pis.append(jnp.dot(xh, wi_ref[h].astype(MXU_DT), preferred_element_type=F32))
    r = _sigmoid(jnp.concatenate(prs, axis=1) + ba)
    i = _sigmoid(jnp.concatenate(pis, axis=1) + bi)
    la = -RG_C * sp * r
    a = jnp.exp(la)
    mult = jnp.sqrt(-_expm1(2.0 * la))
    return sp, r, i, a, mult


def _scan_tile(t, nctx_tiles, nt, reverse):
    if not reverse:
        return t
    return jnp.where(t < nctx_tiles, nctx_tiles - 1 - t, nt - 1 - (t - nctx_tiles))


def _unscan_tile(t, nctx_tiles, nt, reverse):
    if not reverse:
        return nt - 1 - t
    return jnp.where(t < nt - nctx_tiles, nctx_tiles + t, t - (nt - nctx_tiles))


def _rg_param_specs(d, nh, hd):
    vec = pl.BlockSpec((1, d), lambda b, t: (0, 0))
    mat = pl.BlockSpec((nh, hd, hd), lambda b, t: (0, 0, 0))
    return [vec, mat, vec, mat, vec]


def _rg_fwd(xc, lam, wa, ba, wi, bi, reverse, nctx_tiles, tr, name):
    bsz, tlen, d = xc.shape
    nh, hd = wa.shape[0], wa.shape[1]
    nt = tlen // tr
    tmap = lambda b, t: (b, _scan_tile(t, nctx_tiles, nt, reverse), 0)

    def body(x_ref, lam_ref, wa_ref, ba_ref, wi_ref, bi_ref, h_ref, a_scr, b_scr, carry):
        @pl.when(pl.program_id(1) == 0)
        def _():
            carry[...] = jnp.zeros_like(carry)

        x = x_ref[0]
        _, _, i, a, mult = _rg_gates(x, lam_ref[...], wa_ref, ba_ref[...], wi_ref, bi_ref[...], nh, hd)
        a_scr[...] = a
        b_scr[...] = mult * (i * x)

        def blk(j, h):
            for r in range(8):
                row = (tr - 1 - (j * 8 + r)) if reverse else (j * 8 + r)
                h = a_scr[pl.ds(row, 1), :] * h + b_scr[pl.ds(row, 1), :]
                h_ref[0, pl.ds(row, 1), :] = h
            return h

        carry[0:1, :] = lax.fori_loop(0, tr // 8, blk, carry[0:1, :])

    return pl.pallas_call(
        body, out_shape=jax.ShapeDtypeStruct((bsz, tlen, d), F32), grid=(bsz, nt),
        in_specs=[pl.BlockSpec((1, tr, d), tmap)] + _rg_param_specs(d, nh, hd),
        out_specs=pl.BlockSpec((1, tr, d), tmap),
        scratch_shapes=[pltpu.VMEM((tr, d), F32), pltpu.VMEM((tr, d), F32), pltpu.VMEM((8, d), F32)],
        compiler_params=_params(("arbitrary", "arbitrary")), name=name)(
            xc, lam.reshape(1, d), wa, ba.reshape(1, d), wi, bi.reshape(1, d))


def _rg_bwd(xc, h, dmix, z3, lam, wa, ba, wi, bi, addend, reverse, nctx_tiles, tr, name):
    bsz, tlen, d = xc.shape
    nh, hd = wa.shape[0], wa.shape[1]
    nt = tlen // tr
    r8 = tr // 8

    def tile_of(t):
        return _unscan_tile(t, nctx_tiles, nt, reverse)

    tmap = lambda b, t: (b, tile_of(t), 0)
    gmap = lambda b, t: (b, tile_of(t), 1)

    def halo_map(b, t):
        tt = tile_of(t)
        if not reverse:
            return (b, jnp.maximum(tt * r8 - 1, 0), 0)
        return (b, jnp.where(tt == nt - 1, 0, (tt + 1) * r8), 0)

    has_add = addend is not None

    def body(*refs):
        x_ref, h_ref, halo_ref, dr_ref, z_ref, lam_ref, wa_ref, ba_ref, wi_ref, bi_ref = refs[:10]
        pos = 10
        add_ref = refs[pos] if has_add else None
        pos += int(has_add)
        dx_ref, dwa_ref, dwi_ref, dv_ref, a_scr, g_scr, dh_scr, carry = refs[pos:pos + 8]
        b = pl.program_id(0)
        t = pl.program_id(1)
        tt = tile_of(t)

        @pl.when((b == 0) & (t == 0))
        def _():
            dwa_ref[...] = jnp.zeros_like(dwa_ref)
            dwi_ref[...] = jnp.zeros_like(dwi_ref)
            dv_ref[...] = jnp.zeros_like(dv_ref)

        @pl.when(t == 0)
        def _():
            carry[...] = jnp.zeros_like(carry)

        x = x_ref[0]
        sp, r, i, a, mult = _rg_gates(x, lam_ref[...], wa_ref, ba_ref[...], wi_ref, bi_ref[...], nh, hd)
        a_scr[...] = a
        dh_scr[...] = dr_ref[0] * _gelu(z_ref[0])

        def blk(j, cc):
            for rr in range(8):
                row = (j * 8 + rr) if reverse else (tr - 1 - (j * 8 + rr))
                g = dh_scr[pl.ds(row, 1), :] + cc
                g_scr[pl.ds(row, 1), :] = g
                cc = a_scr[pl.ds(row, 1), :] * g
            return cc

        carry[0:1, :] = lax.fori_loop(0, r8, blk, carry[0:1, :])
        g = g_scr[...]
        hv = h_ref[0]
        rid = lax.broadcasted_iota(jnp.int32, hv.shape, 0)
        if not reverse:
            valid = jnp.where(tt > 0, 1.0, 0.0)
            hprev = jnp.where(rid == 0, halo_ref[0][7:8] * valid, pltpu.roll(hv, 1, 0))
        else:
            valid = jnp.where(tt == nctx_tiles - 1, 0.0, 1.0)
            hprev = jnp.where(rid == tr - 1, halo_ref[0][0:1] * valid, pltpu.roll(hv, tr - 1, 0))
        dla = g * hprev * a - g * (i * x) * (a * a) / mult
        dpr = dla * (-RG_C * sp) * r * (1.0 - r)
        dpi = g * mult * x * i * (1.0 - i)
        dx = g * mult * i
        dxs = []
        nt_dims = (((1,), (1,)), ((), ()))
        tn_dims = (((0,), (0,)), ((), ()))
        for hh in range(nh):
            sl = slice(hh * hd, (hh + 1) * hd)
            dpr_h = dpr[:, sl].astype(MXU_DT)
            dpi_h = dpi[:, sl].astype(MXU_DT)
            xh = x[:, sl].astype(MXU_DT)
            dxs.append(lax.dot_general(dpr_h, wa_ref[hh].astype(MXU_DT), nt_dims, preferred_element_type=F32)
                       + lax.dot_general(dpi_h, wi_ref[hh].astype(MXU_DT), nt_dims, preferred_element_type=F32))
            dwa_ref[hh] += lax.dot_general(xh, dpr_h, tn_dims, preferred_element_type=F32)
            dwi_ref[hh] += lax.dot_general(xh, dpi_h, tn_dims, preferred_element_type=F32)
        dx = dx + jnp.concatenate(dxs, axis=1)
        if has_add:
            dx = dx + add_ref[0]
        dx_ref[0] = dx
        dv_ref[0:1] += _colsum(dla * (-RG_C * r))
        dv_ref[1:2] += _colsum(dpr)
        dv_ref[2:3] += _colsum(dpi)

    in_specs = [pl.BlockSpec((1, tr, d), tmap), pl.BlockSpec((1, tr, d), tmap), pl.BlockSpec((1, 8, d), halo_map),
                pl.BlockSpec((1, tr, d), tmap), pl.BlockSpec((1, tr, d), gmap)] + _rg_param_specs(d, nh, hd)
    args = [xc, h, h, dmix, z3, lam.reshape(1, d), wa, ba.reshape(1, d), wi, bi.reshape(1, d)]
    if has_add:
        in_specs.append(pl.BlockSpec((1, tr, d), tmap))
        args.append(addend)
    mat = pl.BlockSpec((nh, hd, hd), lambda b, t: (0, 0, 0))
    return pl.pallas_call(
        body, out_shape=(jax.ShapeDtypeStruct((bsz, tlen, d), F32), jax.ShapeDtypeStruct((nh, hd, hd), F32),
                         jax.ShapeDtypeStruct((nh, hd, hd), F32), jax.ShapeDtypeStruct((8, d), F32)),
        grid=(bsz, nt), in_specs=in_specs,
        out_specs=(pl.BlockSpec((1, tr, d), tmap), mat, mat, pl.BlockSpec((8, d), lambda b, t: (0, 0))),
        scratch_shapes=[pltpu.VMEM((tr, d), F32), pltpu.VMEM((tr, d), F32), pltpu.VMEM((tr, d), F32),
                        pltpu.VMEM((8, d), F32)],
        compiler_params=_params(("arbitrary", "arbitrary")), name=name)(*args)


def _s5_operators(a_re, a_im, log_dt, b_re, b_im, c_re, c_im, d_skip):
    hi = lax.Precision.HIGHEST
    tt = S5_T
    lam = lax.complex(a_re, a_im)
    lamdt = lam * jnp.exp(log_dt)[..., None]
    lam_bar = jnp.exp(lamdt)
    b_bar = ((lam_bar - 1.0) / lam)[..., None] * lax.complex(b_re, b_im)
    cm = lax.complex(c_re, c_im)
    taus = jnp.arange(tt + 1, dtype=F32)
    pw = jnp.exp(lamdt[:, :, None, :] * taus[None, None, :, None])
    kern = jnp.real(jnp.einsum('dgop,dgtp,dgpi->dgtoi', cm, pw[:, :, :tt], b_bar, precision=hi))
    g, h = d_skip.shape
    s_idx = jnp.arange(tt)[:, None]
    t_idx = jnp.arange(tt)[None, :]

    def toeplitz(kd, lag, mask):
        sel = kd[:, jnp.clip(lag, 0, tt - 1)]
        sel = jnp.where(mask[None, :, :, None, None], sel, 0.0)
        return sel.transpose(0, 1, 4, 2, 3).reshape(g, tt * h, tt * h)

    tz = toeplitz(kern[0], t_idx - s_idx, t_idx >= s_idx) + toeplitz(kern[1], s_idx - t_idx, s_idx >= t_idx)
    eye = jnp.eye(tt * h, dtype=F32)
    tz = tz + eye[None] * jnp.tile(d_skip, (1, tt))[:, None, :]

    def reim(zc, axis):
        return jnp.concatenate([jnp.real(zc), jnp.imag(zc)], axis=axis)

    wf = pw[0][:, tt - 1 - jnp.arange(tt), None, :] * b_bar[0].transpose(0, 2, 1)[:, None, :, :]
    wb = pw[1][:, jnp.arange(tt), None, :] * b_bar[1].transpose(0, 2, 1)[:, None, :, :]
    wf = reim(wf, -1).reshape(g, tt * h, -1)
    wb = reim(wb, -1).reshape(g, tt * h, -1)
    clf = cm[0].transpose(0, 2, 1)[:, :, None, :] * pw[0][:, 1 + jnp.arange(tt), :].transpose(0, 2, 1)[:, :, :, None]
    clb = cm[1].transpose(0, 2, 1)[:, :, None, :] * pw[1][:, tt - jnp.arange(tt), :].transpose(0, 2, 1)[:, :, :, None]
    vf = jnp.concatenate([jnp.real(clf), -jnp.imag(clf)], axis=1).reshape(g, -1, tt * h)
    vb = jnp.concatenate([jnp.real(clb), -jnp.imag(clb)], axis=1).reshape(g, -1, tt * h)
    lt = pw[:, :, tt]
    rows = []
    for dd in range(2):
        re, im = jnp.real(lt[dd]), jnp.imag(lt[dd])
        rows += [jnp.concatenate([re, re], -1), jnp.concatenate([-im, im], -1)]
    lmul = jnp.stack(rows + [jnp.zeros_like(rows[0])] * 4, axis=1)
    return tz, wf, wb, vf, vb, lmul


def _chunk_order(j, ncc, nc):
    return jnp.where(j < ncc, ncc - 1 - j, nc - 1 - (j - ncc))


def _s5_fwd(u, ops, ncc, name):
    tz, wf, wb, vf, vb, lmul = ops
    g, bsz, nc, th = u.shape
    p2 = wf.shape[-1]
    ph = p2 // 2

    def body(u_ref, tz_ref, wf_ref, wb_ref, vf_ref, vb_ref, l_ref, y_ref, hf_ref, hb_ref, sf, sb):
        lm = l_ref[0]
        a1f, a2f, a1b, a2b = lm[0:1], lm[1:2], lm[2:3], lm[3:4]
        for b in range(bsz):
            ub = u_ref[0, b].astype(MXU_DT)
            sf[b] = jnp.dot(ub, wf_ref[0].astype(MXU_DT), preferred_element_type=F32)
            sb[b] = jnp.dot(ub, wb_ref[0].astype(MXU_DT), preferred_element_type=F32)

        def step(j, hs):
            kb = _chunk_order(j, ncc, nc)
            new = []
            for b in range(bsz):
                hf, hb = hs[2 * b], hs[2 * b + 1]
                hf_ref[0, b, pl.ds(j, 1), :] = hf
                hb_ref[0, b, pl.ds(kb, 1), :] = hb
                new.append(a1f * hf + a2f * pltpu.roll(hf, ph, 1) + sf[b, pl.ds(j, 1), :])
                new.append(a1b * hb + a2b * pltpu.roll(hb, ph, 1) + sb[b, pl.ds(kb, 1), :])
            return tuple(new)

        lax.fori_loop(0, nc, step, tuple(jnp.zeros((1, p2), F32) for _ in range(2 * bsz)))
        for b in range(bsz):
            ub = u_ref[0, b].astype(MXU_DT)
            y_ref[0, b] = (jnp.dot(ub, tz_ref[0].astype(MXU_DT), preferred_element_type=F32)
                           + jnp.dot(hf_ref[0, b].astype(MXU_DT), vf_ref[0].astype(MXU_DT), preferred_element_type=F32)
                           + jnp.dot(hb_ref[0, b].astype(MXU_DT), vb_ref[0].astype(MXU_DT), preferred_element_type=F32))

    act = pl.BlockSpec((1, bsz, nc, th), lambda i: (i, 0, 0, 0))
    st = pl.BlockSpec((1, bsz, nc, p2), lambda i: (i, 0, 0, 0))
    op3 = lambda arr: pl.BlockSpec((1,) + arr.shape[1:], lambda i: (i, 0, 0))
    return pl.pallas_call(
        body, out_shape=(jax.ShapeDtypeStruct(u.shape, F32), jax.ShapeDtypeStruct((g, bsz, nc, p2), F32),
                         jax.ShapeDtypeStruct((g, bsz, nc, p2), F32)),
        grid=(g,), in_specs=[act] + [op3(o) for o in ops], out_specs=(act, st, st),
        scratch_shapes=[pltpu.VMEM((bsz, nc, p2), F32), pltpu.VMEM((bsz, nc, p2), F32)],
        compiler_params=_params(("parallel",)), name=name)(u, *ops)


def _s5_bwd(dy, u, hf, hb, ops, ncc, name):
    tz, wf, wb, vf, vb, lmul = ops
    g, bsz, nc, th = u.shape
    p2 = wf.shape[-1]
    ph = p2 // 2
    tzt, wft, wbt = jnp.swapaxes(tz, 1, 2), jnp.swapaxes(wf, 1, 2), jnp.swapaxes(wb, 1, 2)
    vft, vbt = jnp.swapaxes(vf, 1, 2), jnp.swapaxes(vb, 1, 2)
    tn_dims = (((0,), (0,)), ((), ()))

    def body(dy_ref, u_ref, hf_ref, hb_ref, tzt_ref, wft_ref, wbt_ref, vft_ref, vbt_ref, l_ref,
             du_ref, dtz_ref, dwf_ref, dwb_ref, dvf_ref, dvb_ref, dl_ref, injf, injb, dsf, dsb):
        lm = l_ref[0]
        a1f, a2f, a1b, a2b = lm[0:1], lm[1:2], lm[2:3], lm[3:4]
        for b in range(bsz):
            dyb = dy_ref[0, b].astype(MXU_DT)
            injf[b] = jnp.dot(dyb, vft_ref[0].astype(MXU_DT), preferred_element_type=F32)
            injb[b] = jnp.dot(dyb, vbt_ref[0].astype(MXU_DT), preferred_element_type=F32)

        def step(j, carry):
            qs, (d1f, d2f, d1b, d2b) = carry[:2 * bsz], carry[2 * bsz:]
            kf = nc - 1 - j
            kb = _chunk_order(nc - 1 - j, ncc, nc)
            new = []
            for b in range(bsz):
                qf, qb = qs[2 * b], qs[2 * b + 1]
                dsf[b, pl.ds(kf, 1), :] = qf
                dsb[b, pl.ds(kb, 1), :] = qb
                hinf = hf_ref[0, b, pl.ds(kf, 1), :]
                hinb = hb_ref[0, b, pl.ds(kb, 1), :]
                d1f = d1f + qf * hinf
                d2f = d2f + qf * pltpu.roll(hinf, ph, 1)
                d1b = d1b + qb * hinb
                d2b = d2b + qb * pltpu.roll(hinb, ph, 1)
                new.append(injf[b, pl.ds(kf, 1), :] + a1f * qf + pltpu.roll(a2f * qf, ph, 1))
                new.append(injb[b, pl.ds(kb, 1), :] + a1b * qb + pltpu.roll(a2b * qb, ph, 1))
            return tuple(new) + (d1f, d2f, d1b, d2b)

        zero = jnp.zeros((1, p2), F32)
        fin = lax.fori_loop(0, nc, step, tuple(zero for _ in range(2 * bsz + 4)))
        dl_ref[0] = jnp.concatenate(list(fin[2 * bsz:]) + [jnp.zeros((4, p2), F32)], axis=0)
        dtz = jnp.zeros((th, th), F32)
        dwf = jnp.zeros((th, p2), F32)
        dwb = jnp.zeros((th, p2), F32)
        dvf = jnp.zeros((p2, th), F32)
        dvb = jnp.zeros((p2, th), F32)
        for b in range(bsz):
            dyb = dy_ref[0, b].astype(MXU_DT)
            ub = u_ref[0, b].astype(MXU_DT)
            dsfb = dsf[b].astype(MXU_DT)
            dsbb = dsb[b].astype(MXU_DT)
            du_ref[0, b] = (jnp.dot(dyb, tzt_ref[0].astype(MXU_DT), preferred_element_type=F32)
                            + jnp.dot(dsfb, wft_ref[0].astype(MXU_DT), preferred_element_type=F32)
                            + jnp.dot(dsbb, wbt_ref[0].astype(MXU_DT), preferred_element_type=F32))
            dtz = dtz + lax.dot_general(ub, dyb, tn_dims, preferred_element_type=F32)
            dwf = dwf + lax.dot_general(ub, dsfb, tn_dims, preferred_element_type=F32)
            dwb = dwb + lax.dot_general(ub, dsbb, tn_dims, preferred_element_type=F32)
            dvf = dvf + lax.dot_general(hf_ref[0, b].astype(MXU_DT), dyb, tn_dims, preferred_element_type=F32)
            dvb = dvb + lax.dot_general(hb_ref[0, b].astype(MXU_DT), dyb, tn_dims, preferred_element_type=F32)
        dtz_ref[0] = dtz
        dwf_ref[0] = dwf
        dwb_ref[0] = dwb
        dvf_ref[0] = dvf
        dvb_ref[0] = dvb

    act = pl.BlockSpec((1, bsz, nc, th), lambda i: (i, 0, 0, 0))
    st = pl.BlockSpec((1, bsz, nc, p2), lambda i: (i, 0, 0, 0))
    op3 = lambda arr: pl.BlockSpec((1,) + arr.shape[1:], lambda i: (i, 0, 0))
    ins = [dy, u, hf, hb, tzt, wft, wbt, vft, vbt, lmul]
    outs = [u, tz, wf, wb, vf, vb, lmul]
    res = pl.pallas_call(
        body, out_shape=tuple(jax.ShapeDtypeStruct(o.shape, F32) for o in outs), grid=(g,),
        in_specs=[act, act, st, st] + [op3(o) for o in ins[4:]],
        out_specs=tuple([act] + [op3(o) for o in outs[1:]]),
        scratch_shapes=[pltpu.VMEM((bsz, nc, p2), F32) for _ in range(4)],
        compiler_params=_params(("parallel",)), name=name)(*ins)
    return res[0], tuple(res[1:])


def _to_chunks(s, ctx_len, g):
    bsz, tlen, d = s.shape
    h = d // g
    seq = tlen - ctx_len
    rows = seq // GRID_W
    cpart = s[:, :ctx_len].reshape(bsz, ctx_len // S5_T, S5_T, g, h).transpose(3, 0, 1, 2, 4)
    lpart = s[:, ctx_len:].reshape(bsz, rows, GRID_W, g, h).transpose(3, 0, 2, 1, 4)
    cpart = cpart.reshape(g, bsz, ctx_len // S5_T, S5_T * h)
    lpart = lpart.reshape(g, bsz, seq // S5_T, S5_T * h)
    return jnp.concatenate([cpart, lpart], axis=2)


def _from_chunks(y, ctx_len):
    g, bsz, nc, th = y.shape
    h = th // S5_T
    ncc = ctx_len // S5_T
    seq = (nc - ncc) * S5_T
    rows = seq // GRID_W
    cpart = y[:, :, :ncc].reshape(g, bsz, ncc, S5_T, h).transpose(1, 2, 3, 0, 4).reshape(bsz, ctx_len, g * h)
    lpart = y[:, :, ncc:].reshape(g, bsz, GRID_W, rows, h).transpose(1, 3, 2, 0, 4).reshape(bsz, seq, g * h)
    return jnp.concatenate([cpart, lpart], axis=1)


def _me():
    return lax.axis_index("x"), lax.axis_index("y"), lax.axis_index("c")


def _peer(k):
    x, y, c = _me()
    px = (1 - x) if (k & 4) else x
    py = (1 - y) if (k & 2) else y
    pc = (1 - c) if (k & 1) else c
    return (px, py, pc), 4 * px + 2 * py + pc


def _exchange(x3, gather, name):
    shp = x3.shape[-2:]

    def body(x_ref, o_ref, send_sems, recv_sems, local_sem):
        xi, yi, ci = _me()
        me = 4 * xi + 2 * yi + ci
        mine = x_ref if gather else x_ref.at[me]
        local = pltpu.make_async_copy(mine, o_ref.at[me], local_sem)
        local.start()
        sends = []
        for k in range(1, N_DEV):
            dev, pid = _peer(k)
            src = x_ref if gather else x_ref.at[pid]
            cp = pltpu.make_async_remote_copy(src_ref=src, dst_ref=o_ref.at[me], send_sem=send_sems.at[k - 1],
                                              recv_sem=recv_sems.at[k - 1], device_id=dev, device_id_type=MESH)
            cp.start()
            sends.append(cp)
        for k in range(1, N_DEV):
            dev, pid = _peer(k)
            pltpu.make_async_remote_copy(src_ref=mine, dst_ref=o_ref.at[pid], send_sem=send_sems.at[k - 1],
                                         recv_sem=recv_sems.at[k - 1], device_id=dev, device_id_type=MESH).wait_recv()
        for cp in sends:
            cp.wait_send()
        local.wait()

    return pl.pallas_call(
        body, out_shape=jax.ShapeDtypeStruct((N_DEV,) + shp, x3.dtype),
        in_specs=[pl.BlockSpec(memory_space=pl.ANY)], out_specs=pl.BlockSpec(memory_space=pl.ANY),
        scratch_shapes=[pltpu.SemaphoreType.DMA((N_DEV - 1,)), pltpu.SemaphoreType.DMA((N_DEV - 1,)),
                        pltpu.SemaphoreType.DMA(())],
        name=name)(x3)


def _sum_slots(x3, name):
    _, r, cdim = x3.shape
    tr = _tile(r, 256, 8)

    def body(x_ref, o_ref):
        acc = x_ref[0]
        for s in range(1, N_DEV):
            acc = acc + x_ref[s]
        o_ref[...] = acc

    return pl.pallas_call(
        body, out_shape=jax.ShapeDtypeStruct((r, cdim), F32), grid=(r // tr,),
        in_specs=[pl.BlockSpec((N_DEV, tr, cdim), lambda i: (0, i, 0))], out_specs=pl.BlockSpec((tr, cdim), lambda i: (i, 0)),
        compiler_params=_params(("parallel",)), name=name)(x3)


def _pack(arrs, dtype, lead=0):
    flat = jnp.concatenate([a.reshape(a.shape[:lead] + (-1,)).astype(dtype) for a in arrs], axis=-1)
    n = flat.shape[-1]
    pad = -n % (PACK_W * 16)
    flat = jnp.pad(flat, [(0, 0)] * lead + [(0, pad)])
    return flat.reshape(flat.shape[:lead] + (-1, PACK_W))


def _unpack(buf, shapes, lead=0):
    flat = buf.reshape(buf.shape[:lead] + (-1,))
    out, off = [], 0
    for shp in shapes:
        n = math.prod(shp)
        out.append(flat[..., off:off + n].reshape(buf.shape[:lead] + tuple(shp)))
        off += n
    return out


def _adamw(w, g, m, v, name):
    def fn(tv, rv, mv, seg):
        wv, gv, m0, v0 = tv
        m1 = ADAM_B1 * m0 + (1.0 - ADAM_B1) * gv
        v1 = ADAM_B2 * v0 + (1.0 - ADAM_B2) * (gv * gv)
        m_hat = m1 / (1.0 - ADAM_B1 ** ADAM_STEP)
        v_hat = v1 / (1.0 - ADAM_B2 ** ADAM_STEP)
        delta = -ADAM_LR * (m_hat / (jnp.sqrt(v_hat) + ADAM_EPS) + ADAM_WD * wv)
        return (delta, m1, v1), ()

    outs, _ = _ew(fn, [_full(a[None]) for a in (w, g, m, v)], outs=[(PACK_W, F32)] * 3, name=name)
    return [o[0] for o in outs]


def _layer_fwd(l, xin, modt, wts, sm, cfg):
    bsz, tlen, d = xin.shape
    bt = bsz * tlen
    nct, tr, ctx_len, g = cfg['nct'], cfg['tr'], cfg['ctx_len'], cfg['g']
    ncc = ctx_len // S5_T
    alpha = cfg['alpha']
    nm = lambda s: f"l{l}_{s}"

    def modulate(xv, i_shift, i_scale, name):
        def fn(tv, rv, mv, seg):
            mo = mv[0]
            return (tv[0] * (1.0 + mo[i_scale:i_scale + 1]) + mo[i_shift:i_shift + 1],), ()
        return _ew(fn, [_full(xv)], mods=[modt], outs=[(d, ACT_DT)], nctx_tiles=nct, tr=tr, name=name)[0][0]

    def ln_fwd(xv, mv_, i_gate, gam, bet, name):
        def fn(tv, rv, mv, seg):
            z = alpha * tv[0] + mv[0][i_gate:i_gate + 1] * tv[1]
            mu = jnp.mean(z, axis=-1, keepdims=True)
            zc = z - mu
            var = jnp.mean(zc * zc, axis=-1, keepdims=True)
            return (zc * lax.rsqrt(var + LN_EPS) * rv[0] + rv[1],), ()
        return _ew(fn, [_full(xv), _full(mv_)], rows=[gam.reshape(1, d), bet.reshape(1, d)], mods=[modt],
                   outs=[(d, F32)], nctx_tiles=nct, tr=tr, name=name)[0][0]

    sv = {'x': xin}
    u = modulate(xin, 0, 1, nm("mod1"))
    sv['u'] = u
    z3 = _mm(u.reshape(bt, d), wts['w_in'][l], name=nm("w_in")).reshape(bsz, tlen, 3 * d)
    sv['z3'] = z3
    xc = _conv_fwd(z3, sm['conv_w'][l], sm['conv_b'][l], d, nct, tr, nm("conv"))
    sv['xc'] = xc
    hs = []
    for dd in range(2):
        hs.append(_rg_fwd(xc, sm['rg_lambda'][l, dd], sm['rg_wa'][l, dd], sm['rg_ba'][l, dd], sm['rg_wi'][l, dd],
                          sm['rg_bi'][l, dd], bool(dd), nct, tr, nm(f"rg_fwd{dd}")))
    sv['hf'], sv['hb'] = hs

    def rgout(tv, rv, mv, seg):
        return ((tv[0] + tv[1]) * _gelu(tv[2]),), ()
    rg = _ew(rgout, [_full(hs[0]), _full(hs[1]), (z3, 1, d, 0)], outs=[(d, ACT_DT)], tr=tr, name=nm("rg_out"))[0][0]

    s5u = _to_chunks(z3[:, :, 2 * d:], ctx_len, g)
    sv['s5u'] = s5u
    ops = cfg['s5_ops'][l]
    y, hf5, hb5 = _s5_fwd(s5u, ops, ncc, nm("s5_fwd"))
    sv['hf5'], sv['hb5'] = hf5, hb5
    ytok = _from_chunks(y, ctx_len)
    sv['ytok'] = ytok

    def gelu_fn(tv, rv, mv, seg):
        return (_gelu(tv[0]),), ()
    gact = _ew(gelu_fn, [_full(ytok)], outs=[(d, ACT_DT)], tr=tr, name=nm("s5_gelu"))[0][0]
    sv['gact'] = gact
    gpre = _mm(gact.reshape(bt, d), wts['s5_glu_w'][l], bias=sm['s5_glu_b'][l], name=nm("glu")).reshape(bsz, tlen, d)
    sv['gpre'] = gpre

    def glu_fn(tv, rv, mv, seg):
        return (tv[0].astype(F32) * _sigmoid(tv[1]),), ()
    s5o = _ew(glu_fn, [_full(gact), _full(gpre)], outs=[(d, ACT_DT)], tr=tr, name=nm("glu_out"))[0][0]
    mixin = jnp.concatenate([rg, s5o], axis=-1).reshape(bt, 2 * d)
    sv['mixin'] = mixin
    mo = _mm(mixin, wts['w_out'][l], bias=sm['b_out'][l], name=nm("w_out")).reshape(bsz, tlen, d)
    sv['mo'] = mo
    x1 = ln_fwd(xin, mo, 2, sm['ln1_g'][l], sm['ln1_b'][l], nm("ln1"))
    sv['x1'] = x1
    u2 = modulate(x1, 3, 4, nm("mod2"))
    sv['u2'] = u2
    rl = _mm(u2.reshape(bt, d), wts['mlp_w1'][l], bias=sm['mlp_b1'][l], epi=lambda r: jnp.maximum(r, 0.0),
             out_dtype=ACT_DT, name=nm("mlp1"))
    sv['rl'] = rl
    fo = _mm(rl, wts['mlp_w2'][l], bias=sm['mlp_b2'][l], a_fn=lambda v: v * v, name=nm("mlp2")).reshape(bsz, tlen, d)
    sv['fo'] = fo
    x2 = ln_fwd(x1, fo, 5, sm['ln2_g'][l], sm['ln2_b'][l], nm("ln2"))
    return x2, sv


def _layer_bwd(l, dx2, sv, modt, wts, sm, cfg):
    bsz, tlen, d = dx2.shape
    bt = bsz * tlen
    nct, tr, ctx_len, g = cfg['nct'], cfg['tr'], cfg['ctx_len'], cfg['g']
    ncc = ctx_len // S5_T
    alpha = cfg['alpha']
    nm = lambda s: f"l{l}_{s}"
    gr = {}

    def ln_bwd(xv, mv_, i_gate, gam, dy, name):
        def fn(tv, rv, mv, seg):
            xx, mm_, dyy = tv
            gate = mv[0][i_gate:i_gate + 1]
            z = alpha * xx + gate * mm_
            mu = jnp.mean(z, axis=-1, keepdims=True)
            zc = z - mu
            var = jnp.mean(zc * zc, axis=-1, keepdims=True)
            rstd = lax.rsqrt(var + LN_EPS)
            xhat = zc * rstd
            dxh = dyy * rv[0]
            dz = rstd * (dxh - jnp.mean(dxh, axis=-1, keepdims=True) - xhat * jnp.mean(dxh * xhat, axis=-1, keepdims=True))
            dm = gate * dz
            acc = jnp.concatenate([_colsum(dz * mm_), _colsum(dyy * xhat), _colsum(dyy), _colsum(dm)], axis=0)
            return (alpha * dz, dm), (acc,)
        o, a = _ew(fn, [_full(xv), _full(mv_), _full(dy)], rows=[gam.reshape(1, d)], mods=[modt],
                   outs=[(d, F32), (d, F32)], accs=[(4, d)], nctx_tiles=nct, tr=tr, name=name)
        return o[0], o[1], a[0]

    def mod_bwd(du, xv, i_scale, addend, name):
        def fn(tv, rv, mv, seg):
            duu, xx, add = tv
            acc = jnp.concatenate([_colsum(duu), _colsum(duu * xx)], axis=0)
            return (add + duu * (1.0 + mv[0][i_scale:i_scale + 1]),), (acc,)
        o, a = _ew(fn, [_full(du), _full(xv), _full(addend)], mods=[modt], outs=[(d, F32)], accs=[(2, d)],
                   nctx_tiles=nct, tr=tr, name=name)
        return o[0], a[0]

    dx1a, dfo, acc2 = ln_bwd(sv['x1'], sv['fo'], 5, sm['ln2_g'][l], dx2, nm("ln2_bwd"))
    dfo2 = dfo.reshape(bt, d)
    dhp = _mm(dfo2, wts['mlp_w2'][l], tb=True, epi=lambda r, rl: r * (2.0 * rl.astype(F32)), extras=[sv['rl']],
              out_dtype=ACT_DT, name=nm("mlp2_dx"))
    gr['mlp_w2'] = _mm(sv['rl'], dfo2, ta=True, a_fn=lambda v: v * v, name=nm("mlp2_dw"))

    def colsum_fn(tv, rv, mv, seg):
        return (), (_colsum(tv[0].astype(F32)),)
    ff = dhp.shape[-1]
    gr['mlp_b1'] = _ew(colsum_fn, [_full(dhp.reshape(bsz, tlen, ff))], accs=[(1, ff)], tr=_tile(tlen, 64, 8),
                       name=nm("mlp_b1"))[1][0][:, 1, 0].sum(0)
    du2 = _mm(dhp, wts['mlp_w1'][l], tb=True, name=nm("mlp1_dx")).reshape(bsz, tlen, d)
    gr['mlp_w1'] = _mm(sv['u2'].reshape(bt, d), dhp, ta=True, name=nm("mlp1_dw"))
    dx1, accm2 = mod_bwd(du2, sv['x1'], 4, dx1a, nm("mod2_bwd"))
    gr['ln2_g'] = acc2[:, :, 1].sum((0, 1))
    gr['ln2_b'] = acc2[:, :, 2].sum((0, 1))
    gr['mlp_b2'] = acc2[:, :, 3].sum((0, 1))

    dxa, dmo, acc1 = ln_bwd(sv['x'], sv['mo'], 2, sm['ln1_g'][l], dx1, nm("ln1_bwd"))
    gr['ln1_g'] = acc1[:, :, 1].sum((0, 1))
    gr['ln1_b'] = acc1[:, :, 2].sum((0, 1))
    gr['b_out'] = acc1[:, :, 3].sum((0, 1))
    dmo2 = dmo.reshape(bt, d)
    dmix = _mm(dmo2, wts['w_out'][l], tb=True, name=nm("w_out_dx")).reshape(bsz, tlen, 2 * d)
    gr['w_out'] = _mm(sv['mixin'], dmo2, ta=True, name=nm("w_out_dw"))

    def glu_bwd(tv, rv, mv, seg):
        ds, ga, gp = tv
        ga = ga.astype(F32)
        sg = _sigmoid(gp)
        dg = ds * ga * sg * (1.0 - sg)
        return (dg, ds * sg), (_colsum(dg),)
    o, a = _ew(glu_bwd, [(dmix, 1, d, 0), _full(sv['gact']), _full(sv['gpre'])], outs=[(d, ACT_DT), (d, F32)],
               accs=[(1, d)], tr=tr, name=nm("glu_bwd"))
    dgp, t1 = o
    gr['s5_glu_b'] = a[0][:, 1, 0].sum(0)
    dgp2 = dgp.reshape(bt, d)
    dyt = _mm(dgp2, wts['s5_glu_w'][l], tb=True, epi=lambda r, t1v, yv: (r + t1v) * _gelu_grad(yv),
              extras=[t1.reshape(bt, d), sv['ytok'].reshape(bt, d)], name=nm("glu_dx")).reshape(bsz, tlen, d)
    gr['s5_glu_w'] = _mm(sv['gact'].reshape(bt, d), dgp2, ta=True, name=nm("glu_dw"))
    du5, dops = _s5_bwd(_to_chunks(dyt, ctx_len, g), sv['s5u'], sv['hf5'], sv['hb5'], cfg['s5_ops'][l], ncc, nm("s5_bwd"))
    ds5u = _from_chunks(du5, ctx_len)
    s5g = cfg['s5_vjp'][l](dops)
    for name, val in zip(S5_NAMES, s5g):
        gr[name] = val

    z3 = sv['z3']
    dxc = None
    for dd in range(2):
        dxc, dwa, dwi, dv = _rg_bwd(sv['xc'], sv['hf'] if dd == 0 else sv['hb'], dmix, z3, sm['rg_lambda'][l, dd],
                                    sm['rg_wa'][l, dd], sm['rg_ba'][l, dd], sm['rg_wi'][l, dd], sm['rg_bi'][l, dd],
                                    dxc, bool(dd), nct, tr, nm(f"rg_bwd{dd}"))
        gr[f'rg_wa{dd}'], gr[f'rg_wi{dd}'] = dwa, dwi
        gr[f'rg_lambda{dd}'] = dv[0] * (-_sigmoid(-sm['rg_lambda'][l, dd]))
        gr[f'rg_ba{dd}'], gr[f'rg_bi{dd}'] = dv[1], dv[2]

    def gate_bwd(tv, rv, mv, seg):
        return (tv[0] * (tv[1] + tv[2]) * _gelu_grad(tv[3]),), ()
    dgate = _ew(gate_bwd, [(dmix, 0, d, 0), _full(sv['hf']), _full(sv['hb']), (z3, 1, d, 0)], outs=[(d, F32)], tr=tr,
                name=nm("gate_bwd"))[0][0]
    drgx, accc = _conv_bwd(dxc, z3, sm['conv_w'][l], d, nct, tr, nm("conv_bwd"))
    gr['conv_w'] = accc[:, 0:4].sum(0)
    gr['conv_b'] = accc[:, 4].sum(0)
    dz = jnp.concatenate([drgx, dgate, ds5u], axis=-1).reshape(bt, 3 * d)
    du = _mm(dz, wts['w_in'][l], tb=True, name=nm("w_in_dx")).reshape(bsz, tlen, d)
    gr['w_in'] = _mm(sv['u'].reshape(bt, d), dz, ta=True, name=nm("w_in_dw"))
    dxin, accm1 = mod_bwd(du, sv['x'], 1, dxa, nm("mod1_bwd"))
    dmod = jnp.stack([accm1[:, :, 0], accm1[:, :, 1], acc1[:, :, 0], accm2[:, :, 0], accm2[:, :, 1], acc2[:, :, 0]], axis=2)
    return dxin, dmod, gr


def kernel(x, c, ctx, c_ctx, ada_w, ada_b, ln1_g, ln1_b, w_in, conv_w, conv_b, rg_lambda, rg_wa, rg_ba, rg_wi, rg_bi, s5_a_re, s5_a_im, s5_log_dt, s5_b_re, s5_b_im, s5_c_re, s5_c_im, s5_d, s5_glu_w, s5_glu_b, w_out, b_out, ln2_g, ln2_b, mlp_w1, mlp_b1, mlp_w2, mlp_b2, loss_target, m_c_ctx, m_ada_w, m_ada_b, m_ln1_g, m_ln1_b, m_w_in, m_conv_w, m_conv_b, m_rg_lambda, m_rg_wa, m_rg_ba, m_rg_wi, m_rg_bi, m_s5_a_re, m_s5_a_im, m_s5_log_dt, m_s5_b_re, m_s5_b_im, m_s5_c_re, m_s5_c_im, m_s5_d, m_s5_glu_w, m_s5_glu_b, m_w_out, m_b_out, m_ln2_g, m_ln2_b, m_mlp_w1, m_mlp_b1, m_mlp_w2, m_mlp_b2, v_c_ctx, v_ada_w, v_ada_b, v_ln1_g, v_ln1_b, v_w_in, v_conv_w, v_conv_b, v_rg_lambda, v_rg_wa, v_rg_ba, v_rg_wi, v_rg_bi, v_s5_a_re, v_s5_a_im, v_s5_log_dt, v_s5_b_re, v_s5_b_im, v_s5_c_re, v_s5_c_im, v_s5_d, v_s5_glu_w, v_s5_glu_b, v_w_out, v_b_out, v_ln2_g, v_ln2_b, v_mlp_w1, v_mlp_b1, v_mlp_w2, v_mlp_b2):
    loc = dict(locals())
    w = {n: loc[n] for n in WEIGHTS}
    mom = {n: loc["m_" + n] for n in WEIGHTS}
    vel = {n: loc["v_" + n] for n in WEIGHTS}
    bsz, seq, d = x.shape
    ctx_len = ctx.shape[1]
    tlen = ctx_len + seq
    depth = ada_w.shape[0]
    g = s5_a_re.shape[2]
    nmod = ada_b.shape[1] // d
    modc = ada_w.shape[2]
    tr = _tile(ctx_len, ROW_TILE, 8)
    cfg = dict(nct=ctx_len // tr, tr=tr, ctx_len=ctx_len, g=g, alpha=(2.0 * depth) ** 0.25)
    xi, yi, ci = _me()
    me = 4 * xi + 2 * yi + ci

    small_shapes = [c.shape] + [w[n].shape for n in CHAN_SHARDED]
    got = _exchange(_pack([c] + [w[n] for n in CHAN_SHARDED], F32), True, "gather_small")
    parts = _unpack(got, small_shapes, lead=1)
    c_all = parts[0].reshape(N_DEV * bsz, d)
    sm = {n: w[n] for n in WEIGHTS if n not in BIG and n not in CHAN_SHARDED and n != 'ada_w'}
    for n, p in zip(CHAN_SHARDED, parts[1:]):
        sm[n] = jnp.moveaxis(p, 0, -2).reshape(p.shape[1:-1] + (-1,))

    a_ext = jnp.concatenate([c_all, jnp.broadcast_to(c_ctx[None], (N_DEV, d))], axis=0)
    nrow = a_ext.shape[0]
    my_ada_b = lax.dynamic_slice_in_dim(ada_b, me * modc, modc, axis=1)
    mod_cols = jnp.stack([_mm(a_ext, ada_w[l], bias=my_ada_b[l], a_fn=_silu, name=f"l{l}_ada") for l in range(depth)])
    mod_all = _exchange(mod_cols.reshape(depth * nrow, modc), True, "gather_mod").reshape(N_DEV, depth, nrow, modc)
    mod_all = mod_all.transpose(1, 2, 0, 3).reshape(depth, nrow, nmod, d)
    mod_mine = lax.dynamic_slice_in_dim(mod_all, me * bsz, bsz, axis=1)
    mod_ctx = jnp.broadcast_to(mod_all[:, N_DEV * bsz][:, None], mod_mine.shape)
    modts = jnp.stack([mod_ctx, mod_mine], axis=2)

    big_shapes = [w[n].shape for n in BIG]
    gotw = _unpack(_exchange(_pack([w[n] for n in BIG], WIRE_DT), True, "gather_weights"), big_shapes, lead=1)
    wts = {}
    for n, p in zip(BIG, gotw):
        if n in COL_SHARDED:
            wts[n] = p.transpose(1, 2, 0, 3).reshape(p.shape[1], p.shape[2], -1)
        else:
            wts[n] = p.transpose(1, 0, 2, 3).reshape(p.shape[1], -1, p.shape[3])

    s5_ops, s5_vjp = [], []
    for l in range(depth):
        ops, vjp = jax.vjp(_s5_operators, s5_a_re[l], s5_a_im[l], s5_log_dt[l], s5_b_re[l], s5_b_im[l], s5_c_re[l],
                           s5_c_im[l], s5_d[l].reshape(g, -1))
        s5_ops.append(ops)
        s5_vjp.append(vjp)
    cfg['s5_ops'], cfg['s5_vjp'] = s5_ops, s5_vjp

    act = jnp.concatenate([ctx, x], axis=1)
    saved = []
    for l in range(depth):
        act, sv = _layer_fwd(l, act, modts[l], wts, sm, cfg)
        saved.append(sv)

    def loss_fn(tv, rv, mv, seg):
        err = tv[0] - tv[1]
        keep = jnp.where(seg == 1, 1.0, 0.0)
        return (err * (keep / d),), (_colsum(err * err) * keep,)
    o, a = _ew(loss_fn, [_full(act), (loss_target, 0, d, cfg['nct'])], outs=[(d, F32)], accs=[(1, d)],
               nctx_tiles=cfg['nct'], tr=tr, name="loss")
    dact = o[0]
    loss = lax.psum(0.5 * jnp.sum(a[0][:, 1]) / d, ("x", "y", "c"))

    grads = [None] * depth
    dmods = [None] * depth
    for l in reversed(range(depth)):
        dact, dmods[l], grads[l] = _layer_bwd(l, dact, saved[l], modts[l], wts, sm, cfg)
    grad_x = dact[:, ctx_len:]

    dmod = jnp.stack(dmods)
    mine = jnp.concatenate([dmod[:, :, 1].reshape(depth, bsz, nmod * d),
                            dmod[:, :, 0].sum(1).reshape(depth, 1, nmod * d)], axis=1)
    got = _exchange(mine.reshape(depth * (bsz + 1), nmod * d), True, "gather_dmod")
    got = got.reshape(N_DEV, depth, bsz + 1, nmod * d)
    dmod_rows = jnp.concatenate([got[:, :, :bsz].transpose(1, 0, 2, 3).reshape(depth, N_DEV * bsz, nmod * d),
                                 got[:, :, bsz].transpose(1, 0, 2)], axis=1)
    dmod_cols = lax.dynamic_slice_in_dim(dmod_rows, me * modc, modc, axis=2)
    g_ada_w = jnp.stack([_mm(a_ext, dmod_cols[l], ta=True, a_fn=_silu, name=f"l{l}_ada_dw") for l in range(depth)])

    def rowsum_fn(tv, rv, mv, seg):
        return (), (_colsum(tv[0]),)
    g_ada_b = _ew(rowsum_fn, [_full(dmod_rows)], accs=[(1, nmod * d)], name="ada_db")[1][0][:, 1, 0]
    dsilu = 0.0
    for l in range(depth):
        dsilu = dsilu + _mm(dmod_cols[l, N_DEV * bsz:], ada_w[l], tb=True, name=f"l{l}_ada_dc").sum(0)
    sig = _sigmoid(c_ctx)
    g_c_ctx_part = dsilu * (sig * (1.0 + c_ctx * (1.0 - sig)))

    def blocks(n, gl):
        if n in COL_SHARDED:
            return gl.reshape(gl.shape[0], N_DEV, -1).transpose(1, 0, 2)
        return gl.reshape(N_DEV, -1, gl.shape[1])
    big_parts = [jnp.stack([blocks(n, grads[l][n]) for l in range(depth)], axis=1) for n in BIG]
    summed = _sum_slots(_exchange(_pack(big_parts, F32, lead=1), False, "scatter_grads"), "sum_grads")
    g_big = dict(zip(BIG, _unpack(summed, big_shapes)))

    def both(name, l):
        return jnp.stack([grads[l][f'{name}{dd}'] for dd in range(2)])
    rep = {n: jnp.stack([grads[l][n] for l in range(depth)]) for n in
           ('ln1_g', 'ln1_b', 'conv_w', 'conv_b', 's5_glu_b', 'b_out', 'ln2_g', 'ln2_b', 'mlp_b1', 'mlp_b2') + S5_NAMES}
    for n in ('rg_lambda', 'rg_wa', 'rg_ba', 'rg_wi', 'rg_bi'):
        rep[n] = jnp.stack([both(n, l) for l in range(depth)])
    rep['c_ctx'] = g_c_ctx_part
    rep_names = [n for n in WEIGHTS if n in rep]
    rep_shapes = [rep[n].shape for n in rep_names]
    buf = _pack([rep[n] for n in rep_names], F32)
    rows = buf.shape[0]
    buf = jnp.pad(buf, ((0, -rows % (8 * N_DEV)), (0, 0)))
    part = _sum_slots(_exchange(buf.reshape(N_DEV, -1, PACK_W), False, "scatter_rep"), "sum_rep")
    full = _exchange(part, True, "gather_rep").reshape(-1, PACK_W)
    g_rep = dict(zip(rep_names, _unpack(full, rep_shapes)))

    grad = {}
    for n in WEIGHTS:
        if n in BIG:
            grad[n] = g_big[n]
        elif n == 'ada_w':
            grad[n] = g_ada_w
        elif n == 'ada_b':
            grad[n] = g_ada_b
        elif n in CHAN_SHARDED:
            grad[n] = lax.dynamic_slice_in_dim(g_rep[n], me * w[n].shape[-1], w[n].shape[-1], axis=g_rep[n].ndim - 1)
        else:
            grad[n] = g_rep[n].reshape(w[n].shape)

    shapes = [w[n].shape for n in WEIGHTS]
    packed = [_pack([src[n] for n in WEIGHTS], F32) for src in (w, grad, mom, vel)]
    delta, new_m, new_v = [_unpack(o, shapes) for o in _adamw(*packed, name="adamw")]
    return (loss, grad_x, *[grad[n] for n in WEIGHTS], *delta, *new_m, *new_v)
```

```python
import functools
import math

import jax
import jax.numpy as jnp
import numpy as np
from jax import lax
from jax.experimental import pallas as pl
from jax.experimental.pallas import tpu as pltpu

F32 = jnp.float32
MXU_DT = jnp.bfloat16
ACT_DT = jnp.bfloat16
WIRE_DT = jnp.bfloat16

N_DEV = 8
GRID_W = 64
RG_C = 8.0
LN_EPS = 1e-5
S5_T = 16
S5_GB = 8
ROW_TILE = 256
VMEM_LIMIT = 56 * 1024 * 1024
PACK_W = 1024

ADAM_LR = 0.001
ADAM_B1 = 0.9
ADAM_B2 = 0.999
ADAM_EPS = 1e-08
ADAM_WD = 0.01
ADAM_STEP = 10

WEIGHTS = ['c_ctx', 'ada_w', 'ada_b', 'ln1_g', 'ln1_b', 'w_in', 'conv_w', 'conv_b', 'rg_lambda', 'rg_wa', 'rg_ba',
           'rg_wi', 'rg_bi', 's5_a_re', 's5_a_im', 's5_log_dt', 's5_b_re', 's5_b_im', 's5_c_re', 's5_c_im', 's5_d',
           's5_glu_w', 's5_glu_b', 'w_out', 'b_out', 'ln2_g', 'ln2_b', 'mlp_w1', 'mlp_b1', 'mlp_w2', 'mlp_b2']
COL_SHARDED = ('w_in', 'mlp_w1')
BIG = ('w_in', 's5_glu_w', 'w_out', 'mlp_w1', 'mlp_w2')
CHAN_SHARDED = ('conv_w', 'rg_lambda', 'rg_ba', 'rg_bi')
S5_NAMES = ('s5_a_re', 's5_a_im', 's5_log_dt', 's5_b_re', 's5_b_im', 's5_c_re', 's5_c_im', 's5_d')
MESH = pl.DeviceIdType.MESH


def _tile(n, pref, align):
    t = (min(pref, n) // align) * align
    while t >= align:
        if n % t == 0:
            return t
        t -= align
    return n


def _params(sem):
    return pltpu.CompilerParams(dimension_semantics=sem, vmem_limit_bytes=VMEM_LIMIT)


def _sigmoid(v):
    return 1.0 / (1.0 + jnp.exp(-v))


def _silu(v):
    return v * _sigmoid(v)


_GELU_K = math.sqrt(2.0 / math.pi)


def _gelu(v):
    return 0.5 * v * (1.0 + jnp.tanh(_GELU_K * (v + 0.044715 * v * v * v)))


def _gelu_grad(v):
    th = jnp.tanh(_GELU_K * (v + 0.044715 * v * v * v))
    return 0.5 * (1.0 + th) + 0.5 * v * (1.0 - th * th) * _GELU_K * (1.0 + 3.0 * 0.044715 * v * v)


def _expm1(v):
    series = v * (1.0 + v * (0.5 + v * (1.0 / 6.0 + v * (1.0 / 24.0 + v * (1.0 / 120.0)))))
    return jnp.where(jnp.abs(v) < 0.1, series, jnp.exp(v) - 1.0)


def _softplus(v):
    return jnp.maximum(v, 0.0) + jnp.log(1.0 + jnp.exp(-jnp.abs(v)))


def _dot(a, b):
    return jnp.dot(a.astype(MXU_DT), b.astype(MXU_DT), preferred_element_type=F32)


def _dot_tn(a, b):
    return lax.dot_general(a.astype(MXU_DT), b.astype(MXU_DT), (((0,), (0,)), ((), ())), preferred_element_type=F32)


def _dot_nt(a, b):
    return lax.dot_general(a.astype(MXU_DT), b.astype(MXU_DT), (((1,), (1,)), ((), ())), preferred_element_type=F32)


def _mm(a, b, *, ta=False, tb=False, b_blocked=False, out_blocks=0, bias=None, a_fn=None, epi=None, extras=(),
        out_dtype=F32, name, tm=1088, tn=1024, tk=1024):
    if ta:
        kdim, m = a.shape
    else:
        m, kdim = a.shape
    bcol = b.shape[2] if b_blocked else None
    blog = (b.shape[1], b.shape[0] * b.shape[2]) if b_blocked else b.shape
    if tb:
        n, kb = blog
    else:
        kb, n = blog
    assert kdim == kb, (a.shape, b.shape, ta, tb)
    assert not (ta and tb)
    tm = _tile(m, tm, 128 if ta else 16)
    n_lim = bcol if (b_blocked and not tb) else (n // out_blocks if out_blocks else n)
    tn = _tile(n_lim, tn, 128)
    tk = _tile(bcol if (b_blocked and tb) else kdim, tk, 16 if ta else 128)
    nk = kdim // tk
    a_spec = pl.BlockSpec((tk, tm), lambda i, j, k: (k, i)) if ta else pl.BlockSpec((tm, tk), lambda i, j, k: (i, k))
    if not b_blocked:
        b_spec = pl.BlockSpec((tn, tk), lambda i, j, k: (j, k)) if tb else pl.BlockSpec((tk, tn), lambda i, j, k: (k, j))
    elif tb:
        qk = bcol // tk
        b_spec = pl.BlockSpec((None, tn, tk), lambda i, j, k: (k // qk, j, k % qk))
    else:
        qn = bcol // tn
        b_spec = pl.BlockSpec((None, tk, tn), lambda i, j, k: (j // qn, k, j % qn))
    in_specs = [a_spec, b_spec]
    args = [a, b]
    has_bias = bias is not None
    if has_bias:
        in_specs.append(pl.BlockSpec((1, tn), lambda i, j, k: (0, j)))
        args.append(bias.reshape(1, n).astype(F32))
    for e in extras:
        assert e.shape == (m, n), (e.shape, m, n)
        in_specs.append(pl.BlockSpec((tm, tn), lambda i, j, k: (i, j)))
        args.append(e)
    nex = len(extras)
    dn = (((0 if ta else 1,), (1 if tb else 0,)), ((), ()))
    if out_blocks:
        qo = (n // out_blocks) // tn
        out_shape = jax.ShapeDtypeStruct((out_blocks, m, n // out_blocks), out_dtype)
        out_spec = pl.BlockSpec((None, tm, tn), lambda i, j, k: (j // qo, i, j % qo))
    else:
        out_shape = jax.ShapeDtypeStruct((m, n), out_dtype)
        out_spec = pl.BlockSpec((tm, tn), lambda i, j, k: (i, j))

    def body(*refs):
        a_ref, b_ref = refs[0], refs[1]
        pos = 2
        bias_ref = refs[pos] if has_bias else None
        pos += int(has_bias)
        ex_refs = refs[pos:pos + nex]
        o_ref = refs[pos + nex]
        av = a_ref[...]
        if a_fn is not None:
            av = a_fn(av.astype(F32))
        part = lax.dot_general(av.astype(MXU_DT), b_ref[...].astype(MXU_DT), dn, preferred_element_type=F32)

        def finish(r):
            if has_bias:
                r = r + bias_ref[...]
            if epi is not None:
                r = epi(r, *[e[...] for e in ex_refs])
            o_ref[...] = r.astype(out_dtype)

        if nk == 1:
            finish(part)
            return
        acc_ref = refs[pos + nex + 1]
        k = pl.program_id(2)

        @pl.when(k == 0)
        def _():
            acc_ref[...] = part

        @pl.when(k > 0)
        def _():
            acc_ref[...] += part

        @pl.when(k == nk - 1)
        def _():
            finish(acc_ref[...])

    return pl.pallas_call(
        body, out_shape=out_shape, grid=(m // tm, n // tn, nk), in_specs=in_specs, out_specs=out_spec,
        scratch_shapes=[pltpu.VMEM((tm, tn), F32)] if nk > 1 else [],
        compiler_params=_params(("parallel", "parallel", "arbitrary")), name=name)(*args)


def _ew(fn, tiles, rows=(), mods=(), outs=(), accs=(), *, name, nctx_tiles=0, tr=None):
    bsz, tlen = tiles[0][0].shape[0], tiles[0][0].shape[1]
    if tr is None:
        tr = _tile(tlen, ROW_TILE, 8)
    nt = tlen // tr
    in_specs, args = [], []
    for arr, cb, width, toff in tiles:
        in_specs.append(pl.BlockSpec((1, tr, width), functools.partial(
            lambda b, t, cb, toff: (b, jnp.maximum(t - toff, 0), cb), cb=cb, toff=toff)))
        args.append(arr)
    for r in rows:
        in_specs.append(pl.BlockSpec(r.shape, lambda b, t: (0, 0)))
        args.append(r)

    def seg_of(t):
        return jnp.where(t >= nctx_tiles, 1, 0)

    for mo in mods:
        in_specs.append(pl.BlockSpec((1, 1) + mo.shape[2:], lambda b, t: (b, seg_of(t), 0, 0)))
        args.append(mo)
    out_shape, out_specs = [], []
    for width, dt in outs:
        out_shape.append(jax.ShapeDtypeStruct((bsz, tlen, width), dt))
        out_specs.append(pl.BlockSpec((1, tr, width), lambda b, t: (b, t, 0)))
    for kk, cc in accs:
        out_shape.append(jax.ShapeDtypeStruct((bsz, 2, kk, cc), F32))
        out_specs.append(pl.BlockSpec((1, 1, kk, cc), lambda b, t: (b, seg_of(t), 0, 0)))
    nti, nr, nm, no, na = len(tiles), len(rows), len(mods), len(outs), len(accs)

    def body(*refs):
        t = pl.program_id(1)
        tv = [r[0] for r in refs[:nti]]
        rv = [r[...] for r in refs[nti:nti + nr]]
        mv = [r[0, 0] for r in refs[nti + nr:nti + nr + nm]]
        o_refs = refs[nti + nr + nm:nti + nr + nm + no]
        a_refs = refs[nti + nr + nm + no:]
        seg = seg_of(t)
        ov, av = fn(tv, rv, mv, seg)
        for r, v in zip(o_refs, ov):
            r[0] = v.astype(r.dtype)
        if na:
            @pl.when((t == 0) | (t == nctx_tiles))
            def _():
                for r in a_refs:
                    r[...] = jnp.zeros_like(r)

            for r, v in zip(a_refs, av):
                r[0, 0] += v

    res = pl.pallas_call(
        body, out_shape=tuple(out_shape), grid=(bsz, nt), in_specs=in_specs, out_specs=tuple(out_specs),
        compiler_params=_params(("arbitrary", "arbitrary")), name=name)(*args)
    return res[:no], res[no:]


def _full(arr):
    return (arr, 0, arr.shape[-1], 0)


def _colsum(v):
    return jnp.sum(v, axis=0, keepdims=True)


def _shifted(x, prev8, next8, first, last, k):
    tr = x.shape[0]
    rid = lax.broadcasted_iota(jnp.int32, x.shape, 0)
    keep_prev = jnp.where(first, 0.0, 1.0)
    keep_next = jnp.where(last, 0.0, 1.0)
    if k == -1:
        return jnp.where(rid == 0, prev8[7:8] * keep_prev, pltpu.roll(x, 1, 0))
    if k == -2:
        r = pltpu.roll(x, 2, 0)
        r = jnp.where(rid == 1, prev8[7:8] * keep_prev, r)
        return jnp.where(rid == 0, prev8[6:7] * keep_prev, r)
    if k == 1:
        return jnp.where(rid == tr - 1, next8[0:1] * keep_next, pltpu.roll(x, tr - 1, 0))
    if k == 2:
        r = pltpu.roll(x, tr - 2, 0)
        r = jnp.where(rid == tr - 2, next8[0:1] * keep_next, r)
        return jnp.where(rid == tr - 1, next8[1:2] * keep_next, r)
    raise ValueError(k)


def _halo_specs(tr, width, cb, n8):
    cur = pl.BlockSpec((1, tr, width), lambda b, t: (b, t, cb))
    prev = pl.BlockSpec((1, 8, width), lambda b, t: (b, jnp.maximum(t * (tr // 8) - 1, 0), cb))
    nxt = pl.BlockSpec((1, 8, width), lambda b, t: (b, jnp.minimum((t + 1) * (tr // 8), n8 - 1), cb))
    return [cur, prev, nxt]


def _conv_fwd(z3, conv_w, conv_b, d, nctx_tiles, tr, name):
    bsz, tlen, _ = z3.shape
    nt = tlen // tr

    def body(x_ref, xp_ref, xn_ref, w_ref, b_ref, o_ref):
        t = pl.program_id(1)
        first = (t == 0) | (t == nctx_tiles)
        last = (t == nctx_tiles - 1) | (t == nt - 1)
        x, p8, n8 = x_ref[0], xp_ref[0], xn_ref[0]
        w = w_ref[...]
        o_ref[0] = (b_ref[...] + w[0:1] * _shifted(x, p8, n8, first, last, -1) + w[1:2] * x
                    + w[2:3] * _shifted(x, p8, n8, first, last, 1) + w[3:4] * _shifted(x, p8, n8, first, last, 2))

    return pl.pallas_call(
        body, out_shape=jax.ShapeDtypeStruct((bsz, tlen, d), F32), grid=(bsz, nt),
        in_specs=_halo_specs(tr, d, 0, tlen // 8) + [pl.BlockSpec((4, d), lambda b, t: (0, 0)),
                                                      pl.BlockSpec((1, d), lambda b, t: (0, 0))],
        out_specs=pl.BlockSpec((1, tr, d), lambda b, t: (b, t, 0)),
        compiler_params=_params(("parallel", "parallel")), name=name)(z3, z3, z3, conv_w, conv_b.reshape(1, d))


def _conv_bwd(dxc, z3, conv_w, d, nctx_tiles, tr, name):
    bsz, tlen, _ = dxc.shape
    nt = tlen // tr

    def body(g_ref, gp_ref, gn_ref, x_ref, xp_ref, xn_ref, w_ref, o_ref, acc_ref):
        t = pl.program_id(1)
        first = (t == 0) | (t == nctx_tiles)
        last = (t == nctx_tiles - 1) | (t == nt - 1)
        g, gp, gn = g_ref[0], gp_ref[0], gn_ref[0]
        x, xp, xn = x_ref[0], xp_ref[0], xn_ref[0]
        w = w_ref[...]
        o_ref[0] = (w[0:1] * _shifted(g, gp, gn, first, last, 1) + w[1:2] * g
                    + w[2:3] * _shifted(g, gp, gn, first, last, -1) + w[3:4] * _shifted(g, gp, gn, first, last, -2))

        @pl.when(t == 0)
        def _():
            acc_ref[...] = jnp.zeros_like(acc_ref)

        acc_ref[0, 0:1] += _colsum(g * _shifted(x, xp, xn, first, last, -1))
        acc_ref[0, 1:2] += _colsum(g * x)
        acc_ref[0, 2:3] += _colsum(g * _shifted(x, xp, xn, first, last, 1))
        acc_ref[0, 3:4] += _colsum(g * _shifted(x, xp, xn, first, last, 2))
        acc_ref[0, 4:5] += _colsum(g)

    return pl.pallas_call(
        body, out_shape=(jax.ShapeDtypeStruct((bsz, tlen, d), F32), jax.ShapeDtypeStruct((bsz, 8, d), F32)),
        grid=(bsz, nt),
        in_specs=_halo_specs(tr, d, 0, tlen // 8) + _halo_specs(tr, d, 0, tlen // 8)
        + [pl.BlockSpec((4, d), lambda b, t: (0, 0))],
        out_specs=(pl.BlockSpec((1, tr, d), lambda b, t: (b, t, 0)), pl.BlockSpec((1, 8, d), lambda b, t: (b, 0, 0))),
        compiler_params=_params(("arbitrary", "arbitrary")), name=name)(dxc, dxc, dxc, z3, z3, z3, conv_w)


def _rg_gates(x, lam, wa_ref, ba, wi_ref, bi, nh, hd):
    sp = _softplus(-lam)
    prs, pis = [], []
    for h in range(nh):
        xh = x[:, h * hd:(h + 1) * hd]
        prs.append(_dot(xh, wa_ref[h]))
        pis.append(_dot(xh, wi_ref[h]))
    r = _sigmoid(jnp.concatenate(prs, axis=1) + ba)
    i = _sigmoid(jnp.concatenate(pis, axis=1) + bi)
    la = -RG_C * sp * r
    a = jnp.exp(la)
    mult = jnp.sqrt(-_expm1(2.0 * la))
    return sp, r, i, a, mult


def _scan_tile(t, nctx_tiles, nt, reverse):
    if not reverse:
        return t
    return jnp.where(t < nctx_tiles, nctx_tiles - 1 - t, nt - 1 - (t - nctx_tiles))


def _unscan_tile(t, nctx_tiles, nt, reverse):
    if not reverse:
        return nt - 1 - t
    return jnp.where(t < nt - nctx_tiles, nctx_tiles + t, t - (nt - nctx_tiles))


def _rg_param_specs(d, nh, hd):
    vec = pl.BlockSpec((1, d), lambda b, t: (0, 0))
    mat = pl.BlockSpec((nh, hd, hd), lambda b, t: (0, 0, 0))
    return [vec, mat, vec, mat, vec]


def _rg_fwd(xc, lam, wa, ba, wi, bi, reverse, nctx_tiles, tr, name):
    bsz, tlen, d = xc.shape
    nh, hd = wa.shape[0], wa.shape[1]
    nt = tlen // tr
    tmap = lambda b, t: (b, _scan_tile(t, nctx_tiles, nt, reverse), 0)

    def body(x_ref, lam_ref, wa_ref, ba_ref, wi_ref, bi_ref, h_ref, a_scr, b_scr, carry):
        @pl.when(pl.program_id(1) == 0)
        def _():
            carry[...] = jnp.zeros_like(carry)

        x = x_ref[0]
        _, _, i, a, mult = _rg_gates(x, lam_ref[...], wa_ref, ba_ref[...], wi_ref, bi_ref[...], nh, hd)
        a_scr[...] = a
        b_scr[...] = mult * (i * x)

        def blk(j, h):
            for r in range(8):
                row = (tr - 1 - (j * 8 + r)) if reverse else (j * 8 + r)
                h = a_scr[pl.ds(row, 1), :] * h + b_scr[pl.ds(row, 1), :]
                h_ref[0, pl.ds(row, 1), :] = h
            return h

        carry[0:1, :] = lax.fori_loop(0, tr // 8, blk, carry[0:1, :])

    return pl.pallas_call(
        body, out_shape=jax.ShapeDtypeStruct((bsz, tlen, d), F32), grid=(bsz, nt),
        in_specs=[pl.BlockSpec((1, tr, d), tmap)] + _rg_param_specs(d, nh, hd),
        out_specs=pl.BlockSpec((1, tr, d), tmap),
        scratch_shapes=[pltpu.VMEM((tr, d), F32), pltpu.VMEM((tr, d), F32), pltpu.VMEM((8, d), F32)],
        compiler_params=_params(("arbitrary", "arbitrary")), name=name)(
            xc, lam.reshape(1, d), wa, ba.reshape(1, d), wi, bi.reshape(1, d))


def _rg_bwd(xc, h, dmix, z3, lam, wa, ba, wi, bi, addend, reverse, nctx_tiles, tr, name):
    bsz, tlen, d = xc.shape
    nh, hd = wa.shape[0], wa.shape[1]
    nt = tlen // tr
    r8 = tr // 8

    def tile_of(t):
        return _unscan_tile(t, nctx_tiles, nt, reverse)

    tmap = lambda b, t: (b, tile_of(t), 0)
    gmap = lambda b, t: (b, tile_of(t), 1)

    def halo_map(b, t):
        tt = tile_of(t)
        if not reverse:
            return (b, jnp.maximum(tt * r8 - 1, 0), 0)
        return (b, jnp.where(tt == nt - 1, 0, (tt + 1) * r8), 0)

    has_add = addend is not None

    def body(*refs):
        x_ref, h_ref, halo_ref, dr_ref, z_ref, lam_ref, wa_ref, ba_ref, wi_ref, bi_ref = refs[:10]
        pos = 10
        add_ref = refs[pos] if has_add else None
        pos += int(has_add)
        dx_ref, dwa_ref, dwi_ref, dv_ref, a_scr, g_scr, dh_scr, carry = refs[pos:pos + 8]
        b = pl.program_id(0)
        t = pl.program_id(1)
        tt = tile_of(t)

        @pl.when((b == 0) & (t == 0))
        def _():
            dwa_ref[...] = jnp.zeros_like(dwa_ref)
            dwi_ref[...] = jnp.zeros_like(dwi_ref)
            dv_ref[...] = jnp.zeros_like(dv_ref)

        @pl.when(t == 0)
        def _():
            carry[...] = jnp.zeros_like(carry)

        x = x_ref[0]
        sp, r, i, a, mult = _rg_gates(x, lam_ref[...], wa_ref, ba_ref[...], wi_ref, bi_ref[...], nh, hd)
        a_scr[...] = a
        dh_scr[...] = dr_ref[0] * _gelu(z_ref[0])

        def blk(j, cc):
            for rr in range(8):
                row = (j * 8 + rr) if reverse else (tr - 1 - (j * 8 + rr))
                g = dh_scr[pl.ds(row, 1), :] + cc
                g_scr[pl.ds(row, 1), :] = g
                cc = a_scr[pl.ds(row, 1), :] * g
            return cc

        carry[0:1, :] = lax.fori_loop(0, r8, blk, carry[0:1, :])
        g = g_scr[...]
        hv = h_ref[0]
        rid = lax.broadcasted_iota(jnp.int32, hv.shape, 0)
        if not reverse:
            valid = jnp.where(tt > 0, 1.0, 0.0)
            hprev = jnp.where(rid == 0, halo_ref[0][7:8] * valid, pltpu.roll(hv, 1, 0))
        else:
            valid = jnp.where(tt == nctx_tiles - 1, 0.0, 1.0)
            hprev = jnp.where(rid == tr - 1, halo_ref[0][0:1] * valid, pltpu.roll(hv, tr - 1, 0))
        dla = g * hprev * a - g * (i * x) * (a * a) / mult
        dpr = dla * (-RG_C * sp) * r * (1.0 - r)
        dpi = g * mult * x * i * (1.0 - i)
        dx = g * mult * i
        dxs = []
        for hh in range(nh):
            sl = slice(hh * hd, (hh + 1) * hd)
            dxs.append(_dot_nt(dpr[:, sl], wa_ref[hh]) + _dot_nt(dpi[:, sl], wi_ref[hh]))
            dwa_ref[hh] += _dot_tn(x[:, sl], dpr[:, sl])
            dwi_ref[hh] += _dot_tn(x[:, sl], dpi[:, sl])
        dx = dx + jnp.concatenate(dxs, axis=1)
        if has_add:
            dx = dx + add_ref[0]
        dx_ref[0] = dx
        dv_ref[0:1] += _colsum(dla * (-RG_C * r))
        dv_ref[1:2] += _colsum(dpr)
        dv_ref[2:3] += _colsum(dpi)

    in_specs = [pl.BlockSpec((1, tr, d), tmap), pl.BlockSpec((1, tr, d), tmap), pl.BlockSpec((1, 8, d), halo_map),
                pl.BlockSpec((1, tr, d), tmap), pl.BlockSpec((1, tr, d), gmap)] + _rg_param_specs(d, nh, hd)
    args = [xc, h, h, dmix, z3, lam.reshape(1, d), wa, ba.reshape(1, d), wi, bi.reshape(1, d)]
    if has_add:
        in_specs.append(pl.BlockSpec((1, tr, d), tmap))
        args.append(addend)
    mat = pl.BlockSpec((nh, hd, hd), lambda b, t: (0, 0, 0))
    return pl.pallas_call(
        body, out_shape=(jax.ShapeDtypeStruct((bsz, tlen, d), F32), jax.ShapeDtypeStruct((nh, hd, hd), F32),
                         jax.ShapeDtypeStruct((nh, hd, hd), F32), jax.ShapeDtypeStruct((8, d), F32)),
        grid=(bsz, nt), in_specs=in_specs,
        out_specs=(pl.BlockSpec((1, tr, d), tmap), mat, mat, pl.BlockSpec((8, d), lambda b, t: (0, 0))),
        scratch_shapes=[pltpu.VMEM((tr, d), F32), pltpu.VMEM((tr, d), F32), pltpu.VMEM((tr, d), F32),
                        pltpu.VMEM((8, d), F32)],
        compiler_params=_params(("arbitrary", "arbitrary")), name=name)(*args)


def _s5_operators(a_re, a_im, log_dt, b_re, b_im, c_re, c_im, d_skip):
    hi = lax.Precision.HIGHEST
    tt = S5_T
    lam = lax.complex(a_re, a_im)
    lamdt = lam * jnp.exp(log_dt)[..., None]
    lam_bar = jnp.exp(lamdt)
    b_bar = ((lam_bar - 1.0) / lam)[..., None] * lax.complex(b_re, b_im)
    cm = lax.complex(c_re, c_im)
    taus = jnp.arange(tt + 1, dtype=F32)
    pw = jnp.exp(lamdt[:, :, None, :] * taus[None, None, :, None])
    kern = jnp.real(jnp.einsum('dgop,dgtp,dgpi->dgtoi', cm, pw[:, :, :tt], b_bar, precision=hi))
    g, h = d_skip.shape
    s_idx = np.arange(tt)[None, :, None]
    t_idx = np.arange(tt)[None, None, :]
    lag = np.arange(tt)[:, None, None]
    sel = jnp.asarray(np.stack([t_idx - s_idx == lag, s_idx - t_idx == lag]).astype(np.float32))
    tz = jnp.einsum('dxst,dgxoi->gsito', sel, kern, precision=hi).reshape(g, tt * h, tt * h)
    tz = tz + jnp.eye(tt * h, dtype=F32)[None] * jnp.tile(d_skip, (1, tt))[:, None, :]

    def reim(zc, axis):
        return jnp.concatenate([jnp.real(zc), jnp.imag(zc)], axis=axis)

    bt0 = b_bar[0].transpose(0, 2, 1)[:, None, :, :]
    bt1 = b_bar[1].transpose(0, 2, 1)[:, None, :, :]
    wf = reim(pw[0][:, :tt][:, ::-1][:, :, None, :] * bt0, -1).reshape(g, tt * h, -1)
    wb = reim(pw[1][:, :tt][:, :, None, :] * bt1, -1).reshape(g, tt * h, -1)
    ct0 = cm[0].transpose(0, 2, 1)[:, :, None, :]
    ct1 = cm[1].transpose(0, 2, 1)[:, :, None, :]
    clf = ct0 * pw[0][:, 1:tt + 1].transpose(0, 2, 1)[:, :, :, None]
    clb = ct1 * pw[1][:, 1:tt + 1][:, ::-1].transpose(0, 2, 1)[:, :, :, None]
    vf = jnp.concatenate([jnp.real(clf), -jnp.imag(clf)], axis=1).reshape(g, -1, tt * h)
    vb = jnp.concatenate([jnp.real(clb), -jnp.imag(clb)], axis=1).reshape(g, -1, tt * h)
    lt = pw[:, :, tt]
    rows = []
    for dd in range(2):
        re, im = jnp.real(lt[dd]), jnp.imag(lt[dd])
        rows += [jnp.concatenate([re, re], -1), jnp.concatenate([-im, im], -1)]
    return tz, wf, wb, vf, vb, jnp.stack(rows)


def _chunk_order(j, ncc, nc):
    return jnp.where(j < ncc, ncc - 1 - j, nc - 1 - (j - ncc))


def _s5_specs(bsz, nc, ops, lm):
    gb = S5_GB
    act = lambda width: pl.BlockSpec((1, gb * nc, width), lambda i, b: (b, i, 0))
    opspecs = [pl.BlockSpec((gb,) + o.shape[1:], lambda i, b: (i, 0, 0)) for o in ops]
    lspec = pl.BlockSpec((4, gb, lm.shape[-1]), lambda i, b: (0, i, 0))
    return act, opspecs, lspec


def _s5_fwd(u, ops, lm, ncc, nc, name):
    bsz, gn, th = u.shape
    g = gn // nc
    gb = S5_GB
    p2 = ops[1].shape[-1]
    ph = p2 // 2

    def body(u_ref, tz_ref, wf_ref, wb_ref, vf_ref, vb_ref, l_ref, y_ref, hf_ref, hb_ref, sf, sfs, sb, sbs):
        a1f, a2f, a1b, a2b = l_ref[0], l_ref[1], l_ref[2], l_ref[3]
        for gi in range(gb):
            ug = u_ref[0, pl.ds(gi * nc, nc), :]
            s1 = _dot(ug, wf_ref[gi])
            s2 = _dot(ug, wb_ref[gi])
            sf[pl.ds(gi, nc, stride=gb), :] = s1
            sfs[pl.ds(gi, nc, stride=gb), :] = pltpu.roll(s1, ph, 1)
            sb[pl.ds(gi, nc, stride=gb), :] = s2
            sbs[pl.ds(gi, nc, stride=gb), :] = pltpu.roll(s2, ph, 1)

        def step(j, hs):
            hf, hfs, hb, hbs = hs
            rf = pl.ds(pl.multiple_of(j * gb, gb), gb)
            rb = pl.ds(pl.multiple_of(_chunk_order(j, ncc, nc) * gb, gb), gb)
            s1, s1s, s2, s2s = sf[rf, :], sfs[rf, :], sb[rb, :], sbs[rb, :]
            sf[rf, :] = hf
            sb[rb, :] = hb
            return (a1f * hf + a2f * hfs + s1, a1f * hfs - a2f * hf + s1s,
                    a1b * hb + a2b * hbs + s2, a1b * hbs - a2b * hb + s2s)

        zero = jnp.zeros((gb, p2), F32)
        lax.fori_loop(0, nc, step, (zero, zero, zero, zero))
        for gi in range(gb):
            rows = pl.ds(gi * nc, nc)
            hfg = sf[pl.ds(gi, nc, stride=gb), :]
            hbg = sb[pl.ds(gi, nc, stride=gb), :]
            hf_ref[0, rows, :] = hfg
            hb_ref[0, rows, :] = hbg
            y_ref[0, rows, :] = (_dot(u_ref[0, rows, :], tz_ref[gi]) + _dot(hfg, vf_ref[gi]) + _dot(hbg, vb_ref[gi]))

    act, opspecs, lspec = _s5_specs(bsz, nc, ops, lm)
    return pl.pallas_call(
        body, out_shape=(jax.ShapeDtypeStruct((bsz, gn, th), F32), jax.ShapeDtypeStruct((bsz, gn, p2), F32),
                         jax.ShapeDtypeStruct((bsz, gn, p2), F32)),
        grid=(g // gb, bsz), in_specs=[act(th)] + opspecs + [lspec], out_specs=(act(th), act(p2), act(p2)),
        scratch_shapes=[pltpu.VMEM((gb * nc, p2), F32) for _ in range(4)],
        compiler_params=_params(("parallel", "arbitrary")), name=name)(u, *ops, lm)


def _s5_bwd(dy, u, hf, hb, ops_t, lm, ncc, nc, name):
    bsz, gn, th = u.shape
    g = gn // nc
    gb = S5_GB
    p2 = lm.shape[-1]
    ph = p2 // 2

    def body(dy_ref, u_ref, hf_ref, hb_ref, tzt_ref, wft_ref, wbt_ref, vft_ref, vbt_ref, l_ref,
             du_ref, dtz_ref, dwf_ref, dwb_ref, dvf_ref, dvb_ref, dl_ref, jf, jfs, jb, jbs, hfk, hbk):
        a1f, a2f, a1b, a2b = l_ref[0], l_ref[1], l_ref[2], l_ref[3]

        @pl.when(pl.program_id(1) == 0)
        def _():
            for r in (dtz_ref, dwf_ref, dwb_ref, dvf_ref, dvb_ref, dl_ref):
                r[...] = jnp.zeros_like(r)

        for gi in range(gb):
            rows = pl.ds(gi * nc, nc)
            dyg = dy_ref[0, rows, :]
            i1 = _dot(dyg, vft_ref[gi])
            i2 = _dot(dyg, vbt_ref[gi])
            jf[pl.ds(gi, nc, stride=gb), :] = i1
            jfs[pl.ds(gi, nc, stride=gb), :] = pltpu.roll(i1, ph, 1)
            jb[pl.ds(gi, nc, stride=gb), :] = i2
            jbs[pl.ds(gi, nc, stride=gb), :] = pltpu.roll(i2, ph, 1)
            hfk[pl.ds(gi, nc, stride=gb), :] = hf_ref[0, rows, :]
            hbk[pl.ds(gi, nc, stride=gb), :] = hb_ref[0, rows, :]

        def step(j, carry):
            qf, qfs, qb, qbs, d1f, d2f, d1b, d2b = carry
            rf = pl.ds(pl.multiple_of((nc - 1 - j) * gb, gb), gb)
            rb = pl.ds(pl.multiple_of(_chunk_order(nc - 1 - j, ncc, nc) * gb, gb), gb)
            i1, i1s, i2, i2s = jf[rf, :], jfs[rf, :], jb[rb, :], jbs[rb, :]
            h1, h2 = hfk[rf, :], hbk[rb, :]
            jf[rf, :] = qf
            jb[rb, :] = qb
            return (i1 + a1f * qf - a2f * qfs, i1s + a1f * qfs + a2f * qf,
                    i2 + a1b * qb - a2b * qbs, i2s + a1b * qbs + a2b * qb,
                    d1f + qf * h1, d2f + qfs * h1, d1b + qb * h2, d2b + qbs * h2)

        zero = jnp.zeros((gb, p2), F32)
        fin = lax.fori_loop(0, nc, step, (zero,) * 8)
        dl_ref[0] += fin[4]
        dl_ref[1] += pltpu.roll(fin[5], ph, 1)
        dl_ref[2] += fin[6]
        dl_ref[3] += pltpu.roll(fin[7], ph, 1)
        for gi in range(gb):
            rows = pl.ds(gi * nc, nc)
            dyg = dy_ref[0, rows, :]
            ug = u_ref[0, rows, :]
            dsf = jf[pl.ds(gi, nc, stride=gb), :]
            dsb = jb[pl.ds(gi, nc, stride=gb), :]
            du_ref[0, rows, :] = (_dot(dyg, tzt_ref[gi]) + _dot(dsf, wft_ref[gi]) + _dot(dsb, wbt_ref[gi])).astype(du_ref.dtype)
            dtz_ref[gi] += _dot_tn(ug, dyg)
            dwf_ref[gi] += _dot_tn(ug, dsf)
            dwb_ref[gi] += _dot_tn(ug, dsb)
            dvf_ref[gi] += _dot_tn(hf_ref[0, rows, :], dyg)
            dvb_ref[gi] += _dot_tn(hb_ref[0, rows, :], dyg)

    act, opspecs, lspec = _s5_specs(bsz, nc, ops_t, lm)
    gshape = lambda o: jax.ShapeDtypeStruct(o.shape[:1] + o.shape[1:][::-1], F32)
    gspec = lambda o: pl.BlockSpec((gb,) + o.shape[1:][::-1], lambda i, b: (i, 0, 0))
    res = pl.pallas_call(
        body, out_shape=tuple([jax.ShapeDtypeStruct((bsz, gn, th), ACT_DT)] + [gshape(o) for o in ops_t]
                              + [jax.ShapeDtypeStruct(lm.shape, F32)]),
        grid=(g // gb, bsz), in_specs=[act(th), act(th), act(p2), act(p2)] + opspecs + [lspec],
        out_specs=tuple([act(th)] + [gspec(o) for o in ops_t] + [lspec]),
        scratch_shapes=[pltpu.VMEM((gb * nc, p2), F32) for _ in range(6)],
        compiler_params=_params(("parallel", "arbitrary")), name=name)(dy, u, hf, hb, *ops_t, lm)
    return res[0], tuple(res[1:])


def _to_chunks(s, ctx_len, g):
    bsz, tlen, d = s.shape
    h = d // g
    seq = tlen - ctx_len
    rows = seq // GRID_W
    cpart = s[:, :ctx_len].reshape(bsz, ctx_len // S5_T, S5_T, g, h).transpose(0, 3, 1, 2, 4)
    lpart = s[:, ctx_len:].reshape(bsz, rows, GRID_W, g, h).transpose(0, 3, 2, 1, 4)
    cpart = cpart.reshape(bsz, g, ctx_len // S5_T, S5_T * h)
    lpart = lpart.reshape(bsz, g, seq // S5_T, S5_T * h)
    return jnp.concatenate([cpart, lpart], axis=2).reshape(bsz, g * (tlen // S5_T), S5_T * h)


def _from_chunks(y, ctx_len, g):
    bsz, gn, th = y.shape
    nc = gn // g
    h = th // S5_T
    ncc = ctx_len // S5_T
    seq = (nc - ncc) * S5_T
    rows = seq // GRID_W
    y = y.reshape(bsz, g, nc, th)
    cpart = y[:, :, :ncc].reshape(bsz, g, ncc, S5_T, h).transpose(0, 2, 3, 1, 4).reshape(bsz, ctx_len, g * h)
    lpart = y[:, :, ncc:].reshape(bsz, g, GRID_W, rows, h).transpose(0, 3, 2, 1, 4).reshape(bsz, seq, g * h)
    return jnp.concatenate([cpart, lpart], axis=1)


def _me():
    return lax.axis_index("x"), lax.axis_index("y"), lax.axis_index("c")


def _peer(k):
    x, y, c = _me()
    px = (1 - x) if (k & 4) else x
    py = (1 - y) if (k & 2) else y
    pc = (1 - c) if (k & 1) else c
    return (px, py, pc), 4 * px + 2 * py + pc


def _exchange(arrs, gather, name):
    n = len(arrs)
    npeer = N_DEV - 1

    def body(*refs):
        x_refs, o_refs = refs[:n], refs[n:2 * n]
        send_sems, recv_sems, local_sems = refs[2 * n:]
        xi, yi, ci = _me()
        me = 4 * xi + 2 * yi + ci
        mine = [x if gather else x.at[me] for x in x_refs]
        local = [pltpu.make_async_copy(mine[i], o_refs[i].at[me], local_sems.at[i]) for i in range(n)]
        for cp in local:
            cp.start()
        sends = []
        for k in range(1, N_DEV):
            dev, pid = _peer(k)
            for i in range(n):
                src = x_refs[i] if gather else x_refs[i].at[pid]
                cp = pltpu.make_async_remote_copy(
                    src_ref=src, dst_ref=o_refs[i].at[me], send_sem=send_sems.at[i * npeer + k - 1],
                    recv_sem=recv_sems.at[i * npeer + k - 1], device_id=dev, device_id_type=MESH)
                cp.start()
                sends.append(cp)
        for k in range(1, N_DEV):
            dev, pid = _peer(k)
            for i in range(n):
                pltpu.make_async_remote_copy(
                    src_ref=mine[i], dst_ref=o_refs[i].at[pid], send_sem=send_sems.at[i * npeer + k - 1],
                    recv_sem=recv_sems.at[i * npeer + k - 1], device_id=dev, device_id_type=MESH).wait_recv()
        for cp in sends:
            cp.wait_send()
        for cp in local:
            cp.wait()

    anyspec = pl.BlockSpec(memory_space=pl.ANY)
    return pl.pallas_call(
        body, out_shape=tuple(jax.ShapeDtypeStruct((N_DEV,) + a.shape[-2:], a.dtype) for a in arrs),
        in_specs=[anyspec] * n, out_specs=tuple([anyspec] * n),
        scratch_shapes=[pltpu.SemaphoreType.DMA((n * npeer,)), pltpu.SemaphoreType.DMA((n * npeer,)),
                        pltpu.SemaphoreType.DMA((n,))],
        name=name)(*arrs)


def _sum_slots(x3, name):
    _, r, cdim = x3.shape
    tr = _tile(r, 256, 8)

    def body(x_ref, o_ref):
        acc = x_ref[0]
        for s in range(1, N_DEV):
            acc = acc + x_ref[s]
        o_ref[...] = acc

    return pl.pallas_call(
        body, out_shape=jax.ShapeDtypeStruct((r, cdim), F32), grid=(r // tr,),
        in_specs=[pl.BlockSpec((N_DEV, tr, cdim), lambda i: (0, i, 0))], out_specs=pl.BlockSpec((tr, cdim), lambda i: (i, 0)),
        compiler_params=_params(("parallel",)), name=name)(x3)


def _pack(arrs, dtype, lead=0):
    flat = jnp.concatenate([a.reshape(a.shape[:lead] + (-1,)).astype(dtype) for a in arrs], axis=-1)
    n = flat.shape[-1]
    pad = -n % (PACK_W * 16)
    flat = jnp.pad(flat, [(0, 0)] * lead + [(0, pad)])
    return flat.reshape(flat.shape[:lead] + (-1, PACK_W))


def _unpack(buf, shapes, lead=0):
    flat = buf.reshape(buf.shape[:lead] + (-1,))
    out, off = [], 0
    for shp in shapes:
        n = math.prod(shp)
        out.append(flat[..., off:off + n].reshape(buf.shape[:lead] + tuple(shp)))
        off += n
    return out


def _adamw_math(wv, gv, m0, v0):
    m1 = ADAM_B1 * m0 + (1.0 - ADAM_B1) * gv
    v1 = ADAM_B2 * v0 + (1.0 - ADAM_B2) * (gv * gv)
    m_hat = m1 / (1.0 - ADAM_B1 ** ADAM_STEP)
    v_hat = v1 / (1.0 - ADAM_B2 ** ADAM_STEP)
    delta = -ADAM_LR * (m_hat / (jnp.sqrt(v_hat) + ADAM_EPS) + ADAM_WD * wv)
    return delta, m1, v1


def _adamw(w, g, m, v, name):
    def fn(tv, rv, mv, seg):
        return _adamw_math(*tv), ()

    outs, _ = _ew(fn, [_full(a[None]) for a in (w, g, m, v)], outs=[(w.shape[-1], F32)] * 3, name=name)
    return [o[0] for o in outs]


def _sum_adamw(slots, w, m, v, name):
    _, r, cdim = slots.shape
    tr = _tile(r, 128, 16)

    def body(s_ref, w_ref, m_ref, v_ref, g_ref, d_ref, mo_ref, vo_ref):
        gv = s_ref[0].astype(F32)
        for s in range(1, N_DEV):
            gv = gv + s_ref[s].astype(F32)
        g_ref[...] = gv
        d_ref[...], mo_ref[...], vo_ref[...] = _adamw_math(w_ref[...], gv, m_ref[...], v_ref[...])

    flat = pl.BlockSpec((tr, cdim), lambda i: (i, 0))
    return pl.pallas_call(
        body, out_shape=tuple(jax.ShapeDtypeStruct((r, cdim), F32) for _ in range(4)), grid=(r // tr,),
        in_specs=[pl.BlockSpec((N_DEV, tr, cdim), lambda i: (0, i, 0)), flat, flat, flat], out_specs=(flat,) * 4,
        compiler_params=_params(("parallel",)), name=name)(slots, w, m, v)


def _layer_fwd(l, xin, modt, wts, sm, cfg):
    bsz, tlen, d = xin.shape
    bt = bsz * tlen
    nct, tr, ctx_len, g = cfg['nct'], cfg['tr'], cfg['ctx_len'], cfg['g']
    ncc, nc = ctx_len // S5_T, tlen // S5_T
    alpha = cfg['alpha']
    nm = lambda s: f"l{l}_{s}"

    def modulate(xv, i_shift, i_scale, name):
        def fn(tv, rv, mv, seg):
            mo = mv[0]
            return (tv[0] * (1.0 + mo[i_scale:i_scale + 1]) + mo[i_shift:i_shift + 1],), ()
        return _ew(fn, [_full(xv)], mods=[modt], outs=[(d, ACT_DT)], nctx_tiles=nct, tr=tr, name=name)[0][0]

    def ln_fwd(xv, mv_, i_gate, gam, bet, name):
        def fn(tv, rv, mv, seg):
            z = alpha * tv[0] + mv[0][i_gate:i_gate + 1] * tv[1]
            mu = jnp.mean(z, axis=-1, keepdims=True)
            zc = z - mu
            var = jnp.mean(zc * zc, axis=-1, keepdims=True)
            return (zc * lax.rsqrt(var + LN_EPS) * rv[0] + rv[1],), ()
        return _ew(fn, [_full(xv), _full(mv_)], rows=[gam.reshape(1, d), bet.reshape(1, d)], mods=[modt],
                   outs=[(d, F32)], nctx_tiles=nct, tr=tr, name=name)[0][0]

    sv = {'x': xin}
    u = modulate(xin, 0, 1, nm("mod1"))
    sv['u'] = u
    z3 = _mm(u.reshape(bt, d), wts['w_in'][l], b_blocked=True, name=nm("w_in")).reshape(bsz, tlen, 3 * d)
    sv['z3'] = z3
    xc = _conv_fwd(z3, sm['conv_w'][l], sm['conv_b'][l], d, nct, tr, nm("conv"))
    sv['xc'] = xc
    hs = []
    for dd in range(2):
        hs.append(_rg_fwd(xc, sm['rg_lambda'][l, dd], sm['rg_wa'][l, dd], sm['rg_ba'][l, dd], sm['rg_wi'][l, dd],
                          sm['rg_bi'][l, dd], bool(dd), nct, tr, nm(f"rg_fwd{dd}")))
    sv['hf'], sv['hb'] = hs

    s5u = _to_chunks(z3[:, :, 2 * d:].astype(ACT_DT), ctx_len, g)
    sv['s5u'] = s5u
    y, hf5, hb5 = _s5_fwd(s5u, cfg['s5_ops'][l], cfg['s5_lm'][l], ncc, nc, nm("s5_fwd"))
    sv['hf5'], sv['hb5'] = hf5, hb5
    ytok = _from_chunks(y, ctx_len, g)
    sv['ytok'] = ytok

    def gelu_fn(tv, rv, mv, seg):
        return (_gelu(tv[0]),), ()
    gact = _ew(gelu_fn, [_full(ytok)], outs=[(d, ACT_DT)], tr=tr, name=nm("s5_gelu"))[0][0]
    sv['gact'] = gact
    gpre = _mm(gact.reshape(bt, d), wts['s5_glu_w'][l], bias=sm['s5_glu_b'][l], name=nm("glu")).reshape(bsz, tlen, d)
    sv['gpre'] = gpre

    def mix_fn(tv, rv, mv, seg):
        rg = (tv[0] + tv[1]) * _gelu(tv[2])
        s5o = tv[3].astype(F32) * _sigmoid(tv[4])
        return (jnp.concatenate([rg, s5o], axis=1),), ()
    mixin = _ew(mix_fn, [_full(hs[0]), _full(hs[1]), (z3, 1, d, 0), _full(gact), _full(gpre)], outs=[(2 * d, ACT_DT)],
                tr=tr, name=nm("mix"))[0][0].reshape(bt, 2 * d)
    sv['mixin'] = mixin
    mo = _mm(mixin, wts['w_out'][l], bias=sm['b_out'][l], name=nm("w_out")).reshape(bsz, tlen, d)
    sv['mo'] = mo
    x1 = ln_fwd(xin, mo, 2, sm['ln1_g'][l], sm['ln1_b'][l], nm("ln1"))
    sv['x1'] = x1
    u2 = modulate(x1, 3, 4, nm("mod2"))
    sv['u2'] = u2
    rl = _mm(u2.reshape(bt, d), wts['mlp_w1'][l], b_blocked=True, bias=sm['mlp_b1'][l],
             epi=lambda r: jnp.maximum(r, 0.0), out_dtype=ACT_DT, name=nm("mlp1"))
    sv['rl'] = rl
    fo = _mm(rl, wts['mlp_w2'][l], bias=sm['mlp_b2'][l], a_fn=lambda v: v * v, name=nm("mlp2")).reshape(bsz, tlen, d)
    sv['fo'] = fo
    x2 = ln_fwd(x1, fo, 5, sm['ln2_g'][l], sm['ln2_b'][l], nm("ln2"))
    return x2, sv


def _layer_bwd(l, dx2, sv, modt, wts, sm, cfg):
    bsz, tlen, d = dx2.shape
    bt = bsz * tlen
    nct, tr, ctx_len, g = cfg['nct'], cfg['tr'], cfg['ctx_len'], cfg['g']
    ncc, nc = ctx_len // S5_T, tlen // S5_T
    alpha = cfg['alpha']
    nm = lambda s: f"l{l}_{s}"
    gr = {}

    def ln_bwd(xv, mv_, i_gate, gam, dy, name):
        def fn(tv, rv, mv, seg):
            xx, mm_, dyy = tv
            gate = mv[0][i_gate:i_gate + 1]
            z = alpha * xx + gate * mm_
            mu = jnp.mean(z, axis=-1, keepdims=True)
            zc = z - mu
            var = jnp.mean(zc * zc, axis=-1, keepdims=True)
            rstd = lax.rsqrt(var + LN_EPS)
            xhat = zc * rstd
            dxh = dyy * rv[0]
            dz = rstd * (dxh - jnp.mean(dxh, axis=-1, keepdims=True) - xhat * jnp.mean(dxh * xhat, axis=-1, keepdims=True))
            dm = gate * dz
            acc = jnp.concatenate([_colsum(dz * mm_), _colsum(dyy * xhat), _colsum(dyy), _colsum(dm)], axis=0)
            return (alpha * dz, dm), (acc,)
        o, a = _ew(fn, [_full(xv), _full(mv_), _full(dy)], rows=[gam.reshape(1, d)], mods=[modt],
                   outs=[(d, F32), (d, ACT_DT)], accs=[(4, d)], nctx_tiles=nct, tr=tr, name=name)
        return o[0], o[1], a[0]

    def mod_bwd(du, xv, i_scale, addend, name):
        def fn(tv, rv, mv, seg):
            duu, xx, add = tv
            acc = jnp.concatenate([_colsum(duu), _colsum(duu * xx)], axis=0)
            return (add + duu * (1.0 + mv[0][i_scale:i_scale + 1]),), (acc,)
        o, a = _ew(fn, [_full(du), _full(xv), _full(addend)], mods=[modt], outs=[(d, F32)], accs=[(2, d)],
                   nctx_tiles=nct, tr=tr, name=name)
        return o[0], a[0]

    def row_blocks(gw):
        return gw.reshape(N_DEV, -1, gw.shape[-1])

    dx1a, dfo, acc2 = ln_bwd(sv['x1'], sv['fo'], 5, sm['ln2_g'][l], dx2, nm("ln2_bwd"))
    dfo2 = dfo.reshape(bt, d)
    dhp = _mm(dfo2, wts['mlp_w2'][l], tb=True, epi=lambda r, rl: r * (2.0 * rl.astype(F32)), extras=[sv['rl']],
              out_dtype=ACT_DT, name=nm("mlp2_dx"))
    gr['mlp_w2'] = row_blocks(_mm(sv['rl'], dfo2, ta=True, a_fn=lambda v: v * v, out_dtype=WIRE_DT, name=nm("mlp2_dw")))

    def colsum_fn(tv, rv, mv, seg):
        return (), (_colsum(tv[0].astype(F32)),)
    ff = dhp.shape[-1]
    gr['mlp_b1'] = _ew(colsum_fn, [_full(dhp.reshape(bsz, tlen, ff))], accs=[(1, ff)], tr=_tile(tlen, 64, 8),
                       name=nm("mlp_b1"))[1][0][:, 1, 0].sum(0)
    du2 = _mm(dhp, wts['mlp_w1'][l], tb=True, b_blocked=True, name=nm("mlp1_dx")).reshape(bsz, tlen, d)
    gr['mlp_w1'] = _mm(sv['u2'].reshape(bt, d), dhp, ta=True, out_blocks=N_DEV, out_dtype=WIRE_DT, name=nm("mlp1_dw"))
    dx1, accm2 = mod_bwd(du2, sv['x1'], 4, dx1a, nm("mod2_bwd"))
    gr['ln2_g'] = acc2[:, :, 1].sum((0, 1))
    gr['ln2_b'] = acc2[:, :, 2].sum((0, 1))
    gr['mlp_b2'] = acc2[:, :, 3].sum((0, 1))

    dxa, dmo, acc1 = ln_bwd(sv['x'], sv['mo'], 2, sm['ln1_g'][l], dx1, nm("ln1_bwd"))
    gr['ln1_g'] = acc1[:, :, 1].sum((0, 1))
    gr['ln1_b'] = acc1[:, :, 2].sum((0, 1))
    gr['b_out'] = acc1[:, :, 3].sum((0, 1))
    dmo2 = dmo.reshape(bt, d)
    dmix = _mm(dmo2, wts['w_out'][l], tb=True, name=nm("w_out_dx")).reshape(bsz, tlen, 2 * d)
    gr['w_out'] = row_blocks(_mm(sv['mixin'], dmo2, ta=True, out_dtype=WIRE_DT, name=nm("w_out_dw")))

    def glu_bwd(tv, rv, mv, seg):
        ds, ga, gp = tv
        ga = ga.astype(F32)
        sg = _sigmoid(gp)
        dg = ds * ga * sg * (1.0 - sg)
        return (dg, ds * sg), (_colsum(dg),)
    o, a = _ew(glu_bwd, [(dmix, 1, d, 0), _full(sv['gact']), _full(sv['gpre'])], outs=[(d, ACT_DT), (d, F32)],
               accs=[(1, d)], tr=tr, name=nm("glu_bwd"))
    dgp, t1 = o
    gr['s5_glu_b'] = a[0][:, 1, 0].sum(0)
    dgp2 = dgp.reshape(bt, d)
    dyt = _mm(dgp2, wts['s5_glu_w'][l], tb=True, epi=lambda r, t1v, yv: (r + t1v) * _gelu_grad(yv),
              extras=[t1.reshape(bt, d), sv['ytok'].reshape(bt, d)], out_dtype=ACT_DT, tn=512,
              name=nm("glu_dx")).reshape(bsz, tlen, d)
    gr['s5_glu_w'] = row_blocks(_mm(sv['gact'].reshape(bt, d), dgp2, ta=True, out_dtype=WIRE_DT, name=nm("glu_dw")))
    du5, dops = _s5_bwd(_to_chunks(dyt, ctx_len, g), sv['s5u'], sv['hf5'], sv['hb5'], cfg['s5_ops_t'][l],
                        cfg['s5_lm'][l], ncc, nc, nm("s5_bwd"))
    ds5u = _from_chunks(du5, ctx_len, g)
    for name, val in zip(S5_NAMES, cfg['s5_vjp'][l](dops)):
        gr[name] = val

    z3 = sv['z3']
    dxc = None
    for dd in range(2):
        dxc, dwa, dwi, dv = _rg_bwd(sv['xc'], sv['hf'] if dd == 0 else sv['hb'], dmix, z3, sm['rg_lambda'][l, dd],
                                    sm['rg_wa'][l, dd], sm['rg_ba'][l, dd], sm['rg_wi'][l, dd], sm['rg_bi'][l, dd],
                                    dxc, bool(dd), nct, tr, nm(f"rg_bwd{dd}"))
        gr[f'rg_wa{dd}'], gr[f'rg_wi{dd}'] = dwa, dwi
        gr[f'rg_lambda{dd}'] = dv[0] * (-_sigmoid(-sm['rg_lambda'][l, dd]))
        gr[f'rg_ba{dd}'], gr[f'rg_bi{dd}'] = dv[1], dv[2]
    drgx, accc = _conv_bwd(dxc, z3, sm['conv_w'][l], d, nct, tr, nm("conv_bwd"))
    gr['conv_w'] = accc[:, 0:4].sum(0)
    gr['conv_b'] = accc[:, 4].sum(0)

    def dz_fn(tv, rv, mv, seg):
        dgate = tv[0] * (tv[1] + tv[2]) * _gelu_grad(tv[3])
        return (jnp.concatenate([tv[4], dgate, tv[5].astype(F32)], axis=1),), ()
    dz = _ew(dz_fn, [(dmix, 0, d, 0), _full(sv['hf']), _full(sv['hb']), (z3, 1, d, 0), _full(drgx), _full(ds5u)],
             outs=[(3 * d, ACT_DT)], tr=tr, name=nm("dz"))[0][0].reshape(bt, 3 * d)
    du = _mm(dz, wts['w_in'][l], tb=True, b_blocked=True, name=nm("w_in_dx")).reshape(bsz, tlen, d)
    gr['w_in'] = _mm(sv['u'].reshape(bt, d), dz, ta=True, out_blocks=N_DEV, out_dtype=WIRE_DT, name=nm("w_in_dw"))
    dxin, accm1 = mod_bwd(du, sv['x'], 1, dxa, nm("mod1_bwd"))
    dmod = jnp.stack([accm1[:, :, 0], accm1[:, :, 1], acc1[:, :, 0], accm2[:, :, 0], accm2[:, :, 1], acc2[:, :, 0]], axis=2)
    return dxin, dmod, gr


def kernel(x, c, ctx, c_ctx, ada_w, ada_b, ln1_g, ln1_b, w_in, conv_w, conv_b, rg_lambda, rg_wa, rg_ba, rg_wi, rg_bi, s5_a_re, s5_a_im, s5_log_dt, s5_b_re, s5_b_im, s5_c_re, s5_c_im, s5_d, s5_glu_w, s5_glu_b, w_out, b_out, ln2_g, ln2_b, mlp_w1, mlp_b1, mlp_w2, mlp_b2, loss_target, m_c_ctx, m_ada_w, m_ada_b, m_ln1_g, m_ln1_b, m_w_in, m_conv_w, m_conv_b, m_rg_lambda, m_rg_wa, m_rg_ba, m_rg_wi, m_rg_bi, m_s5_a_re, m_s5_a_im, m_s5_log_dt, m_s5_b_re, m_s5_b_im, m_s5_c_re, m_s5_c_im, m_s5_d, m_s5_glu_w, m_s5_glu_b, m_w_out, m_b_out, m_ln2_g, m_ln2_b, m_mlp_w1, m_mlp_b1, m_mlp_w2, m_mlp_b2, v_c_ctx, v_ada_w, v_ada_b, v_ln1_g, v_ln1_b, v_w_in, v_conv_w, v_conv_b, v_rg_lambda, v_rg_wa, v_rg_ba, v_rg_wi, v_rg_bi, v_s5_a_re, v_s5_a_im, v_s5_log_dt, v_s5_b_re, v_s5_b_im, v_s5_c_re, v_s5_c_im, v_s5_d, v_s5_glu_w, v_s5_glu_b, v_w_out, v_b_out, v_ln2_g, v_ln2_b, v_mlp_w1, v_mlp_b1, v_mlp_w2, v_mlp_b2):
    loc = dict(locals())
    w = {n: loc[n] for n in WEIGHTS}
    mom = {n: loc["m_" + n] for n in WEIGHTS}
    vel = {n: loc["v_" + n] for n in WEIGHTS}
    bsz, seq, d = x.shape
    ctx_len = ctx.shape[1]
    depth = ada_w.shape[0]
    g = s5_a_re.shape[2]
    nmod = ada_b.shape[1] // d
    modc = ada_w.shape[2]
    tr = _tile(ctx_len, ROW_TILE, 8)
    cfg = dict(nct=ctx_len // tr, tr=tr, ctx_len=ctx_len, g=g, alpha=(2.0 * depth) ** 0.25)
    xi, yi, ci = _me()
    me = 4 * xi + 2 * yi + ci

    small_shapes = [c.shape] + [w[n].shape for n in CHAN_SHARDED]
    got = _exchange([_pack([c] + [w[n] for n in CHAN_SHARDED], F32)], True, "gather_small")[0]
    parts = _unpack(got, small_shapes, lead=1)
    c_all = parts[0].reshape(N_DEV * bsz, d)
    sm = {n: w[n] for n in WEIGHTS if n not in BIG and n not in CHAN_SHARDED and n != 'ada_w'}
    for n, p in zip(CHAN_SHARDED, parts[1:]):
        sm[n] = jnp.moveaxis(p, 0, -2).reshape(p.shape[1:-1] + (-1,))

    a_ext = jnp.concatenate([c_all, jnp.broadcast_to(c_ctx[None], (N_DEV, d))], axis=0)
    nrow = a_ext.shape[0]
    my_ada_b = lax.dynamic_slice_in_dim(ada_b, me * modc, modc, axis=1)
    mod_cols = jnp.stack([_mm(a_ext, ada_w[l], bias=my_ada_b[l], a_fn=_silu, name=f"l{l}_ada") for l in range(depth)])
    mod_all = _exchange([mod_cols.reshape(depth * nrow, modc)], True, "gather_mod")[0]
    mod_all = mod_all.reshape(N_DEV, depth, nrow, modc).transpose(1, 2, 0, 3).reshape(depth, nrow, nmod, d)
    mod_mine = lax.dynamic_slice_in_dim(mod_all, me * bsz, bsz, axis=1)
    mod_ctx = jnp.broadcast_to(mod_all[:, N_DEV * bsz][:, None], mod_mine.shape)
    modts = jnp.stack([mod_ctx, mod_mine], axis=2)

    wts = {n: [] for n in BIG}
    for l in range(depth):
        got = _exchange([w[n][l].astype(WIRE_DT) for n in BIG], True, f"gather_w{l}")
        for n, p in zip(BIG, got):
            wts[n].append(p if n in COL_SHARDED else p.reshape(-1, p.shape[-1]))

    cfg['s5_ops'], cfg['s5_ops_t'], cfg['s5_lm'], cfg['s5_vjp'] = [], [], [], []
    for l in range(depth):
        ops, vjp = jax.vjp(_s5_operators, s5_a_re[l], s5_a_im[l], s5_log_dt[l], s5_b_re[l], s5_b_im[l], s5_c_re[l],
                           s5_c_im[l], s5_d[l].reshape(g, -1))
        cfg['s5_ops'].append(tuple(o.astype(MXU_DT) for o in ops[:5]))
        cfg['s5_ops_t'].append(tuple(jnp.swapaxes(o, 1, 2).astype(MXU_DT) for o in ops[:5]))
        cfg['s5_lm'].append(ops[5])
        cfg['s5_vjp'].append(vjp)

    act = jnp.concatenate([ctx, x], axis=1)
    saved = []
    for l in range(depth):
        act, sv = _layer_fwd(l, act, modts[l], wts, sm, cfg)
        saved.append(sv)

    def loss_fn(tv, rv, mv, seg):
        err = tv[0] - tv[1]
        keep = jnp.where(seg == 1, 1.0, 0.0)
        return (err * (keep / d),), (_colsum(err * err) * keep,)
    o, a = _ew(loss_fn, [_full(act), (loss_target, 0, d, cfg['nct'])], outs=[(d, F32)], accs=[(1, d)],
               nctx_tiles=cfg['nct'], tr=tr, name="loss")
    dact = o[0]
    loss = lax.psum(0.5 * jnp.sum(a[0][:, 1]) / d, ("x", "y", "c"))

    grads = [None] * depth
    dmods = [None] * depth
    slots = {n: [None] * depth for n in BIG}
    for l in reversed(range(depth)):
        dact, dmods[l], grads[l] = _layer_bwd(l, dact, saved[l], modts[l], wts, sm, cfg)
        got = _exchange([grads[l][n] for n in BIG], False, f"scatter_g{l}")
        for n, p in zip(BIG, got):
            slots[n][l] = p
    grad_x = dact[:, ctx_len:]

    dmod = jnp.stack(dmods)
    mine = jnp.concatenate([dmod[:, :, 1].reshape(depth, bsz, nmod * d),
                            dmod[:, :, 0].sum(1).reshape(depth, 1, nmod * d)], axis=1)
    got = _exchange([mine.reshape(depth * (bsz + 1), nmod * d)], True, "gather_dmod")[0]
    got = got.reshape(N_DEV, depth, bsz + 1, nmod * d)
    dmod_rows = jnp.concatenate([got[:, :, :bsz].transpose(1, 0, 2, 3).reshape(depth, N_DEV * bsz, nmod * d),
                                 got[:, :, bsz].transpose(1, 0, 2)], axis=1)
    dmod_cols = lax.dynamic_slice_in_dim(dmod_rows, me * modc, modc, axis=2)
    g_ada_w = jnp.stack([_mm(a_ext, dmod_cols[l], ta=True, a_fn=_silu, name=f"l{l}_ada_dw") for l in range(depth)])

    def rowsum_fn(tv, rv, mv, seg):
        return (), (_colsum(tv[0]),)
    g_ada_b = _ew(rowsum_fn, [_full(dmod_rows)], accs=[(1, nmod * d)], name="ada_db")[1][0][:, 1, 0]
    dsilu = 0.0
    for l in range(depth):
        dsilu = dsilu + _mm(dmod_cols[l, N_DEV * bsz:], ada_w[l], tb=True, name=f"l{l}_ada_dc").sum(0)
    sig = _sigmoid(c_ctx)
    g_c_ctx_part = dsilu * (sig * (1.0 + c_ctx * (1.0 - sig)))

    def both(name, l):
        return jnp.stack([grads[l][f'{name}{dd}'] for dd in range(2)])
    rep = {n: jnp.stack([grads[l][n] for l in range(depth)]) for n in
           ('ln1_g', 'ln1_b', 'conv_w', 'conv_b', 's5_glu_b', 'b_out', 'ln2_g', 'ln2_b', 'mlp_b1', 'mlp_b2') + S5_NAMES}
    for n in ('rg_lambda', 'rg_wa', 'rg_ba', 'rg_wi', 'rg_bi'):
        rep[n] = jnp.stack([both(n, l) for l in range(depth)])
    rep['c_ctx'] = g_c_ctx_part
    rep_names = [n for n in WEIGHTS if n in rep]
    rep_shapes = [rep[n].shape for n in rep_names]
    buf = _pack([rep[n] for n in rep_names], F32)
    buf = jnp.pad(buf, ((0, -buf.shape[0] % (8 * N_DEV)), (0, 0)))
    part = _sum_slots(_exchange([buf.reshape(N_DEV, -1, PACK_W)], False, "scatter_rep")[0], "sum_rep")
    full = _exchange([part], True, "gather_rep")[0].reshape(-1, PACK_W)
    g_rep = dict(zip(rep_names, _unpack(full, rep_shapes)))

    grad, delta, new_m, new_v = {}, {}, {}, {}
    for n in BIG:
        res = [_sum_adamw(slots[n][l], w[n][l], mom[n][l], vel[n][l], f"l{l}_adamw_{n}") for l in range(depth)]
        grad[n], delta[n], new_m[n], new_v[n] = [jnp.stack([r[i] for r in res]) for i in range(4)]
    flat2 = lambda t: t.reshape(-1, t.shape[-1])
    grad['ada_w'] = g_ada_w
    res = _adamw(flat2(ada_w), flat2(g_ada_w), flat2(mom['ada_w']), flat2(vel['ada_w']), "adamw_ada_w")
    delta['ada_w'], new_m['ada_w'], new_v['ada_w'] = [r.reshape(ada_w.shape) for r in res]
    rest = [n for n in WEIGHTS if n not in BIG and n != 'ada_w']
    for n in rest:
        if n == 'ada_b':
            grad[n] = g_ada_b
        elif n in CHAN_SHARDED:
            grad[n] = lax.dynamic_slice_in_dim(g_rep[n], me * w[n].shape[-1], w[n].shape[-1], axis=g_rep[n].ndim - 1)
        else:
            grad[n] = g_rep[n].reshape(w[n].shape)
    shapes = [w[n].shape for n in rest]
    packed = [_pack([src[n] for n in rest], F32) for src in (w, grad, mom, vel)]
    for dst, o in zip((delta, new_m, new_v), _adamw(*packed, name="adamw_rest")):
        dst.update(zip(rest, _unpack(o, shapes)))
    return (loss, grad_x, *[grad[n] for n in WEIGHTS], *[delta[n] for n in WEIGHTS], *[new_m[n] for n in WEIGHTS],
            *[new_v[n] for n in WEIGHTS])
```

```python
import functools
import math

import jax
import jax.numpy as jnp
import numpy as np
from jax import lax
from jax.experimental import pallas as pl
from jax.experimental.pallas import tpu as pltpu

F32 = jnp.float32
MXU_DT = jnp.bfloat16
ACT_DT = jnp.bfloat16
WIRE_DT = jnp.bfloat16

N_DEV = 8
GRID_W = 64
RG_C = 8.0
LN_EPS = 1e-5
S5_T = 16
S5_GB = 8
ROW_TILE = 256
VMEM_LIMIT = 56 * 1024 * 1024
PACK_W = 1024
SMALL_PARAM = 65536

ADAM_LR = 0.001
ADAM_B1 = 0.9
ADAM_B2 = 0.999
ADAM_EPS = 1e-08
ADAM_WD = 0.01
ADAM_STEP = 10

WEIGHTS = ['c_ctx', 'ada_w', 'ada_b', 'ln1_g', 'ln1_b', 'w_in', 'conv_w', 'conv_b', 'rg_lambda', 'rg_wa', 'rg_ba',
           'rg_wi', 'rg_bi', 's5_a_re', 's5_a_im', 's5_log_dt', 's5_b_re', 's5_b_im', 's5_c_re', 's5_c_im', 's5_d',
           's5_glu_w', 's5_glu_b', 'w_out', 'b_out', 'ln2_g', 'ln2_b', 'mlp_w1', 'mlp_b1', 'mlp_w2', 'mlp_b2']
COL_SHARDED = ('w_in', 'mlp_w1')
MIX_W = ('w_in', 's5_glu_w', 'w_out')
MLP_W = ('mlp_w1', 'mlp_w2')
BIG = MIX_W + MLP_W
CHAN_SHARDED = ('conv_w', 'rg_lambda', 'rg_ba', 'rg_bi')
S5_NAMES = ('s5_a_re', 's5_a_im', 's5_log_dt', 's5_b_re', 's5_b_im', 's5_c_re', 's5_c_im', 's5_d')
MESH = pl.DeviceIdType.MESH


def _tile(n, pref, align):
    t = (min(pref, n) // align) * align
    while t >= align:
        if n % t == 0:
            return t
        t -= align
    return n


def _params(sem):
    return pltpu.CompilerParams(dimension_semantics=sem, vmem_limit_bytes=VMEM_LIMIT)


def _sigmoid(v):
    return 0.5 * jnp.tanh(0.5 * v) + 0.5


def _silu(v):
    return v * _sigmoid(v)


_GELU_K = math.sqrt(2.0 / math.pi)


def _gelu(v):
    return 0.5 * v * (1.0 + jnp.tanh(_GELU_K * (v + 0.044715 * v * v * v)))


def _gelu_grad(v):
    th = jnp.tanh(_GELU_K * (v + 0.044715 * v * v * v))
    return 0.5 * (1.0 + th) + 0.5 * v * (1.0 - th * th) * _GELU_K * (1.0 + 3.0 * 0.044715 * v * v)


def _one_minus_sq(la, a):
    v = 2.0 * la
    series = -v * (1.0 + v * (0.5 + v * (1.0 / 6.0 + v * (1.0 / 24.0 + v * (1.0 / 120.0)))))
    return jnp.where(v > -0.1, series, 1.0 - a * a)


def _softplus(v):
    return jnp.maximum(v, 0.0) + jnp.log(1.0 + jnp.exp(-jnp.abs(v)))


def _dot(a, b):
    return jnp.dot(a.astype(MXU_DT), b.astype(MXU_DT), preferred_element_type=F32)


def _dot_tn(a, b):
    return lax.dot_general(a.astype(MXU_DT), b.astype(MXU_DT), (((0,), (0,)), ((), ())), preferred_element_type=F32)


def _dot_nt(a, b):
    return lax.dot_general(a.astype(MXU_DT), b.astype(MXU_DT), (((1,), (1,)), ((), ())), preferred_element_type=F32)


def _mm(a, b, *, ta=False, tb=False, b_blocked=False, out_blocks=0, bias=None, a_fn=None, epi=None, extras=(),
        colsum=False, out_dtype=F32, name, tm=1088, tn=1024, tk=1024):
    if ta:
        kdim, m = a.shape
    else:
        m, kdim = a.shape
    bcol = b.shape[2] if b_blocked else None
    blog = (b.shape[1], b.shape[0] * b.shape[2]) if b_blocked else b.shape
    if tb:
        n, kb = blog
    else:
        kb, n = blog
    assert kdim == kb, (a.shape, b.shape, ta, tb)
    assert not (ta and tb)
    tm = _tile(m, tm, 128 if ta else 16)
    n_lim = bcol if (b_blocked and not tb) else (n // out_blocks if out_blocks else n)
    tn = _tile(n_lim, tn, 128)
    tk = _tile(bcol if (b_blocked and tb) else kdim, tk, 16 if ta else 128)
    nk = kdim // tk
    a_spec = pl.BlockSpec((tk, tm), lambda i, j, k: (k, i)) if ta else pl.BlockSpec((tm, tk), lambda i, j, k: (i, k))
    if not b_blocked:
        b_spec = pl.BlockSpec((tn, tk), lambda i, j, k: (j, k)) if tb else pl.BlockSpec((tk, tn), lambda i, j, k: (k, j))
    elif tb:
        qk = bcol // tk
        b_spec = pl.BlockSpec((None, tn, tk), lambda i, j, k: (k // qk, j, k % qk))
    else:
        qn = bcol // tn
        b_spec = pl.BlockSpec((None, tk, tn), lambda i, j, k: (j // qn, k, j % qn))
    in_specs = [a_spec, b_spec]
    args = [a, b]
    has_bias = bias is not None
    if has_bias:
        in_specs.append(pl.BlockSpec((1, tn), lambda i, j, k: (0, j)))
        args.append(bias.reshape(1, n).astype(F32))
    for e in extras:
        assert e.shape == (m, n), (e.shape, m, n)
        in_specs.append(pl.BlockSpec((tm, tn), lambda i, j, k: (i, j)))
        args.append(e)
    nex = len(extras)
    dn = (((0 if ta else 1,), (1 if tb else 0,)), ((), ()))
    if out_blocks:
        qo = (n // out_blocks) // tn
        out_shape = jax.ShapeDtypeStruct((out_blocks, m, n // out_blocks), out_dtype)
        out_spec = pl.BlockSpec((None, tm, tn), lambda i, j, k: (j // qo, i, j % qo))
    else:
        out_shape = jax.ShapeDtypeStruct((m, n), out_dtype)
        out_spec = pl.BlockSpec((tm, tn), lambda i, j, k: (i, j))
    out_shapes, out_specs = [out_shape], [out_spec]
    grid = (m // tm, n // tn, nk)
    if colsum:
        out_shapes.append(jax.ShapeDtypeStruct((1, n), F32))
        out_specs.append(pl.BlockSpec((1, tn), lambda i, j, k: (0, j)))
        swap = lambda sp: pl.BlockSpec(sp.block_shape, lambda j, i, k, f=sp.index_map: f(i, j, k))
        in_specs = [swap(sp) for sp in in_specs]
        out_specs = [swap(sp) for sp in out_specs]
        grid = (n // tn, m // tm, nk)
    nout = len(out_shapes)

    def body(*refs):
        a_ref, b_ref = refs[0], refs[1]
        row_tile = pl.program_id(1)
        pos = 2
        bias_ref = refs[pos] if has_bias else None
        pos += int(has_bias)
        ex_refs = refs[pos:pos + nex]
        o_ref = refs[pos + nex]
        av = a_ref[...]
        if a_fn is not None:
            av = a_fn(av.astype(F32))
        part = lax.dot_general(av.astype(MXU_DT), b_ref[...].astype(MXU_DT), dn, preferred_element_type=F32)

        def finish(r):
            if has_bias:
                r = r + bias_ref[...]
            if epi is not None:
                r = epi(r, *[e[...] for e in ex_refs])
            o_ref[...] = r.astype(out_dtype)
            if colsum:
                cs_ref = refs[pos + nex + 1]

                @pl.when(row_tile == 0)
                def _():
                    cs_ref[...] = _colsum(r)

                @pl.when(row_tile > 0)
                def _():
                    cs_ref[...] += _colsum(r)

        if nk == 1:
            finish(part)
            return
        acc_ref = refs[pos + nex + nout]
        k = pl.program_id(2)

        @pl.when(k == 0)
        def _():
            acc_ref[...] = part

        @pl.when(k > 0)
        def _():
            acc_ref[...] += part

        @pl.when(k == nk - 1)
        def _():
            finish(acc_ref[...])

    res = pl.pallas_call(
        body, out_shape=tuple(out_shapes), grid=grid, in_specs=in_specs, out_specs=tuple(out_specs),
        scratch_shapes=[pltpu.VMEM((tm, tn), F32)] if nk > 1 else [],
        compiler_params=_params(("parallel", "arbitrary" if colsum else "parallel", "arbitrary")), name=name)(*args)
    return res if colsum else res[0]


def _ew(fn, tiles, rows=(), mods=(), outs=(), accs=(), *, name, nctx_tiles=0, tr=None):
    bsz, tlen = tiles[0][0].shape[0], tiles[0][0].shape[1]
    if tr is None:
        tr = _tile(tlen, ROW_TILE, 8)
    nt = tlen // tr
    in_specs, args = [], []
    for arr, cb, width, toff in tiles:
        in_specs.append(pl.BlockSpec((1, tr, width), functools.partial(
            lambda b, t, cb, toff: (b, jnp.maximum(t - toff, 0), cb), cb=cb, toff=toff)))
        args.append(arr)
    for r in rows:
        in_specs.append(pl.BlockSpec(r.shape, lambda b, t: (0, 0)))
        args.append(r)

    def seg_of(t):
        return jnp.where(t >= nctx_tiles, 1, 0)

    for mo in mods:
        in_specs.append(pl.BlockSpec((1, 1) + mo.shape[2:], lambda b, t: (b, seg_of(t), 0, 0)))
        args.append(mo)
    out_shape, out_specs = [], []
    for width, dt in outs:
        out_shape.append(jax.ShapeDtypeStruct((bsz, tlen, width), dt))
        out_specs.append(pl.BlockSpec((1, tr, width), lambda b, t: (b, t, 0)))
    for kk, cc in accs:
        out_shape.append(jax.ShapeDtypeStruct((bsz, 2, kk, cc), F32))
        out_specs.append(pl.BlockSpec((1, 1, kk, cc), lambda b, t: (b, seg_of(t), 0, 0)))
    nti, nr, nm, no, na = len(tiles), len(rows), len(mods), len(outs), len(accs)

    def body(*refs):
        t = pl.program_id(1)
        tv = [r[0] for r in refs[:nti]]
        rv = [r[...] for r in refs[nti:nti + nr]]
        mv = [r[0, 0] for r in refs[nti + nr:nti + nr + nm]]
        o_refs = refs[nti + nr + nm:nti + nr + nm + no]
        a_refs = refs[nti + nr + nm + no:]
        seg = seg_of(t)
        ov, av = fn(tv, rv, mv, seg)
        for r, v in zip(o_refs, ov):
            r[0] = v.astype(r.dtype)
        if na:
            @pl.when((t == 0) | (t == nctx_tiles))
            def _():
                for r in a_refs:
                    r[...] = jnp.zeros_like(r)

            for r, v in zip(a_refs, av):
                r[0, 0] += v

    res = pl.pallas_call(
        body, out_shape=tuple(out_shape), grid=(bsz, nt), in_specs=in_specs, out_specs=tuple(out_specs),
        compiler_params=_params(("arbitrary", "arbitrary")), name=name)(*args)
    return res[:no], res[no:]


def _full(arr):
    return (arr, 0, arr.shape[-1], 0)


def _colsum(v):
    return jnp.sum(v, axis=0, keepdims=True)


def _shifted(x, prev8, next8, first, last, k):
    tr = x.shape[0]
    rid = lax.broadcasted_iota(jnp.int32, x.shape, 0)
    keep_prev = jnp.where(first, 0.0, 1.0)
    keep_next = jnp.where(last, 0.0, 1.0)
    if k == -1:
        return jnp.where(rid == 0, prev8[7:8] * keep_prev, pltpu.roll(x, 1, 0))
    if k == -2:
        r = pltpu.roll(x, 2, 0)
        r = jnp.where(rid == 1, prev8[7:8] * keep_prev, r)
        return jnp.where(rid == 0, prev8[6:7] * keep_prev, r)
    if k == 1:
        return jnp.where(rid == tr - 1, next8[0:1] * keep_next, pltpu.roll(x, tr - 1, 0))
    if k == 2:
        r = pltpu.roll(x, tr - 2, 0)
        r = jnp.where(rid == tr - 2, next8[0:1] * keep_next, r)
        return jnp.where(rid == tr - 1, next8[1:2] * keep_next, r)
    raise ValueError(k)


def _halo_specs(tr, width, cb, n8):
    cur = pl.BlockSpec((1, tr, width), lambda b, t: (b, t, cb))
    prev = pl.BlockSpec((1, 8, width), lambda b, t: (b, jnp.maximum(t * (tr // 8) - 1, 0), cb))
    nxt = pl.BlockSpec((1, 8, width), lambda b, t: (b, jnp.minimum((t + 1) * (tr // 8), n8 - 1), cb))
    return [cur, prev, nxt]


def _conv_fwd(z3, conv_w, conv_b, d, nctx_tiles, tr, name):
    bsz, tlen, _ = z3.shape
    nt = tlen // tr

    def body(x_ref, xp_ref, xn_ref, w_ref, b_ref, o_ref):
        t = pl.program_id(1)
        first = (t == 0) | (t == nctx_tiles)
        last = (t == nctx_tiles - 1) | (t == nt - 1)
        x, p8, n8 = x_ref[0], xp_ref[0], xn_ref[0]
        w = w_ref[...]
        o_ref[0] = (b_ref[...] + w[0:1] * _shifted(x, p8, n8, first, last, -1) + w[1:2] * x
                    + w[2:3] * _shifted(x, p8, n8, first, last, 1) + w[3:4] * _shifted(x, p8, n8, first, last, 2))

    return pl.pallas_call(
        body, out_shape=jax.ShapeDtypeStruct((bsz, tlen, d), F32), grid=(bsz, nt),
        in_specs=_halo_specs(tr, d, 0, tlen // 8) + [pl.BlockSpec((4, d), lambda b, t: (0, 0)),
                                                      pl.BlockSpec((1, d), lambda b, t: (0, 0))],
        out_specs=pl.BlockSpec((1, tr, d), lambda b, t: (b, t, 0)),
        compiler_params=_params(("parallel", "parallel")), name=name)(z3, z3, z3, conv_w, conv_b.reshape(1, d))


def _conv_bwd(dxc, z3, conv_w, d, nctx_tiles, tr, name):
    bsz, tlen, _ = dxc.shape
    nt = tlen // tr

    def body(g_ref, gp_ref, gn_ref, x_ref, xp_ref, xn_ref, w_ref, o_ref, acc_ref):
        t = pl.program_id(1)
        first = (t == 0) | (t == nctx_tiles)
        last = (t == nctx_tiles - 1) | (t == nt - 1)
        g, gp, gn = g_ref[0], gp_ref[0], gn_ref[0]
        x, xp, xn = x_ref[0], xp_ref[0], xn_ref[0]
        w = w_ref[...]
        o_ref[0] = (w[0:1] * _shifted(g, gp, gn, first, last, 1) + w[1:2] * g
                    + w[2:3] * _shifted(g, gp, gn, first, last, -1) + w[3:4] * _shifted(g, gp, gn, first, last, -2))

        @pl.when(t == 0)
        def _():
            acc_ref[...] = jnp.zeros_like(acc_ref)

        acc_ref[0, 0:1] += _colsum(g * _shifted(x, xp, xn, first, last, -1))
        acc_ref[0, 1:2] += _colsum(g * x)
        acc_ref[0, 2:3] += _colsum(g * _shifted(x, xp, xn, first, last, 1))
        acc_ref[0, 3:4] += _colsum(g * _shifted(x, xp, xn, first, last, 2))
        acc_ref[0, 4:5] += _colsum(g)

    return pl.pallas_call(
        body, out_shape=(jax.ShapeDtypeStruct((bsz, tlen, d), F32), jax.ShapeDtypeStruct((bsz, 8, d), F32)),
        grid=(bsz, nt),
        in_specs=_halo_specs(tr, d, 0, tlen // 8) + _halo_specs(tr, d, 0, tlen // 8)
        + [pl.BlockSpec((4, d), lambda b, t: (0, 0))],
        out_specs=(pl.BlockSpec((1, tr, d), lambda b, t: (b, t, 0)), pl.BlockSpec((1, 8, d), lambda b, t: (b, 0, 0))),
        compiler_params=_params(("arbitrary", "arbitrary")), name=name)(dxc, dxc, dxc, z3, z3, z3, conv_w)


def _rg_gates(x, lam, wa_ref, ba, wi_ref, bi, nh, hd):
    sp = _softplus(-lam)
    prs, pis = [], []
    for h in range(nh):
        xh = x[:, h * hd:(h + 1) * hd]
        prs.append(_dot(xh, wa_ref[h]))
        pis.append(_dot(xh, wi_ref[h]))
    r = 1.0 / (1.0 + jnp.exp(-(jnp.concatenate(prs, axis=1) + ba)))
    i = _sigmoid(jnp.concatenate(pis, axis=1) + bi)
    la = -RG_C * sp * r
    a = jnp.exp(la)
    m2 = _one_minus_sq(la, a)
    return sp, r, i, a, jnp.sqrt(m2), lax.rsqrt(m2)


def _scan_tile(t, nctx_tiles, nt, reverse):
    if not reverse:
        return t
    return jnp.where(t < nctx_tiles, nctx_tiles - 1 - t, nt - 1 - (t - nctx_tiles))


def _unscan_tile(t, nctx_tiles, nt, reverse):
    if not reverse:
        return nt - 1 - t
    return jnp.where(t < nt - nctx_tiles, nctx_tiles + t, t - (nt - nctx_tiles))


def _rg_param_specs(d, nh, hd):
    vec = pl.BlockSpec((1, d), lambda b, t: (0, 0))
    mat = pl.BlockSpec((nh, hd, hd), lambda b, t: (0, 0, 0))
    return [vec, mat, vec, mat, vec]


def _ride_split(refs, n_in, n_ride, n_out, n_scratch):
    pos = [n_in, n_in + n_ride, n_in + n_ride + n_out, n_in + 2 * n_ride + n_out, n_in + 2 * n_ride + n_out + n_scratch]
    return (refs[:pos[0]], refs[pos[0]:pos[1]], refs[pos[1]:pos[2]], refs[pos[2]:pos[3]], refs[pos[3]:pos[4]],
            refs[pos[4]:])


def _rg_fwd(xc, lam, wa, ba, wi, bi, reverse, nctx_tiles, tr, name, ride=None):
    bsz, tlen, d = xc.shape
    nh, hd = wa.shape[0], wa.shape[1]
    nt = tlen // tr
    tmap = lambda b, t: (b, _scan_tile(t, nctx_tiles, nt, reverse), 0)
    rarrs, rgather = ride if ride else ([], True)
    nr = len(rarrs)

    def body(*refs):
        (x_ref, lam_ref, wa_ref, ba_ref, wi_ref, bi_ref), rin, (h_ref,), rout, (a_scr, b_scr, carry), sems = _ride_split(
            refs, 6, nr, 1, 3)
        first = (pl.program_id(0) == 0) & (pl.program_id(1) == 0)
        last = (pl.program_id(0) == bsz - 1) & (pl.program_id(1) == nt - 1)
        if nr:
            @pl.when(first)
            def _():
                _exchange_start(rin, rout, sems, rgather)

        @pl.when(pl.program_id(1) == 0)
        def _():
            carry[...] = jnp.zeros_like(carry)

        x = x_ref[0]
        _, _, i, a, mult, _ = _rg_gates(x, lam_ref[...], wa_ref, ba_ref[...], wi_ref, bi_ref[...], nh, hd)
        a_scr[...] = a
        b_scr[...] = mult * (i * x)

        def blk(j, h):
            for r in range(8):
                row = (tr - 1 - (j * 8 + r)) if reverse else (j * 8 + r)
                h = a_scr[pl.ds(row, 1), :] * h + b_scr[pl.ds(row, 1), :]
                h_ref[0, pl.ds(row, 1), :] = h
            return h

        carry[0:1, :] = lax.fori_loop(0, tr // 8, blk, carry[0:1, :])
        if nr:
            @pl.when(last)
            def _():
                _exchange_finish(rin, rout, sems, rgather)

    res = pl.pallas_call(
        body, out_shape=(jax.ShapeDtypeStruct((bsz, tlen, d), F32),) + _exchange_out_shapes(rarrs), grid=(bsz, nt),
        in_specs=[pl.BlockSpec((1, tr, d), tmap)] + _rg_param_specs(d, nh, hd) + [_ANY] * nr,
        out_specs=(pl.BlockSpec((1, tr, d), tmap),) + (_ANY,) * nr,
        scratch_shapes=[pltpu.VMEM((tr, d), F32), pltpu.VMEM((tr, d), F32), pltpu.VMEM((8, d), F32)]
        + (_exchange_sems(nr) if nr else []),
        compiler_params=_params(("arbitrary", "arbitrary")), name=name)(
            xc, lam.reshape(1, d), wa, ba.reshape(1, d), wi, bi.reshape(1, d), *rarrs)
    return res[0], list(res[1:])


def _rg_bwd(xc, h, dmix, z3, lam, wa, ba, wi, bi, addend, reverse, nctx_tiles, tr, name, ride=None):
    bsz, tlen, d = xc.shape
    nh, hd = wa.shape[0], wa.shape[1]
    nt = tlen // tr
    r8 = tr // 8

    def tile_of(t):
        return _unscan_tile(t, nctx_tiles, nt, reverse)

    tmap = lambda b, t: (b, tile_of(t), 0)
    gmap = lambda b, t: (b, tile_of(t), 1)

    def halo_map(b, t):
        tt = tile_of(t)
        if not reverse:
            return (b, jnp.maximum(tt * r8 - 1, 0), 0)
        return (b, jnp.where(tt == nt - 1, 0, (tt + 1) * r8), 0)

    has_add = addend is not None
    rarrs, rgather = ride if ride else ([], True)
    nr = len(rarrs)

    def body(*refs):
        ins, rin, (dx_ref, dwa_ref, dwi_ref, dv_ref), rout, (a_scr, g_scr, dh_scr, carry), sems = _ride_split(
            refs, 10 + int(has_add), nr, 4, 4)
        x_ref, h_ref, halo_ref, dr_ref, z_ref, lam_ref, wa_ref, ba_ref, wi_ref, bi_ref = ins[:10]
        add_ref = ins[10] if has_add else None
        b = pl.program_id(0)
        t = pl.program_id(1)
        tt = tile_of(t)
        if nr:
            @pl.when((b == 0) & (t == 0))
            def _():
                _exchange_start(rin, rout, sems, rgather)

        @pl.when((b == 0) & (t == 0))
        def _():
            dwa_ref[...] = jnp.zeros_like(dwa_ref)
            dwi_ref[...] = jnp.zeros_like(dwi_ref)
            dv_ref[...] = jnp.zeros_like(dv_ref)

        @pl.when(t == 0)
        def _():
            carry[...] = jnp.zeros_like(carry)

        x = x_ref[0]
        sp, r, i, a, mult, inv_mult = _rg_gates(x, lam_ref[...], wa_ref, ba_ref[...], wi_ref, bi_ref[...], nh, hd)
        a_scr[...] = a
        dh_scr[...] = dr_ref[0] * _gelu(z_ref[0])

        def blk(j, cc):
            for rr in range(8):
                row = (j * 8 + rr) if reverse else (tr - 1 - (j * 8 + rr))
                g = dh_scr[pl.ds(row, 1), :] + cc
                g_scr[pl.ds(row, 1), :] = g
                cc = a_scr[pl.ds(row, 1), :] * g
            return cc

        carry[0:1, :] = lax.fori_loop(0, r8, blk, carry[0:1, :])
        g = g_scr[...]
        hv = h_ref[0]
        rid = lax.broadcasted_iota(jnp.int32, hv.shape, 0)
        if not reverse:
            valid = jnp.where(tt > 0, 1.0, 0.0)
            hprev = jnp.where(rid == 0, halo_ref[0][7:8] * valid, pltpu.roll(hv, 1, 0))
        else:
            valid = jnp.where(tt == nctx_tiles - 1, 0.0, 1.0)
            hprev = jnp.where(rid == tr - 1, halo_ref[0][0:1] * valid, pltpu.roll(hv, tr - 1, 0))
        dla = g * hprev * a - g * (i * x) * (a * a) * inv_mult
        dpr = dla * (-RG_C * sp) * r * (1.0 - r)
        dpi = g * mult * x * i * (1.0 - i)
        dx = g * mult * i
        dxs = []
        for hh in range(nh):
            sl = slice(hh * hd, (hh + 1) * hd)
            dxs.append(_dot_nt(dpr[:, sl], wa_ref[hh]) + _dot_nt(dpi[:, sl], wi_ref[hh]))
            dwa_ref[hh] += _dot_tn(x[:, sl], dpr[:, sl])
            dwi_ref[hh] += _dot_tn(x[:, sl], dpi[:, sl])
        dx = dx + jnp.concatenate(dxs, axis=1)
        if has_add:
            dx = dx + add_ref[0]
        dx_ref[0] = dx
        dv_ref[0:1] += _colsum(dla * (-RG_C * r))
        dv_ref[1:2] += _colsum(dpr)
        dv_ref[2:3] += _colsum(dpi)
        if nr:
            @pl.when((b == bsz - 1) & (t == nt - 1))
            def _():
                _exchange_finish(rin, rout, sems, rgather)

    in_specs = [pl.BlockSpec((1, tr, d), tmap), pl.BlockSpec((1, tr, d), tmap), pl.BlockSpec((1, 8, d), halo_map),
                pl.BlockSpec((1, tr, d), tmap), pl.BlockSpec((1, tr, d), gmap)] + _rg_param_specs(d, nh, hd)
    args = [xc, h, h, dmix, z3, lam.reshape(1, d), wa, ba.reshape(1, d), wi, bi.reshape(1, d)]
    if has_add:
        in_specs.append(pl.BlockSpec((1, tr, d), tmap))
        args.append(addend)
    mat = pl.BlockSpec((nh, hd, hd), lambda b, t: (0, 0, 0))
    res = pl.pallas_call(
        body, out_shape=(jax.ShapeDtypeStruct((bsz, tlen, d), F32), jax.ShapeDtypeStruct((nh, hd, hd), F32),
                         jax.ShapeDtypeStruct((nh, hd, hd), F32), jax.ShapeDtypeStruct((8, d), F32))
        + _exchange_out_shapes(rarrs),
        grid=(bsz, nt), in_specs=in_specs + [_ANY] * nr,
        out_specs=(pl.BlockSpec((1, tr, d), tmap), mat, mat, pl.BlockSpec((8, d), lambda b, t: (0, 0))) + (_ANY,) * nr,
        scratch_shapes=[pltpu.VMEM((tr, d), F32), pltpu.VMEM((tr, d), F32), pltpu.VMEM((tr, d), F32),
                        pltpu.VMEM((8, d), F32)] + (_exchange_sems(nr) if nr else []),
        compiler_params=_params(("arbitrary", "arbitrary")), name=name)(*args, *rarrs)
    return res[0], res[1], res[2], res[3], list(res[4:])


def _s5_operators(a_re, a_im, log_dt, b_re, b_im, c_re, c_im, d_skip):
    hi = lax.Precision.HIGHEST
    tt = S5_T
    lam = lax.complex(a_re, a_im)
    lamdt = lam * jnp.exp(log_dt)[..., None]
    lam_bar = jnp.exp(lamdt)
    b_bar = ((lam_bar - 1.0) / lam)[..., None] * lax.complex(b_re, b_im)
    cm = lax.complex(c_re, c_im)
    taus = jnp.arange(tt + 1, dtype=F32)
    pw = jnp.exp(lamdt[:, :, None, :] * taus[None, None, :, None])
    kern = jnp.real(jnp.einsum('dgop,dgtp,dgpi->dgtoi', cm, pw[:, :, :tt], b_bar, precision=hi))
    g, h = d_skip.shape
    s_idx = np.arange(tt)[None, :, None]
    t_idx = np.arange(tt)[None, None, :]
    lag = np.arange(tt)[:, None, None]
    sel = jnp.asarray(np.stack([t_idx - s_idx == lag, s_idx - t_idx == lag]).astype(np.float32))
    tz = jnp.einsum('dxst,dgxoi->gsito', sel, kern, precision=hi).reshape(g, tt * h, tt * h)
    tz = tz + jnp.eye(tt * h, dtype=F32)[None] * jnp.tile(d_skip, (1, tt))[:, None, :]

    def reim(zc, axis):
        return jnp.concatenate([jnp.real(zc), jnp.imag(zc)], axis=axis)

    bt0 = b_bar[0].transpose(0, 2, 1)[:, None, :, :]
    bt1 = b_bar[1].transpose(0, 2, 1)[:, None, :, :]
    wf = reim(pw[0][:, :tt][:, ::-1][:, :, None, :] * bt0, -1).reshape(g, tt * h, -1)
    wb = reim(pw[1][:, :tt][:, :, None, :] * bt1, -1).reshape(g, tt * h, -1)
    ct0 = cm[0].transpose(0, 2, 1)[:, :, None, :]
    ct1 = cm[1].transpose(0, 2, 1)[:, :, None, :]
    clf = ct0 * pw[0][:, 1:tt + 1].transpose(0, 2, 1)[:, :, :, None]
    clb = ct1 * pw[1][:, 1:tt + 1][:, ::-1].transpose(0, 2, 1)[:, :, :, None]
    vf = jnp.concatenate([jnp.real(clf), -jnp.imag(clf)], axis=1).reshape(g, -1, tt * h)
    vb = jnp.concatenate([jnp.real(clb), -jnp.imag(clb)], axis=1).reshape(g, -1, tt * h)
    lt = pw[:, :, tt]
    rows = []
    for dd in range(2):
        re, im = jnp.real(lt[dd]), jnp.imag(lt[dd])
        rows += [jnp.concatenate([re, re], -1), jnp.concatenate([-im, im], -1)]
    return tz, wf, wb, vf, vb, jnp.stack(rows)


def _chunk_order(j, ncc, nc):
    return jnp.where(j < ncc, ncc - 1 - j, nc - 1 - (j - ncc))


def _s5_specs(bsz, nc, ops, lm):
    gb = S5_GB
    act = lambda width: pl.BlockSpec((1, gb * nc, width), lambda i, b: (b, i, 0))
    opspecs = [pl.BlockSpec((gb,) + o.shape[1:], lambda i, b: (i, 0, 0)) for o in ops]
    lspec = pl.BlockSpec((4, gb, lm.shape[-1]), lambda i, b: (0, i, 0))
    return act, opspecs, lspec


def _s5_fwd(u, ops, lm, ncc, nc, name):
    bsz, gn, th = u.shape
    g = gn // nc
    gb = S5_GB
    p2 = ops[1].shape[-1]
    ph = p2 // 2

    def body(u_ref, tz_ref, wf_ref, wb_ref, vf_ref, vb_ref, l_ref, y_ref, hf_ref, hb_ref, sf, sfs, sb, sbs):
        a1f, a2f, a1b, a2b = l_ref[0], l_ref[1], l_ref[2], l_ref[3]
        for gi in range(gb):
            ug = u_ref[0, pl.ds(gi * nc, nc), :]
            s1 = _dot(ug, wf_ref[gi])
            s2 = _dot(ug, wb_ref[gi])
            sf[pl.ds(gi, nc, stride=gb), :] = s1
            sfs[pl.ds(gi, nc, stride=gb), :] = pltpu.roll(s1, ph, 1)
            sb[pl.ds(gi, nc, stride=gb), :] = s2
            sbs[pl.ds(gi, nc, stride=gb), :] = pltpu.roll(s2, ph, 1)

        def step(j, hs):
            hf, hfs, hb, hbs = hs
            rf = pl.ds(pl.multiple_of(j * gb, gb), gb)
            rb = pl.ds(pl.multiple_of(_chunk_order(j, ncc, nc) * gb, gb), gb)
            s1, s1s, s2, s2s = sf[rf, :], sfs[rf, :], sb[rb, :], sbs[rb, :]
            sf[rf, :] = hf
            sb[rb, :] = hb
            return (a1f * hf + a2f * hfs + s1, a1f * hfs - a2f * hf + s1s,
                    a1b * hb + a2b * hbs + s2, a1b * hbs - a2b * hb + s2s)

        zero = jnp.zeros((gb, p2), F32)
        lax.fori_loop(0, nc, step, (zero, zero, zero, zero))
        for gi in range(gb):
            rows = pl.ds(gi * nc, nc)
            hfg = sf[pl.ds(gi, nc, stride=gb), :]
            hbg = sb[pl.ds(gi, nc, stride=gb), :]
            hf_ref[0, rows, :] = hfg
            hb_ref[0, rows, :] = hbg
            y_ref[0, rows, :] = (_dot(u_ref[0, rows, :], tz_ref[gi]) + _dot(hfg, vf_ref[gi])
                                 + _dot(hbg, vb_ref[gi])).astype(y_ref.dtype)

    act, opspecs, lspec = _s5_specs(bsz, nc, ops, lm)
    return pl.pallas_call(
        body, out_shape=(jax.ShapeDtypeStruct((bsz, gn, th), ACT_DT), jax.ShapeDtypeStruct((bsz, gn, p2), F32),
                         jax.ShapeDtypeStruct((bsz, gn, p2), F32)),
        grid=(g // gb, bsz), in_specs=[act(th)] + opspecs + [lspec], out_specs=(act(th), act(p2), act(p2)),
        scratch_shapes=[pltpu.VMEM((gb * nc, p2), F32) for _ in range(4)],
        compiler_params=_params(("parallel", "arbitrary")), name=name)(u, *ops, lm)


def _s5_bwd(dy, u, hf, hb, ops_t, lm, ncc, nc, name):
    bsz, gn, th = u.shape
    g = gn // nc
    gb = S5_GB
    p2 = lm.shape[-1]
    ph = p2 // 2

    def body(dy_ref, u_ref, hf_ref, hb_ref, tzt_ref, wft_ref, wbt_ref, vft_ref, vbt_ref, l_ref,
             du_ref, dtz_ref, dwf_ref, dwb_ref, dvf_ref, dvb_ref, dl_ref, jf, jfs, jb, jbs, hfk, hbk):
        a1f, a2f, a1b, a2b = l_ref[0], l_ref[1], l_ref[2], l_ref[3]

        @pl.when(pl.program_id(1) == 0)
        def _():
            for r in (dtz_ref, dwf_ref, dwb_ref, dvf_ref, dvb_ref, dl_ref):
                r[...] = jnp.zeros_like(r)

        for gi in range(gb):
            rows = pl.ds(gi * nc, nc)
            dyg = dy_ref[0, rows, :]
            i1 = _dot(dyg, vft_ref[gi])
            i2 = _dot(dyg, vbt_ref[gi])
            jf[pl.ds(gi, nc, stride=gb), :] = i1
            jfs[pl.ds(gi, nc, stride=gb), :] = pltpu.roll(i1, ph, 1)
            jb[pl.ds(gi, nc, stride=gb), :] = i2
            jbs[pl.ds(gi, nc, stride=gb), :] = pltpu.roll(i2, ph, 1)
            hfk[pl.ds(gi, nc, stride=gb), :] = hf_ref[0, rows, :]
            hbk[pl.ds(gi, nc, stride=gb), :] = hb_ref[0, rows, :]

        def step(j, carry):
            qf, qfs, qb, qbs, d1f, d2f, d1b, d2b = carry
            rf = pl.ds(pl.multiple_of((nc - 1 - j) * gb, gb), gb)
            rb = pl.ds(pl.multiple_of(_chunk_order(nc - 1 - j, ncc, nc) * gb, gb), gb)
            i1, i1s, i2, i2s = jf[rf, :], jfs[rf, :], jb[rb, :], jbs[rb, :]
            h1, h2 = hfk[rf, :], hbk[rb, :]
            jf[rf, :] = qf
            jb[rb, :] = qb
            return (i1 + a1f * qf - a2f * qfs, i1s + a1f * qfs + a2f * qf,
                    i2 + a1b * qb - a2b * qbs, i2s + a1b * qbs + a2b * qb,
                    d1f + qf * h1, d2f + qfs * h1, d1b + qb * h2, d2b + qbs * h2)

        zero = jnp.zeros((gb, p2), F32)
        fin = lax.fori_loop(0, nc, step, (zero,) * 8)
        dl_ref[0] += fin[4]
        dl_ref[1] += pltpu.roll(fin[5], ph, 1)
        dl_ref[2] += fin[6]
        dl_ref[3] += pltpu.roll(fin[7], ph, 1)
        for gi in range(gb):
            rows = pl.ds(gi * nc, nc)
            dyg = dy_ref[0, rows, :]
            ug = u_ref[0, rows, :]
            dsf = jf[pl.ds(gi, nc, stride=gb), :]
            dsb = jb[pl.ds(gi, nc, stride=gb), :]
            du_ref[0, rows, :] = (_dot(dyg, tzt_ref[gi]) + _dot(dsf, wft_ref[gi]) + _dot(dsb, wbt_ref[gi])).astype(du_ref.dtype)
            dtz_ref[gi] += _dot_tn(ug, dyg)
            dwf_ref[gi] += _dot_tn(ug, dsf)
            dwb_ref[gi] += _dot_tn(ug, dsb)
            dvf_ref[gi] += _dot_tn(hf_ref[0, rows, :], dyg)
            dvb_ref[gi] += _dot_tn(hb_ref[0, rows, :], dyg)

    act, opspecs, lspec = _s5_specs(bsz, nc, ops_t, lm)
    gshape = lambda o: jax.ShapeDtypeStruct(o.shape[:1] + o.shape[1:][::-1], F32)
    gspec = lambda o: pl.BlockSpec((gb,) + o.shape[1:][::-1], lambda i, b: (i, 0, 0))
    res = pl.pallas_call(
        body, out_shape=tuple([jax.ShapeDtypeStruct((bsz, gn, th), ACT_DT)] + [gshape(o) for o in ops_t]
                              + [jax.ShapeDtypeStruct(lm.shape, F32)]),
        grid=(g // gb, bsz), in_specs=[act(th), act(th), act(p2), act(p2)] + opspecs + [lspec],
        out_specs=tuple([act(th)] + [gspec(o) for o in ops_t] + [lspec]),
        scratch_shapes=[pltpu.VMEM((gb * nc, p2), F32) for _ in range(6)],
        compiler_params=_params(("parallel", "arbitrary")), name=name)(dy, u, hf, hb, *ops_t, lm)
    return res[0], tuple(res[1:])


def _to_chunks(s, ctx_len, g):
    bsz, tlen, d = s.shape
    h = d // g
    seq = tlen - ctx_len
    rows = seq // GRID_W
    cpart = s[:, :ctx_len].reshape(bsz, ctx_len // S5_T, S5_T, g, h).transpose(0, 3, 1, 2, 4)
    lpart = s[:, ctx_len:].reshape(bsz, rows, GRID_W, g, h).transpose(0, 3, 2, 1, 4)
    cpart = cpart.reshape(bsz, g, ctx_len // S5_T, S5_T * h)
    lpart = lpart.reshape(bsz, g, seq // S5_T, S5_T * h)
    return jnp.concatenate([cpart, lpart], axis=2).reshape(bsz, g * (tlen // S5_T), S5_T * h)


def _from_chunks(y, ctx_len, g):
    bsz, gn, th = y.shape
    nc = gn // g
    h = th // S5_T
    ncc = ctx_len // S5_T
    seq = (nc - ncc) * S5_T
    rows = seq // GRID_W
    y = y.reshape(bsz, g, nc, th)
    cpart = y[:, :, :ncc].reshape(bsz, g, ncc, S5_T, h).transpose(0, 2, 3, 1, 4).reshape(bsz, ctx_len, g * h)
    lpart = y[:, :, ncc:].reshape(bsz, g, GRID_W, rows, h).transpose(0, 3, 2, 1, 4).reshape(bsz, seq, g * h)
    return jnp.concatenate([cpart, lpart], axis=1)


def _me():
    return lax.axis_index("x"), lax.axis_index("y"), lax.axis_index("c")


def _peer(k):
    x, y, c = _me()
    px = (1 - x) if (k & 4) else x
    py = (1 - y) if (k & 2) else y
    pc = (1 - c) if (k & 1) else c
    return (px, py, pc), 4 * px + 2 * py + pc


def _exchange(arrs, gather, name):
    n = len(arrs)

    def body(*refs):
        _exchange_start(refs[:n], refs[n:2 * n], refs[2 * n:], gather)
        _exchange_finish(refs[:n], refs[n:2 * n], refs[2 * n:], gather)

    return pl.pallas_call(
        body, out_shape=_exchange_out_shapes(arrs), in_specs=[_ANY] * n, out_specs=tuple([_ANY] * n),
        scratch_shapes=_exchange_sems(n), name=name)(*arrs)


_ANY = pl.BlockSpec(memory_space=pl.ANY)


def _exchange_out_shapes(arrs):
    return tuple(jax.ShapeDtypeStruct((N_DEV,) + a.shape[-2:], a.dtype) for a in arrs)


def _exchange_sems(n):
    return [pltpu.SemaphoreType.DMA((n * (N_DEV - 1),)), pltpu.SemaphoreType.DMA((n * (N_DEV - 1),)),
            pltpu.SemaphoreType.DMA((n,))]


def _exchange_copies(x_refs, o_refs, sems, gather, with_recvs):
    send_sems, recv_sems, local_sems = sems
    n = len(x_refs)
    npeer = N_DEV - 1
    xi, yi, ci = _me()
    me = 4 * xi + 2 * yi + ci
    mine = [x if gather else x.at[me] for x in x_refs]
    local = [pltpu.make_async_copy(mine[i], o_refs[i].at[me], local_sems.at[i]) for i in range(n)]
    sends, recvs = [], []
    for k in range(1, N_DEV):
        dev, pid = _peer(k)
        for i in range(n):
            slot = i * npeer + k - 1
            sends.append(pltpu.make_async_remote_copy(
                src_ref=x_refs[i] if gather else x_refs[i].at[pid], dst_ref=o_refs[i].at[me],
                send_sem=send_sems.at[slot], recv_sem=recv_sems.at[slot], device_id=dev, device_id_type=MESH))
            if with_recvs:
                recvs.append(pltpu.make_async_remote_copy(
                    src_ref=mine[i], dst_ref=o_refs[i].at[pid], send_sem=send_sems.at[slot],
                    recv_sem=recv_sems.at[slot], device_id=dev, device_id_type=MESH))
    return local, sends, recvs


def _exchange_start(x_refs, o_refs, sems, gather):
    local, sends, _ = _exchange_copies(x_refs, o_refs, sems, gather, False)
    for cp in local + sends:
        cp.start()


def _exchange_finish(x_refs, o_refs, sems, gather):
    local, sends, recvs = _exchange_copies(x_refs, o_refs, sems, gather, True)
    for cp in recvs:
        cp.wait_recv()
    for cp in sends:
        cp.wait_send()
    for cp in local:
        cp.wait()


def _sum_slots(x3, name):
    _, r, cdim = x3.shape
    tr = _tile(r, 256, 8)

    def body(x_ref, o_ref):
        acc = x_ref[0]
        for s in range(1, N_DEV):
            acc = acc + x_ref[s]
        o_ref[...] = acc

    return pl.pallas_call(
        body, out_shape=jax.ShapeDtypeStruct((r, cdim), F32), grid=(r // tr,),
        in_specs=[pl.BlockSpec((N_DEV, tr, cdim), lambda i: (0, i, 0))], out_specs=pl.BlockSpec((tr, cdim), lambda i: (i, 0)),
        compiler_params=_params(("parallel",)), name=name)(x3)


def _pack(arrs, dtype, lead=0):
    flat = jnp.concatenate([a.reshape(a.shape[:lead] + (-1,)).astype(dtype) for a in arrs], axis=-1)
    n = flat.shape[-1]
    pad = -n % (PACK_W * 16)
    flat = jnp.pad(flat, [(0, 0)] * lead + [(0, pad)])
    return flat.reshape(flat.shape[:lead] + (-1, PACK_W))


def _unpack(buf, shapes, lead=0):
    flat = buf.reshape(buf.shape[:lead] + (-1,))
    out, off = [], 0
    for shp in shapes:
        n = math.prod(shp)
        out.append(flat[..., off:off + n].reshape(buf.shape[:lead] + tuple(shp)))
        off += n
    return out


def _adamw_math(wv, gv, m0, v0):
    m1 = ADAM_B1 * m0 + (1.0 - ADAM_B1) * gv
    v1 = ADAM_B2 * v0 + (1.0 - ADAM_B2) * (gv * gv)
    m_hat = m1 / (1.0 - ADAM_B1 ** ADAM_STEP)
    v_hat = v1 / (1.0 - ADAM_B2 ** ADAM_STEP)
    delta = -ADAM_LR * (m_hat / (jnp.sqrt(v_hat) + ADAM_EPS) + ADAM_WD * wv)
    return delta, m1, v1


def _adamw(w, g, m, v, name):
    def fn(tv, rv, mv, seg):
        return _adamw_math(*tv), ()

    outs, _ = _ew(fn, [_full(a[None]) for a in (w, g, m, v)], outs=[(w.shape[-1], F32)] * 3, name=name)
    return [o[0] for o in outs]


def _sum_adamw(slots, w, m, v, name):
    _, r, cdim = slots.shape
    tr = _tile(r, 128, 16)

    def body(s_ref, w_ref, m_ref, v_ref, g_ref, d_ref, mo_ref, vo_ref):
        gv = s_ref[0].astype(F32)
        for s in range(1, N_DEV):
            gv = gv + s_ref[s].astype(F32)
        g_ref[...] = gv
        d_ref[...], mo_ref[...], vo_ref[...] = _adamw_math(w_ref[...], gv, m_ref[...], v_ref[...])

    flat = pl.BlockSpec((tr, cdim), lambda i: (i, 0))
    return pl.pallas_call(
        body, out_shape=tuple(jax.ShapeDtypeStruct((r, cdim), F32) for _ in range(4)), grid=(r // tr,),
        in_specs=[pl.BlockSpec((N_DEV, tr, cdim), lambda i: (0, i, 0)), flat, flat, flat], out_specs=(flat,) * 4,
        compiler_params=_params(("parallel",)), name=name)(slots, w, m, v)


def _gathered(n, p):
    return p if n in COL_SHARDED else p.reshape(-1, p.shape[-1])


def _layer_fwd(l, xin, modt, wts, sm, cfg):
    bsz, tlen, d = xin.shape
    bt = bsz * tlen
    nct, tr, ctx_len, g = cfg['nct'], cfg['tr'], cfg['ctx_len'], cfg['g']
    ncc, nc = ctx_len // S5_T, tlen // S5_T
    alpha = cfg['alpha']
    nm = lambda s: f"l{l}_{s}"

    def modulate(xv, i_shift, i_scale, name):
        def fn(tv, rv, mv, seg):
            mo = mv[0]
            return (tv[0] * (1.0 + mo[i_scale:i_scale + 1]) + mo[i_shift:i_shift + 1],), ()
        return _ew(fn, [_full(xv)], mods=[modt], outs=[(d, ACT_DT)], nctx_tiles=nct, tr=tr, name=name)[0][0]

    def ln_fwd(xv, mv_, i_gate, gam, bet, name):
        def fn(tv, rv, mv, seg):
            z = alpha * tv[0] + mv[0][i_gate:i_gate + 1] * tv[1]
            mu = jnp.mean(z, axis=-1, keepdims=True)
            zc = z - mu
            var = jnp.mean(zc * zc, axis=-1, keepdims=True)
            return (zc * lax.rsqrt(var + LN_EPS) * rv[0] + rv[1],), ()
        return _ew(fn, [_full(xv), _full(mv_)], rows=[gam.reshape(1, d), bet.reshape(1, d)], mods=[modt],
                   outs=[(d, F32)], nctx_tiles=nct, tr=tr, name=name)[0][0]

    sv = {'x': xin}
    u = modulate(xin, 0, 1, nm("mod1"))
    sv['u'] = u
    z3 = _mm(u.reshape(bt, d), wts['w_in'][l], b_blocked=True, name=nm("w_in")).reshape(bsz, tlen, 3 * d)
    sv['z3'] = z3
    xc = _conv_fwd(z3, sm['conv_w'][l], sm['conv_b'][l], d, nct, tr, nm("conv"))
    sv['xc'] = xc
    hs = []
    for dd in range(2):
        names, lay = (MLP_W, l) if dd == 0 else (MIX_W if l + 1 < cfg['depth'] else (), l + 1)
        h, got = _rg_fwd(xc, sm['rg_lambda'][l, dd], sm['rg_wa'][l, dd], sm['rg_ba'][l, dd], sm['rg_wi'][l, dd],
                         sm['rg_bi'][l, dd], bool(dd), nct, tr, nm(f"rg_fwd{dd}"),
                         ride=([cfg['wloc'][n][lay] for n in names], True) if names else None)
        hs.append(h)
        for n, p in zip(names, got):
            wts[n][lay] = _gathered(n, p)
    sv['hf'], sv['hb'] = hs

    def copy_fn(tv, rv, mv, seg):
        return (tv[0],), ()
    s5u = _to_chunks(_ew(copy_fn, [(z3, 2, d, 0)], outs=[(d, ACT_DT)], tr=tr, name=nm("s5_in"))[0][0], ctx_len, g)
    sv['s5u'] = s5u
    y, hf5, hb5 = _s5_fwd(s5u, cfg['s5_ops'][l], cfg['s5_lm'][l], ncc, nc, nm("s5_fwd"))
    sv['hf5'], sv['hb5'] = hf5, hb5
    ytok = _from_chunks(y, ctx_len, g)
    sv['ytok'] = ytok

    def gelu_fn(tv, rv, mv, seg):
        return (_gelu(tv[0].astype(F32)),), ()
    gact = _ew(gelu_fn, [_full(ytok)], outs=[(d, ACT_DT)], tr=tr, name=nm("s5_gelu"))[0][0]
    sv['gact'] = gact
    gpre = _mm(gact.reshape(bt, d), wts['s5_glu_w'][l], bias=sm['s5_glu_b'][l], name=nm("glu")).reshape(bsz, tlen, d)
    sv['gpre'] = gpre

    def mix_fn(tv, rv, mv, seg):
        rg = (tv[0] + tv[1]) * _gelu(tv[2])
        s5o = tv[3].astype(F32) * _sigmoid(tv[4])
        return (jnp.concatenate([rg, s5o], axis=1),), ()
    mixin = _ew(mix_fn, [_full(hs[0]), _full(hs[1]), (z3, 1, d, 0), _full(gact), _full(gpre)], outs=[(2 * d, ACT_DT)],
                tr=tr, name=nm("mix"))[0][0].reshape(bt, 2 * d)
    sv['mixin'] = mixin
    mo = _mm(mixin, wts['w_out'][l], bias=sm['b_out'][l], name=nm("w_out")).reshape(bsz, tlen, d)
    sv['mo'] = mo
    x1 = ln_fwd(xin, mo, 2, sm['ln1_g'][l], sm['ln1_b'][l], nm("ln1"))
    sv['x1'] = x1
    u2 = modulate(x1, 3, 4, nm("mod2"))
    sv['u2'] = u2
    rl = _mm(u2.reshape(bt, d), wts['mlp_w1'][l], b_blocked=True, bias=sm['mlp_b1'][l],
             epi=lambda r: jnp.maximum(r, 0.0), out_dtype=ACT_DT, name=nm("mlp1"))
    sv['rl'] = rl
    fo = _mm(rl, wts['mlp_w2'][l], bias=sm['mlp_b2'][l], a_fn=lambda v: v * v, name=nm("mlp2")).reshape(bsz, tlen, d)
    sv['fo'] = fo
    x2 = ln_fwd(x1, fo, 5, sm['ln2_g'][l], sm['ln2_b'][l], nm("ln2"))
    return x2, sv


def _layer_bwd(l, dx2, sv, modt, wts, sm, cfg, pending):
    bsz, tlen, d = dx2.shape
    bt = bsz * tlen
    nct, tr, ctx_len, g = cfg['nct'], cfg['tr'], cfg['ctx_len'], cfg['g']
    ncc, nc = ctx_len // S5_T, tlen // S5_T
    alpha = cfg['alpha']
    nm = lambda s: f"l{l}_{s}"
    gr = {}

    def ln_bwd(xv, mv_, i_gate, gam, dy, name):
        def fn(tv, rv, mv, seg):
            xx, mm_, dyy = tv
            gate = mv[0][i_gate:i_gate + 1]
            z = alpha * xx + gate * mm_
            mu = jnp.mean(z, axis=-1, keepdims=True)
            zc = z - mu
            var = jnp.mean(zc * zc, axis=-1, keepdims=True)
            rstd = lax.rsqrt(var + LN_EPS)
            xhat = zc * rstd
            dxh = dyy * rv[0]
            dz = rstd * (dxh - jnp.mean(dxh, axis=-1, keepdims=True) - xhat * jnp.mean(dxh * xhat, axis=-1, keepdims=True))
            dm = gate * dz
            acc = jnp.concatenate([_colsum(dz * mm_), _colsum(dyy * xhat), _colsum(dyy), _colsum(dm)], axis=0)
            return (alpha * dz, dm), (acc,)
        o, a = _ew(fn, [_full(xv), _full(mv_), _full(dy)], rows=[gam.reshape(1, d)], mods=[modt],
                   outs=[(d, F32), (d, ACT_DT)], accs=[(4, d)], nctx_tiles=nct, tr=tr, name=name)
        return o[0], o[1], a[0]

    def mod_bwd(du, xv, i_scale, addend, name):
        def fn(tv, rv, mv, seg):
            duu, xx, add = tv
            acc = jnp.concatenate([_colsum(duu), _colsum(duu * xx)], axis=0)
            return (add + duu * (1.0 + mv[0][i_scale:i_scale + 1]),), (acc,)
        o, a = _ew(fn, [_full(du), _full(xv), _full(addend)], mods=[modt], outs=[(d, F32)], accs=[(2, d)],
                   nctx_tiles=nct, tr=tr, name=name)
        return o[0], a[0]

    def row_blocks(gw):
        return gw.reshape(N_DEV, -1, gw.shape[-1])

    dx1a, dfo, acc2 = ln_bwd(sv['x1'], sv['fo'], 5, sm['ln2_g'][l], dx2, nm("ln2_bwd"))
    dfo2 = dfo.reshape(bt, d)
    dhp, db1 = _mm(dfo2, wts['mlp_w2'][l], tb=True, epi=lambda r, rl: r * (2.0 * rl.astype(F32)), extras=[sv['rl']],
                   colsum=True, out_dtype=ACT_DT, name=nm("mlp2_dx"))
    gr['mlp_b1'] = db1[0]
    gr['mlp_w2'] = row_blocks(_mm(sv['rl'], dfo2, ta=True, a_fn=lambda v: v * v, out_dtype=WIRE_DT, name=nm("mlp2_dw")))
    du2 = _mm(dhp, wts['mlp_w1'][l], tb=True, b_blocked=True, name=nm("mlp1_dx")).reshape(bsz, tlen, d)
    gr['mlp_w1'] = _mm(sv['u2'].reshape(bt, d), dhp, ta=True, out_blocks=N_DEV, out_dtype=WIRE_DT, name=nm("mlp1_dw"))
    dx1, accm2 = mod_bwd(du2, sv['x1'], 4, dx1a, nm("mod2_bwd"))
    gr['ln2_g'] = acc2[:, :, 1].sum((0, 1))
    gr['ln2_b'] = acc2[:, :, 2].sum((0, 1))
    gr['mlp_b2'] = acc2[:, :, 3].sum((0, 1))

    dxa, dmo, acc1 = ln_bwd(sv['x'], sv['mo'], 2, sm['ln1_g'][l], dx1, nm("ln1_bwd"))
    gr['ln1_g'] = acc1[:, :, 1].sum((0, 1))
    gr['ln1_b'] = acc1[:, :, 2].sum((0, 1))
    gr['b_out'] = acc1[:, :, 3].sum((0, 1))
    dmo2 = dmo.reshape(bt, d)
    dmix = _mm(dmo2, wts['w_out'][l], tb=True, name=nm("w_out_dx")).reshape(bsz, tlen, 2 * d)
    gr['w_out'] = row_blocks(_mm(sv['mixin'], dmo2, ta=True, out_dtype=WIRE_DT, name=nm("w_out_dw")))

    def glu_bwd(tv, rv, mv, seg):
        ds, ga, gp = tv
        ga = ga.astype(F32)
        sg = _sigmoid(gp)
        dg = ds * ga * sg * (1.0 - sg)
        return (dg, ds * sg), (_colsum(dg),)
    o, a = _ew(glu_bwd, [(dmix, 1, d, 0), _full(sv['gact']), _full(sv['gpre'])], outs=[(d, ACT_DT), (d, F32)],
               accs=[(1, d)], tr=tr, name=nm("glu_bwd"))
    dgp, t1 = o
    gr['s5_glu_b'] = a[0][:, 1, 0].sum(0)
    dgp2 = dgp.reshape(bt, d)
    dyt = _mm(dgp2, wts['s5_glu_w'][l], tb=True, epi=lambda r, t1v, yv: (r + t1v) * _gelu_grad(yv.astype(F32)),
              extras=[t1.reshape(bt, d), sv['ytok'].reshape(bt, d)], out_dtype=ACT_DT, tn=512,
              name=nm("glu_dx")).reshape(bsz, tlen, d)
    gr['s5_glu_w'] = row_blocks(_mm(sv['gact'].reshape(bt, d), dgp2, ta=True, out_dtype=WIRE_DT, name=nm("glu_dw")))
    du5, dops = _s5_bwd(_to_chunks(dyt, ctx_len, g), sv['s5u'], sv['hf5'], sv['hb5'], cfg['s5_ops_t'][l],
                        cfg['s5_lm'][l], ncc, nc, nm("s5_bwd"))
    ds5u = _from_chunks(du5, ctx_len, g)
    gr['s5_dops'] = dops

    z3 = sv['z3']
    dxc = None
    for dd in range(2):
        riders = (pending + [(n, l, gr.pop(n)) for n in MLP_W]) if dd == 0 else [(n, l, gr.pop(n)) for n in MIX_W[1:]]
        dxc, dwa, dwi, dv, got = _rg_bwd(sv['xc'], sv['hf'] if dd == 0 else sv['hb'], dmix, z3, sm['rg_lambda'][l, dd],
                                         sm['rg_wa'][l, dd], sm['rg_ba'][l, dd], sm['rg_wi'][l, dd],
                                         sm['rg_bi'][l, dd], dxc, bool(dd), nct, tr, nm(f"rg_bwd{dd}"),
                                         ride=([r[2] for r in riders], False))
        for (n, lay, _), p in zip(riders, got):
            cfg['slots'][n][lay] = p
        gr[f'rg_wa{dd}'], gr[f'rg_wi{dd}'] = dwa, dwi
        gr[f'rg_lambda{dd}'] = dv[0] * (-_sigmoid(-sm['rg_lambda'][l, dd]))
        gr[f'rg_ba{dd}'], gr[f'rg_bi{dd}'] = dv[1], dv[2]
    drgx, accc = _conv_bwd(dxc, z3, sm['conv_w'][l], d, nct, tr, nm("conv_bwd"))
    gr['conv_w'] = accc[:, 0:4].sum(0)
    gr['conv_b'] = accc[:, 4].sum(0)

    def dz_fn(tv, rv, mv, seg):
        dgate = tv[0] * (tv[1] + tv[2]) * _gelu_grad(tv[3])
        return (jnp.concatenate([tv[4], dgate, tv[5].astype(F32)], axis=1),), ()
    dz = _ew(dz_fn, [(dmix, 0, d, 0), _full(sv['hf']), _full(sv['hb']), (z3, 1, d, 0), _full(drgx), _full(ds5u)],
             outs=[(3 * d, ACT_DT)], tr=tr, name=nm("dz"))[0][0].reshape(bt, 3 * d)
    du = _mm(dz, wts['w_in'][l], tb=True, b_blocked=True, name=nm("w_in_dx")).reshape(bsz, tlen, d)
    still = [('w_in', l, _mm(sv['u'].reshape(bt, d), dz, ta=True, out_blocks=N_DEV, out_dtype=WIRE_DT, name=nm("w_in_dw")))]
    dxin, accm1 = mod_bwd(du, sv['x'], 1, dxa, nm("mod1_bwd"))
    dmod = jnp.stack([accm1[:, :, 0], accm1[:, :, 1], acc1[:, :, 0], accm2[:, :, 0], accm2[:, :, 1], acc2[:, :, 0]], axis=2)
    return dxin, dmod, gr, still


def kernel(x, c, ctx, c_ctx, ada_w, ada_b, ln1_g, ln1_b, w_in, conv_w, conv_b, rg_lambda, rg_wa, rg_ba, rg_wi, rg_bi, s5_a_re, s5_a_im, s5_log_dt, s5_b_re, s5_b_im, s5_c_re, s5_c_im, s5_d, s5_glu_w, s5_glu_b, w_out, b_out, ln2_g, ln2_b, mlp_w1, mlp_b1, mlp_w2, mlp_b2, loss_target, m_c_ctx, m_ada_w, m_ada_b, m_ln1_g, m_ln1_b, m_w_in, m_conv_w, m_conv_b, m_rg_lambda, m_rg_wa, m_rg_ba, m_rg_wi, m_rg_bi, m_s5_a_re, m_s5_a_im, m_s5_log_dt, m_s5_b_re, m_s5_b_im, m_s5_c_re, m_s5_c_im, m_s5_d, m_s5_glu_w, m_s5_glu_b, m_w_out, m_b_out, m_ln2_g, m_ln2_b, m_mlp_w1, m_mlp_b1, m_mlp_w2, m_mlp_b2, v_c_ctx, v_ada_w, v_ada_b, v_ln1_g, v_ln1_b, v_w_in, v_conv_w, v_conv_b, v_rg_lambda, v_rg_wa, v_rg_ba, v_rg_wi, v_rg_bi, v_s5_a_re, v_s5_a_im, v_s5_log_dt, v_s5_b_re, v_s5_b_im, v_s5_c_re, v_s5_c_im, v_s5_d, v_s5_glu_w, v_s5_glu_b, v_w_out, v_b_out, v_ln2_g, v_ln2_b, v_mlp_w1, v_mlp_b1, v_mlp_w2, v_mlp_b2):
    loc = dict(locals())
    w = {n: loc[n] for n in WEIGHTS}
    mom = {n: loc["m_" + n] for n in WEIGHTS}
    vel = {n: loc["v_" + n] for n in WEIGHTS}
    bsz, seq, d = x.shape
    ctx_len = ctx.shape[1]
    depth = ada_w.shape[0]
    g = s5_a_re.shape[2]
    nmod = ada_b.shape[1] // d
    modc = ada_w.shape[2]
    tr = _tile(ctx_len, ROW_TILE, 8)
    cfg = dict(nct=ctx_len // tr, tr=tr, ctx_len=ctx_len, g=g, alpha=(2.0 * depth) ** 0.25)
    xi, yi, ci = _me()
    me = 4 * xi + 2 * yi + ci

    small_shapes = [c.shape] + [w[n].shape for n in CHAN_SHARDED]
    got = _exchange([_pack([c] + [w[n] for n in CHAN_SHARDED], F32)], True, "gather_small")[0]
    parts = _unpack(got, small_shapes, lead=1)
    c_all = parts[0].reshape(N_DEV * bsz, d)
    sm = {n: w[n] for n in WEIGHTS if n not in BIG and n not in CHAN_SHARDED and n != 'ada_w'}
    for n, p in zip(CHAN_SHARDED, parts[1:]):
        sm[n] = jnp.moveaxis(p, 0, -2).reshape(p.shape[1:-1] + (-1,))

    a_ext = jnp.concatenate([c_all, jnp.broadcast_to(c_ctx[None], (N_DEV, d))], axis=0)
    nrow = a_ext.shape[0]
    my_ada_b = lax.dynamic_slice_in_dim(ada_b, me * modc, modc, axis=1)
    mod_cols = jnp.stack([_mm(a_ext, ada_w[l], bias=my_ada_b[l], a_fn=_silu, name=f"l{l}_ada") for l in range(depth)])
    mod_all = _exchange([mod_cols.reshape(depth * nrow, modc)], True, "gather_mod")[0]
    mod_all = mod_all.reshape(N_DEV, depth, nrow, modc).transpose(1, 2, 0, 3).reshape(depth, nrow, nmod, d)
    mod_mine = lax.dynamic_slice_in_dim(mod_all, me * bsz, bsz, axis=1)
    mod_ctx = jnp.broadcast_to(mod_all[:, N_DEV * bsz][:, None], mod_mine.shape)
    modts = jnp.stack([mod_ctx, mod_mine], axis=2)

    cfg['depth'] = depth
    cfg['wloc'] = {n: [w[n][l].astype(WIRE_DT) for l in range(depth)] for n in BIG}
    wts = {n: [None] * depth for n in BIG}
    for n, p in zip(MIX_W, _exchange([cfg['wloc'][n][0] for n in MIX_W], True, "gather_w0")):
        wts[n][0] = _gathered(n, p)

    fold = lambda t: jnp.moveaxis(t, 0, 1).reshape((2, depth * g) + t.shape[3:])
    s5_in = [fold(w[n]) for n in S5_NAMES[:7]] + [s5_d.reshape(depth * g, -1)]
    ops_all, s5_vjp = jax.vjp(_s5_operators, *s5_in)
    lay = lambda t, l, ax=0: lax.slice_in_dim(t, l * g, (l + 1) * g, axis=ax)
    cfg['s5_ops'] = [tuple(lay(o, l).astype(MXU_DT) for o in ops_all[:5]) for l in range(depth)]
    cfg['s5_ops_t'] = [tuple(jnp.swapaxes(lay(o, l), 1, 2).astype(MXU_DT) for o in ops_all[:5]) for l in range(depth)]
    cfg['s5_lm'] = [lay(ops_all[5], l, 1) for l in range(depth)]

    act = jnp.concatenate([ctx, x], axis=1)
    saved = []
    for l in range(depth):
        act, sv = _layer_fwd(l, act, modts[l], wts, sm, cfg)
        saved.append(sv)

    def loss_fn(tv, rv, mv, seg):
        err = tv[0] - tv[1]
        keep = jnp.where(seg == 1, 1.0, 0.0)
        return (err * (keep / d),), (_colsum(err * err) * keep,)
    o, a = _ew(loss_fn, [_full(act), (loss_target, 0, d, cfg['nct'])], outs=[(d, F32)], accs=[(1, d)],
               nctx_tiles=cfg['nct'], tr=tr, name="loss")
    dact = o[0]
    loss = lax.psum(0.5 * jnp.sum(a[0][:, 1]) / d, ("x", "y", "c"))

    grads = [None] * depth
    dmods = [None] * depth
    slots = cfg['slots'] = {n: [None] * depth for n in BIG}
    pending = []
    for l in reversed(range(depth)):
        dact, dmods[l], grads[l], pending = _layer_bwd(l, dact, saved[l], modts[l], wts, sm, cfg, pending)
    for (n, lay, _), p in zip(pending, _exchange([r[2] for r in pending], False, "scatter_last")):
        slots[n][lay] = p
    grad_x = dact[:, ctx_len:]

    dmod = jnp.stack(dmods)
    mine = jnp.concatenate([dmod[:, :, 1].reshape(depth, bsz, nmod * d),
                            dmod[:, :, 0].sum(1).reshape(depth, 1, nmod * d)], axis=1)
    got = _exchange([mine.reshape(depth * (bsz + 1), nmod * d)], True, "gather_dmod")[0]
    got = got.reshape(N_DEV, depth, bsz + 1, nmod * d)
    dmod_rows = jnp.concatenate([got[:, :, :bsz].transpose(1, 0, 2, 3).reshape(depth, N_DEV * bsz, nmod * d),
                                 got[:, :, bsz].transpose(1, 0, 2)], axis=1)
    dmod_cols = lax.dynamic_slice_in_dim(dmod_rows, me * modc, modc, axis=2)
    g_ada_w = jnp.stack([_mm(a_ext, dmod_cols[l], ta=True, a_fn=_silu, name=f"l{l}_ada_dw") for l in range(depth)])

    def rowsum_fn(tv, rv, mv, seg):
        return (), (_colsum(tv[0]),)
    g_ada_b = _ew(rowsum_fn, [_full(dmod_rows)], accs=[(1, nmod * d)], name="ada_db")[1][0][:, 1, 0]
    dsilu = 0.0
    for l in range(depth):
        dsilu = dsilu + _mm(dmod_cols[l, N_DEV * bsz:], ada_w[l], tb=True, name=f"l{l}_ada_dc").sum(0)
    sig = _sigmoid(c_ctx)
    g_c_ctx_part = dsilu * (sig * (1.0 + c_ctx * (1.0 - sig)))

    def both(name, l):
        return jnp.stack([grads[l][f'{name}{dd}'] for dd in range(2)])
    rep = {n: jnp.stack([grads[l][n] for l in range(depth)]) for n in
           ('ln1_g', 'ln1_b', 'conv_w', 'conv_b', 's5_glu_b', 'b_out', 'ln2_g', 'ln2_b', 'mlp_b1', 'mlp_b2')}
    for n in ('rg_lambda', 'rg_wa', 'rg_ba', 'rg_wi', 'rg_bi'):
        rep[n] = jnp.stack([both(n, l) for l in range(depth)])
    rep['c_ctx'] = g_c_ctx_part
    dops_all = tuple(jnp.concatenate([grads[l]['s5_dops'][i] for l in range(depth)], axis=1 if i == 5 else 0)
                     for i in range(6))
    s5g = s5_vjp(dops_all)
    for n, val in zip(S5_NAMES[:7], s5g):
        rep[n] = jnp.moveaxis(val.reshape((2, depth, g) + val.shape[2:]), 0, 1)
    rep['s5_d'] = s5g[7].reshape(depth, -1)
    small_names = [n for n in WEIGHTS if n in rep and math.prod(rep[n].shape) <= SMALL_PARAM]
    large_names = [n for n in WEIGHTS if n in rep and math.prod(rep[n].shape) > SMALL_PARAM]
    g_rep = {}
    bufs = []
    for names in (small_names, large_names):
        buf = _pack([rep[n] for n in names], F32)
        bufs.append(jnp.pad(buf, ((0, -buf.shape[0] % (8 * N_DEV)), (0, 0))).reshape(N_DEV, -1, PACK_W))
    parts = [_sum_slots(p, f"sum_rep{i}") for i, p in enumerate(_exchange(bufs, False, "scatter_rep"))]
    for names, full in zip((small_names, large_names), _exchange(parts, True, "gather_rep")):
        g_rep.update(zip(names, _unpack(full.reshape(-1, PACK_W), [rep[n].shape for n in names])))

    grad, delta, new_m, new_v = {}, {}, {}, {}
    for n in BIG:
        res = [_sum_adamw(slots[n][l], w[n][l], mom[n][l], vel[n][l], f"l{l}_adamw_{n}") for l in range(depth)]
        grad[n], delta[n], new_m[n], new_v[n] = [jnp.stack([r[i] for r in res]) for i in range(4)]
    grad['ada_w'] = g_ada_w
    rest = [n for n in WEIGHTS if n not in BIG and n != 'ada_w']
    for n in rest:
        if n == 'ada_b':
            grad[n] = g_ada_b
        elif n in CHAN_SHARDED:
            grad[n] = lax.dynamic_slice_in_dim(g_rep[n], me * w[n].shape[-1], w[n].shape[-1], axis=g_rep[n].ndim - 1)
        else:
            grad[n] = g_rep[n].reshape(w[n].shape)
    flat2 = lambda t: t.reshape(-1, t.shape[-1])
    for n in ('ada_w', 'rg_wa', 'rg_wi'):
        res = _adamw(flat2(w[n]), flat2(grad[n]), flat2(mom[n]), flat2(vel[n]), f"adamw_{n}")
        delta[n], new_m[n], new_v[n] = [r.reshape(w[n].shape) for r in res]
    for tag, names in (("small", [n for n in rest if math.prod(w[n].shape) <= SMALL_PARAM]),
                       ("s5", [n for n in rest if math.prod(w[n].shape) > SMALL_PARAM and n not in ('rg_wa', 'rg_wi')])):
        if not names:
            continue
        shapes = [w[n].shape for n in names]
        packed = [_pack([src[n] for n in names], F32) for src in (w, grad, mom, vel)]
        for dst, o in zip((delta, new_m, new_v), _adamw(*packed, name=f"adamw_{tag}")):
            dst.update(zip(names, _unpack(o, shapes)))
    return (loss, grad_x, *[grad[n] for n in WEIGHTS], *[delta[n] for n in WEIGHTS], *[new_m[n] for n in WEIGHTS],
            *[new_v[n] for n in WEIGHTS])
```

```python
import functools
import math

import jax
import jax.numpy as jnp
import numpy as np
from jax import lax
from jax.experimental import pallas as pl
from jax.experimental.pallas import tpu as pltpu

F32 = jnp.float32
MXU_DT = jnp.bfloat16
ACT_DT = jnp.bfloat16
WIRE_DT = jnp.bfloat16

N_DEV = 8
GRID_W = 64
RG_C = 8.0
LN_EPS = 1e-5
S5_T = 16
S5_GB = 8
ROW_TILE = 256
VMEM_LIMIT = 56 * 1024 * 1024
PACK_W = 1024
SMALL_PARAM = 65536

ADAM_LR = 0.001
ADAM_B1 = 0.9
ADAM_B2 = 0.999
ADAM_EPS = 1e-08
ADAM_WD = 0.01
ADAM_STEP = 10

WEIGHTS = ['c_ctx', 'ada_w', 'ada_b', 'ln1_g', 'ln1_b', 'w_in', 'conv_w', 'conv_b', 'rg_lambda', 'rg_wa', 'rg_ba',
           'rg_wi', 'rg_bi', 's5_a_re', 's5_a_im', 's5_log_dt', 's5_b_re', 's5_b_im', 's5_c_re', 's5_c_im', 's5_d',
           's5_glu_w', 's5_glu_b', 'w_out', 'b_out', 'ln2_g', 'ln2_b', 'mlp_w1', 'mlp_b1', 'mlp_w2', 'mlp_b2']
COL_SHARDED = ('w_in', 'mlp_w1')
MIX_W = ('w_in', 's5_glu_w', 'w_out')
MLP_W = ('mlp_w1', 'mlp_w2')
BIG = MIX_W + MLP_W
CHAN_SHARDED = ('conv_w', 'rg_lambda', 'rg_ba', 'rg_bi')
S5_NAMES = ('s5_a_re', 's5_a_im', 's5_log_dt', 's5_b_re', 's5_b_im', 's5_c_re', 's5_c_im', 's5_d')
MESH = pl.DeviceIdType.MESH


def _tile(n, pref, align):
    t = (min(pref, n) // align) * align
    while t >= align:
        if n % t == 0:
            return t
        t -= align
    return n


def _params(sem):
    return pltpu.CompilerParams(dimension_semantics=sem, vmem_limit_bytes=VMEM_LIMIT)


def _sigmoid(v):
    return 0.5 * jnp.tanh(0.5 * v) + 0.5


def _silu(v):
    return v * _sigmoid(v)


_GELU_K = math.sqrt(2.0 / math.pi)


def _gelu(v):
    return 0.5 * v * (1.0 + jnp.tanh(_GELU_K * (v + 0.044715 * v * v * v)))


def _gelu_grad(v):
    th = jnp.tanh(_GELU_K * (v + 0.044715 * v * v * v))
    return 0.5 * (1.0 + th) + 0.5 * v * (1.0 - th * th) * _GELU_K * (1.0 + 3.0 * 0.044715 * v * v)


def _one_minus_sq(la, a):
    v = 2.0 * la
    series = -v * (1.0 + v * (0.5 + v * (1.0 / 6.0 + v * (1.0 / 24.0 + v * (1.0 / 120.0)))))
    return jnp.where(v > -0.1, series, 1.0 - a * a)


def _softplus(v):
    return jnp.maximum(v, 0.0) + jnp.log(1.0 + jnp.exp(-jnp.abs(v)))


def _dot(a, b):
    return jnp.dot(a.astype(MXU_DT), b.astype(MXU_DT), preferred_element_type=F32)


def _dot_tn(a, b):
    return lax.dot_general(a.astype(MXU_DT), b.astype(MXU_DT), (((0,), (0,)), ((), ())), preferred_element_type=F32)


def _dot_nt(a, b):
    return lax.dot_general(a.astype(MXU_DT), b.astype(MXU_DT), (((1,), (1,)), ((), ())), preferred_element_type=F32)


def _mm(a, b, *, ta=False, tb=False, b_blocked=False, out_blocks=0, bias=None, a_fn=None, epi=None, extras=(),
        colsum=False, out_dtype=F32, name, tm=1088, tn=1024, tk=1024):
    if ta:
        kdim, m = a.shape
    else:
        m, kdim = a.shape
    bcol = b.shape[2] if b_blocked else None
    blog = (b.shape[1], b.shape[0] * b.shape[2]) if b_blocked else b.shape
    if tb:
        n, kb = blog
    else:
        kb, n = blog
    assert kdim == kb, (a.shape, b.shape, ta, tb)
    assert not (ta and tb)
    tm = _tile(m, tm, 128 if ta else 16)
    n_lim = bcol if (b_blocked and not tb) else (n // out_blocks if out_blocks else n)
    tn = _tile(n_lim, tn, 128)
    tk = _tile(bcol if (b_blocked and tb) else kdim, tk, 16 if ta else 128)
    nk = kdim // tk
    a_spec = pl.BlockSpec((tk, tm), lambda i, j, k: (k, i)) if ta else pl.BlockSpec((tm, tk), lambda i, j, k: (i, k))
    if not b_blocked:
        b_spec = pl.BlockSpec((tn, tk), lambda i, j, k: (j, k)) if tb else pl.BlockSpec((tk, tn), lambda i, j, k: (k, j))
    elif tb:
        qk = bcol // tk
        b_spec = pl.BlockSpec((None, tn, tk), lambda i, j, k: (k // qk, j, k % qk))
    else:
        qn = bcol // tn
        b_spec = pl.BlockSpec((None, tk, tn), lambda i, j, k: (j // qn, k, j % qn))
    in_specs = [a_spec, b_spec]
    args = [a, b]
    has_bias = bias is not None
    if has_bias:
        in_specs.append(pl.BlockSpec((1, tn), lambda i, j, k: (0, j)))
        args.append(bias.reshape(1, n).astype(F32))
    for e in extras:
        assert e.shape == (m, n), (e.shape, m, n)
        in_specs.append(pl.BlockSpec((tm, tn), lambda i, j, k: (i, j)))
        args.append(e)
    nex = len(extras)
    dn = (((0 if ta else 1,), (1 if tb else 0,)), ((), ()))
    if out_blocks:
        qo = (n // out_blocks) // tn
        out_shape = jax.ShapeDtypeStruct((out_blocks, m, n // out_blocks), out_dtype)
        out_spec = pl.BlockSpec((None, tm, tn), lambda i, j, k: (j // qo, i, j % qo))
    else:
        out_shape = jax.ShapeDtypeStruct((m, n), out_dtype)
        out_spec = pl.BlockSpec((tm, tn), lambda i, j, k: (i, j))
    out_shapes, out_specs = [out_shape], [out_spec]
    grid = (m // tm, n // tn, nk)
    if colsum:
        out_shapes.append(jax.ShapeDtypeStruct((1, n), F32))
        out_specs.append(pl.BlockSpec((1, tn), lambda i, j, k: (0, j)))
        swap = lambda sp: pl.BlockSpec(sp.block_shape, lambda j, i, k, f=sp.index_map: f(i, j, k))
        in_specs = [swap(sp) for sp in in_specs]
        out_specs = [swap(sp) for sp in out_specs]
        grid = (n // tn, m // tm, nk)
    nout = len(out_shapes)

    def body(*refs):
        a_ref, b_ref = refs[0], refs[1]
        row_tile = pl.program_id(1)
        pos = 2
        bias_ref = refs[pos] if has_bias else None
        pos += int(has_bias)
        ex_refs = refs[pos:pos + nex]
        o_ref = refs[pos + nex]
        av = a_ref[...]
        if a_fn is not None:
            av = a_fn(av.astype(F32))
        part = lax.dot_general(av.astype(MXU_DT), b_ref[...].astype(MXU_DT), dn, preferred_element_type=F32)

        def finish(r):
            if has_bias:
                r = r + bias_ref[...]
            if epi is not None:
                r = epi(r, *[e[...] for e in ex_refs])
            o_ref[...] = r.astype(out_dtype)
            if colsum:
                cs_ref = refs[pos + nex + 1]

                @pl.when(row_tile == 0)
                def _():
                    cs_ref[...] = _colsum(r)

                @pl.when(row_tile > 0)
                def _():
                    cs_ref[...] += _colsum(r)

        if nk == 1:
            finish(part)
            return
        acc_ref = refs[pos + nex + nout]
        k = pl.program_id(2)

        @pl.when(k == 0)
        def _():
            acc_ref[...] = part

        @pl.when(k > 0)
        def _():
            acc_ref[...] += part

        @pl.when(k == nk - 1)
        def _():
            finish(acc_ref[...])

    res = pl.pallas_call(
        body, out_shape=tuple(out_shapes), grid=grid, in_specs=in_specs, out_specs=tuple(out_specs),
        scratch_shapes=[pltpu.VMEM((tm, tn), F32)] if nk > 1 else [],
        compiler_params=_params(("parallel", "arbitrary" if colsum else "parallel", "arbitrary")), name=name)(*args)
    return res if colsum else res[0]


def _ew(fn, tiles, rows=(), mods=(), outs=(), accs=(), *, name, nctx_tiles=0, tr=None):
    bsz, tlen = tiles[0][0].shape[0], tiles[0][0].shape[1]
    if tr is None:
        tr = _tile(tlen, ROW_TILE, 8)
    nt = tlen // tr
    in_specs, args = [], []
    for arr, cb, width, toff in tiles:
        in_specs.append(pl.BlockSpec((1, tr, width), functools.partial(
            lambda b, t, cb, toff: (b, jnp.maximum(t - toff, 0), cb), cb=cb, toff=toff)))
        args.append(arr)
    for r in rows:
        in_specs.append(pl.BlockSpec(r.shape, lambda b, t: (0, 0)))
        args.append(r)

    def seg_of(t):
        return jnp.where(t >= nctx_tiles, 1, 0)

    for mo in mods:
        in_specs.append(pl.BlockSpec((1, 1) + mo.shape[2:], lambda b, t: (b, seg_of(t), 0, 0)))
        args.append(mo)
    out_shape, out_specs = [], []
    for width, dt in outs:
        out_shape.append(jax.ShapeDtypeStruct((bsz, tlen, width), dt))
        out_specs.append(pl.BlockSpec((1, tr, width), lambda b, t: (b, t, 0)))
    for kk, cc in accs:
        out_shape.append(jax.ShapeDtypeStruct((bsz, 2, kk, cc), F32))
        out_specs.append(pl.BlockSpec((1, 1, kk, cc), lambda b, t: (b, seg_of(t), 0, 0)))
    nti, nr, nm, no, na = len(tiles), len(rows), len(mods), len(outs), len(accs)

    def body(*refs):
        t = pl.program_id(1)
        tv = [r[0] for r in refs[:nti]]
        rv = [r[...] for r in refs[nti:nti + nr]]
        mv = [r[0, 0] for r in refs[nti + nr:nti + nr + nm]]
        o_refs = refs[nti + nr + nm:nti + nr + nm + no]
        a_refs = refs[nti + nr + nm + no:]
        seg = seg_of(t)
        ov, av = fn(tv, rv, mv, seg)
        for r, v in zip(o_refs, ov):
            r[0] = v.astype(r.dtype)
        if na:
            @pl.when((t == 0) | (t == nctx_tiles))
            def _():
                for r in a_refs:
                    r[...] = jnp.zeros_like(r)

            for r, v in zip(a_refs, av):
                r[0, 0] += v

    res = pl.pallas_call(
        body, out_shape=tuple(out_shape), grid=(bsz, nt), in_specs=in_specs, out_specs=tuple(out_specs),
        compiler_params=_params(("arbitrary", "arbitrary")), name=name)(*args)
    return res[:no], res[no:]


def _full(arr):
    return (arr, 0, arr.shape[-1], 0)


def _colsum(v):
    return jnp.sum(v, axis=0, keepdims=True)


def _shifted(x, prev8, next8, first, last, k):
    tr = x.shape[0]
    rid = lax.broadcasted_iota(jnp.int32, x.shape, 0)
    keep_prev = jnp.where(first, 0.0, 1.0)
    keep_next = jnp.where(last, 0.0, 1.0)
    if k == -1:
        return jnp.where(rid == 0, prev8[7:8] * keep_prev, pltpu.roll(x, 1, 0))
    if k == -2:
        r = pltpu.roll(x, 2, 0)
        r = jnp.where(rid == 1, prev8[7:8] * keep_prev, r)
        return jnp.where(rid == 0, prev8[6:7] * keep_prev, r)
    if k == 1:
        return jnp.where(rid == tr - 1, next8[0:1] * keep_next, pltpu.roll(x, tr - 1, 0))
    if k == 2:
        r = pltpu.roll(x, tr - 2, 0)
        r = jnp.where(rid == tr - 2, next8[0:1] * keep_next, r)
        return jnp.where(rid == tr - 1, next8[1:2] * keep_next, r)
    raise ValueError(k)


def _halo_specs(tr, width, cb, n8):
    cur = pl.BlockSpec((1, tr, width), lambda b, t: (b, t, cb))
    prev = pl.BlockSpec((1, 8, width), lambda b, t: (b, jnp.maximum(t * (tr // 8) - 1, 0), cb))
    nxt = pl.BlockSpec((1, 8, width), lambda b, t: (b, jnp.minimum((t + 1) * (tr // 8), n8 - 1), cb))
    return [cur, prev, nxt]


def _conv_fwd(z3, conv_w, conv_b, d, nctx_tiles, tr, name):
    bsz, tlen, _ = z3.shape
    nt = tlen // tr

    def body(x_ref, xp_ref, xn_ref, w_ref, b_ref, o_ref):
        t = pl.program_id(1)
        first = (t == 0) | (t == nctx_tiles)
        last = (t == nctx_tiles - 1) | (t == nt - 1)
        x, p8, n8 = x_ref[0], xp_ref[0], xn_ref[0]
        w = w_ref[...]
        o_ref[0] = (b_ref[...] + w[0:1] * _shifted(x, p8, n8, first, last, -1) + w[1:2] * x
                    + w[2:3] * _shifted(x, p8, n8, first, last, 1) + w[3:4] * _shifted(x, p8, n8, first, last, 2))

    return pl.pallas_call(
        body, out_shape=jax.ShapeDtypeStruct((bsz, tlen, d), F32), grid=(bsz, nt),
        in_specs=_halo_specs(tr, d, 0, tlen // 8) + [pl.BlockSpec((4, d), lambda b, t: (0, 0)),
                                                      pl.BlockSpec((1, d), lambda b, t: (0, 0))],
        out_specs=pl.BlockSpec((1, tr, d), lambda b, t: (b, t, 0)),
        compiler_params=_params(("parallel", "parallel")), name=name)(z3, z3, z3, conv_w, conv_b.reshape(1, d))


def _conv_bwd(dxc, z3, conv_w, d, nctx_tiles, tr, name):
    bsz, tlen, _ = dxc.shape
    nt = tlen // tr

    def body(g_ref, gp_ref, gn_ref, x_ref, xp_ref, xn_ref, w_ref, o_ref, acc_ref):
        t = pl.program_id(1)
        first = (t == 0) | (t == nctx_tiles)
        last = (t == nctx_tiles - 1) | (t == nt - 1)
        g, gp, gn = g_ref[0], gp_ref[0], gn_ref[0]
        x, xp, xn = x_ref[0], xp_ref[0], xn_ref[0]
        w = w_ref[...]
        o_ref[0] = (w[0:1] * _shifted(g, gp, gn, first, last, 1) + w[1:2] * g
                    + w[2:3] * _shifted(g, gp, gn, first, last, -1) + w[3:4] * _shifted(g, gp, gn, first, last, -2))

        @pl.when(t == 0)
        def _():
            acc_ref[...] = jnp.zeros_like(acc_ref)

        acc_ref[0, 0:1] += _colsum(g * _shifted(x, xp, xn, first, last, -1))
        acc_ref[0, 1:2] += _colsum(g * x)
        acc_ref[0, 2:3] += _colsum(g * _shifted(x, xp, xn, first, last, 1))
        acc_ref[0, 3:4] += _colsum(g * _shifted(x, xp, xn, first, last, 2))
        acc_ref[0, 4:5] += _colsum(g)

    return pl.pallas_call(
        body, out_shape=(jax.ShapeDtypeStruct((bsz, tlen, d), F32), jax.ShapeDtypeStruct((bsz, 8, d), F32)),
        grid=(bsz, nt),
        in_specs=_halo_specs(tr, d, 0, tlen // 8) + _halo_specs(tr, d, 0, tlen // 8)
        + [pl.BlockSpec((4, d), lambda b, t: (0, 0))],
        out_specs=(pl.BlockSpec((1, tr, d), lambda b, t: (b, t, 0)), pl.BlockSpec((1, 8, d), lambda b, t: (b, 0, 0))),
        compiler_params=_params(("arbitrary", "arbitrary")), name=name)(dxc, dxc, dxc, z3, z3, z3, conv_w)


def _rg_gates(x, lam, wa_ref, ba, wi_ref, bi, nh, hd):
    sp = _softplus(-lam)
    prs, pis = [], []
    for h in range(nh):
        xh = x[:, h * hd:(h + 1) * hd]
        prs.append(_dot(xh, wa_ref[h]))
        pis.append(_dot(xh, wi_ref[h]))
    r = 1.0 / (1.0 + jnp.exp(-(jnp.concatenate(prs, axis=1) + ba)))
    i = _sigmoid(jnp.concatenate(pis, axis=1) + bi)
    la = -RG_C * sp * r
    a = jnp.exp(la)
    m2 = _one_minus_sq(la, a)
    return sp, r, i, a, jnp.sqrt(m2), lax.rsqrt(m2)


def _scan_tile(t, nctx_tiles, nt, reverse):
    if not reverse:
        return t
    return jnp.where(t < nctx_tiles, nctx_tiles - 1 - t, nt - 1 - (t - nctx_tiles))


def _unscan_tile(t, nctx_tiles, nt, reverse):
    if not reverse:
        return nt - 1 - t
    return jnp.where(t < nt - nctx_tiles, nctx_tiles + t, t - (nt - nctx_tiles))


def _rg_param_specs(d, nh, hd):
    vec = pl.BlockSpec((1, d), lambda b, t: (0, 0))
    mat = pl.BlockSpec((nh, hd, hd), lambda b, t: (0, 0, 0))
    return [vec, mat, vec, mat, vec]


def _ride_split(refs, n_in, n_ride, n_out, n_scratch):
    pos = [n_in, n_in + n_ride, n_in + n_ride + n_out, n_in + 2 * n_ride + n_out, n_in + 2 * n_ride + n_out + n_scratch]
    return (refs[:pos[0]], refs[pos[0]:pos[1]], refs[pos[1]:pos[2]], refs[pos[2]:pos[3]], refs[pos[3]:pos[4]],
            refs[pos[4]:])


def _rg_fwd(xc, lam, wa, ba, wi, bi, reverse, nctx_tiles, tr, name, ride=None):
    bsz, tlen, d = xc.shape
    nh, hd = wa.shape[0], wa.shape[1]
    nt = tlen // tr
    tmap = lambda b, t: (b, _scan_tile(t, nctx_tiles, nt, reverse), 0)
    rarrs, rgather = ride if ride else ([], True)
    nr = len(rarrs)

    def body(*refs):
        (x_ref, lam_ref, wa_ref, ba_ref, wi_ref, bi_ref), rin, (h_ref,), rout, (a_scr, b_scr, carry), sems = _ride_split(
            refs, 6, nr, 1, 3)
        first = (pl.program_id(0) == 0) & (pl.program_id(1) == 0)
        last = (pl.program_id(0) == bsz - 1) & (pl.program_id(1) == nt - 1)
        if nr:
            @pl.when(first)
            def _():
                _exchange_start(rin, rout, sems, rgather)

        @pl.when(pl.program_id(1) == 0)
        def _():
            carry[...] = jnp.zeros_like(carry)

        x = x_ref[0]
        _, _, i, a, mult, _ = _rg_gates(x, lam_ref[...], wa_ref, ba_ref[...], wi_ref, bi_ref[...], nh, hd)
        a_scr[...] = a
        b_scr[...] = mult * (i * x)

        def blk(j, h):
            for r in range(8):
                row = (tr - 1 - (j * 8 + r)) if reverse else (j * 8 + r)
                h = a_scr[pl.ds(row, 1), :] * h + b_scr[pl.ds(row, 1), :]
                h_ref[0, pl.ds(row, 1), :] = h
            return h

        carry[0:1, :] = lax.fori_loop(0, tr // 8, blk, carry[0:1, :])
        if nr:
            @pl.when(last)
            def _():
                _exchange_finish(rin, rout, sems, rgather)

    res = pl.pallas_call(
        body, out_shape=(jax.ShapeDtypeStruct((bsz, tlen, d), F32),) + _exchange_out_shapes(rarrs), grid=(bsz, nt),
        in_specs=[pl.BlockSpec((1, tr, d), tmap)] + _rg_param_specs(d, nh, hd) + [_ANY] * nr,
        out_specs=(pl.BlockSpec((1, tr, d), tmap),) + (_ANY,) * nr,
        scratch_shapes=[pltpu.VMEM((tr, d), F32), pltpu.VMEM((tr, d), F32), pltpu.VMEM((8, d), F32)]
        + (_exchange_sems(nr) if nr else []),
        compiler_params=_params(("arbitrary", "arbitrary")), name=name)(
            xc, lam.reshape(1, d), wa, ba.reshape(1, d), wi, bi.reshape(1, d), *rarrs)
    return res[0], list(res[1:])


def _rg_bwd(xc, h, dmix, z3, lam, wa, ba, wi, bi, addend, reverse, nctx_tiles, tr, name, ride=None):
    bsz, tlen, d = xc.shape
    nh, hd = wa.shape[0], wa.shape[1]
    nt = tlen // tr
    r8 = tr // 8

    def tile_of(t):
        return _unscan_tile(t, nctx_tiles, nt, reverse)

    tmap = lambda b, t: (b, tile_of(t), 0)
    gmap = lambda b, t: (b, tile_of(t), 1)

    def halo_map(b, t):
        tt = tile_of(t)
        if not reverse:
            return (b, jnp.maximum(tt * r8 - 1, 0), 0)
        return (b, jnp.where(tt == nt - 1, 0, (tt + 1) * r8), 0)

    has_add = addend is not None
    rarrs, rgather = ride if ride else ([], True)
    nr = len(rarrs)

    def body(*refs):
        ins, rin, (dx_ref, dwa_ref, dwi_ref, dv_ref), rout, (a_scr, g_scr, dh_scr, carry), sems = _ride_split(
            refs, 10 + int(has_add), nr, 4, 4)
        x_ref, h_ref, halo_ref, dr_ref, z_ref, lam_ref, wa_ref, ba_ref, wi_ref, bi_ref = ins[:10]
        add_ref = ins[10] if has_add else None
        b = pl.program_id(0)
        t = pl.program_id(1)
        tt = tile_of(t)
        if nr:
            @pl.when((b == 0) & (t == 0))
            def _():
                _exchange_start(rin, rout, sems, rgather)

        @pl.when((b == 0) & (t == 0))
        def _():
            dwa_ref[...] = jnp.zeros_like(dwa_ref)
            dwi_ref[...] = jnp.zeros_like(dwi_ref)
            dv_ref[...] = jnp.zeros_like(dv_ref)

        @pl.when(t == 0)
        def _():
            carry[...] = jnp.zeros_like(carry)

        x = x_ref[0]
        sp, r, i, a, mult, inv_mult = _rg_gates(x, lam_ref[...], wa_ref, ba_ref[...], wi_ref, bi_ref[...], nh, hd)
        a_scr[...] = a
        dh_scr[...] = dr_ref[0] * _gelu(z_ref[0])

        def blk(j, cc):
            for rr in range(8):
                row = (j * 8 + rr) if reverse else (tr - 1 - (j * 8 + rr))
                g = dh_scr[pl.ds(row, 1), :] + cc
                g_scr[pl.ds(row, 1), :] = g
                cc = a_scr[pl.ds(row, 1), :] * g
            return cc

        carry[0:1, :] = lax.fori_loop(0, r8, blk, carry[0:1, :])
        g = g_scr[...]
        hv = h_ref[0]
        rid = lax.broadcasted_iota(jnp.int32, hv.shape, 0)
        if not reverse:
            valid = jnp.where(tt > 0, 1.0, 0.0)
            hprev = jnp.where(rid == 0, halo_ref[0][7:8] * valid, pltpu.roll(hv, 1, 0))
        else:
            valid = jnp.where(tt == nctx_tiles - 1, 0.0, 1.0)
            hprev = jnp.where(rid == tr - 1, halo_ref[0][0:1] * valid, pltpu.roll(hv, tr - 1, 0))
        dla = g * hprev * a - g * (i * x) * (a * a) * inv_mult
        dpr = dla * (-RG_C * sp) * r * (1.0 - r)
        dpi = g * mult * x * i * (1.0 - i)
        dx = g * mult * i
        dxs = []
        for hh in range(nh):
            sl = slice(hh * hd, (hh + 1) * hd)
            dxs.append(_dot_nt(dpr[:, sl], wa_ref[hh]) + _dot_nt(dpi[:, sl], wi_ref[hh]))
            dwa_ref[hh] += _dot_tn(x[:, sl], dpr[:, sl])
            dwi_ref[hh] += _dot_tn(x[:, sl], dpi[:, sl])
        dx = dx + jnp.concatenate(dxs, axis=1)
        if has_add:
            dx = dx + add_ref[0]
        dx_ref[0] = dx
        dv_ref[0:1] += _colsum(dla * (-RG_C * r))
        dv_ref[1:2] += _colsum(dpr)
        dv_ref[2:3] += _colsum(dpi)
        if nr:
            @pl.when((b == bsz - 1) & (t == nt - 1))
            def _():
                _exchange_finish(rin, rout, sems, rgather)

    in_specs = [pl.BlockSpec((1, tr, d), tmap), pl.BlockSpec((1, tr, d), tmap), pl.BlockSpec((1, 8, d), halo_map),
                pl.BlockSpec((1, tr, d), tmap), pl.BlockSpec((1, tr, d), gmap)] + _rg_param_specs(d, nh, hd)
    args = [xc, h, h, dmix, z3, lam.reshape(1, d), wa, ba.reshape(1, d), wi, bi.reshape(1, d)]
    if has_add:
        in_specs.append(pl.BlockSpec((1, tr, d), tmap))
        args.append(addend)
    mat = pl.BlockSpec((nh, hd, hd), lambda b, t: (0, 0, 0))
    res = pl.pallas_call(
        body, out_shape=(jax.ShapeDtypeStruct((bsz, tlen, d), F32), jax.ShapeDtypeStruct((nh, hd, hd), F32),
                         jax.ShapeDtypeStruct((nh, hd, hd), F32), jax.ShapeDtypeStruct((8, d), F32))
        + _exchange_out_shapes(rarrs),
        grid=(bsz, nt), in_specs=in_specs + [_ANY] * nr,
        out_specs=(pl.BlockSpec((1, tr, d), tmap), mat, mat, pl.BlockSpec((8, d), lambda b, t: (0, 0))) + (_ANY,) * nr,
        scratch_shapes=[pltpu.VMEM((tr, d), F32), pltpu.VMEM((tr, d), F32), pltpu.VMEM((tr, d), F32),
                        pltpu.VMEM((8, d), F32)] + (_exchange_sems(nr) if nr else []),
        compiler_params=_params(("arbitrary", "arbitrary")), name=name)(*args, *rarrs)
    return res[0], res[1], res[2], res[3], list(res[4:])


def _s5_operators(a_re, a_im, log_dt, b_re, b_im, c_re, c_im, d_skip):
    tt = S5_T
    g, h = d_skip.shape
    th = tt * h
    dt = jnp.exp(log_dt)[..., None]
    xr, xi = a_re * dt, a_im * dt
    taus = jnp.arange(tt + 1, dtype=F32)[None, None, :, None]
    mag = jnp.exp(xr[:, :, None, :] * taus)
    pw_re, pw_im = mag * jnp.cos(xi[:, :, None, :] * taus), mag * jnp.sin(xi[:, :, None, :] * taus)
    nr, ni = pw_re[:, :, 1] - 1.0, pw_im[:, :, 1]
    den = a_re * a_re + a_im * a_im
    cf_re, cf_im = (nr * a_re + ni * a_im) / den, (ni * a_re - nr * a_im) / den
    bt_re, bt_im = jnp.swapaxes(b_re, 2, 3), jnp.swapaxes(b_im, 2, 3)
    bb_re = cf_re[:, :, None] * bt_re - cf_im[:, :, None] * bt_im
    bb_im = cf_re[:, :, None] * bt_im + cf_im[:, :, None] * bt_re

    def outer(p_re, p_im, q_re, q_im):
        pr, pi = p_re[:, :, None, :], p_im[:, :, None, :]
        qr, qi = q_re[:, None], q_im[:, None]
        return (pr * qr - pi * qi).reshape(g, -1, pr.shape[-1]), (pr * qi + pi * qr).reshape(g, -1, pr.shape[-1])

    ops, kerns = [], []
    for dd in range(2):
        e_re, e_im = outer(pw_re[dd], pw_im[dd], c_re[dd], c_im[dd])
        kerns.append(jnp.einsum('gic,gmc->gim', jnp.concatenate([bb_re[dd], -bb_im[dd]], -1),
                                jnp.concatenate([e_re[:, :th], e_im[:, :th]], -1), precision=lax.Precision.HIGHEST))
        v_re, v_im = e_re[:, h:].reshape(g, tt, h, -1), e_im[:, h:].reshape(g, tt, h, -1)
        if dd == 1:
            v_re, v_im = v_re[:, ::-1], v_im[:, ::-1]
        vt = jnp.concatenate([v_re, -v_im], -1).reshape(g, th, -1)
        pr, pi = pw_re[dd][:, :tt], pw_im[dd][:, :tt]
        if dd == 0:
            pr, pi = pr[:, ::-1], pi[:, ::-1]
        w_re, w_im = outer(pr, pi, bb_re[dd], bb_im[dd])
        ops.append((jnp.concatenate([w_re, w_im], -1), jnp.swapaxes(vt, 1, 2)))
    kb_rev = kerns[1].reshape(g, h, tt, h)[:, :, ::-1].reshape(g, h, th)
    blocks = []
    for s in range(tt):
        fwd = jnp.pad(kerns[0][:, :, :th - s * h], ((0, 0), (0, 0), (s * h, 0)))
        bwd = jnp.pad(kb_rev[:, :, (tt - 1 - s) * h:], ((0, 0), (0, 0), (0, (tt - 1 - s) * h)))
        blocks.append(fwd + bwd)
    tz = jnp.stack(blocks, axis=1).reshape(g, th, th)
    tz = tz + jnp.eye(th, dtype=F32)[None] * jnp.tile(d_skip, (1, tt))[:, None, :]
    rows = []
    for dd in range(2):
        re, im = pw_re[dd][:, tt], pw_im[dd][:, tt]
        rows += [jnp.concatenate([re, re], -1), jnp.concatenate([-im, im], -1)]
    return tz, ops[0][0], ops[1][0], ops[0][1], ops[1][1], jnp.stack(rows)


def _chunk_order(j, ncc, nc):
    return jnp.where(j < ncc, ncc - 1 - j, nc - 1 - (j - ncc))


def _s5_specs(nc, ops, lm):
    gb = S5_GB
    act = lambda n_chunks, width: pl.BlockSpec((1, gb * n_chunks, width), lambda i, b: (b, i, 0))
    opspecs = [pl.BlockSpec((gb,) + o.shape[1:], lambda i, b: (i, 0, 0)) for o in ops]
    lspec = pl.BlockSpec((4, gb, lm.shape[-1]), lambda i, b: (0, i, 0))
    return act, opspecs, lspec


def _rows2(c_ref, l_ref, gi, ncc, ncl):
    return jnp.concatenate([c_ref[0, pl.ds(gi * ncc, ncc), :], l_ref[0, pl.ds(gi * ncl, ncl), :]], axis=0)


def _s5_fwd(uc, ul, ops, lm, name, ride=None):
    bsz, _, th = uc.shape
    g = ops[0].shape[0]
    ncc, ncl = uc.shape[1] // g, ul.shape[1] // g
    nc = ncc + ncl
    gb = S5_GB
    p2 = ops[1].shape[-1]
    ph = p2 // 2
    rarrs, rgather = ride if ride else ([], True)
    nr = len(rarrs)

    def body(*refs):
        ((uc_ref, ul_ref, tz_ref, wf_ref, wb_ref, vf_ref, vb_ref, l_ref), rin, (yc_ref, yl_ref, hf_ref, hb_ref), rout,
         (sf, sfs, sb, sbs), sems) = _ride_split(refs, 8, nr, 4, 4)
        if nr:
            @pl.when((pl.program_id(0) == 0) & (pl.program_id(1) == 0))
            def _():
                _exchange_start(rin, rout, sems, rgather)

        a1f, a2f, a1b, a2b = l_ref[0], l_ref[1], l_ref[2], l_ref[3]
        for gi in range(gb):
            ug = _rows2(uc_ref, ul_ref, gi, ncc, ncl)
            s1 = _dot(ug, wf_ref[gi])
            s2 = _dot(ug, wb_ref[gi])
            sf[pl.ds(gi, nc, stride=gb), :] = s1
            sfs[pl.ds(gi, nc, stride=gb), :] = pltpu.roll(s1, ph, 1)
            sb[pl.ds(gi, nc, stride=gb), :] = s2
            sbs[pl.ds(gi, nc, stride=gb), :] = pltpu.roll(s2, ph, 1)

        def step(j, hs):
            hf, hfs, hb, hbs = hs
            rf = pl.ds(pl.multiple_of(j * gb, gb), gb)
            rb = pl.ds(pl.multiple_of(_chunk_order(j, ncc, nc) * gb, gb), gb)
            s1, s1s, s2, s2s = sf[rf, :], sfs[rf, :], sb[rb, :], sbs[rb, :]
            sf[rf, :] = hf
            sb[rb, :] = hb
            return (a1f * hf + a2f * hfs + s1, a1f * hfs - a2f * hf + s1s,
                    a1b * hb + a2b * hbs + s2, a1b * hbs - a2b * hb + s2s)

        zero = jnp.zeros((gb, p2), F32)
        lax.fori_loop(0, nc, step, (zero, zero, zero, zero))
        for gi in range(gb):
            rows = pl.ds(gi * nc, nc)
            hfg = sf[pl.ds(gi, nc, stride=gb), :]
            hbg = sb[pl.ds(gi, nc, stride=gb), :]
            hf_ref[0, rows, :] = hfg
            hb_ref[0, rows, :] = hbg
            yg = (_dot(_rows2(uc_ref, ul_ref, gi, ncc, ncl), tz_ref[gi]) + _dot(hfg, vf_ref[gi])
                  + _dot(hbg, vb_ref[gi])).astype(yc_ref.dtype)
            yc_ref[0, pl.ds(gi * ncc, ncc), :] = yg[:ncc]
            yl_ref[0, pl.ds(gi * ncl, ncl), :] = yg[ncc:]
        if nr:
            @pl.when((pl.program_id(0) == g // gb - 1) & (pl.program_id(1) == bsz - 1))
            def _():
                _exchange_finish(rin, rout, sems, rgather)

    act, opspecs, lspec = _s5_specs(nc, ops, lm)
    res = pl.pallas_call(
        body, out_shape=(jax.ShapeDtypeStruct(uc.shape, ACT_DT), jax.ShapeDtypeStruct(ul.shape, ACT_DT),
                         jax.ShapeDtypeStruct((bsz, g * nc, p2), F32), jax.ShapeDtypeStruct((bsz, g * nc, p2), F32))
        + _exchange_out_shapes(rarrs),
        grid=(g // gb, bsz), in_specs=[act(ncc, th), act(ncl, th)] + opspecs + [lspec] + [_ANY] * nr,
        out_specs=(act(ncc, th), act(ncl, th), act(nc, p2), act(nc, p2)) + (_ANY,) * nr,
        scratch_shapes=[pltpu.VMEM((gb * nc, p2), F32) for _ in range(4)] + (_exchange_sems(nr) if nr else []),
        compiler_params=_params(("arbitrary", "arbitrary")), name=name)(uc, ul, *ops, lm, *rarrs)
    return res[0], res[1], res[2], res[3], list(res[4:])


def _s5_bwd(dyc, dyl, uc, ul, hf, hb, ops_t, lm, name):
    bsz, _, th = uc.shape
    g = ops_t[0].shape[0]
    ncc, ncl = uc.shape[1] // g, ul.shape[1] // g
    nc = ncc + ncl
    gb = S5_GB
    p2 = lm.shape[-1]
    ph = p2 // 2

    def body(dyc_ref, dyl_ref, uc_ref, ul_ref, hf_ref, hb_ref, tzt_ref, wft_ref, wbt_ref, vft_ref, vbt_ref, l_ref,
             duc_ref, dul_ref, dtz_ref, dwf_ref, dwb_ref, dvf_ref, dvb_ref, dl_ref, jf, jfs, jb, jbs, hfk, hbk):
        a1f, a2f, a1b, a2b = l_ref[0], l_ref[1], l_ref[2], l_ref[3]

        @pl.when(pl.program_id(1) == 0)
        def _():
            for r in (dtz_ref, dwf_ref, dwb_ref, dvf_ref, dvb_ref, dl_ref):
                r[...] = jnp.zeros_like(r)

        for gi in range(gb):
            rows = pl.ds(gi * nc, nc)
            dyg = _rows2(dyc_ref, dyl_ref, gi, ncc, ncl)
            i1 = _dot(dyg, vft_ref[gi])
            i2 = _dot(dyg, vbt_ref[gi])
            jf[pl.ds(gi, nc, stride=gb), :] = i1
            jfs[pl.ds(gi, nc, stride=gb), :] = pltpu.roll(i1, ph, 1)
            jb[pl.ds(gi, nc, stride=gb), :] = i2
            jbs[pl.ds(gi, nc, stride=gb), :] = pltpu.roll(i2, ph, 1)
            hfk[pl.ds(gi, nc, stride=gb), :] = hf_ref[0, rows, :]
            hbk[pl.ds(gi, nc, stride=gb), :] = hb_ref[0, rows, :]

        def step(j, carry):
            qf, qfs, qb, qbs, d1f, d2f, d1b, d2b = carry
            rf = pl.ds(pl.multiple_of((nc - 1 - j) * gb, gb), gb)
            rb = pl.ds(pl.multiple_of(_chunk_order(nc - 1 - j, ncc, nc) * gb, gb), gb)
            i1, i1s, i2, i2s = jf[rf, :], jfs[rf, :], jb[rb, :], jbs[rb, :]
            h1, h2 = hfk[rf, :], hbk[rb, :]
            jf[rf, :] = qf
            jb[rb, :] = qb
            return (i1 + a1f * qf - a2f * qfs, i1s + a1f * qfs + a2f * qf,
                    i2 + a1b * qb - a2b * qbs, i2s + a1b * qbs + a2b * qb,
                    d1f + qf * h1, d2f + qfs * h1, d1b + qb * h2, d2b + qbs * h2)

        zero = jnp.zeros((gb, p2), F32)
        fin = lax.fori_loop(0, nc, step, (zero,) * 8)
        dl_ref[0] += fin[4]
        dl_ref[1] += pltpu.roll(fin[5], ph, 1)
        dl_ref[2] += fin[6]
        dl_ref[3] += pltpu.roll(fin[7], ph, 1)
        for gi in range(gb):
            rows = pl.ds(gi * nc, nc)
            dyg = _rows2(dyc_ref, dyl_ref, gi, ncc, ncl)
            ug = _rows2(uc_ref, ul_ref, gi, ncc, ncl)
            dsf = jf[pl.ds(gi, nc, stride=gb), :]
            dsb = jb[pl.ds(gi, nc, stride=gb), :]
            dug = (_dot(dyg, tzt_ref[gi]) + _dot(dsf, wft_ref[gi]) + _dot(dsb, wbt_ref[gi])).astype(duc_ref.dtype)
            duc_ref[0, pl.ds(gi * ncc, ncc), :] = dug[:ncc]
            dul_ref[0, pl.ds(gi * ncl, ncl), :] = dug[ncc:]
            dtz_ref[gi] += _dot_tn(ug, dyg)
            dwf_ref[gi] += _dot_tn(ug, dsf)
            dwb_ref[gi] += _dot_tn(ug, dsb)
            dvf_ref[gi] += _dot_tn(hf_ref[0, rows, :], dyg)
            dvb_ref[gi] += _dot_tn(hb_ref[0, rows, :], dyg)

    act, opspecs, lspec = _s5_specs(nc, ops_t, lm)
    gshape = lambda o: jax.ShapeDtypeStruct(o.shape[:1] + o.shape[1:][::-1], F32)
    gspec = lambda o: pl.BlockSpec((gb,) + o.shape[1:][::-1], lambda i, b: (i, 0, 0))
    res = pl.pallas_call(
        body, out_shape=tuple([jax.ShapeDtypeStruct(uc.shape, ACT_DT), jax.ShapeDtypeStruct(ul.shape, ACT_DT)]
                              + [gshape(o) for o in ops_t] + [jax.ShapeDtypeStruct(lm.shape, F32)]),
        grid=(g // gb, bsz),
        in_specs=[act(ncc, th), act(ncl, th), act(ncc, th), act(ncl, th), act(nc, p2), act(nc, p2)] + opspecs + [lspec],
        out_specs=tuple([act(ncc, th), act(ncl, th)] + [gspec(o) for o in ops_t] + [lspec]),
        scratch_shapes=[pltpu.VMEM((gb * nc, p2), F32) for _ in range(6)],
        compiler_params=_params(("parallel", "arbitrary")), name=name)(dyc, dyl, uc, ul, hf, hb, *ops_t, lm)
    return res[0], res[1], tuple(res[2:])


def _to_chunks(s, ctx_len, g):
    bsz, tlen, d = s.shape
    h = d // g
    seq = tlen - ctx_len
    rows = seq // GRID_W
    cpart = s[:, :ctx_len].reshape(bsz, ctx_len // S5_T, S5_T, g, h).transpose(0, 3, 1, 2, 4)
    lpart = s[:, ctx_len:].reshape(bsz, rows, GRID_W, g, h).transpose(0, 3, 2, 1, 4)
    return (cpart.reshape(bsz, g * (ctx_len // S5_T), S5_T * h), lpart.reshape(bsz, g * (seq // S5_T), S5_T * h))


def _from_chunks(yc, yl, g):
    bsz, _, th = yc.shape
    h = th // S5_T
    ncc, ncl = yc.shape[1] // g, yl.shape[1] // g
    rows = ncl * S5_T // GRID_W
    cpart = yc.reshape(bsz, g, ncc, S5_T, h).transpose(0, 2, 3, 1, 4).reshape(bsz, ncc * S5_T, g * h)
    lpart = yl.reshape(bsz, g, GRID_W, rows, h).transpose(0, 3, 2, 1, 4).reshape(bsz, ncl * S5_T, g * h)
    return jnp.concatenate([cpart, lpart], axis=1)


def _me():
    return lax.axis_index("x"), lax.axis_index("y"), lax.axis_index("c")


def _peer(k):
    x, y, c = _me()
    px = (1 - x) if (k & 4) else x
    py = (1 - y) if (k & 2) else y
    pc = (1 - c) if (k & 1) else c
    return (px, py, pc), 4 * px + 2 * py + pc


def _exchange(arrs, gather, name):
    n = len(arrs)

    def body(*refs):
        _exchange_start(refs[:n], refs[n:2 * n], refs[2 * n:], gather)
        _exchange_finish(refs[:n], refs[n:2 * n], refs[2 * n:], gather)

    return pl.pallas_call(
        body, out_shape=_exchange_out_shapes(arrs), in_specs=[_ANY] * n, out_specs=tuple([_ANY] * n),
        scratch_shapes=_exchange_sems(n), name=name)(*arrs)


_ANY = pl.BlockSpec(memory_space=pl.ANY)


def _exchange_out_shapes(arrs):
    return tuple(jax.ShapeDtypeStruct((N_DEV,) + a.shape[-2:], a.dtype) for a in arrs)


def _exchange_sems(n):
    return [pltpu.SemaphoreType.DMA((n * (N_DEV - 1),)), pltpu.SemaphoreType.DMA((n * (N_DEV - 1),)),
            pltpu.SemaphoreType.DMA((n,))]


def _exchange_copies(x_refs, o_refs, sems, gather, with_recvs):
    send_sems, recv_sems, local_sems = sems
    n = len(x_refs)
    npeer = N_DEV - 1
    xi, yi, ci = _me()
    me = 4 * xi + 2 * yi + ci
    mine = [x if gather else x.at[me] for x in x_refs]
    local = [pltpu.make_async_copy(mine[i], o_refs[i].at[me], local_sems.at[i]) for i in range(n)]
    sends, recvs = [], []
    for k in range(1, N_DEV):
        dev, pid = _peer(k)
        for i in range(n):
            slot = i * npeer + k - 1
            sends.append(pltpu.make_async_remote_copy(
                src_ref=x_refs[i] if gather else x_refs[i].at[pid], dst_ref=o_refs[i].at[me],
                send_sem=send_sems.at[slot], recv_sem=recv_sems.at[slot], device_id=dev, device_id_type=MESH))
            if with_recvs:
                recvs.append(pltpu.make_async_remote_copy(
                    src_ref=mine[i], dst_ref=o_refs[i].at[pid], send_sem=send_sems.at[slot],
                    recv_sem=recv_sems.at[slot], device_id=dev, device_id_type=MESH))
    return local, sends, recvs


def _exchange_start(x_refs, o_refs, sems, gather):
    local, sends, _ = _exchange_copies(x_refs, o_refs, sems, gather, False)
    for cp in local + sends:
        cp.start()


def _exchange_finish(x_refs, o_refs, sems, gather):
    local, sends, recvs = _exchange_copies(x_refs, o_refs, sems, gather, True)
    for cp in recvs:
        cp.wait_recv()
    for cp in sends:
        cp.wait_send()
    for cp in local:
        cp.wait()


def _sum_slots(x3, name):
    _, r, cdim = x3.shape
    tr = _tile(r, 256, 8)

    def body(x_ref, o_ref):
        acc = x_ref[0]
        for s in range(1, N_DEV):
            acc = acc + x_ref[s]
        o_ref[...] = acc

    return pl.pallas_call(
        body, out_shape=jax.ShapeDtypeStruct((r, cdim), F32), grid=(r // tr,),
        in_specs=[pl.BlockSpec((N_DEV, tr, cdim), lambda i: (0, i, 0))], out_specs=pl.BlockSpec((tr, cdim), lambda i: (i, 0)),
        compiler_params=_params(("parallel",)), name=name)(x3)


def _pack(arrs, dtype, lead=0):
    flat = jnp.concatenate([a.reshape(a.shape[:lead] + (-1,)).astype(dtype) for a in arrs], axis=-1)
    n = flat.shape[-1]
    pad = -n % (PACK_W * 16)
    flat = jnp.pad(flat, [(0, 0)] * lead + [(0, pad)])
    return flat.reshape(flat.shape[:lead] + (-1, PACK_W))


def _unpack(buf, shapes, lead=0):
    flat = buf.reshape(buf.shape[:lead] + (-1,))
    out, off = [], 0
    for shp in shapes:
        n = math.prod(shp)
        out.append(flat[..., off:off + n].reshape(buf.shape[:lead] + tuple(shp)))
        off += n
    return out


def _adamw_math(wv, gv, m0, v0):
    m1 = ADAM_B1 * m0 + (1.0 - ADAM_B1) * gv
    v1 = ADAM_B2 * v0 + (1.0 - ADAM_B2) * (gv * gv)
    m_hat = m1 / (1.0 - ADAM_B1 ** ADAM_STEP)
    v_hat = v1 / (1.0 - ADAM_B2 ** ADAM_STEP)
    delta = -ADAM_LR * (m_hat / (jnp.sqrt(v_hat) + ADAM_EPS) + ADAM_WD * wv)
    return delta, m1, v1


def _adamw(w, g, m, v, name):
    def fn(tv, rv, mv, seg):
        return _adamw_math(*tv), ()

    outs, _ = _ew(fn, [_full(a[None]) for a in (w, g, m, v)], outs=[(w.shape[-1], F32)] * 3, name=name)
    return [o[0] for o in outs]


def _sum_adamw(slots, w, m, v, name):
    _, r, cdim = slots.shape
    tr = _tile(r, 128, 16)

    def body(s_ref, w_ref, m_ref, v_ref, g_ref, d_ref, mo_ref, vo_ref):
        gv = s_ref[0].astype(F32)
        for s in range(1, N_DEV):
            gv = gv + s_ref[s].astype(F32)
        g_ref[...] = gv
        d_ref[...], mo_ref[...], vo_ref[...] = _adamw_math(w_ref[...], gv, m_ref[...], v_ref[...])

    flat = pl.BlockSpec((tr, cdim), lambda i: (i, 0))
    return pl.pallas_call(
        body, out_shape=tuple(jax.ShapeDtypeStruct((r, cdim), F32) for _ in range(4)), grid=(r // tr,),
        in_specs=[pl.BlockSpec((N_DEV, tr, cdim), lambda i: (0, i, 0)), flat, flat, flat], out_specs=(flat,) * 4,
        compiler_params=_params(("parallel",)), name=name)(slots, w, m, v)


def _gathered(n, p):
    return p if n in COL_SHARDED else p.reshape(-1, p.shape[-1])


def _layer_fwd(l, xin, modt, wts, sm, cfg):
    bsz, tlen, d = xin.shape
    bt = bsz * tlen
    nct, tr, ctx_len, g = cfg['nct'], cfg['tr'], cfg['ctx_len'], cfg['g']
    ncc, nc = ctx_len // S5_T, tlen // S5_T
    alpha = cfg['alpha']
    nm = lambda s: f"l{l}_{s}"

    def modulate(xv, i_shift, i_scale, name):
        def fn(tv, rv, mv, seg):
            mo = mv[0]
            return (tv[0] * (1.0 + mo[i_scale:i_scale + 1]) + mo[i_shift:i_shift + 1],), ()
        return _ew(fn, [_full(xv)], mods=[modt], outs=[(d, ACT_DT)], nctx_tiles=nct, tr=tr, name=name)[0][0]

    def ln_fwd(xv, mv_, i_gate, gam, bet, name):
        def fn(tv, rv, mv, seg):
            z = alpha * tv[0] + mv[0][i_gate:i_gate + 1] * tv[1]
            mu = jnp.mean(z, axis=-1, keepdims=True)
            zc = z - mu
            var = jnp.mean(zc * zc, axis=-1, keepdims=True)
            return (zc * lax.rsqrt(var + LN_EPS) * rv[0] + rv[1],), ()
        return _ew(fn, [_full(xv), _full(mv_)], rows=[gam.reshape(1, d), bet.reshape(1, d)], mods=[modt],
                   outs=[(d, F32)], nctx_tiles=nct, tr=tr, name=name)[0][0]

    sv = {'x': xin}
    u = modulate(xin, 0, 1, nm("mod1"))
    sv['u'] = u
    z3 = _mm(u.reshape(bt, d), wts['w_in'][l], b_blocked=True, name=nm("w_in")).reshape(bsz, tlen, 3 * d)
    sv['z3'] = z3
    xc = _conv_fwd(z3, sm['conv_w'][l], sm['conv_b'][l], d, nct, tr, nm("conv"))
    sv['xc'] = xc
    nxt = l + 1 < cfg['depth']
    riders = [[('mlp_w1', l)], [('mlp_w2', l)] + ([('s5_glu_w', l + 1), ('w_out', l + 1)] if nxt else []),
              [('w_in', l + 1)] if nxt else []]

    def ride_of(rs):
        return ([cfg['wloc'][n][lay] for n, lay in rs], True) if rs else None

    def landed(rs, got):
        for (n, lay), p in zip(rs, got):
            wts[n][lay] = _gathered(n, p)

    hs = []
    for dd in range(2):
        h, got = _rg_fwd(xc, sm['rg_lambda'][l, dd], sm['rg_wa'][l, dd], sm['rg_ba'][l, dd], sm['rg_wi'][l, dd],
                         sm['rg_bi'][l, dd], bool(dd), nct, tr, nm(f"rg_fwd{dd}"), ride=ride_of(riders[dd]))
        hs.append(h)
        landed(riders[dd], got)
    sv['hf'], sv['hb'] = hs

    def copy_fn(tv, rv, mv, seg):
        return (tv[0],), ()
    s5u = _to_chunks(_ew(copy_fn, [(z3, 2, d, 0)], outs=[(d, ACT_DT)], tr=tr, name=nm("s5_in"))[0][0], ctx_len, g)
    sv['s5u'] = s5u
    yc, yl, hf5, hb5, got = _s5_fwd(*s5u, cfg['s5_ops'][l], cfg['s5_lm'][l], nm("s5_fwd"), ride=ride_of(riders[2]))
    landed(riders[2], got)
    sv['hf5'], sv['hb5'] = hf5, hb5
    ytok = _from_chunks(yc, yl, g)
    sv['ytok'] = ytok

    def gelu_fn(tv, rv, mv, seg):
        return (_gelu(tv[0].astype(F32)),), ()
    gact = _ew(gelu_fn, [_full(ytok)], outs=[(d, ACT_DT)], tr=tr, name=nm("s5_gelu"))[0][0]
    sv['gact'] = gact
    gpre = _mm(gact.reshape(bt, d), wts['s5_glu_w'][l], bias=sm['s5_glu_b'][l], name=nm("glu")).reshape(bsz, tlen, d)
    sv['gpre'] = gpre

    def mix_fn(tv, rv, mv, seg):
        rg = (tv[0] + tv[1]) * _gelu(tv[2])
        s5o = tv[3].astype(F32) * _sigmoid(tv[4])
        return (jnp.concatenate([rg, s5o], axis=1),), ()
    mixin = _ew(mix_fn, [_full(hs[0]), _full(hs[1]), (z3, 1, d, 0), _full(gact), _full(gpre)], outs=[(2 * d, ACT_DT)],
                tr=tr, name=nm("mix"))[0][0].reshape(bt, 2 * d)
    sv['mixin'] = mixin
    mo = _mm(mixin, wts['w_out'][l], bias=sm['b_out'][l], name=nm("w_out")).reshape(bsz, tlen, d)
    sv['mo'] = mo
    x1 = ln_fwd(xin, mo, 2, sm['ln1_g'][l], sm['ln1_b'][l], nm("ln1"))
    sv['x1'] = x1
    u2 = modulate(x1, 3, 4, nm("mod2"))
    sv['u2'] = u2
    rl = _mm(u2.reshape(bt, d), wts['mlp_w1'][l], b_blocked=True, bias=sm['mlp_b1'][l],
             epi=lambda r: jnp.maximum(r, 0.0), out_dtype=ACT_DT, name=nm("mlp1"))
    sv['rl'] = rl
    fo = _mm(rl, wts['mlp_w2'][l], bias=sm['mlp_b2'][l], a_fn=lambda v: v * v, name=nm("mlp2")).reshape(bsz, tlen, d)
    sv['fo'] = fo
    x2 = ln_fwd(x1, fo, 5, sm['ln2_g'][l], sm['ln2_b'][l], nm("ln2"))
    return x2, sv


def _layer_bwd(l, dx2, sv, modt, wts, sm, cfg, pending):
    bsz, tlen, d = dx2.shape
    bt = bsz * tlen
    nct, tr, ctx_len, g = cfg['nct'], cfg['tr'], cfg['ctx_len'], cfg['g']
    ncc, nc = ctx_len // S5_T, tlen // S5_T
    alpha = cfg['alpha']
    nm = lambda s: f"l{l}_{s}"
    gr = {}

    def ln_bwd(xv, mv_, i_gate, gam, dy, name):
        def fn(tv, rv, mv, seg):
            xx, mm_, dyy = tv
            gate = mv[0][i_gate:i_gate + 1]
            z = alpha * xx + gate * mm_
            mu = jnp.mean(z, axis=-1, keepdims=True)
            zc = z - mu
            var = jnp.mean(zc * zc, axis=-1, keepdims=True)
            rstd = lax.rsqrt(var + LN_EPS)
            xhat = zc * rstd
            dxh = dyy * rv[0]
            dz = rstd * (dxh - jnp.mean(dxh, axis=-1, keepdims=True) - xhat * jnp.mean(dxh * xhat, axis=-1, keepdims=True))
            dm = gate * dz
            acc = jnp.concatenate([_colsum(dz * mm_), _colsum(dyy * xhat), _colsum(dyy), _colsum(dm)], axis=0)
            return (alpha * dz, dm), (acc,)
        o, a = _ew(fn, [_full(xv), _full(mv_), _full(dy)], rows=[gam.reshape(1, d)], mods=[modt],
                   outs=[(d, F32), (d, ACT_DT)], accs=[(4, d)], nctx_tiles=nct, tr=tr, name=name)
        return o[0], o[1], a[0]

    def mod_bwd(du, xv, i_scale, addend, name):
        def fn(tv, rv, mv, seg):
            duu, xx, add = tv
            acc = jnp.concatenate([_colsum(duu), _colsum(duu * xx)], axis=0)
            return (add + duu * (1.0 + mv[0][i_scale:i_scale + 1]),), (acc,)
        o, a = _ew(fn, [_full(du), _full(xv), _full(addend)], mods=[modt], outs=[(d, F32)], accs=[(2, d)],
                   nctx_tiles=nct, tr=tr, name=name)
        return o[0], a[0]

    def row_blocks(gw):
        return gw.reshape(N_DEV, -1, gw.shape[-1])

    dx1a, dfo, acc2 = ln_bwd(sv['x1'], sv['fo'], 5, sm['ln2_g'][l], dx2, nm("ln2_bwd"))
    dfo2 = dfo.reshape(bt, d)
    dhp, db1 = _mm(dfo2, wts['mlp_w2'][l], tb=True, epi=lambda r, rl: r * (2.0 * rl.astype(F32)), extras=[sv['rl']],
                   colsum=True, out_dtype=ACT_DT, name=nm("mlp2_dx"))
    gr['mlp_b1'] = db1[0]
    gr['mlp_w2'] = row_blocks(_mm(sv['rl'], dfo2, ta=True, a_fn=lambda v: v * v, out_dtype=WIRE_DT, name=nm("mlp2_dw")))
    du2 = _mm(dhp, wts['mlp_w1'][l], tb=True, b_blocked=True, name=nm("mlp1_dx")).reshape(bsz, tlen, d)
    gr['mlp_w1'] = _mm(sv['u2'].reshape(bt, d), dhp, ta=True, out_blocks=N_DEV, out_dtype=WIRE_DT, name=nm("mlp1_dw"))
    dx1, accm2 = mod_bwd(du2, sv['x1'], 4, dx1a, nm("mod2_bwd"))
    gr['ln2_g'] = acc2[:, :, 1].sum((0, 1))
    gr['ln2_b'] = acc2[:, :, 2].sum((0, 1))
    gr['mlp_b2'] = acc2[:, :, 3].sum((0, 1))

    dxa, dmo, acc1 = ln_bwd(sv['x'], sv['mo'], 2, sm['ln1_g'][l], dx1, nm("ln1_bwd"))
    gr['ln1_g'] = acc1[:, :, 1].sum((0, 1))
    gr['ln1_b'] = acc1[:, :, 2].sum((0, 1))
    gr['b_out'] = acc1[:, :, 3].sum((0, 1))
    dmo2 = dmo.reshape(bt, d)
    dmix = _mm(dmo2, wts['w_out'][l], tb=True, name=nm("w_out_dx")).reshape(bsz, tlen, 2 * d)
    gr['w_out'] = row_blocks(_mm(sv['mixin'], dmo2, ta=True, out_dtype=WIRE_DT, name=nm("w_out_dw")))

    def glu_bwd(tv, rv, mv, seg):
        ds, ga, gp = tv
        ga = ga.astype(F32)
        sg = _sigmoid(gp)
        dg = ds * ga * sg * (1.0 - sg)
        return (dg, ds * sg), (_colsum(dg),)
    o, a = _ew(glu_bwd, [(dmix, 1, d, 0), _full(sv['gact']), _full(sv['gpre'])], outs=[(d, ACT_DT), (d, F32)],
               accs=[(1, d)], tr=tr, name=nm("glu_bwd"))
    dgp, t1 = o
    gr['s5_glu_b'] = a[0][:, 1, 0].sum(0)
    dgp2 = dgp.reshape(bt, d)
    dyt = _mm(dgp2, wts['s5_glu_w'][l], tb=True, epi=lambda r, t1v, yv: (r + t1v) * _gelu_grad(yv.astype(F32)),
              extras=[t1.reshape(bt, d), sv['ytok'].reshape(bt, d)], out_dtype=ACT_DT, tn=512,
              name=nm("glu_dx")).reshape(bsz, tlen, d)
    gr['s5_glu_w'] = row_blocks(_mm(sv['gact'].reshape(bt, d), dgp2, ta=True, out_dtype=WIRE_DT, name=nm("glu_dw")))
    duc, dul, dops = _s5_bwd(*_to_chunks(dyt, ctx_len, g), *sv['s5u'], sv['hf5'], sv['hb5'], cfg['s5_ops_t'][l],
                             cfg['s5_lm'][l], nm("s5_bwd"))
    ds5u = _from_chunks(duc, dul, g)
    gr['s5_dops'] = dops

    z3 = sv['z3']
    dxc = None
    for dd in range(2):
        names = ('mlp_w2',) if dd == 0 else ('mlp_w1', 's5_glu_w', 'w_out')
        riders = (pending if dd == 0 else []) + [(n, l, gr.pop(n)) for n in names]
        dxc, dwa, dwi, dv, got = _rg_bwd(sv['xc'], sv['hf'] if dd == 0 else sv['hb'], dmix, z3, sm['rg_lambda'][l, dd],
                                         sm['rg_wa'][l, dd], sm['rg_ba'][l, dd], sm['rg_wi'][l, dd],
                                         sm['rg_bi'][l, dd], dxc, bool(dd), nct, tr, nm(f"rg_bwd{dd}"),
                                         ride=([r[2] for r in riders], False))
        for (n, lay, _), p in zip(riders, got):
            cfg['slots'][n][lay] = p
        gr[f'rg_wa{dd}'], gr[f'rg_wi{dd}'] = dwa, dwi
        gr[f'rg_lambda{dd}'] = dv[0] * (-_sigmoid(-sm['rg_lambda'][l, dd]))
        gr[f'rg_ba{dd}'], gr[f'rg_bi{dd}'] = dv[1], dv[2]
    drgx, accc = _conv_bwd(dxc, z3, sm['conv_w'][l], d, nct, tr, nm("conv_bwd"))
    gr['conv_w'] = accc[:, 0:4].sum(0)
    gr['conv_b'] = accc[:, 4].sum(0)

    def dz_fn(tv, rv, mv, seg):
        dgate = tv[0] * (tv[1] + tv[2]) * _gelu_grad(tv[3])
        return (jnp.concatenate([tv[4], dgate, tv[5].astype(F32)], axis=1),), ()
    dz = _ew(dz_fn, [(dmix, 0, d, 0), _full(sv['hf']), _full(sv['hb']), (z3, 1, d, 0), _full(drgx), _full(ds5u)],
             outs=[(3 * d, ACT_DT)], tr=tr, name=nm("dz"))[0][0].reshape(bt, 3 * d)
    du = _mm(dz, wts['w_in'][l], tb=True, b_blocked=True, name=nm("w_in_dx")).reshape(bsz, tlen, d)
    still = [('w_in', l, _mm(sv['u'].reshape(bt, d), dz, ta=True, out_blocks=N_DEV, out_dtype=WIRE_DT, name=nm("w_in_dw")))]
    dxin, accm1 = mod_bwd(du, sv['x'], 1, dxa, nm("mod1_bwd"))
    dmod = jnp.stack([accm1[:, :, 0], accm1[:, :, 1], acc1[:, :, 0], accm2[:, :, 0], accm2[:, :, 1], acc2[:, :, 0]], axis=2)
    return dxin, dmod, gr, still


def kernel(x, c, ctx, c_ctx, ada_w, ada_b, ln1_g, ln1_b, w_in, conv_w, conv_b, rg_lambda, rg_wa, rg_ba, rg_wi, rg_bi, s5_a_re, s5_a_im, s5_log_dt, s5_b_re, s5_b_im, s5_c_re, s5_c_im, s5_d, s5_glu_w, s5_glu_b, w_out, b_out, ln2_g, ln2_b, mlp_w1, mlp_b1, mlp_w2, mlp_b2, loss_target, m_c_ctx, m_ada_w, m_ada_b, m_ln1_g, m_ln1_b, m_w_in, m_conv_w, m_conv_b, m_rg_lambda, m_rg_wa, m_rg_ba, m_rg_wi, m_rg_bi, m_s5_a_re, m_s5_a_im, m_s5_log_dt, m_s5_b_re, m_s5_b_im, m_s5_c_re, m_s5_c_im, m_s5_d, m_s5_glu_w, m_s5_glu_b, m_w_out, m_b_out, m_ln2_g, m_ln2_b, m_mlp_w1, m_mlp_b1, m_mlp_w2, m_mlp_b2, v_c_ctx, v_ada_w, v_ada_b, v_ln1_g, v_ln1_b, v_w_in, v_conv_w, v_conv_b, v_rg_lambda, v_rg_wa, v_rg_ba, v_rg_wi, v_rg_bi, v_s5_a_re, v_s5_a_im, v_s5_log_dt, v_s5_b_re, v_s5_b_im, v_s5_c_re, v_s5_c_im, v_s5_d, v_s5_glu_w, v_s5_glu_b, v_w_out, v_b_out, v_ln2_g, v_ln2_b, v_mlp_w1, v_mlp_b1, v_mlp_w2, v_mlp_b2):
    loc = dict(locals())
    w = {n: loc[n] for n in WEIGHTS}
    mom = {n: loc["m_" + n] for n in WEIGHTS}
    vel = {n: loc["v_" + n] for n in WEIGHTS}
    bsz, seq, d = x.shape
    ctx_len = ctx.shape[1]
    depth = ada_w.shape[0]
    g = s5_a_re.shape[2]
    nmod = ada_b.shape[1] // d
    modc = ada_w.shape[2]
    tr = _tile(ctx_len, ROW_TILE, 8)
    cfg = dict(nct=ctx_len // tr, tr=tr, ctx_len=ctx_len, g=g, alpha=(2.0 * depth) ** 0.25)
    xi, yi, ci = _me()
    me = 4 * xi + 2 * yi + ci

    small_shapes = [c.shape] + [w[n].shape for n in CHAN_SHARDED]
    got = _exchange([_pack([c] + [w[n] for n in CHAN_SHARDED], F32)], True, "gather_small")[0]
    parts = _unpack(got, small_shapes, lead=1)
    c_all = parts[0].reshape(N_DEV * bsz, d)
    sm = {n: w[n] for n in WEIGHTS if n not in BIG and n not in CHAN_SHARDED and n != 'ada_w'}
    for n, p in zip(CHAN_SHARDED, parts[1:]):
        sm[n] = jnp.moveaxis(p, 0, -2).reshape(p.shape[1:-1] + (-1,))

    a_ext = jnp.concatenate([c_all, jnp.broadcast_to(c_ctx[None], (N_DEV, d))], axis=0)
    nrow = a_ext.shape[0]
    my_ada_b = lax.dynamic_slice_in_dim(ada_b, me * modc, modc, axis=1)
    mod_cols = jnp.stack([_mm(a_ext, ada_w[l], bias=my_ada_b[l], a_fn=_silu, name=f"l{l}_ada") for l in range(depth)])
    mod_all = _exchange([mod_cols.reshape(depth * nrow, modc)], True, "gather_mod")[0]
    mod_all = mod_all.reshape(N_DEV, depth, nrow, modc).transpose(1, 2, 0, 3).reshape(depth, nrow, nmod, d)
    mod_mine = lax.dynamic_slice_in_dim(mod_all, me * bsz, bsz, axis=1)
    mod_ctx = jnp.broadcast_to(mod_all[:, N_DEV * bsz][:, None], mod_mine.shape)
    modts = jnp.stack([mod_ctx, mod_mine], axis=2)

    cfg['depth'] = depth
    cfg['wloc'] = {n: [w[n][l].astype(WIRE_DT) for l in range(depth)] for n in BIG}
    wts = {n: [None] * depth for n in BIG}
    for n, p in zip(MIX_W, _exchange([cfg['wloc'][n][0] for n in MIX_W], True, "gather_w0")):
        wts[n][0] = _gathered(n, p)

    fold = lambda t: jnp.moveaxis(t, 0, 1).reshape((2, depth * g) + t.shape[3:])
    s5_in = [fold(w[n]) for n in S5_NAMES[:7]] + [s5_d.reshape(depth * g, -1)]
    ops_all, s5_vjp = jax.vjp(_s5_operators, *s5_in)
    lay = lambda t, l, ax=0: lax.slice_in_dim(t, l * g, (l + 1) * g, axis=ax)
    cfg['s5_ops'] = [tuple(lay(o, l).astype(MXU_DT) for o in ops_all[:5]) for l in range(depth)]
    cfg['s5_ops_t'] = [tuple(jnp.swapaxes(lay(o, l), 1, 2).astype(MXU_DT) for o in ops_all[:5]) for l in range(depth)]
    cfg['s5_lm'] = [lay(ops_all[5], l, 1) for l in range(depth)]

    act = jnp.concatenate([ctx, x], axis=1)
    saved = []
    for l in range(depth):
        act, sv = _layer_fwd(l, act, modts[l], wts, sm, cfg)
        saved.append(sv)

    def loss_fn(tv, rv, mv, seg):
        err = tv[0] - tv[1]
        keep = jnp.where(seg == 1, 1.0, 0.0)
        return (err * (keep / d),), (_colsum(err * err) * keep,)
    o, a = _ew(loss_fn, [_full(act), (loss_target, 0, d, cfg['nct'])], outs=[(d, F32)], accs=[(1, d)],
               nctx_tiles=cfg['nct'], tr=tr, name="loss")
    dact = o[0]
    loss = lax.psum(0.5 * jnp.sum(a[0][:, 1]) / d, ("x", "y", "c"))

    grads = [None] * depth
    dmods = [None] * depth
    slots = cfg['slots'] = {n: [None] * depth for n in BIG}
    pending = []
    for l in reversed(range(depth)):
        dact, dmods[l], grads[l], pending = _layer_bwd(l, dact, saved[l], modts[l], wts, sm, cfg, pending)
    for (n, lay, _), p in zip(pending, _exchange([r[2] for r in pending], False, "scatter_last")):
        slots[n][lay] = p
    grad_x = dact[:, ctx_len:]

    dmod = jnp.stack(dmods)
    mine = jnp.concatenate([dmod[:, :, 1].reshape(depth, bsz, nmod * d),
                            dmod[:, :, 0].sum(1).reshape(depth, 1, nmod * d)], axis=1)
    got = _exchange([mine.reshape(depth * (bsz + 1), nmod * d)], True, "gather_dmod")[0]
    got = got.reshape(N_DEV, depth, bsz + 1, nmod * d)
    dmod_rows = jnp.concatenate([got[:, :, :bsz].transpose(1, 0, 2, 3).reshape(depth, N_DEV * bsz, nmod * d),
                                 got[:, :, bsz].transpose(1, 0, 2)], axis=1)
    dmod_cols = lax.dynamic_slice_in_dim(dmod_rows, me * modc, modc, axis=2)
    g_ada_w = jnp.stack([_mm(a_ext, dmod_cols[l], ta=True, a_fn=_silu, name=f"l{l}_ada_dw") for l in range(depth)])

    def rowsum_fn(tv, rv, mv, seg):
        return (), (_colsum(tv[0]),)
    g_ada_b = _ew(rowsum_fn, [_full(dmod_rows)], accs=[(1, nmod * d)], name="ada_db")[1][0][:, 1, 0]
    dsilu = 0.0
    for l in range(depth):
        dsilu = dsilu + _mm(dmod_cols[l, N_DEV * bsz:], ada_w[l], tb=True, name=f"l{l}_ada_dc").sum(0)
    sig = _sigmoid(c_ctx)
    g_c_ctx_part = dsilu * (sig * (1.0 + c_ctx * (1.0 - sig)))

    def both(name, l):
        return jnp.stack([grads[l][f'{name}{dd}'] for dd in range(2)])
    rep = {n: jnp.stack([grads[l][n] for l in range(depth)]) for n in
           ('ln1_g', 'ln1_b', 'conv_w', 'conv_b', 's5_glu_b', 'b_out', 'ln2_g', 'ln2_b', 'mlp_b1', 'mlp_b2')}
    for n in ('rg_lambda', 'rg_wa', 'rg_ba', 'rg_wi', 'rg_bi'):
        rep[n] = jnp.stack([both(n, l) for l in range(depth)])
    rep['c_ctx'] = g_c_ctx_part
    dops_all = tuple(jnp.concatenate([grads[l]['s5_dops'][i] for l in range(depth)], axis=1 if i == 5 else 0)
                     for i in range(6))
    s5g = s5_vjp(dops_all)
    for n, val in zip(S5_NAMES[:7], s5g):
        rep[n] = jnp.moveaxis(val.reshape((2, depth, g) + val.shape[2:]), 0, 1)
    rep['s5_d'] = s5g[7].reshape(depth, -1)
    small_names = [n for n in WEIGHTS if n in rep and math.prod(rep[n].shape) <= SMALL_PARAM]
    large_names = [n for n in WEIGHTS if n in rep and math.prod(rep[n].shape) > SMALL_PARAM]
    g_rep = {}
    bufs = []
    for names in (small_names, large_names):
        buf = _pack([rep[n] for n in names], F32)
        bufs.append(jnp.pad(buf, ((0, -buf.shape[0] % (8 * N_DEV)), (0, 0))).reshape(N_DEV, -1, PACK_W))
    parts = [_sum_slots(p, f"sum_rep{i}") for i, p in enumerate(_exchange(bufs, False, "scatter_rep"))]
    for names, full in zip((small_names, large_names), _exchange(parts, True, "gather_rep")):
        g_rep.update(zip(names, _unpack(full.reshape(-1, PACK_W), [rep[n].shape for n in names])))

    grad, delta, new_m, new_v = {}, {}, {}, {}
    for n in BIG:
        res = [_sum_adamw(slots[n][l], w[n][l], mom[n][l], vel[n][l], f"l{l}_adamw_{n}") for l in range(depth)]
        grad[n], delta[n], new_m[n], new_v[n] = [jnp.stack([r[i] for r in res]) for i in range(4)]
    grad['ada_w'] = g_ada_w
    rest = [n for n in WEIGHTS if n not in BIG and n != 'ada_w']
    for n in rest:
        if n == 'ada_b':
            grad[n] = g_ada_b
        elif n in CHAN_SHARDED:
            grad[n] = lax.dynamic_slice_in_dim(g_rep[n], me * w[n].shape[-1], w[n].shape[-1], axis=g_rep[n].ndim - 1)
        else:
            grad[n] = g_rep[n].reshape(w[n].shape)
    flat2 = lambda t: t.reshape(-1, t.shape[-1])
    for n in ('ada_w', 'rg_wa', 'rg_wi'):
        res = _adamw(flat2(w[n]), flat2(grad[n]), flat2(mom[n]), flat2(vel[n]), f"adamw_{n}")
        delta[n], new_m[n], new_v[n] = [r.reshape(w[n].shape) for r in res]
    for tag, names in (("small", [n for n in rest if math.prod(w[n].shape) <= SMALL_PARAM]),
                       ("s5", [n for n in rest if math.prod(w[n].shape) > SMALL_PARAM and n not in ('rg_wa', 'rg_wi')])):
        if not names:
            continue
        shapes = [w[n].shape for n in names]
        packed = [_pack([src[n] for n in names], F32) for src in (w, grad, mom, vel)]
        for dst, o in zip((delta, new_m, new_v), _adamw(*packed, name=f"adamw_{tag}")):
            dst.update(zip(names, _unpack(o, shapes)))
    return (loss, grad_x, *[grad[n] for n in WEIGHTS], *[delta[n] for n in WEIGHTS], *[new_m[n] for n in WEIGHTS],
            *[new_v[n] for n in WEIGHTS])
```

```python
import functools
import math

import jax
import jax.numpy as jnp
import numpy as np
from jax import lax
from jax.experimental import pallas as pl
from jax.experimental.pallas import tpu as pltpu

F32 = jnp.float32
MXU_DT = jnp.bfloat16
ACT_DT = jnp.bfloat16
WIRE_DT = jnp.bfloat16

N_DEV = 8
GRID_W = 64
RG_C = 8.0
LN_EPS = 1e-5
S5_T = 16
S5_GB = 8
ROW_TILE = 256
VMEM_LIMIT = 56 * 1024 * 1024
PACK_W = 1024
SMALL_PARAM = 65536
WGRAD_TK = 2176

ADAM_LR = 0.001
ADAM_B1 = 0.9
ADAM_B2 = 0.999
ADAM_EPS = 1e-08
ADAM_WD = 0.01
ADAM_STEP = 10

WEIGHTS = ['c_ctx', 'ada_w', 'ada_b', 'ln1_g', 'ln1_b', 'w_in', 'conv_w', 'conv_b', 'rg_lambda', 'rg_wa', 'rg_ba',
           'rg_wi', 'rg_bi', 's5_a_re', 's5_a_im', 's5_log_dt', 's5_b_re', 's5_b_im', 's5_c_re', 's5_c_im', 's5_d',
           's5_glu_w', 's5_glu_b', 'w_out', 'b_out', 'ln2_g', 'ln2_b', 'mlp_w1', 'mlp_b1', 'mlp_w2', 'mlp_b2']
COL_SHARDED = ('w_in', 'mlp_w1')
MIX_W = ('w_in', 's5_glu_w', 'w_out')
MLP_W = ('mlp_w1', 'mlp_w2')
BIG = MIX_W + MLP_W
CHAN_SHARDED = ('conv_w', 'rg_lambda', 'rg_ba', 'rg_bi')
S5_NAMES = ('s5_a_re', 's5_a_im', 's5_log_dt', 's5_b_re', 's5_b_im', 's5_c_re', 's5_c_im', 's5_d')
MESH = pl.DeviceIdType.MESH


def _tile(n, pref, align):
    t = (min(pref, n) // align) * align
    while t >= align:
        if n % t == 0:
            return t
        t -= align
    return n


def _params(sem):
    return pltpu.CompilerParams(dimension_semantics=sem, vmem_limit_bytes=VMEM_LIMIT)


def _sigmoid(v):
    return 0.5 * jnp.tanh(0.5 * v) + 0.5


def _silu(v):
    return v * _sigmoid(v)


_GELU_K = math.sqrt(2.0 / math.pi)


def _gelu(v):
    return 0.5 * v * (1.0 + jnp.tanh(_GELU_K * (v + 0.044715 * v * v * v)))


def _gelu_grad(v):
    th = jnp.tanh(_GELU_K * (v + 0.044715 * v * v * v))
    return 0.5 * (1.0 + th) + 0.5 * v * (1.0 - th * th) * _GELU_K * (1.0 + 3.0 * 0.044715 * v * v)


def _one_minus_sq(la, a):
    v = 2.0 * la
    series = -v * (1.0 + v * (0.5 + v * (1.0 / 6.0 + v * (1.0 / 24.0 + v * (1.0 / 120.0)))))
    return jnp.where(v > -0.1, series, 1.0 - a * a)


def _softplus(v):
    return jnp.maximum(v, 0.0) + jnp.log(1.0 + jnp.exp(-jnp.abs(v)))


def _dot(a, b):
    return jnp.dot(a.astype(MXU_DT), b.astype(MXU_DT), preferred_element_type=F32)


def _dot_tn(a, b):
    return lax.dot_general(a.astype(MXU_DT), b.astype(MXU_DT), (((0,), (0,)), ((), ())), preferred_element_type=F32)


def _dot_nt(a, b):
    return lax.dot_general(a.astype(MXU_DT), b.astype(MXU_DT), (((1,), (1,)), ((), ())), preferred_element_type=F32)


def _mm(a, b, *, ta=False, tb=False, b_blocked=False, out_blocks=0, bias=None, a_fn=None, epi=None, extras=(),
        colsum=False, out_t=False, out_dtype=F32, name, tm=1088, tn=1024, tk=1024):
    if ta:
        kdim, m = a.shape
    else:
        m, kdim = a.shape
    bcol = b.shape[2] if b_blocked else None
    blog = (b.shape[1], b.shape[0] * b.shape[2]) if b_blocked else b.shape
    if tb:
        n, kb = blog
    else:
        kb, n = blog
    assert kdim == kb, (a.shape, b.shape, ta, tb)
    assert not (ta and tb)
    tm = _tile(m, tm, 128 if ta else 16)
    n_lim = bcol if (b_blocked and not tb) else (n // out_blocks if out_blocks else n)
    tn = _tile(n_lim, tn, 128)
    tk = _tile(bcol if (b_blocked and tb) else kdim, tk, 16 if ta else 128)
    nk = kdim // tk
    a_spec = pl.BlockSpec((tk, tm), lambda i, j, k: (k, i)) if ta else pl.BlockSpec((tm, tk), lambda i, j, k: (i, k))
    if not b_blocked:
        b_spec = pl.BlockSpec((tn, tk), lambda i, j, k: (j, k)) if tb else pl.BlockSpec((tk, tn), lambda i, j, k: (k, j))
    elif tb:
        qk = bcol // tk
        b_spec = pl.BlockSpec((None, tn, tk), lambda i, j, k: (k // qk, j, k % qk))
    else:
        qn = bcol // tn
        b_spec = pl.BlockSpec((None, tk, tn), lambda i, j, k: (j // qn, k, j % qn))
    in_specs = [a_spec, b_spec]
    args = [a, b]
    has_bias = bias is not None
    if has_bias:
        in_specs.append(pl.BlockSpec((1, tn), lambda i, j, k: (0, j)))
        args.append(bias.reshape(1, n).astype(F32))
    for e in extras:
        assert e.shape == (m, n), (e.shape, m, n)
        in_specs.append(pl.BlockSpec((tm, tn), lambda i, j, k: (i, j)))
        args.append(e)
    nex = len(extras)
    dn = (((0 if ta else 1,), (1 if tb else 0,)), ((), ()))
    if out_blocks:
        qo = (n // out_blocks) // tn
        out_shape = jax.ShapeDtypeStruct((out_blocks, m, n // out_blocks), out_dtype)
        out_spec = pl.BlockSpec((None, tm, tn), lambda i, j, k: (j // qo, i, j % qo))
    else:
        out_shape = jax.ShapeDtypeStruct((m, n), out_dtype)
        out_spec = pl.BlockSpec((tm, tn), lambda i, j, k: (i, j))
    out_shapes, out_specs = [out_shape], [out_spec]
    grid = (m // tm, n // tn, nk)
    assert not (colsum and out_t)
    if out_t:
        out_shapes.append(jax.ShapeDtypeStruct((n, m), out_dtype))
        out_specs.append(pl.BlockSpec((tn, tm), lambda i, j, k: (j, i)))
    if colsum:
        out_shapes.append(jax.ShapeDtypeStruct((1, n), F32))
        out_specs.append(pl.BlockSpec((1, tn), lambda i, j, k: (0, j)))
        swap = lambda sp: pl.BlockSpec(sp.block_shape, lambda j, i, k, f=sp.index_map: f(i, j, k))
        in_specs = [swap(sp) for sp in in_specs]
        out_specs = [swap(sp) for sp in out_specs]
        grid = (n // tn, m // tm, nk)
    nout = len(out_shapes)

    def body(*refs):
        a_ref, b_ref = refs[0], refs[1]
        row_tile = pl.program_id(1)
        pos = 2
        bias_ref = refs[pos] if has_bias else None
        pos += int(has_bias)
        ex_refs = refs[pos:pos + nex]
        o_ref = refs[pos + nex]
        av = a_ref[...]
        if a_fn is not None:
            av = a_fn(av.astype(F32))
        part = lax.dot_general(av.astype(MXU_DT), b_ref[...].astype(MXU_DT), dn, preferred_element_type=F32)

        def finish(r):
            if has_bias:
                r = r + bias_ref[...]
            if epi is not None:
                r = epi(r, *[e[...] for e in ex_refs])
            o_ref[...] = r.astype(out_dtype)
            if out_t:
                refs[pos + nex + 1][...] = r.T.astype(out_dtype)
            if colsum:
                cs_ref = refs[pos + nex + 1]

                @pl.when(row_tile == 0)
                def _():
                    cs_ref[...] = _colsum(r)

                @pl.when(row_tile > 0)
                def _():
                    cs_ref[...] += _colsum(r)

        if nk == 1:
            finish(part)
            return
        acc_ref = refs[pos + nex + nout]
        k = pl.program_id(2)

        @pl.when(k == 0)
        def _():
            acc_ref[...] = part

        @pl.when(k > 0)
        def _():
            acc_ref[...] += part

        @pl.when(k == nk - 1)
        def _():
            finish(acc_ref[...])

    res = pl.pallas_call(
        body, out_shape=tuple(out_shapes), grid=grid, in_specs=in_specs, out_specs=tuple(out_specs),
        scratch_shapes=[pltpu.VMEM((tm, tn), F32)] if nk > 1 else [],
        compiler_params=_params(("parallel", "arbitrary" if colsum else "parallel", "arbitrary")), name=name)(*args)
    return res if (colsum or out_t) else res[0]


def _ew(fn, tiles, rows=(), mods=(), outs=(), accs=(), *, name, nctx_tiles=0, tr=None, transposed=False):
    bsz, tlen = tiles[0][0].shape[0], tiles[0][0].shape[1]
    if tr is None:
        tr = _tile(tlen, ROW_TILE, 8)
    nt = tlen // tr
    in_specs, args = [], []
    for arr, cb, width, toff in tiles:
        in_specs.append(pl.BlockSpec((1, tr, width), functools.partial(
            lambda b, t, cb, toff: (b, jnp.maximum(t - toff, 0), cb), cb=cb, toff=toff)))
        args.append(arr)
    for r in rows:
        in_specs.append(pl.BlockSpec(r.shape, lambda b, t: (0, 0)))
        args.append(r)

    def seg_of(t):
        return jnp.where(t >= nctx_tiles, 1, 0)

    for mo in mods:
        in_specs.append(pl.BlockSpec((1, 1) + mo.shape[2:], lambda b, t: (b, seg_of(t), 0, 0)))
        args.append(mo)
    out_shape, out_specs = [], []
    for width, dt in outs:
        out_shape.append(jax.ShapeDtypeStruct((bsz, tlen, width), dt))
        out_specs.append(pl.BlockSpec((1, tr, width), lambda b, t: (b, t, 0)))
    for kk, cc in accs:
        out_shape.append(jax.ShapeDtypeStruct((bsz, 2, kk, cc), F32))
        out_specs.append(pl.BlockSpec((1, 1, kk, cc), lambda b, t: (b, seg_of(t), 0, 0)))
    nti, nr, nm, no, na = len(tiles), len(rows), len(mods), len(outs), len(accs)
    if transposed:
        out_shape.append(jax.ShapeDtypeStruct((outs[0][0], bsz * tlen), outs[0][1]))
        out_specs.append(pl.BlockSpec((outs[0][0], tr), lambda b, t: (0, b * nt + t)))

    def body(*refs):
        t = pl.program_id(1)
        tv = [r[0] for r in refs[:nti]]
        rv = [r[...] for r in refs[nti:nti + nr]]
        mv = [r[0, 0] for r in refs[nti + nr:nti + nr + nm]]
        o_refs = refs[nti + nr + nm:nti + nr + nm + no]
        a_refs = refs[nti + nr + nm + no:nti + nr + nm + no + na]
        seg = seg_of(t)
        ov, av = fn(tv, rv, mv, seg)
        for r, v in zip(o_refs, ov):
            r[0] = v.astype(r.dtype)
        if transposed:
            refs[-1][...] = ov[0].astype(F32).T.astype(refs[-1].dtype)
        if na:
            @pl.when((t == 0) | (t == nctx_tiles))
            def _():
                for r in a_refs:
                    r[...] = jnp.zeros_like(r)

            for r, v in zip(a_refs, av):
                r[0, 0] += v

    res = pl.pallas_call(
        body, out_shape=tuple(out_shape), grid=(bsz, nt), in_specs=in_specs, out_specs=tuple(out_specs),
        compiler_params=_params(("arbitrary", "arbitrary")), name=name)(*args)
    return (res[:no], res[no:no + na], res[-1]) if transposed else (res[:no], res[no:])


def _full(arr):
    return (arr, 0, arr.shape[-1], 0)


def _colsum(v):
    return jnp.sum(v, axis=0, keepdims=True)


def _shifted(x, prev8, next8, first, last, k):
    tr = x.shape[0]
    rid = lax.broadcasted_iota(jnp.int32, x.shape, 0)
    keep_prev = jnp.where(first, 0.0, 1.0)
    keep_next = jnp.where(last, 0.0, 1.0)
    if k == -1:
        return jnp.where(rid == 0, prev8[7:8] * keep_prev, pltpu.roll(x, 1, 0))
    if k == -2:
        r = pltpu.roll(x, 2, 0)
        r = jnp.where(rid == 1, prev8[7:8] * keep_prev, r)
        return jnp.where(rid == 0, prev8[6:7] * keep_prev, r)
    if k == 1:
        return jnp.where(rid == tr - 1, next8[0:1] * keep_next, pltpu.roll(x, tr - 1, 0))
    if k == 2:
        r = pltpu.roll(x, tr - 2, 0)
        r = jnp.where(rid == tr - 2, next8[0:1] * keep_next, r)
        return jnp.where(rid == tr - 1, next8[1:2] * keep_next, r)
    raise ValueError(k)


def _halo_specs(tr, width, cb, n8):
    cur = pl.BlockSpec((1, tr, width), lambda b, t: (b, t, cb))
    prev = pl.BlockSpec((1, 8, width), lambda b, t: (b, jnp.maximum(t * (tr // 8) - 1, 0), cb))
    nxt = pl.BlockSpec((1, 8, width), lambda b, t: (b, jnp.minimum((t + 1) * (tr // 8), n8 - 1), cb))
    return [cur, prev, nxt]


def _conv_fwd(z3, conv_w, conv_b, d, nctx_tiles, tr, name):
    bsz, tlen, _ = z3.shape
    nt = tlen // tr

    def body(x_ref, xp_ref, xn_ref, w_ref, b_ref, o_ref):
        t = pl.program_id(1)
        first = (t == 0) | (t == nctx_tiles)
        last = (t == nctx_tiles - 1) | (t == nt - 1)
        x, p8, n8 = x_ref[0], xp_ref[0], xn_ref[0]
        w = w_ref[...]
        o_ref[0] = (b_ref[...] + w[0:1] * _shifted(x, p8, n8, first, last, -1) + w[1:2] * x
                    + w[2:3] * _shifted(x, p8, n8, first, last, 1) + w[3:4] * _shifted(x, p8, n8, first, last, 2))

    return pl.pallas_call(
        body, out_shape=jax.ShapeDtypeStruct((bsz, tlen, d), F32), grid=(bsz, nt),
        in_specs=_halo_specs(tr, d, 0, tlen // 8) + [pl.BlockSpec((4, d), lambda b, t: (0, 0)),
                                                      pl.BlockSpec((1, d), lambda b, t: (0, 0))],
        out_specs=pl.BlockSpec((1, tr, d), lambda b, t: (b, t, 0)),
        compiler_params=_params(("parallel", "parallel")), name=name)(z3, z3, z3, conv_w, conv_b.reshape(1, d))


def _conv_bwd(dxc, z3, conv_w, d, nctx_tiles, tr, name):
    bsz, tlen, _ = dxc.shape
    nt = tlen // tr

    def body(g_ref, gp_ref, gn_ref, x_ref, xp_ref, xn_ref, w_ref, o_ref, acc_ref):
        t = pl.program_id(1)
        first = (t == 0) | (t == nctx_tiles)
        last = (t == nctx_tiles - 1) | (t == nt - 1)
        g, gp, gn = g_ref[0], gp_ref[0], gn_ref[0]
        x, xp, xn = x_ref[0], xp_ref[0], xn_ref[0]
        w = w_ref[...]
        o_ref[0] = (w[0:1] * _shifted(g, gp, gn, first, last, 1) + w[1:2] * g
                    + w[2:3] * _shifted(g, gp, gn, first, last, -1) + w[3:4] * _shifted(g, gp, gn, first, last, -2))

        @pl.when(t == 0)
        def _():
            acc_ref[...] = jnp.zeros_like(acc_ref)

        acc_ref[0, 0:1] += _colsum(g * _shifted(x, xp, xn, first, last, -1))
        acc_ref[0, 1:2] += _colsum(g * x)
        acc_ref[0, 2:3] += _colsum(g * _shifted(x, xp, xn, first, last, 1))
        acc_ref[0, 3:4] += _colsum(g * _shifted(x, xp, xn, first, last, 2))
        acc_ref[0, 4:5] += _colsum(g)

    return pl.pallas_call(
        body, out_shape=(jax.ShapeDtypeStruct((bsz, tlen, d), F32), jax.ShapeDtypeStruct((bsz, 8, d), F32)),
        grid=(bsz, nt),
        in_specs=_halo_specs(tr, d, 0, tlen // 8) + _halo_specs(tr, d, 0, tlen // 8)
        + [pl.BlockSpec((4, d), lambda b, t: (0, 0))],
        out_specs=(pl.BlockSpec((1, tr, d), lambda b, t: (b, t, 0)), pl.BlockSpec((1, 8, d), lambda b, t: (b, 0, 0))),
        compiler_params=_params(("arbitrary", "arbitrary")), name=name)(dxc, dxc, dxc, z3, z3, z3, conv_w)


def _rg_gates(x, lam, wa_ref, ba, wi_ref, bi, nh, hd):
    sp = _softplus(-lam)
    prs, pis = [], []
    for h in range(nh):
        xh = x[:, h * hd:(h + 1) * hd]
        prs.append(_dot(xh, wa_ref[h]))
        pis.append(_dot(xh, wi_ref[h]))
    r = 1.0 / (1.0 + jnp.exp(-(jnp.concatenate(prs, axis=1) + ba)))
    i = _sigmoid(jnp.concatenate(pis, axis=1) + bi)
    la = -RG_C * sp * r
    a = jnp.exp(la)
    m2 = _one_minus_sq(la, a)
    return sp, r, i, a, jnp.sqrt(m2), lax.rsqrt(m2)


def _scan_tile(t, nctx_tiles, nt, reverse):
    if not reverse:
        return t
    return jnp.where(t < nctx_tiles, nctx_tiles - 1 - t, nt - 1 - (t - nctx_tiles))


def _unscan_tile(t, nctx_tiles, nt, reverse):
    if not reverse:
        return nt - 1 - t
    return jnp.where(t < nt - nctx_tiles, nctx_tiles + t, t - (nt - nctx_tiles))


def _rg_param_specs(d, nh, hd):
    vec = pl.BlockSpec((1, d), lambda b, t: (0, 0))
    mat = pl.BlockSpec((nh, hd, hd), lambda b, t: (0, 0, 0))
    return [vec, mat, vec, mat, vec]


def _ride_split(refs, n_in, n_ride, n_out, n_scratch):
    pos = [n_in, n_in + n_ride, n_in + n_ride + n_out, n_in + 2 * n_ride + n_out, n_in + 2 * n_ride + n_out + n_scratch]
    return (refs[:pos[0]], refs[pos[0]:pos[1]], refs[pos[1]:pos[2]], refs[pos[2]:pos[3]], refs[pos[3]:pos[4]],
            refs[pos[4]:])


def _rg_fwd(xc, lam, wa, ba, wi, bi, reverse, nctx_tiles, tr, name, ride=None):
    bsz, tlen, d = xc.shape
    nh, hd = wa.shape[0], wa.shape[1]
    nt = tlen // tr
    tmap = lambda b, t: (b, _scan_tile(t, nctx_tiles, nt, reverse), 0)
    rarrs, rgather = ride if ride else ([], True)
    nr = len(rarrs)

    def body(*refs):
        (x_ref, lam_ref, wa_ref, ba_ref, wi_ref, bi_ref), rin, (h_ref,), rout, (a_scr, b_scr, carry), sems = _ride_split(
            refs, 6, nr, 1, 3)
        first = (pl.program_id(0) == 0) & (pl.program_id(1) == 0)
        last = (pl.program_id(0) == bsz - 1) & (pl.program_id(1) == nt - 1)
        if nr:
            @pl.when(first)
            def _():
                _exchange_start(rin, rout, sems, rgather)

        @pl.when(pl.program_id(1) == 0)
        def _():
            carry[...] = jnp.zeros_like(carry)

        x = x_ref[0]
        _, _, i, a, mult, _ = _rg_gates(x, lam_ref[...], wa_ref, ba_ref[...], wi_ref, bi_ref[...], nh, hd)
        a_scr[...] = a
        b_scr[...] = mult * (i * x)

        def blk(j, h):
            for r in range(8):
                row = (tr - 1 - (j * 8 + r)) if reverse else (j * 8 + r)
                h = a_scr[pl.ds(row, 1), :] * h + b_scr[pl.ds(row, 1), :]
                h_ref[0, pl.ds(row, 1), :] = h
            return h

        carry[0:1, :] = lax.fori_loop(0, tr // 8, blk, carry[0:1, :])
        if nr:
            @pl.when(last)
            def _():
                _exchange_finish(rin, rout, sems, rgather)

    res = pl.pallas_call(
        body, out_shape=(jax.ShapeDtypeStruct((bsz, tlen, d), F32),) + _exchange_out_shapes(rarrs), grid=(bsz, nt),
        in_specs=[pl.BlockSpec((1, tr, d), tmap)] + _rg_param_specs(d, nh, hd) + [_ANY] * nr,
        out_specs=(pl.BlockSpec((1, tr, d), tmap),) + (_ANY,) * nr,
        scratch_shapes=[pltpu.VMEM((tr, d), F32), pltpu.VMEM((tr, d), F32), pltpu.VMEM((8, d), F32)]
        + (_exchange_sems(nr) if nr else []),
        compiler_params=_params(("arbitrary", "arbitrary")), name=name)(
            xc, lam.reshape(1, d), wa, ba.reshape(1, d), wi, bi.reshape(1, d), *rarrs)
    return res[0], list(res[1:])


def _rg_bwd(xc, h, dmix, z3, lam, wa, ba, wi, bi, addend, reverse, nctx_tiles, tr, name, ride=None):
    bsz, tlen, d = xc.shape
    nh, hd = wa.shape[0], wa.shape[1]
    nt = tlen // tr
    r8 = tr // 8

    def tile_of(t):
        return _unscan_tile(t, nctx_tiles, nt, reverse)

    tmap = lambda b, t: (b, tile_of(t), 0)
    gmap = lambda b, t: (b, tile_of(t), 1)

    def halo_map(b, t):
        tt = tile_of(t)
        if not reverse:
            return (b, jnp.maximum(tt * r8 - 1, 0), 0)
        return (b, jnp.where(tt == nt - 1, 0, (tt + 1) * r8), 0)

    has_add = addend is not None
    rarrs, rgather = ride if ride else ([], True)
    nr = len(rarrs)

    def body(*refs):
        ins, rin, (dx_ref, dwa_ref, dwi_ref, dv_ref), rout, (a_scr, g_scr, dh_scr, carry), sems = _ride_split(
            refs, 10 + int(has_add), nr, 4, 4)
        x_ref, h_ref, halo_ref, dr_ref, z_ref, lam_ref, wa_ref, ba_ref, wi_ref, bi_ref = ins[:10]
        add_ref = ins[10] if has_add else None
        b = pl.program_id(0)
        t = pl.program_id(1)
        tt = tile_of(t)
        if nr:
            @pl.when((b == 0) & (t == 0))
            def _():
                _exchange_start(rin, rout, sems, rgather)

        @pl.when((b == 0) & (t == 0))
        def _():
            dwa_ref[...] = jnp.zeros_like(dwa_ref)
            dwi_ref[...] = jnp.zeros_like(dwi_ref)
            dv_ref[...] = jnp.zeros_like(dv_ref)

        @pl.when(t == 0)
        def _():
            carry[...] = jnp.zeros_like(carry)

        x = x_ref[0]
        sp, r, i, a, mult, inv_mult = _rg_gates(x, lam_ref[...], wa_ref, ba_ref[...], wi_ref, bi_ref[...], nh, hd)
        a_scr[...] = a
        dh_scr[...] = dr_ref[0] * _gelu(z_ref[0])

        def blk(j, cc):
            for rr in range(8):
                row = (j * 8 + rr) if reverse else (tr - 1 - (j * 8 + rr))
                g = dh_scr[pl.ds(row, 1), :] + cc
                g_scr[pl.ds(row, 1), :] = g
                cc = a_scr[pl.ds(row, 1), :] * g
            return cc

        carry[0:1, :] = lax.fori_loop(0, r8, blk, carry[0:1, :])
        g = g_scr[...]
        hv = h_ref[0]
        rid = lax.broadcasted_iota(jnp.int32, hv.shape, 0)
        if not reverse:
            valid = jnp.where(tt > 0, 1.0, 0.0)
            hprev = jnp.where(rid == 0, halo_ref[0][7:8] * valid, pltpu.roll(hv, 1, 0))
        else:
            valid = jnp.where(tt == nctx_tiles - 1, 0.0, 1.0)
            hprev = jnp.where(rid == tr - 1, halo_ref[0][0:1] * valid, pltpu.roll(hv, tr - 1, 0))
        dla = g * hprev * a - g * (i * x) * (a * a) * inv_mult
        dpr = dla * (-RG_C * sp) * r * (1.0 - r)
        dpi = g * mult * x * i * (1.0 - i)
        dx = g * mult * i
        dxs = []
        for hh in range(nh):
            sl = slice(hh * hd, (hh + 1) * hd)
            dxs.append(_dot_nt(dpr[:, sl], wa_ref[hh]) + _dot_nt(dpi[:, sl], wi_ref[hh]))
            dwa_ref[hh] += _dot_tn(x[:, sl], dpr[:, sl])
            dwi_ref[hh] += _dot_tn(x[:, sl], dpi[:, sl])
        dx = dx + jnp.concatenate(dxs, axis=1)
        if has_add:
            dx = dx + add_ref[0]
        dx_ref[0] = dx
        dv_ref[0:1] += _colsum(dla * (-RG_C * r))
        dv_ref[1:2] += _colsum(dpr)
        dv_ref[2:3] += _colsum(dpi)
        if nr:
            @pl.when((b == bsz - 1) & (t == nt - 1))
            def _():
                _exchange_finish(rin, rout, sems, rgather)

    in_specs = [pl.BlockSpec((1, tr, d), tmap), pl.BlockSpec((1, tr, d), tmap), pl.BlockSpec((1, 8, d), halo_map),
                pl.BlockSpec((1, tr, d), tmap), pl.BlockSpec((1, tr, d), gmap)] + _rg_param_specs(d, nh, hd)
    args = [xc, h, h, dmix, z3, lam.reshape(1, d), wa, ba.reshape(1, d), wi, bi.reshape(1, d)]
    if has_add:
        in_specs.append(pl.BlockSpec((1, tr, d), tmap))
        args.append(addend)
    mat = pl.BlockSpec((nh, hd, hd), lambda b, t: (0, 0, 0))
    res = pl.pallas_call(
        body, out_shape=(jax.ShapeDtypeStruct((bsz, tlen, d), F32), jax.ShapeDtypeStruct((nh, hd, hd), F32),
                         jax.ShapeDtypeStruct((nh, hd, hd), F32), jax.ShapeDtypeStruct((8, d), F32))
        + _exchange_out_shapes(rarrs),
        grid=(bsz, nt), in_specs=in_specs + [_ANY] * nr,
        out_specs=(pl.BlockSpec((1, tr, d), tmap), mat, mat, pl.BlockSpec((8, d), lambda b, t: (0, 0))) + (_ANY,) * nr,
        scratch_shapes=[pltpu.VMEM((tr, d), F32), pltpu.VMEM((tr, d), F32), pltpu.VMEM((tr, d), F32),
                        pltpu.VMEM((8, d), F32)] + (_exchange_sems(nr) if nr else []),
        compiler_params=_params(("arbitrary", "arbitrary")), name=name)(*args, *rarrs)
    return res[0], res[1], res[2], res[3], list(res[4:])


def _s5_operators(a_re, a_im, log_dt, b_re, b_im, c_re, c_im, d_skip):
    tt = S5_T
    g, h = d_skip.shape
    th = tt * h
    dt = jnp.exp(log_dt)[..., None]
    xr, xi = a_re * dt, a_im * dt
    taus = jnp.arange(tt + 1, dtype=F32)[None, None, :, None]
    mag = jnp.exp(xr[:, :, None, :] * taus)
    pw_re, pw_im = mag * jnp.cos(xi[:, :, None, :] * taus), mag * jnp.sin(xi[:, :, None, :] * taus)
    nr, ni = pw_re[:, :, 1] - 1.0, pw_im[:, :, 1]
    den = a_re * a_re + a_im * a_im
    cf_re, cf_im = (nr * a_re + ni * a_im) / den, (ni * a_re - nr * a_im) / den
    bt_re, bt_im = jnp.swapaxes(b_re, 2, 3), jnp.swapaxes(b_im, 2, 3)
    bb_re = cf_re[:, :, None] * bt_re - cf_im[:, :, None] * bt_im
    bb_im = cf_re[:, :, None] * bt_im + cf_im[:, :, None] * bt_re

    def outer(p_re, p_im, q_re, q_im):
        pr, pi = p_re[:, :, None, :], p_im[:, :, None, :]
        qr, qi = q_re[:, None], q_im[:, None]
        return (pr * qr - pi * qi).reshape(g, -1, pr.shape[-1]), (pr * qi + pi * qr).reshape(g, -1, pr.shape[-1])

    ops, kerns = [], []
    for dd in range(2):
        e_re, e_im = outer(pw_re[dd], pw_im[dd], c_re[dd], c_im[dd])
        kerns.append(jnp.einsum('gic,gmc->gim', jnp.concatenate([bb_re[dd], -bb_im[dd]], -1),
                                jnp.concatenate([e_re[:, :th], e_im[:, :th]], -1), precision=lax.Precision.HIGHEST))
        v_re, v_im = e_re[:, h:].reshape(g, tt, h, -1), e_im[:, h:].reshape(g, tt, h, -1)
        if dd == 1:
            v_re, v_im = v_re[:, ::-1], v_im[:, ::-1]
        vt = jnp.concatenate([v_re, -v_im], -1).reshape(g, th, -1)
        pr, pi = pw_re[dd][:, :tt], pw_im[dd][:, :tt]
        if dd == 0:
            pr, pi = pr[:, ::-1], pi[:, ::-1]
        w_re, w_im = outer(pr, pi, bb_re[dd], bb_im[dd])
        ops.append((jnp.concatenate([w_re, w_im], -1), jnp.swapaxes(vt, 1, 2)))
    kb_rev = kerns[1].reshape(g, h, tt, h)[:, :, ::-1].reshape(g, h, th)
    blocks = []
    for s in range(tt):
        fwd = jnp.pad(kerns[0][:, :, :th - s * h], ((0, 0), (0, 0), (s * h, 0)))
        bwd = jnp.pad(kb_rev[:, :, (tt - 1 - s) * h:], ((0, 0), (0, 0), (0, (tt - 1 - s) * h)))
        blocks.append(fwd + bwd)
    tz = jnp.stack(blocks, axis=1).reshape(g, th, th)
    tz = tz + jnp.eye(th, dtype=F32)[None] * jnp.tile(d_skip, (1, tt))[:, None, :]
    rows = []
    for dd in range(2):
        re, im = pw_re[dd][:, tt], pw_im[dd][:, tt]
        rows += [jnp.concatenate([re, re], -1), jnp.concatenate([-im, im], -1)]
    return tz, ops[0][0], ops[1][0], ops[0][1], ops[1][1], jnp.stack(rows)


def _chunk_order(j, ncc, nc):
    return jnp.where(j < ncc, ncc - 1 - j, nc - 1 - (j - ncc))


def _s5_specs(nc, ops, lm):
    gb = S5_GB
    act = lambda n_chunks, width: pl.BlockSpec((1, gb * n_chunks, width), lambda i, b: (b, i, 0))
    opspecs = [pl.BlockSpec((gb,) + o.shape[1:], lambda i, b: (i, 0, 0)) for o in ops]
    lspec = pl.BlockSpec((4, gb, lm.shape[-1]), lambda i, b: (0, i, 0))
    return act, opspecs, lspec


def _s5_fwd(u, ops, lm, ncc, name, ride=None):
    bsz, gn, th = u.shape
    g = ops[0].shape[0]
    nc = gn // g
    gb = S5_GB
    p2 = ops[1].shape[-1]
    ph = p2 // 2
    rarrs, rgather = ride if ride else ([], True)
    nr = len(rarrs)

    def body(*refs):
        ((u_ref, tz_ref, wf_ref, wb_ref, vf_ref, vb_ref, l_ref), rin, (y_ref, hf_ref, hb_ref), rout,
         (sf, sfs, sb, sbs), sems) = _ride_split(refs, 7, nr, 3, 4)
        if nr:
            @pl.when((pl.program_id(0) == 0) & (pl.program_id(1) == 0))
            def _():
                _exchange_start(rin, rout, sems, rgather)

        a1f, a2f, a1b, a2b = l_ref[0], l_ref[1], l_ref[2], l_ref[3]
        for gi in range(gb):
            ug = u_ref[0, pl.ds(gi * nc, nc), :]
            s1 = _dot(ug, wf_ref[gi])
            s2 = _dot(ug, wb_ref[gi])
            sf[pl.ds(gi, nc, stride=gb), :] = s1
            sfs[pl.ds(gi, nc, stride=gb), :] = pltpu.roll(s1, ph, 1)
            sb[pl.ds(gi, nc, stride=gb), :] = s2
            sbs[pl.ds(gi, nc, stride=gb), :] = pltpu.roll(s2, ph, 1)

        def step(j, hs):
            hf, hfs, hb, hbs = hs
            rf = pl.ds(pl.multiple_of(j * gb, gb), gb)
            rb = pl.ds(pl.multiple_of(_chunk_order(j, ncc, nc) * gb, gb), gb)
            s1, s1s, s2, s2s = sf[rf, :], sfs[rf, :], sb[rb, :], sbs[rb, :]
            sf[rf, :] = hf
            sb[rb, :] = hb
            return (a1f * hf + a2f * hfs + s1, a1f * hfs - a2f * hf + s1s,
                    a1b * hb + a2b * hbs + s2, a1b * hbs - a2b * hb + s2s)

        zero = jnp.zeros((gb, p2), F32)
        lax.fori_loop(0, nc, step, (zero, zero, zero, zero))
        for gi in range(gb):
            rows = pl.ds(gi * nc, nc)
            hfg = sf[pl.ds(gi, nc, stride=gb), :]
            hbg = sb[pl.ds(gi, nc, stride=gb), :]
            hf_ref[0, rows, :] = hfg
            hb_ref[0, rows, :] = hbg
            y_ref[0, rows, :] = (_dot(u_ref[0, rows, :], tz_ref[gi]) + _dot(hfg, vf_ref[gi])
                                 + _dot(hbg, vb_ref[gi])).astype(y_ref.dtype)
        if nr:
            @pl.when((pl.program_id(0) == g // gb - 1) & (pl.program_id(1) == bsz - 1))
            def _():
                _exchange_finish(rin, rout, sems, rgather)

    act, opspecs, lspec = _s5_specs(nc, ops, lm)
    res = pl.pallas_call(
        body, out_shape=(jax.ShapeDtypeStruct(u.shape, ACT_DT), jax.ShapeDtypeStruct((bsz, gn, p2), F32),
                         jax.ShapeDtypeStruct((bsz, gn, p2), F32)) + _exchange_out_shapes(rarrs),
        grid=(g // gb, bsz), in_specs=[act(nc, th)] + opspecs + [lspec] + [_ANY] * nr,
        out_specs=(act(nc, th), act(nc, p2), act(nc, p2)) + (_ANY,) * nr,
        scratch_shapes=[pltpu.VMEM((gb * nc, p2), F32) for _ in range(4)] + (_exchange_sems(nr) if nr else []),
        compiler_params=_params(("arbitrary", "arbitrary")), name=name)(u, *ops, lm, *rarrs)
    return res[0], res[1], res[2], list(res[3:])


def _s5_bwd(dy, u, hf, hb, ops_t, lm, ncc, name):
    bsz, gn, th = u.shape
    g = ops_t[0].shape[0]
    nc = gn // g
    gb = S5_GB
    p2 = lm.shape[-1]
    ph = p2 // 2

    def body(dy_ref, u_ref, hf_ref, hb_ref, tzt_ref, wft_ref, wbt_ref, vft_ref, vbt_ref, l_ref,
             du_ref, dtz_ref, dwf_ref, dwb_ref, dvf_ref, dvb_ref, dl_ref, jf, jfs, jb, jbs, hfk, hbk):
        a1f, a2f, a1b, a2b = l_ref[0], l_ref[1], l_ref[2], l_ref[3]

        @pl.when(pl.program_id(1) == 0)
        def _():
            for r in (dtz_ref, dwf_ref, dwb_ref, dvf_ref, dvb_ref, dl_ref):
                r[...] = jnp.zeros_like(r)

        for gi in range(gb):
            rows = pl.ds(gi * nc, nc)
            dyg = dy_ref[0, rows, :]
            i1 = _dot(dyg, vft_ref[gi])
            i2 = _dot(dyg, vbt_ref[gi])
            jf[pl.ds(gi, nc, stride=gb), :] = i1
            jfs[pl.ds(gi, nc, stride=gb), :] = pltpu.roll(i1, ph, 1)
            jb[pl.ds(gi, nc, stride=gb), :] = i2
            jbs[pl.ds(gi, nc, stride=gb), :] = pltpu.roll(i2, ph, 1)
            hfk[pl.ds(gi, nc, stride=gb), :] = hf_ref[0, rows, :]
            hbk[pl.ds(gi, nc, stride=gb), :] = hb_ref[0, rows, :]

        def step(j, carry):
            qf, qfs, qb, qbs, d1f, d2f, d1b, d2b = carry
            rf = pl.ds(pl.multiple_of((nc - 1 - j) * gb, gb), gb)
            rb = pl.ds(pl.multiple_of(_chunk_order(nc - 1 - j, ncc, nc) * gb, gb), gb)
            i1, i1s, i2, i2s = jf[rf, :], jfs[rf, :], jb[rb, :], jbs[rb, :]
            h1, h2 = hfk[rf, :], hbk[rb, :]
            jf[rf, :] = qf
            jb[rb, :] = qb
            return (i1 + a1f * qf - a2f * qfs, i1s + a1f * qfs + a2f * qf,
                    i2 + a1b * qb - a2b * qbs, i2s + a1b * qbs + a2b * qb,
                    d1f + qf * h1, d2f + qfs * h1, d1b + qb * h2, d2b + qbs * h2)

        zero = jnp.zeros((gb, p2), F32)
        fin = lax.fori_loop(0, nc, step, (zero,) * 8)
        dl_ref[0] += fin[4]
        dl_ref[1] += pltpu.roll(fin[5], ph, 1)
        dl_ref[2] += fin[6]
        dl_ref[3] += pltpu.roll(fin[7], ph, 1)
        for gi in range(gb):
            rows = pl.ds(gi * nc, nc)
            dyg = dy_ref[0, rows, :]
            ug = u_ref[0, rows, :]
            dsf = jf[pl.ds(gi, nc, stride=gb), :]
            dsb = jb[pl.ds(gi, nc, stride=gb), :]
            du_ref[0, rows, :] = (_dot(dyg, tzt_ref[gi]) + _dot(dsf, wft_ref[gi]) + _dot(dsb, wbt_ref[gi])).astype(du_ref.dtype)
            dtz_ref[gi] += _dot_tn(ug, dyg)
            dwf_ref[gi] += _dot_tn(ug, dsf)
            dwb_ref[gi] += _dot_tn(ug, dsb)
            dvf_ref[gi] += _dot_tn(hf_ref[0, rows, :], dyg)
            dvb_ref[gi] += _dot_tn(hb_ref[0, rows, :], dyg)

    act, opspecs, lspec = _s5_specs(nc, ops_t, lm)
    gshape = lambda o: jax.ShapeDtypeStruct(o.shape[:1] + o.shape[1:][::-1], F32)
    gspec = lambda o: pl.BlockSpec((gb,) + o.shape[1:][::-1], lambda i, b: (i, 0, 0))
    res = pl.pallas_call(
        body, out_shape=tuple([jax.ShapeDtypeStruct(u.shape, ACT_DT)] + [gshape(o) for o in ops_t]
                              + [jax.ShapeDtypeStruct(lm.shape, F32)]),
        grid=(g // gb, bsz),
        in_specs=[act(nc, th), act(nc, th), act(nc, p2), act(nc, p2)] + opspecs + [lspec],
        out_specs=tuple([act(nc, th)] + [gspec(o) for o in ops_t] + [lspec]),
        scratch_shapes=[pltpu.VMEM((gb * nc, p2), F32) for _ in range(6)],
        compiler_params=_params(("parallel", "arbitrary")), name=name)(dy, u, hf, hb, *ops_t, lm)
    return res[0], tuple(res[1:])


def _to_chunks(s, ctx_len, g):
    bsz, tlen, d = s.shape
    h = d // g
    seq = tlen - ctx_len
    rows = seq // GRID_W
    cpart = s[:, :ctx_len].reshape(bsz, ctx_len // S5_T, S5_T, g, h).transpose(0, 3, 1, 2, 4)
    lpart = s[:, ctx_len:].reshape(bsz, rows, GRID_W, g, h).transpose(0, 3, 2, 1, 4)
    cpart = cpart.reshape(bsz, g, ctx_len // S5_T, S5_T * h)
    lpart = lpart.reshape(bsz, g, seq // S5_T, S5_T * h)
    return jnp.concatenate([cpart, lpart], axis=2).reshape(bsz, g * (tlen // S5_T), S5_T * h)


def _from_chunks(y, ctx_len, g):
    bsz, gn, th = y.shape
    nc = gn // g
    h = th // S5_T
    ncc = ctx_len // S5_T
    seq = (nc - ncc) * S5_T
    rows = seq // GRID_W
    y = y.reshape(bsz, g, nc, th)
    cpart = y[:, :, :ncc].reshape(bsz, g, ncc, S5_T, h).transpose(0, 2, 3, 1, 4).reshape(bsz, ctx_len, g * h)
    lpart = y[:, :, ncc:].reshape(bsz, g, GRID_W, rows, h).transpose(0, 3, 2, 1, 4).reshape(bsz, seq, g * h)
    return jnp.concatenate([cpart, lpart], axis=1)


def _me():
    return lax.axis_index("x"), lax.axis_index("y"), lax.axis_index("c")


def _peer(k):
    x, y, c = _me()
    px = (1 - x) if (k & 4) else x
    py = (1 - y) if (k & 2) else y
    pc = (1 - c) if (k & 1) else c
    return (px, py, pc), 4 * px + 2 * py + pc


def _exchange(arrs, gather, name):
    n = len(arrs)

    def body(*refs):
        _exchange_start(refs[:n], refs[n:2 * n], refs[2 * n:], gather)
        _exchange_finish(refs[:n], refs[n:2 * n], refs[2 * n:], gather)

    return pl.pallas_call(
        body, out_shape=_exchange_out_shapes(arrs), in_specs=[_ANY] * n, out_specs=tuple([_ANY] * n),
        scratch_shapes=_exchange_sems(n), name=name)(*arrs)


_ANY = pl.BlockSpec(memory_space=pl.ANY)


def _exchange_out_shapes(arrs):
    return tuple(jax.ShapeDtypeStruct((N_DEV,) + a.shape[-2:], a.dtype) for a in arrs)


def _exchange_sems(n):
    return [pltpu.SemaphoreType.DMA((n * (N_DEV - 1),)), pltpu.SemaphoreType.DMA((n * (N_DEV - 1),)),
            pltpu.SemaphoreType.DMA((n,))]


def _exchange_copies(x_refs, o_refs, sems, gather, with_recvs):
    send_sems, recv_sems, local_sems = sems
    n = len(x_refs)
    npeer = N_DEV - 1
    xi, yi, ci = _me()
    me = 4 * xi + 2 * yi + ci
    mine = [x if gather else x.at[me] for x in x_refs]
    local = [pltpu.make_async_copy(mine[i], o_refs[i].at[me], local_sems.at[i]) for i in range(n)]
    sends, recvs = [], []
    for k in range(1, N_DEV):
        dev, pid = _peer(k)
        for i in range(n):
            slot = i * npeer + k - 1
            sends.append(pltpu.make_async_remote_copy(
                src_ref=x_refs[i] if gather else x_refs[i].at[pid], dst_ref=o_refs[i].at[me],
                send_sem=send_sems.at[slot], recv_sem=recv_sems.at[slot], device_id=dev, device_id_type=MESH))
            if with_recvs:
                recvs.append(pltpu.make_async_remote_copy(
                    src_ref=mine[i], dst_ref=o_refs[i].at[pid], send_sem=send_sems.at[slot],
                    recv_sem=recv_sems.at[slot], device_id=dev, device_id_type=MESH))
    return local, sends, recvs


def _exchange_start(x_refs, o_refs, sems, gather):
    local, sends, _ = _exchange_copies(x_refs, o_refs, sems, gather, False)
    for cp in local + sends:
        cp.start()


def _exchange_finish(x_refs, o_refs, sems, gather):
    local, sends, recvs = _exchange_copies(x_refs, o_refs, sems, gather, True)
    for cp in recvs:
        cp.wait_recv()
    for cp in sends:
        cp.wait_send()
    for cp in local:
        cp.wait()


def _sum_slots(x3, name):
    _, r, cdim = x3.shape
    tr = _tile(r, 256, 16)

    def body(x_ref, o_ref):
        acc = x_ref[0].astype(F32)
        for s in range(1, N_DEV):
            acc = acc + x_ref[s].astype(F32)
        o_ref[...] = acc

    return pl.pallas_call(
        body, out_shape=jax.ShapeDtypeStruct((r, cdim), F32), grid=(r // tr,),
        in_specs=[pl.BlockSpec((N_DEV, tr, cdim), lambda i: (0, i, 0))], out_specs=pl.BlockSpec((tr, cdim), lambda i: (i, 0)),
        compiler_params=_params(("parallel",)), name=name)(x3)


def _pack(arrs, dtype, lead=0):
    flat = jnp.concatenate([a.reshape(a.shape[:lead] + (-1,)).astype(dtype) for a in arrs], axis=-1)
    n = flat.shape[-1]
    pad = -n % (PACK_W * 16)
    flat = jnp.pad(flat, [(0, 0)] * lead + [(0, pad)])
    return flat.reshape(flat.shape[:lead] + (-1, PACK_W))


def _unpack(buf, shapes, lead=0):
    flat = buf.reshape(buf.shape[:lead] + (-1,))
    out, off = [], 0
    for shp in shapes:
        n = math.prod(shp)
        out.append(flat[..., off:off + n].reshape(buf.shape[:lead] + tuple(shp)))
        off += n
    return out


def _adamw_math(wv, gv, m0, v0):
    m1 = ADAM_B1 * m0 + (1.0 - ADAM_B1) * gv
    v1 = ADAM_B2 * v0 + (1.0 - ADAM_B2) * (gv * gv)
    m_hat = m1 / (1.0 - ADAM_B1 ** ADAM_STEP)
    v_hat = v1 / (1.0 - ADAM_B2 ** ADAM_STEP)
    delta = -ADAM_LR * (m_hat / (jnp.sqrt(v_hat) + ADAM_EPS) + ADAM_WD * wv)
    return delta, m1, v1


def _adamw(w, g, m, v, name):
    def fn(tv, rv, mv, seg):
        return _adamw_math(*tv), ()

    outs, _ = _ew(fn, [_full(a[None]) for a in (w, g, m, v)], outs=[(w.shape[-1], F32)] * 3, name=name)
    return [o[0] for o in outs]


def _sum_adamw(slots, w, m, v, name):
    _, r, cdim = slots.shape
    tr = _tile(r, 128, 16)

    def body(s_ref, w_ref, m_ref, v_ref, g_ref, d_ref, mo_ref, vo_ref):
        gv = s_ref[0].astype(F32)
        for s in range(1, N_DEV):
            gv = gv + s_ref[s].astype(F32)
        g_ref[...] = gv
        d_ref[...], mo_ref[...], vo_ref[...] = _adamw_math(w_ref[...], gv, m_ref[...], v_ref[...])

    flat = pl.BlockSpec((tr, cdim), lambda i: (i, 0))
    return pl.pallas_call(
        body, out_shape=tuple(jax.ShapeDtypeStruct((r, cdim), F32) for _ in range(4)), grid=(r // tr,),
        in_specs=[pl.BlockSpec((N_DEV, tr, cdim), lambda i: (0, i, 0)), flat, flat, flat], out_specs=(flat,) * 4,
        compiler_params=_params(("parallel",)), name=name)(slots, w, m, v)


def _gathered(n, p):
    return p if n in COL_SHARDED else p.reshape(-1, p.shape[-1])


def _layer_fwd(l, xin, modt, wts, sm, cfg):
    bsz, tlen, d = xin.shape
    bt = bsz * tlen
    nct, tr, ctx_len, g = cfg['nct'], cfg['tr'], cfg['ctx_len'], cfg['g']
    ncc, nc = ctx_len // S5_T, tlen // S5_T
    alpha = cfg['alpha']
    nm = lambda s: f"l{l}_{s}"

    def modulate(xv, i_shift, i_scale, name):
        def fn(tv, rv, mv, seg):
            mo = mv[0]
            return (tv[0] * (1.0 + mo[i_scale:i_scale + 1]) + mo[i_shift:i_shift + 1],), ()
        o, _, ot = _ew(fn, [_full(xv)], mods=[modt], outs=[(d, ACT_DT)], nctx_tiles=nct, tr=tr, name=name, transposed=True)
        return o[0], ot

    def ln_fwd(xv, mv_, i_gate, gam, bet, name):
        def fn(tv, rv, mv, seg):
            z = alpha * tv[0] + mv[0][i_gate:i_gate + 1] * tv[1]
            mu = jnp.mean(z, axis=-1, keepdims=True)
            zc = z - mu
            var = jnp.mean(zc * zc, axis=-1, keepdims=True)
            return (zc * lax.rsqrt(var + LN_EPS) * rv[0] + rv[1],), ()
        return _ew(fn, [_full(xv), _full(mv_)], rows=[gam.reshape(1, d), bet.reshape(1, d)], mods=[modt],
                   outs=[(d, F32)], nctx_tiles=nct, tr=tr, name=name)[0][0]

    sv = {'x': xin}
    u, sv['uT'] = modulate(xin, 0, 1, nm("mod1"))
    z3 = _mm(u.reshape(bt, d), wts['w_in'][l], b_blocked=True, name=nm("w_in")).reshape(bsz, tlen, 3 * d)
    sv['z3'] = z3
    xc = _conv_fwd(z3, sm['conv_w'][l], sm['conv_b'][l], d, nct, tr, nm("conv"))
    sv['xc'] = xc
    nxt = l + 1 < cfg['depth']
    riders = [[('mlp_w1', l)], [('mlp_w2', l)] + ([('s5_glu_w', l + 1), ('w_out', l + 1)] if nxt else []),
              [('w_in', l + 1)] if nxt else []]

    def ride_of(rs):
        return ([cfg['wloc'][n][lay] for n, lay in rs], True) if rs else None

    def landed(rs, got):
        for (n, lay), p in zip(rs, got):
            wts[n][lay] = _gathered(n, p)

    hs = []
    for dd in range(2):
        h, got = _rg_fwd(xc, sm['rg_lambda'][l, dd], sm['rg_wa'][l, dd], sm['rg_ba'][l, dd], sm['rg_wi'][l, dd],
                         sm['rg_bi'][l, dd], bool(dd), nct, tr, nm(f"rg_fwd{dd}"), ride=ride_of(riders[dd]))
        hs.append(h)
        landed(riders[dd], got)
    sv['hf'], sv['hb'] = hs

    def copy_fn(tv, rv, mv, seg):
        return (tv[0],), ()
    s5u = _to_chunks(_ew(copy_fn, [(z3, 2, d, 0)], outs=[(d, ACT_DT)], tr=tr, name=nm("s5_in"))[0][0], ctx_len, g)
    sv['s5u'] = s5u
    y, hf5, hb5, got = _s5_fwd(s5u, cfg['s5_ops'][l], cfg['s5_lm'][l], ncc, nm("s5_fwd"), ride=ride_of(riders[2]))
    landed(riders[2], got)
    sv['hf5'], sv['hb5'] = hf5, hb5
    ytok = _from_chunks(y, ctx_len, g)
    sv['ytok'] = ytok

    def gelu_fn(tv, rv, mv, seg):
        return (_gelu(tv[0].astype(F32)),), ()
    o, _, sv['gactT'] = _ew(gelu_fn, [_full(ytok)], outs=[(d, ACT_DT)], tr=tr, name=nm("s5_gelu"), transposed=True)
    gact = o[0]
    sv['gact'] = gact
    gpre = _mm(gact.reshape(bt, d), wts['s5_glu_w'][l], bias=sm['s5_glu_b'][l], name=nm("glu")).reshape(bsz, tlen, d)
    sv['gpre'] = gpre

    def mix_fn(tv, rv, mv, seg):
        rg = (tv[0] + tv[1]) * _gelu(tv[2])
        s5o = tv[3].astype(F32) * _sigmoid(tv[4])
        return (jnp.concatenate([rg, s5o], axis=1),), ()
    o, _, sv['mixinT'] = _ew(mix_fn, [_full(hs[0]), _full(hs[1]), (z3, 1, d, 0), _full(gact), _full(gpre)],
                             outs=[(2 * d, ACT_DT)], tr=tr, name=nm("mix"), transposed=True)
    mixin = o[0].reshape(bt, 2 * d)
    mo = _mm(mixin, wts['w_out'][l], bias=sm['b_out'][l], name=nm("w_out")).reshape(bsz, tlen, d)
    sv['mo'] = mo
    x1 = ln_fwd(xin, mo, 2, sm['ln1_g'][l], sm['ln1_b'][l], nm("ln1"))
    sv['x1'] = x1
    u2, sv['u2T'] = modulate(x1, 3, 4, nm("mod2"))
    rl, sv['rlT'] = _mm(u2.reshape(bt, d), wts['mlp_w1'][l], b_blocked=True, bias=sm['mlp_b1'][l],
                        epi=lambda r: jnp.maximum(r, 0.0), out_t=True, out_dtype=ACT_DT, tm=2176, name=nm("mlp1"))
    sv['rl'] = rl
    fo = _mm(rl, wts['mlp_w2'][l], bias=sm['mlp_b2'][l], a_fn=lambda v: v * v, name=nm("mlp2")).reshape(bsz, tlen, d)
    sv['fo'] = fo
    x2 = ln_fwd(x1, fo, 5, sm['ln2_g'][l], sm['ln2_b'][l], nm("ln2"))
    return x2, sv


def _layer_bwd(l, dx2, sv, modt, wts, sm, cfg, pending):
    bsz, tlen, d = dx2.shape
    bt = bsz * tlen
    nct, tr, ctx_len, g = cfg['nct'], cfg['tr'], cfg['ctx_len'], cfg['g']
    ncc, nc = ctx_len // S5_T, tlen // S5_T
    alpha = cfg['alpha']
    nm = lambda s: f"l{l}_{s}"
    gr = {}

    def ln_bwd(xv, mv_, i_gate, gam, dy, name):
        def fn(tv, rv, mv, seg):
            xx, mm_, dyy = tv
            gate = mv[0][i_gate:i_gate + 1]
            z = alpha * xx + gate * mm_
            mu = jnp.mean(z, axis=-1, keepdims=True)
            zc = z - mu
            var = jnp.mean(zc * zc, axis=-1, keepdims=True)
            rstd = lax.rsqrt(var + LN_EPS)
            xhat = zc * rstd
            dxh = dyy * rv[0]
            dz = rstd * (dxh - jnp.mean(dxh, axis=-1, keepdims=True) - xhat * jnp.mean(dxh * xhat, axis=-1, keepdims=True))
            dm = gate * dz
            acc = jnp.concatenate([_colsum(dz * mm_), _colsum(dyy * xhat), _colsum(dyy), _colsum(dm)], axis=0)
            return (alpha * dz, dm), (acc,)
        o, a = _ew(fn, [_full(xv), _full(mv_), _full(dy)], rows=[gam.reshape(1, d)], mods=[modt],
                   outs=[(d, F32), (d, ACT_DT)], accs=[(4, d)], nctx_tiles=nct, tr=tr, name=name)
        return o[0], o[1], a[0]

    def mod_bwd(du, xv, i_scale, addend, name):
        def fn(tv, rv, mv, seg):
            duu, xx, add = tv
            acc = jnp.concatenate([_colsum(duu), _colsum(duu * xx)], axis=0)
            return (add + duu * (1.0 + mv[0][i_scale:i_scale + 1]),), (acc,)
        o, a = _ew(fn, [_full(du), _full(xv), _full(addend)], mods=[modt], outs=[(d, F32)], accs=[(2, d)],
                   nctx_tiles=nct, tr=tr, name=name)
        return o[0], a[0]

    def row_blocks(gw):
        return gw.reshape(N_DEV, -1, gw.shape[-1])

    dx1a, dfo, acc2 = ln_bwd(sv['x1'], sv['fo'], 5, sm['ln2_g'][l], dx2, nm("ln2_bwd"))
    dfo2 = dfo.reshape(bt, d)
    dhp, db1 = _mm(dfo2, wts['mlp_w2'][l], tb=True, epi=lambda r, rl: r * (2.0 * rl.astype(F32)), extras=[sv['rl']],
                   colsum=True, out_dtype=ACT_DT, name=nm("mlp2_dx"))
    gr['mlp_b1'] = db1[0]
    gr['mlp_w2'] = row_blocks(_mm(sv['rlT'], dfo2, a_fn=lambda v: v * v, out_dtype=WIRE_DT, tm=512, tk=WGRAD_TK,
                                  name=nm("mlp2_dw")))
    du2 = _mm(dhp, wts['mlp_w1'][l], tb=True, b_blocked=True, name=nm("mlp1_dx")).reshape(bsz, tlen, d)
    gr['mlp_w1'] = _mm(sv['u2T'], dhp, out_blocks=N_DEV, out_dtype=WIRE_DT, tk=WGRAD_TK, name=nm("mlp1_dw"))
    dx1, accm2 = mod_bwd(du2, sv['x1'], 4, dx1a, nm("mod2_bwd"))
    gr['ln2_g'] = acc2[:, :, 1].sum((0, 1))
    gr['ln2_b'] = acc2[:, :, 2].sum((0, 1))
    gr['mlp_b2'] = acc2[:, :, 3].sum((0, 1))

    dxa, dmo, acc1 = ln_bwd(sv['x'], sv['mo'], 2, sm['ln1_g'][l], dx1, nm("ln1_bwd"))
    gr['ln1_g'] = acc1[:, :, 1].sum((0, 1))
    gr['ln1_b'] = acc1[:, :, 2].sum((0, 1))
    gr['b_out'] = acc1[:, :, 3].sum((0, 1))
    dmo2 = dmo.reshape(bt, d)
    dmix = _mm(dmo2, wts['w_out'][l], tb=True, name=nm("w_out_dx")).reshape(bsz, tlen, 2 * d)
    gr['w_out'] = row_blocks(_mm(sv['mixinT'], dmo2, out_dtype=WIRE_DT, tk=WGRAD_TK, name=nm("w_out_dw")))

    def glu_bwd(tv, rv, mv, seg):
        ds, ga, gp = tv
        ga = ga.astype(F32)
        sg = _sigmoid(gp)
        dg = ds * ga * sg * (1.0 - sg)
        return (dg, ds * sg), (_colsum(dg),)
    o, a = _ew(glu_bwd, [(dmix, 1, d, 0), _full(sv['gact']), _full(sv['gpre'])], outs=[(d, ACT_DT), (d, F32)],
               accs=[(1, d)], tr=tr, name=nm("glu_bwd"))
    dgp, t1 = o
    gr['s5_glu_b'] = a[0][:, 1, 0].sum(0)
    dgp2 = dgp.reshape(bt, d)
    dyt = _mm(dgp2, wts['s5_glu_w'][l], tb=True, epi=lambda r, t1v, yv: (r + t1v) * _gelu_grad(yv.astype(F32)),
              extras=[t1.reshape(bt, d), sv['ytok'].reshape(bt, d)], out_dtype=ACT_DT, tn=512,
              name=nm("glu_dx")).reshape(bsz, tlen, d)
    gr['s5_glu_w'] = row_blocks(_mm(sv['gactT'], dgp2, out_dtype=WIRE_DT, tk=WGRAD_TK, name=nm("glu_dw")))
    du5, dops = _s5_bwd(_to_chunks(dyt, ctx_len, g), sv['s5u'], sv['hf5'], sv['hb5'], cfg['s5_ops_t'][l],
                        cfg['s5_lm'][l], ncc, nm("s5_bwd"))
    ds5u = _from_chunks(du5, ctx_len, g)
    gr['s5_dops'] = dops

    z3 = sv['z3']
    dxc = None
    for dd in range(2):
        names = ('mlp_w2',) if dd == 0 else ('mlp_w1', 's5_glu_w', 'w_out')
        riders = (pending if dd == 0 else []) + [(n, l, gr.pop(n)) for n in names]
        dxc, dwa, dwi, dv, got = _rg_bwd(sv['xc'], sv['hf'] if dd == 0 else sv['hb'], dmix, z3, sm['rg_lambda'][l, dd],
                                         sm['rg_wa'][l, dd], sm['rg_ba'][l, dd], sm['rg_wi'][l, dd],
                                         sm['rg_bi'][l, dd], dxc, bool(dd), nct, tr, nm(f"rg_bwd{dd}"),
                                         ride=([r[2] for r in riders], False))
        for (n, lay, _), p in zip(riders, got):
            cfg['slots'][n][lay] = p
        gr[f'rg_wa{dd}'], gr[f'rg_wi{dd}'] = dwa, dwi
        gr[f'rg_lambda{dd}'] = dv[0] * (-_sigmoid(-sm['rg_lambda'][l, dd]))
        gr[f'rg_ba{dd}'], gr[f'rg_bi{dd}'] = dv[1], dv[2]
    drgx, accc = _conv_bwd(dxc, z3, sm['conv_w'][l], d, nct, tr, nm("conv_bwd"))
    gr['conv_w'] = accc[:, 0:4].sum(0)
    gr['conv_b'] = accc[:, 4].sum(0)

    def dz_fn(tv, rv, mv, seg):
        dgate = tv[0] * (tv[1] + tv[2]) * _gelu_grad(tv[3])
        return (jnp.concatenate([tv[4], dgate, tv[5].astype(F32)], axis=1),), ()
    dz = _ew(dz_fn, [(dmix, 0, d, 0), _full(sv['hf']), _full(sv['hb']), (z3, 1, d, 0), _full(drgx), _full(ds5u)],
             outs=[(3 * d, ACT_DT)], tr=tr, name=nm("dz"))[0][0].reshape(bt, 3 * d)
    du = _mm(dz, wts['w_in'][l], tb=True, b_blocked=True, name=nm("w_in_dx")).reshape(bsz, tlen, d)
    still = [('w_in', l, _mm(sv['uT'], dz, out_blocks=N_DEV, out_dtype=WIRE_DT, tk=WGRAD_TK, name=nm("w_in_dw")))]
    dxin, accm1 = mod_bwd(du, sv['x'], 1, dxa, nm("mod1_bwd"))
    dmod = jnp.stack([accm1[:, :, 0], accm1[:, :, 1], acc1[:, :, 0], accm2[:, :, 0], accm2[:, :, 1], acc2[:, :, 0]], axis=2)
    return dxin, dmod, gr, still


def kernel(x, c, ctx, c_ctx, ada_w, ada_b, ln1_g, ln1_b, w_in, conv_w, conv_b, rg_lambda, rg_wa, rg_ba, rg_wi, rg_bi, s5_a_re, s5_a_im, s5_log_dt, s5_b_re, s5_b_im, s5_c_re, s5_c_im, s5_d, s5_glu_w, s5_glu_b, w_out, b_out, ln2_g, ln2_b, mlp_w1, mlp_b1, mlp_w2, mlp_b2, loss_target, m_c_ctx, m_ada_w, m_ada_b, m_ln1_g, m_ln1_b, m_w_in, m_conv_w, m_conv_b, m_rg_lambda, m_rg_wa, m_rg_ba, m_rg_wi, m_rg_bi, m_s5_a_re, m_s5_a_im, m_s5_log_dt, m_s5_b_re, m_s5_b_im, m_s5_c_re, m_s5_c_im, m_s5_d, m_s5_glu_w, m_s5_glu_b, m_w_out, m_b_out, m_ln2_g, m_ln2_b, m_mlp_w1, m_mlp_b1, m_mlp_w2, m_mlp_b2, v_c_ctx, v_ada_w, v_ada_b, v_ln1_g, v_ln1_b, v_w_in, v_conv_w, v_conv_b, v_rg_lambda, v_rg_wa, v_rg_ba, v_rg_wi, v_rg_bi, v_s5_a_re, v_s5_a_im, v_s5_log_dt, v_s5_b_re, v_s5_b_im, v_s5_c_re, v_s5_c_im, v_s5_d, v_s5_glu_w, v_s5_glu_b, v_w_out, v_b_out, v_ln2_g, v_ln2_b, v_mlp_w1, v_mlp_b1, v_mlp_w2, v_mlp_b2):
    loc = dict(locals())
    w = {n: loc[n] for n in WEIGHTS}
    mom = {n: loc["m_" + n] for n in WEIGHTS}
    vel = {n: loc["v_" + n] for n in WEIGHTS}
    bsz, seq, d = x.shape
    ctx_len = ctx.shape[1]
    depth = ada_w.shape[0]
    g = s5_a_re.shape[2]
    nmod = ada_b.shape[1] // d
    modc = ada_w.shape[2]
    tr = _tile(ctx_len, ROW_TILE, 8)
    cfg = dict(nct=ctx_len // tr, tr=tr, ctx_len=ctx_len, g=g, alpha=(2.0 * depth) ** 0.25)
    xi, yi, ci = _me()
    me = 4 * xi + 2 * yi + ci

    small_shapes = [c.shape] + [w[n].shape for n in CHAN_SHARDED]
    got = _exchange([_pack([c] + [w[n] for n in CHAN_SHARDED], F32)], True, "gather_small")[0]
    parts = _unpack(got, small_shapes, lead=1)
    c_all = parts[0].reshape(N_DEV * bsz, d)
    sm = {n: w[n] for n in WEIGHTS if n not in BIG and n not in CHAN_SHARDED and n != 'ada_w'}
    for n, p in zip(CHAN_SHARDED, parts[1:]):
        sm[n] = jnp.moveaxis(p, 0, -2).reshape(p.shape[1:-1] + (-1,))

    a_ext = jnp.concatenate([c_all, jnp.broadcast_to(c_ctx[None], (N_DEV, d))], axis=0)
    nrow = a_ext.shape[0]
    my_ada_b = lax.dynamic_slice_in_dim(ada_b, me * modc, modc, axis=1)
    mod_cols = jnp.stack([_mm(a_ext, ada_w[l], bias=my_ada_b[l], a_fn=_silu, name=f"l{l}_ada") for l in range(depth)])
    mod_all = _exchange([mod_cols.reshape(depth * nrow, modc)], True, "gather_mod")[0]
    mod_all = mod_all.reshape(N_DEV, depth, nrow, modc).transpose(1, 2, 0, 3).reshape(depth, nrow, nmod, d)
    mod_mine = lax.dynamic_slice_in_dim(mod_all, me * bsz, bsz, axis=1)
    mod_ctx = jnp.broadcast_to(mod_all[:, N_DEV * bsz][:, None], mod_mine.shape)
    modts = jnp.stack([mod_ctx, mod_mine], axis=2)

    cfg['depth'] = depth
    cfg['wloc'] = {n: [w[n][l].astype(WIRE_DT) for l in range(depth)] for n in BIG}
    wts = {n: [None] * depth for n in BIG}
    for n, p in zip(MIX_W, _exchange([cfg['wloc'][n][0] for n in MIX_W], True, "gather_w0")):
        wts[n][0] = _gathered(n, p)

    fold = lambda t: jnp.moveaxis(t, 0, 1).reshape((2, depth * g) + t.shape[3:])
    s5_in = [fold(w[n]) for n in S5_NAMES[:7]] + [s5_d.reshape(depth * g, -1)]
    ops_all, s5_vjp = jax.vjp(_s5_operators, *s5_in)
    lay = lambda t, l, ax=0: lax.slice_in_dim(t, l * g, (l + 1) * g, axis=ax)
    cfg['s5_ops'] = [tuple(lay(o, l).astype(MXU_DT) for o in ops_all[:5]) for l in range(depth)]
    cfg['s5_ops_t'] = [tuple(jnp.swapaxes(lay(o, l), 1, 2).astype(MXU_DT) for o in ops_all[:5]) for l in range(depth)]
    cfg['s5_lm'] = [lay(ops_all[5], l, 1) for l in range(depth)]

    act = jnp.concatenate([ctx, x], axis=1)
    saved = []
    for l in range(depth):
        act, sv = _layer_fwd(l, act, modts[l], wts, sm, cfg)
        saved.append(sv)

    def loss_fn(tv, rv, mv, seg):
        err = tv[0] - tv[1]
        keep = jnp.where(seg == 1, 1.0, 0.0)
        return (err * (keep / d),), (_colsum(err * err) * keep,)
    o, a = _ew(loss_fn, [_full(act), (loss_target, 0, d, cfg['nct'])], outs=[(d, F32)], accs=[(1, d)],
               nctx_tiles=cfg['nct'], tr=tr, name="loss")
    dact = o[0]
    loss = lax.psum(0.5 * jnp.sum(a[0][:, 1]) / d, ("x", "y", "c"))

    grads = [None] * depth
    dmods = [None] * depth
    slots = cfg['slots'] = {n: [None] * depth for n in BIG}
    pending = []
    for l in reversed(range(depth)):
        dact, dmods[l], grads[l], pending = _layer_bwd(l, dact, saved[l], modts[l], wts, sm, cfg, pending)
    for (n, lay, _), p in zip(pending, _exchange([r[2] for r in pending], False, "scatter_last")):
        slots[n][lay] = p
    grad_x = dact[:, ctx_len:]

    dmod = jnp.stack(dmods)
    mine = jnp.concatenate([dmod[:, :, 1].reshape(depth, bsz, nmod * d),
                            dmod[:, :, 0].sum(1).reshape(depth, 1, nmod * d)], axis=1)
    got = _exchange([mine.reshape(depth * (bsz + 1), nmod * d)], True, "gather_dmod")[0]
    got = got.reshape(N_DEV, depth, bsz + 1, nmod * d)
    dmod_rows = jnp.concatenate([got[:, :, :bsz].transpose(1, 0, 2, 3).reshape(depth, N_DEV * bsz, nmod * d),
                                 got[:, :, bsz].transpose(1, 0, 2)], axis=1)
    dmod_cols = lax.dynamic_slice_in_dim(dmod_rows, me * modc, modc, axis=2)
    g_ada_w = jnp.stack([_mm(a_ext, dmod_cols[l], ta=True, a_fn=_silu, name=f"l{l}_ada_dw") for l in range(depth)])

    def rowsum_fn(tv, rv, mv, seg):
        return (), (_colsum(tv[0]),)
    g_ada_b = _ew(rowsum_fn, [_full(dmod_rows)], accs=[(1, nmod * d)], name="ada_db")[1][0][:, 1, 0]
    dsilu = 0.0
    for l in range(depth):
        dsilu = dsilu + _mm(dmod_cols[l, N_DEV * bsz:], ada_w[l], tb=True, name=f"l{l}_ada_dc").sum(0)
    sig = _sigmoid(c_ctx)
    g_c_ctx_part = dsilu * (sig * (1.0 + c_ctx * (1.0 - sig)))

    def both(name, l):
        return jnp.stack([grads[l][f'{name}{dd}'] for dd in range(2)])
    rep = {n: jnp.stack([grads[l][n] for l in range(depth)]) for n in
           ('ln1_g', 'ln1_b', 'conv_w', 'conv_b', 's5_glu_b', 'b_out', 'ln2_g', 'ln2_b', 'mlp_b1', 'mlp_b2')}
    for n in ('rg_lambda', 'rg_wa', 'rg_ba', 'rg_wi', 'rg_bi'):
        rep[n] = jnp.stack([both(n, l) for l in range(depth)])
    rep['c_ctx'] = g_c_ctx_part
    dops_all = tuple(jnp.concatenate([grads[l]['s5_dops'][i] for l in range(depth)], axis=1 if i == 5 else 0)
                     for i in range(6))
    s5g = s5_vjp(dops_all)
    for n, val in zip(S5_NAMES[:7], s5g):
        rep[n] = jnp.moveaxis(val.reshape((2, depth, g) + val.shape[2:]), 0, 1)
    rep['s5_d'] = s5g[7].reshape(depth, -1)
    small_names = [n for n in WEIGHTS if n in rep and math.prod(rep[n].shape) <= SMALL_PARAM]
    large_names = [n for n in WEIGHTS if n in rep and math.prod(rep[n].shape) > SMALL_PARAM]
    g_rep = {}
    bufs = []
    for names, dt in ((small_names, F32), (large_names, WIRE_DT)):
        buf = _pack([rep[n] for n in names], dt)
        bufs.append(jnp.pad(buf, ((0, -buf.shape[0] % (16 * N_DEV)), (0, 0))).reshape(N_DEV, -1, PACK_W))
    parts = [_sum_slots(p, f"sum_rep{i}").astype(p.dtype) for i, p in enumerate(_exchange(bufs, False, "scatter_rep"))]
    for names, full in zip((small_names, large_names), _exchange(parts, True, "gather_rep")):
        g_rep.update(zip(names, _unpack(full.reshape(-1, PACK_W).astype(F32), [rep[n].shape for n in names])))

    grad, delta, new_m, new_v = {}, {}, {}, {}
    for n in BIG:
        res = [_sum_adamw(slots[n][l], w[n][l], mom[n][l], vel[n][l], f"l{l}_adamw_{n}") for l in range(depth)]
        grad[n], delta[n], new_m[n], new_v[n] = [jnp.stack([r[i] for r in res]) for i in range(4)]
    grad['ada_w'] = g_ada_w
    rest = [n for n in WEIGHTS if n not in BIG and n != 'ada_w']
    for n in rest:
        if n == 'ada_b':
            grad[n] = g_ada_b
        elif n in CHAN_SHARDED:
            grad[n] = lax.dynamic_slice_in_dim(g_rep[n], me * w[n].shape[-1], w[n].shape[-1], axis=g_rep[n].ndim - 1)
        else:
            grad[n] = g_rep[n].reshape(w[n].shape)
    flat2 = lambda t: t.reshape(-1, t.shape[-1])
    for n in ('ada_w', 'rg_wa', 'rg_wi'):
        res = _adamw(flat2(w[n]), flat2(grad[n]), flat2(mom[n]), flat2(vel[n]), f"adamw_{n}")
        delta[n], new_m[n], new_v[n] = [r.reshape(w[n].shape) for r in res]
    for tag, names in (("small", [n for n in rest if math.prod(w[n].shape) <= SMALL_PARAM]),
                       ("s5", [n for n in rest if math.prod(w[n].shape) > SMALL_PARAM and n not in ('rg_wa', 'rg_wi')])):
        if not names:
            continue
        shapes = [w[n].shape for n in names]
        packed = [_pack([src[n] for n in names], F32) for src in (w, grad, mom, vel)]
        for dst, o in zip((delta, new_m, new_v), _adamw(*packed, name=f"adamw_{tag}")):
            dst.update(zip(names, _unpack(o, shapes)))
    return (loss, grad_x, *[grad[n] for n in WEIGHTS], *[delta[n] for n in WEIGHTS], *[new_m[n] for n in WEIGHTS],
            *[new_v[n] for n in WEIGHTS])
```

```python
import functools
import math

import jax
import jax.numpy as jnp
from jax import lax
from jax.experimental import pallas as pl
from jax.experimental.pallas import tpu as pltpu

F32 = jnp.float32
MXU_DT = jnp.bfloat16
ACT_DT = jnp.bfloat16
WIRE_DT = jnp.bfloat16

N_DEV = 8
GRID_W = 64
RG_C = 8.0
LN_EPS = 1e-5
S5_T = 16
S5_GB = 8
ROW_TILE = 256
VMEM_LIMIT = 56 * 1024 * 1024
PACK_W = 1024
SMALL_PARAM = 65536
WGRAD_TK = 2176

ADAM_LR = 0.001
ADAM_B1 = 0.9
ADAM_B2 = 0.999
ADAM_EPS = 1e-08
ADAM_WD = 0.01
ADAM_STEP = 10

WEIGHTS = ['c_ctx', 'ada_w', 'ada_b', 'ln1_g', 'ln1_b', 'w_in', 'conv_w', 'conv_b', 'rg_lambda', 'rg_wa', 'rg_ba',
           'rg_wi', 'rg_bi', 's5_a_re', 's5_a_im', 's5_log_dt', 's5_b_re', 's5_b_im', 's5_c_re', 's5_c_im', 's5_d',
           's5_glu_w', 's5_glu_b', 'w_out', 'b_out', 'ln2_g', 'ln2_b', 'mlp_w1', 'mlp_b1', 'mlp_w2', 'mlp_b2']
COL_SHARDED = ('w_in', 'mlp_w1')
MIX_W = ('w_in', 's5_glu_w', 'w_out')
MLP_W = ('mlp_w1', 'mlp_w2')
BIG = MIX_W + MLP_W
CHAN_SHARDED = ('conv_w', 'rg_lambda', 'rg_ba', 'rg_bi')
S5_NAMES = ('s5_a_re', 's5_a_im', 's5_log_dt', 's5_b_re', 's5_b_im', 's5_c_re', 's5_c_im', 's5_d')
MESH = pl.DeviceIdType.MESH


def _tile(n, pref, align):
    t = (min(pref, n) // align) * align
    while t >= align:
        if n % t == 0:
            return t
        t -= align
    return n


def _params(sem):
    return pltpu.CompilerParams(dimension_semantics=sem, vmem_limit_bytes=VMEM_LIMIT)


def _sigmoid(v):
    return 0.5 * jnp.tanh(0.5 * v) + 0.5


def _silu(v):
    return v * _sigmoid(v)


_GELU_K = math.sqrt(2.0 / math.pi)


def _gelu(v):
    return 0.5 * v * (1.0 + jnp.tanh(_GELU_K * (v + 0.044715 * v * v * v)))


def _gelu_grad(v):
    th = jnp.tanh(_GELU_K * (v + 0.044715 * v * v * v))
    return 0.5 * (1.0 + th) + 0.5 * v * (1.0 - th * th) * _GELU_K * (1.0 + 3.0 * 0.044715 * v * v)


def _one_minus_sq(la, a):
    v = 2.0 * la
    series = -v * (1.0 + v * (0.5 + v * (1.0 / 6.0 + v * (1.0 / 24.0 + v * (1.0 / 120.0)))))
    return jnp.where(v > -0.1, series, 1.0 - a * a)


def _softplus(v):
    return jnp.maximum(v, 0.0) + jnp.log(1.0 + jnp.exp(-jnp.abs(v)))


def _dot(a, b):
    return jnp.dot(a.astype(MXU_DT), b.astype(MXU_DT), preferred_element_type=F32)


def _dot_tn(a, b):
    return lax.dot_general(a.astype(MXU_DT), b.astype(MXU_DT), (((0,), (0,)), ((), ())), preferred_element_type=F32)


def _dot_nt(a, b):
    return lax.dot_general(a.astype(MXU_DT), b.astype(MXU_DT), (((1,), (1,)), ((), ())), preferred_element_type=F32)


def _mm(a, b, *, ta=False, tb=False, b_blocked=False, out_blocks=0, bias=None, a_fn=None, epi=None, extras=(),
        colsum=False, out_t=False, out_dtype=F32, name, tm=1088, tn=1024, tk=1024):
    if ta:
        kdim, m = a.shape
    else:
        m, kdim = a.shape
    bcol = b.shape[2] if b_blocked else None
    blog = (b.shape[1], b.shape[0] * b.shape[2]) if b_blocked else b.shape
    if tb:
        n, kb = blog
    else:
        kb, n = blog
    assert kdim == kb, (a.shape, b.shape, ta, tb)
    assert not (ta and tb)
    tm = _tile(m, tm, 128 if ta else 16)
    n_lim = bcol if (b_blocked and not tb) else (n // out_blocks if out_blocks else n)
    tn = _tile(n_lim, tn, 128)
    tk = _tile(bcol if (b_blocked and tb) else kdim, tk, 16 if ta else 128)
    nk = kdim // tk
    a_spec = pl.BlockSpec((tk, tm), lambda i, j, k: (k, i)) if ta else pl.BlockSpec((tm, tk), lambda i, j, k: (i, k))
    if not b_blocked:
        b_spec = pl.BlockSpec((tn, tk), lambda i, j, k: (j, k)) if tb else pl.BlockSpec((tk, tn), lambda i, j, k: (k, j))
    elif tb:
        qk = bcol // tk
        b_spec = pl.BlockSpec((None, tn, tk), lambda i, j, k: (k // qk, j, k % qk))
    else:
        qn = bcol // tn
        b_spec = pl.BlockSpec((None, tk, tn), lambda i, j, k: (j // qn, k, j % qn))
    in_specs = [a_spec, b_spec]
    args = [a, b]
    has_bias = bias is not None
    if has_bias:
        in_specs.append(pl.BlockSpec((1, tn), lambda i, j, k: (0, j)))
        args.append(bias.reshape(1, n).astype(F32))
    for e in extras:
        assert e.shape == (m, n), (e.shape, m, n)
        in_specs.append(pl.BlockSpec((tm, tn), lambda i, j, k: (i, j)))
        args.append(e)
    nex = len(extras)
    dn = (((0 if ta else 1,), (1 if tb else 0,)), ((), ()))
    if out_blocks:
        qo = (n // out_blocks) // tn
        out_shape = jax.ShapeDtypeStruct((out_blocks, m, n // out_blocks), out_dtype)
        out_spec = pl.BlockSpec((None, tm, tn), lambda i, j, k: (j // qo, i, j % qo))
    else:
        out_shape = jax.ShapeDtypeStruct((m, n), out_dtype)
        out_spec = pl.BlockSpec((tm, tn), lambda i, j, k: (i, j))
    out_shapes, out_specs = [out_shape], [out_spec]
    grid = (m // tm, n // tn, nk)
    assert not (colsum and out_t)
    if out_t:
        out_shapes.append(jax.ShapeDtypeStruct((n, m), out_dtype))
        out_specs.append(pl.BlockSpec((tn, tm), lambda i, j, k: (j, i)))
    if colsum:
        out_shapes.append(jax.ShapeDtypeStruct((1, n), F32))
        out_specs.append(pl.BlockSpec((1, tn), lambda i, j, k: (0, j)))
        swap = lambda sp: pl.BlockSpec(sp.block_shape, lambda j, i, k, f=sp.index_map: f(i, j, k))
        in_specs = [swap(sp) for sp in in_specs]
        out_specs = [swap(sp) for sp in out_specs]
        grid = (n // tn, m // tm, nk)
    nout = len(out_shapes)

    def body(*refs):
        a_ref, b_ref = refs[0], refs[1]
        row_tile = pl.program_id(1)
        pos = 2
        bias_ref = refs[pos] if has_bias else None
        pos += int(has_bias)
        ex_refs = refs[pos:pos + nex]
        o_ref = refs[pos + nex]
        av = a_ref[...]
        if a_fn is not None:
            av = a_fn(av.astype(F32))
        part = lax.dot_general(av.astype(MXU_DT), b_ref[...].astype(MXU_DT), dn, preferred_element_type=F32)

        def finish(r):
            if has_bias:
                r = r + bias_ref[...]
            if epi is not None:
                r = epi(r, *[e[...] for e in ex_refs])
            o_ref[...] = r.astype(out_dtype)
            if out_t:
                refs[pos + nex + 1][...] = r.T.astype(out_dtype)
            if colsum:
                cs_ref = refs[pos + nex + 1]

                @pl.when(row_tile == 0)
                def _():
                    cs_ref[...] = _colsum(r)

                @pl.when(row_tile > 0)
                def _():
                    cs_ref[...] += _colsum(r)

        if nk == 1:
            finish(part)
            return
        acc_ref = refs[pos + nex + nout]
        k = pl.program_id(2)

        @pl.when(k == 0)
        def _():
            acc_ref[...] = part

        @pl.when(k > 0)
        def _():
            acc_ref[...] += part

        @pl.when(k == nk - 1)
        def _():
            finish(acc_ref[...])

    res = pl.pallas_call(
        body, out_shape=tuple(out_shapes), grid=grid, in_specs=in_specs, out_specs=tuple(out_specs),
        scratch_shapes=[pltpu.VMEM((tm, tn), F32)] if nk > 1 else [],
        compiler_params=_params(("parallel", "arbitrary" if colsum else "parallel", "arbitrary")), name=name)(*args)
    return res if (colsum or out_t) else res[0]


def _ew(fn, tiles, rows=(), mods=(), outs=(), accs=(), *, name, nctx_tiles=0, tr=None, transposed=False):
    bsz, tlen = tiles[0][0].shape[0], tiles[0][0].shape[1]
    if tr is None:
        tr = _tile(tlen, ROW_TILE, 8)
    nt = tlen // tr
    in_specs, args = [], []
    for arr, cb, width, toff in tiles:
        in_specs.append(pl.BlockSpec((1, tr, width), functools.partial(
            lambda b, t, cb, toff: (b, jnp.maximum(t - toff, 0), cb), cb=cb, toff=toff)))
        args.append(arr)
    for r in rows:
        in_specs.append(pl.BlockSpec(r.shape, lambda b, t: (0, 0)))
        args.append(r)

    def seg_of(t):
        return jnp.where(t >= nctx_tiles, 1, 0)

    for mo in mods:
        in_specs.append(pl.BlockSpec((1, 1) + mo.shape[2:], lambda b, t: (b, seg_of(t), 0, 0)))
        args.append(mo)
    out_shape, out_specs = [], []
    for width, dt in outs:
        out_shape.append(jax.ShapeDtypeStruct((bsz, tlen, width), dt))
        out_specs.append(pl.BlockSpec((1, tr, width), lambda b, t: (b, t, 0)))
    for kk, cc in accs:
        out_shape.append(jax.ShapeDtypeStruct((bsz, 2, kk, cc), F32))
        out_specs.append(pl.BlockSpec((1, 1, kk, cc), lambda b, t: (b, seg_of(t), 0, 0)))
    nti, nr, nm, no, na = len(tiles), len(rows), len(mods), len(outs), len(accs)
    if transposed:
        out_shape.append(jax.ShapeDtypeStruct((outs[0][0], bsz * tlen), outs[0][1]))
        out_specs.append(pl.BlockSpec((outs[0][0], tr), lambda b, t: (0, b * nt + t)))

    def body(*refs):
        t = pl.program_id(1)
        tv = [r[0] for r in refs[:nti]]
        rv = [r[...] for r in refs[nti:nti + nr]]
        mv = [r[0, 0] for r in refs[nti + nr:nti + nr + nm]]
        o_refs = refs[nti + nr + nm:nti + nr + nm + no]
        a_refs = refs[nti + nr + nm + no:nti + nr + nm + no + na]
        seg = seg_of(t)
        ov, av = fn(tv, rv, mv, seg)
        for r, v in zip(o_refs, ov):
            r[0] = v.astype(r.dtype)
        if transposed:
            refs[-1][...] = ov[0].astype(F32).T.astype(refs[-1].dtype)
        if na:
            @pl.when((t == 0) | (t == nctx_tiles))
            def _():
                for r in a_refs:
                    r[...] = jnp.zeros_like(r)

            for r, v in zip(a_refs, av):
                r[0, 0] += v

    res = pl.pallas_call(
        body, out_shape=tuple(out_shape), grid=(bsz, nt), in_specs=in_specs, out_specs=tuple(out_specs),
        compiler_params=_params(("arbitrary", "arbitrary")), name=name)(*args)
    return (res[:no], res[no:no + na], res[-1]) if transposed else (res[:no], res[no:])


def _full(arr):
    return (arr, 0, arr.shape[-1], 0)


def _colsum(v):
    return jnp.sum(v, axis=0, keepdims=True)


def _shifted(x, prev8, next8, first, last, k):
    tr = x.shape[0]
    rid = lax.broadcasted_iota(jnp.int32, x.shape, 0)
    keep_prev = jnp.where(first, 0.0, 1.0)
    keep_next = jnp.where(last, 0.0, 1.0)
    if k == -1:
        return jnp.where(rid == 0, prev8[7:8] * keep_prev, pltpu.roll(x, 1, 0))
    if k == -2:
        r = pltpu.roll(x, 2, 0)
        r = jnp.where(rid == 1, prev8[7:8] * keep_prev, r)
        return jnp.where(rid == 0, prev8[6:7] * keep_prev, r)
    if k == 1:
        return jnp.where(rid == tr - 1, next8[0:1] * keep_next, pltpu.roll(x, tr - 1, 0))
    if k == 2:
        r = pltpu.roll(x, tr - 2, 0)
        r = jnp.where(rid == tr - 2, next8[0:1] * keep_next, r)
        return jnp.where(rid == tr - 1, next8[1:2] * keep_next, r)
    raise ValueError(k)


def _halo_specs(tr, width, cb, n8):
    cur = pl.BlockSpec((1, tr, width), lambda b, t: (b, t, cb))
    prev = pl.BlockSpec((1, 8, width), lambda b, t: (b, jnp.maximum(t * (tr // 8) - 1, 0), cb))
    nxt = pl.BlockSpec((1, 8, width), lambda b, t: (b, jnp.minimum((t + 1) * (tr // 8), n8 - 1), cb))
    return [cur, prev, nxt]


def _conv_fwd(z3, conv_w, conv_b, d, nctx_tiles, tr, name):
    bsz, tlen, _ = z3.shape
    nt = tlen // tr

    def body(x_ref, xp_ref, xn_ref, w_ref, b_ref, o_ref):
        t = pl.program_id(1)
        first = (t == 0) | (t == nctx_tiles)
        last = (t == nctx_tiles - 1) | (t == nt - 1)
        x, p8, n8 = x_ref[0], xp_ref[0], xn_ref[0]
        w = w_ref[...]
        o_ref[0] = (b_ref[...] + w[0:1] * _shifted(x, p8, n8, first, last, -1) + w[1:2] * x
                    + w[2:3] * _shifted(x, p8, n8, first, last, 1) + w[3:4] * _shifted(x, p8, n8, first, last, 2))

    return pl.pallas_call(
        body, out_shape=jax.ShapeDtypeStruct((bsz, tlen, d), F32), grid=(bsz, nt),
        in_specs=_halo_specs(tr, d, 0, tlen // 8) + [pl.BlockSpec((4, d), lambda b, t: (0, 0)),
                                                      pl.BlockSpec((1, d), lambda b, t: (0, 0))],
        out_specs=pl.BlockSpec((1, tr, d), lambda b, t: (b, t, 0)),
        compiler_params=_params(("parallel", "parallel")), name=name)(z3, z3, z3, conv_w, conv_b.reshape(1, d))


def _conv_bwd(dxc, z3, conv_w, d, nctx_tiles, tr, name):
    bsz, tlen, _ = dxc.shape
    nt = tlen // tr

    def body(g_ref, gp_ref, gn_ref, x_ref, xp_ref, xn_ref, w_ref, o_ref, acc_ref):
        t = pl.program_id(1)
        first = (t == 0) | (t == nctx_tiles)
        last = (t == nctx_tiles - 1) | (t == nt - 1)
        g, gp, gn = g_ref[0], gp_ref[0], gn_ref[0]
        x, xp, xn = x_ref[0], xp_ref[0], xn_ref[0]
        w = w_ref[...]
        o_ref[0] = (w[0:1] * _shifted(g, gp, gn, first, last, 1) + w[1:2] * g
                    + w[2:3] * _shifted(g, gp, gn, first, last, -1) + w[3:4] * _shifted(g, gp, gn, first, last, -2))

        @pl.when(t == 0)
        def _():
            acc_ref[...] = jnp.zeros_like(acc_ref)

        acc_ref[0, 0:1] += _colsum(g * _shifted(x, xp, xn, first, last, -1))
        acc_ref[0, 1:2] += _colsum(g * x)
        acc_ref[0, 2:3] += _colsum(g * _shifted(x, xp, xn, first, last, 1))
        acc_ref[0, 3:4] += _colsum(g * _shifted(x, xp, xn, first, last, 2))
        acc_ref[0, 4:5] += _colsum(g)

    return pl.pallas_call(
        body, out_shape=(jax.ShapeDtypeStruct((bsz, tlen, d), F32), jax.ShapeDtypeStruct((bsz, 8, d), F32)),
        grid=(bsz, nt),
        in_specs=_halo_specs(tr, d, 0, tlen // 8) + _halo_specs(tr, d, 0, tlen // 8)
        + [pl.BlockSpec((4, d), lambda b, t: (0, 0))],
        out_specs=(pl.BlockSpec((1, tr, d), lambda b, t: (b, t, 0)), pl.BlockSpec((1, 8, d), lambda b, t: (b, 0, 0))),
        compiler_params=_params(("arbitrary", "arbitrary")), name=name)(dxc, dxc, dxc, z3, z3, z3, conv_w)


def _rg_gates(x, lam, wa_ref, ba, wi_ref, bi, nh, hd):
    sp = _softplus(-lam)
    prs, pis = [], []
    for h in range(nh):
        xh = x[:, h * hd:(h + 1) * hd]
        prs.append(_dot(xh, wa_ref[h]))
        pis.append(_dot(xh, wi_ref[h]))
    r = 1.0 / (1.0 + jnp.exp(-(jnp.concatenate(prs, axis=1) + ba)))
    i = _sigmoid(jnp.concatenate(pis, axis=1) + bi)
    la = -RG_C * sp * r
    a = jnp.exp(la)
    m2 = _one_minus_sq(la, a)
    return sp, r, i, a, jnp.sqrt(m2), lax.rsqrt(m2)


def _scan_tile(t, nctx_tiles, nt, reverse):
    if not reverse:
        return t
    return jnp.where(t < nctx_tiles, nctx_tiles - 1 - t, nt - 1 - (t - nctx_tiles))


def _unscan_tile(t, nctx_tiles, nt, reverse):
    if not reverse:
        return nt - 1 - t
    return jnp.where(t < nt - nctx_tiles, nctx_tiles + t, t - (nt - nctx_tiles))


def _rg_param_specs(d, nh, hd):
    vec = pl.BlockSpec((1, d), lambda b, t: (0, 0))
    mat = pl.BlockSpec((nh, hd, hd), lambda b, t: (0, 0, 0))
    return [vec, mat, vec, mat, vec]


def _ride_split(refs, n_in, n_ride, n_out, n_scratch):
    pos = [n_in, n_in + n_ride, n_in + n_ride + n_out, n_in + 2 * n_ride + n_out, n_in + 2 * n_ride + n_out + n_scratch]
    return (refs[:pos[0]], refs[pos[0]:pos[1]], refs[pos[1]:pos[2]], refs[pos[2]:pos[3]], refs[pos[3]:pos[4]],
            refs[pos[4]:])


def _rg_fwd(xc, lam, wa, ba, wi, bi, reverse, nctx_tiles, tr, name, ride=None):
    bsz, tlen, d = xc.shape
    nh, hd = wa.shape[0], wa.shape[1]
    nt = tlen // tr
    tmap = lambda b, t: (b, _scan_tile(t, nctx_tiles, nt, reverse), 0)
    rarrs, rgather = ride if ride else ([], True)
    nr = len(rarrs)

    def body(*refs):
        (x_ref, lam_ref, wa_ref, ba_ref, wi_ref, bi_ref), rin, (h_ref,), rout, (a_scr, b_scr, carry), sems = _ride_split(
            refs, 6, nr, 1, 3)
        first = (pl.program_id(0) == 0) & (pl.program_id(1) == 0)
        last = (pl.program_id(0) == bsz - 1) & (pl.program_id(1) == nt - 1)
        if nr:
            @pl.when(first)
            def _():
                _exchange_start(rin, rout, sems, rgather)

        @pl.when(pl.program_id(1) == 0)
        def _():
            carry[...] = jnp.zeros_like(carry)

        x = x_ref[0]
        _, _, i, a, mult, _ = _rg_gates(x, lam_ref[...], wa_ref, ba_ref[...], wi_ref, bi_ref[...], nh, hd)
        a_scr[...] = a
        b_scr[...] = mult * (i * x)

        def blk(j, h):
            for r in range(8):
                row = (tr - 1 - (j * 8 + r)) if reverse else (j * 8 + r)
                h = a_scr[pl.ds(row, 1), :] * h + b_scr[pl.ds(row, 1), :]
                h_ref[0, pl.ds(row, 1), :] = h
            return h

        carry[0:1, :] = lax.fori_loop(0, tr // 8, blk, carry[0:1, :])
        if nr:
            @pl.when(last)
            def _():
                _exchange_finish(rin, rout, sems, rgather)

    res = pl.pallas_call(
        body, out_shape=(jax.ShapeDtypeStruct((bsz, tlen, d), F32),) + _exchange_out_shapes(rarrs), grid=(bsz, nt),
        in_specs=[pl.BlockSpec((1, tr, d), tmap)] + _rg_param_specs(d, nh, hd) + [_ANY] * nr,
        out_specs=(pl.BlockSpec((1, tr, d), tmap),) + (_ANY,) * nr,
        scratch_shapes=[pltpu.VMEM((tr, d), F32), pltpu.VMEM((tr, d), F32), pltpu.VMEM((8, d), F32)]
        + (_exchange_sems(nr) if nr else []),
        compiler_params=_params(("arbitrary", "arbitrary")), name=name)(
            xc, lam.reshape(1, d), wa, ba.reshape(1, d), wi, bi.reshape(1, d), *rarrs)
    return res[0], list(res[1:])


def _rg_bwd(xc, h, dmix, z3, lam, wa, ba, wi, bi, addend, reverse, nctx_tiles, tr, name, ride=None):
    bsz, tlen, d = xc.shape
    nh, hd = wa.shape[0], wa.shape[1]
    nt = tlen // tr
    r8 = tr // 8

    def tile_of(t):
        return _unscan_tile(t, nctx_tiles, nt, reverse)

    tmap = lambda b, t: (b, tile_of(t), 0)
    gmap = lambda b, t: (b, tile_of(t), 1)

    def halo_map(b, t):
        tt = tile_of(t)
        if not reverse:
            return (b, jnp.maximum(tt * r8 - 1, 0), 0)
        return (b, jnp.where(tt == nt - 1, 0, (tt + 1) * r8), 0)

    has_add = addend is not None
    rarrs, rgather = ride if ride else ([], True)
    nr = len(rarrs)

    def body(*refs):
        ins, rin, (dx_ref, dwa_ref, dwi_ref, dv_ref), rout, (a_scr, g_scr, dh_scr, carry), sems = _ride_split(
            refs, 10 + int(has_add), nr, 4, 4)
        x_ref, h_ref, halo_ref, dr_ref, z_ref, lam_ref, wa_ref, ba_ref, wi_ref, bi_ref = ins[:10]
        add_ref = ins[10] if has_add else None
        b = pl.program_id(0)
        t = pl.program_id(1)
        tt = tile_of(t)
        if nr:
            @pl.when((b == 0) & (t == 0))
            def _():
                _exchange_start(rin, rout, sems, rgather)

        @pl.when((b == 0) & (t == 0))
        def _():
            dwa_ref[...] = jnp.zeros_like(dwa_ref)
            dwi_ref[...] = jnp.zeros_like(dwi_ref)
            dv_ref[...] = jnp.zeros_like(dv_ref)

        @pl.when(t == 0)
        def _():
            carry[...] = jnp.zeros_like(carry)

        x = x_ref[0]
        sp, r, i, a, mult, inv_mult = _rg_gates(x, lam_ref[...], wa_ref, ba_ref[...], wi_ref, bi_ref[...], nh, hd)
        a_scr[...] = a
        dh_scr[...] = dr_ref[0] * _gelu(z_ref[0])

        def blk(j, cc):
            for rr in range(8):
                row = (j * 8 + rr) if reverse else (tr - 1 - (j * 8 + rr))
                g = dh_scr[pl.ds(row, 1), :] + cc
                g_scr[pl.ds(row, 1), :] = g
                cc = a_scr[pl.ds(row, 1), :] * g
            return cc

        carry[0:1, :] = lax.fori_loop(0, r8, blk, carry[0:1, :])
        g = g_scr[...]
        hv = h_ref[0]
        rid = lax.broadcasted_iota(jnp.int32, hv.shape, 0)
        if not reverse:
            valid = jnp.where(tt > 0, 1.0, 0.0)
            hprev = jnp.where(rid == 0, halo_ref[0][7:8] * valid, pltpu.roll(hv, 1, 0))
        else:
            valid = jnp.where(tt == nctx_tiles - 1, 0.0, 1.0)
            hprev = jnp.where(rid == tr - 1, halo_ref[0][0:1] * valid, pltpu.roll(hv, tr - 1, 0))
        dla = g * hprev * a - g * (i * x) * (a * a) * inv_mult
        dpr = dla * (-RG_C * sp) * r * (1.0 - r)
        dpi = g * mult * x * i * (1.0 - i)
        dx = g * mult * i
        dxs = []
        for hh in range(nh):
            sl = slice(hh * hd, (hh + 1) * hd)
            dxs.append(_dot_nt(dpr[:, sl], wa_ref[hh]) + _dot_nt(dpi[:, sl], wi_ref[hh]))
            dwa_ref[hh] += _dot_tn(x[:, sl], dpr[:, sl])
            dwi_ref[hh] += _dot_tn(x[:, sl], dpi[:, sl])
        dx = dx + jnp.concatenate(dxs, axis=1)
        if has_add:
            dx = dx + add_ref[0]
        dx_ref[0] = dx
        dv_ref[0:1] += _colsum(dla * (-RG_C * r))
        dv_ref[1:2] += _colsum(dpr)
        dv_ref[2:3] += _colsum(dpi)
        if nr:
            @pl.when((b == bsz - 1) & (t == nt - 1))
            def _():
                _exchange_finish(rin, rout, sems, rgather)

    in_specs = [pl.BlockSpec((1, tr, d), tmap), pl.BlockSpec((1, tr, d), tmap), pl.BlockSpec((1, 8, d), halo_map),
                pl.BlockSpec((1, tr, d), tmap), pl.BlockSpec((1, tr, d), gmap)] + _rg_param_specs(d, nh, hd)
    args = [xc, h, h, dmix, z3, lam.reshape(1, d), wa, ba.reshape(1, d), wi, bi.reshape(1, d)]
    if has_add:
        in_specs.append(pl.BlockSpec((1, tr, d), tmap))
        args.append(addend)
    mat = pl.BlockSpec((nh, hd, hd), lambda b, t: (0, 0, 0))
    res = pl.pallas_call(
        body, out_shape=(jax.ShapeDtypeStruct((bsz, tlen, d), F32), jax.ShapeDtypeStruct((nh, hd, hd), F32),
                         jax.ShapeDtypeStruct((nh, hd, hd), F32), jax.ShapeDtypeStruct((8, d), F32))
        + _exchange_out_shapes(rarrs),
        grid=(bsz, nt), in_specs=in_specs + [_ANY] * nr,
        out_specs=(pl.BlockSpec((1, tr, d), tmap), mat, mat, pl.BlockSpec((8, d), lambda b, t: (0, 0))) + (_ANY,) * nr,
        scratch_shapes=[pltpu.VMEM((tr, d), F32), pltpu.VMEM((tr, d), F32), pltpu.VMEM((tr, d), F32),
                        pltpu.VMEM((8, d), F32)] + (_exchange_sems(nr) if nr else []),
        compiler_params=_params(("arbitrary", "arbitrary")), name=name)(*args, *rarrs)
    return res[0], res[1], res[2], res[3], list(res[4:])


def _s5_operators(a_re, a_im, log_dt, b_re, b_im, c_re, c_im, d_skip):
    tt = S5_T
    g, h = d_skip.shape
    th = tt * h
    dt = jnp.exp(log_dt)[..., None]
    xr, xi = a_re * dt, a_im * dt
    taus = jnp.arange(tt + 1, dtype=F32)[None, None, :, None]
    mag = jnp.exp(xr[:, :, None, :] * taus)
    pw_re, pw_im = mag * jnp.cos(xi[:, :, None, :] * taus), mag * jnp.sin(xi[:, :, None, :] * taus)
    nr, ni = pw_re[:, :, 1] - 1.0, pw_im[:, :, 1]
    den = a_re * a_re + a_im * a_im
    cf_re, cf_im = (nr * a_re + ni * a_im) / den, (ni * a_re - nr * a_im) / den
    bt_re, bt_im = jnp.swapaxes(b_re, 2, 3), jnp.swapaxes(b_im, 2, 3)
    bb_re = cf_re[:, :, None] * bt_re - cf_im[:, :, None] * bt_im
    bb_im = cf_re[:, :, None] * bt_im + cf_im[:, :, None] * bt_re

    def outer(p_re, p_im, q_re, q_im):
        pr, pi = p_re[:, :, None, :], p_im[:, :, None, :]
        qr, qi = q_re[:, None], q_im[:, None]
        return (pr * qr - pi * qi).reshape(g, -1, pr.shape[-1]), (pr * qi + pi * qr).reshape(g, -1, pr.shape[-1])

    ops, kerns = [], []
    for dd in range(2):
        e_re, e_im = outer(pw_re[dd], pw_im[dd], c_re[dd], c_im[dd])
        kerns.append(jnp.einsum('gic,gmc->gim', jnp.concatenate([bb_re[dd], -bb_im[dd]], -1),
                                jnp.concatenate([e_re[:, :th], e_im[:, :th]], -1), precision=lax.Precision.HIGHEST))
        v_re, v_im = e_re[:, h:].reshape(g, tt, h, -1), e_im[:, h:].reshape(g, tt, h, -1)
        if dd == 1:
            v_re, v_im = v_re[:, ::-1], v_im[:, ::-1]
        vt = jnp.concatenate([v_re, -v_im], -1).reshape(g, th, -1)
        pr, pi = pw_re[dd][:, :tt], pw_im[dd][:, :tt]
        if dd == 0:
            pr, pi = pr[:, ::-1], pi[:, ::-1]
        w_re, w_im = outer(pr, pi, bb_re[dd], bb_im[dd])
        ops.append((jnp.concatenate([w_re, w_im], -1), jnp.swapaxes(vt, 1, 2)))
    kb_rev = kerns[1].reshape(g, h, tt, h)[:, :, ::-1].reshape(g, h, th)
    blocks = []
    for s in range(tt):
        fwd = jnp.pad(kerns[0][:, :, :th - s * h], ((0, 0), (0, 0), (s * h, 0)))
        bwd = jnp.pad(kb_rev[:, :, (tt - 1 - s) * h:], ((0, 0), (0, 0), (0, (tt - 1 - s) * h)))
        blocks.append(fwd + bwd)
    tz = jnp.stack(blocks, axis=1).reshape(g, th, th)
    tz = tz + jnp.eye(th, dtype=F32)[None] * jnp.tile(d_skip, (1, tt))[:, None, :]
    rows = []
    for dd in range(2):
        re, im = pw_re[dd][:, tt], pw_im[dd][:, tt]
        rows += [jnp.concatenate([re, re], -1), jnp.concatenate([-im, im], -1)]
    return tz, ops[0][0], ops[1][0], ops[0][1], ops[1][1], jnp.stack(rows)


def _chunk_order(j, ncc, nc):
    return jnp.where(j < ncc, ncc - 1 - j, nc - 1 - (j - ncc))


def _s5_specs(nc, ops, lm):
    gb = S5_GB
    act = lambda n_chunks, width: pl.BlockSpec((1, gb * n_chunks, width), lambda i, b: (b, i, 0))
    opspecs = [pl.BlockSpec((gb,) + o.shape[1:], lambda i, b: (i, 0, 0)) for o in ops]
    lspec = pl.BlockSpec((4, gb, lm.shape[-1]), lambda i, b: (0, i, 0))
    return act, opspecs, lspec


def _s5_fwd(u, ops, lm, ncc, name, ride=None):
    bsz, gn, th = u.shape
    g = ops[0].shape[0]
    nc = gn // g
    gb = S5_GB
    p2 = ops[1].shape[-1]
    ph = p2 // 2
    rarrs, rgather = ride if ride else ([], True)
    nr = len(rarrs)

    def body(*refs):
        ((u_ref, tz_ref, wf_ref, wb_ref, vf_ref, vb_ref, l_ref), rin, (y_ref, hf_ref, hb_ref), rout,
         (sf, sfs, sb, sbs), sems) = _ride_split(refs, 7, nr, 3, 4)
        if nr:
            @pl.when((pl.program_id(0) == 0) & (pl.program_id(1) == 0))
            def _():
                _exchange_start(rin, rout, sems, rgather)

        a1f, a2f, a1b, a2b = l_ref[0], l_ref[1], l_ref[2], l_ref[3]
        for gi in range(gb):
            ug = u_ref[0, pl.ds(gi * nc, nc), :]
            s1 = _dot(ug, wf_ref[gi])
            s2 = _dot(ug, wb_ref[gi])
            sf[pl.ds(gi, nc, stride=gb), :] = s1
            sfs[pl.ds(gi, nc, stride=gb), :] = pltpu.roll(s1, ph, 1)
            sb[pl.ds(gi, nc, stride=gb), :] = s2
            sbs[pl.ds(gi, nc, stride=gb), :] = pltpu.roll(s2, ph, 1)

        def step(j, hs):
            hf, hfs, hb, hbs = hs
            rf = pl.ds(pl.multiple_of(j * gb, gb), gb)
            rb = pl.ds(pl.multiple_of(_chunk_order(j, ncc, nc) * gb, gb), gb)
            s1, s1s, s2, s2s = sf[rf, :], sfs[rf, :], sb[rb, :], sbs[rb, :]
            sf[rf, :] = hf
            sb[rb, :] = hb
            return (a1f * hf + a2f * hfs + s1, a1f * hfs - a2f * hf + s1s,
                    a1b * hb + a2b * hbs + s2, a1b * hbs - a2b * hb + s2s)

        zero = jnp.zeros((gb, p2), F32)
        lax.fori_loop(0, nc, step, (zero, zero, zero, zero))
        for gi in range(gb):
            rows = pl.ds(gi * nc, nc)
            hfg = sf[pl.ds(gi, nc, stride=gb), :]
            hbg = sb[pl.ds(gi, nc, stride=gb), :]
            hf_ref[0, rows, :] = hfg
            hb_ref[0, rows, :] = hbg
            y_ref[0, rows, :] = (_dot(u_ref[0, rows, :], tz_ref[gi]) + _dot(hfg, vf_ref[gi])
                                 + _dot(hbg, vb_ref[gi])).astype(y_ref.dtype)
        if nr:
            @pl.when((pl.program_id(0) == g // gb - 1) & (pl.program_id(1) == bsz - 1))
            def _():
                _exchange_finish(rin, rout, sems, rgather)

    act, opspecs, lspec = _s5_specs(nc, ops, lm)
    res = pl.pallas_call(
        body, out_shape=(jax.ShapeDtypeStruct(u.shape, ACT_DT), jax.ShapeDtypeStruct((bsz, gn, p2), F32),
                         jax.ShapeDtypeStruct((bsz, gn, p2), F32)) + _exchange_out_shapes(rarrs),
        grid=(g // gb, bsz), in_specs=[act(nc, th)] + opspecs + [lspec] + [_ANY] * nr,
        out_specs=(act(nc, th), act(nc, p2), act(nc, p2)) + (_ANY,) * nr,
        scratch_shapes=[pltpu.VMEM((gb * nc, p2), F32) for _ in range(4)] + (_exchange_sems(nr) if nr else []),
        compiler_params=_params(("arbitrary", "arbitrary")), name=name)(u, *ops, lm, *rarrs)
    return res[0], res[1], res[2], list(res[3:])


def _s5_bwd(dy, u, hf, hb, ops_t, lm, ncc, name):
    bsz, gn, th = u.shape
    g = ops_t[0].shape[0]
    nc = gn // g
    gb = S5_GB
    p2 = lm.shape[-1]
    ph = p2 // 2

    def body(dy_ref, u_ref, hf_ref, hb_ref, tzt_ref, wft_ref, wbt_ref, vft_ref, vbt_ref, l_ref,
             du_ref, dtz_ref, dwf_ref, dwb_ref, dvf_ref, dvb_ref, dl_ref, jf, jfs, jb, jbs, hfk, hbk):
        a1f, a2f, a1b, a2b = l_ref[0], l_ref[1], l_ref[2], l_ref[3]

        @pl.when(pl.program_id(1) == 0)
        def _():
            for r in (dtz_ref, dwf_ref, dwb_ref, dvf_ref, dvb_ref, dl_ref):
                r[...] = jnp.zeros_like(r)

        for gi in range(gb):
            rows = pl.ds(gi * nc, nc)
            dyg = dy_ref[0, rows, :]
            i1 = _dot(dyg, vft_ref[gi])
            i2 = _dot(dyg, vbt_ref[gi])
            jf[pl.ds(gi, nc, stride=gb), :] = i1
            jfs[pl.ds(gi, nc, stride=gb), :] = pltpu.roll(i1, ph, 1)
            jb[pl.ds(gi, nc, stride=gb), :] = i2
            jbs[pl.ds(gi, nc, stride=gb), :] = pltpu.roll(i2, ph, 1)
            hfk[pl.ds(gi, nc, stride=gb), :] = hf_ref[0, rows, :]
            hbk[pl.ds(gi, nc, stride=gb), :] = hb_ref[0, rows, :]

        def step(j, carry):
            qf, qfs, qb, qbs, d1f, d2f, d1b, d2b = carry
            rf = pl.ds(pl.multiple_of((nc - 1 - j) * gb, gb), gb)
            rb = pl.ds(pl.multiple_of(_chunk_order(nc - 1 - j, ncc, nc) * gb, gb), gb)
            i1, i1s, i2, i2s = jf[rf, :], jfs[rf, :], jb[rb, :], jbs[rb, :]
            h1, h2 = hfk[rf, :], hbk[rb, :]
            jf[rf, :] = qf
            jb[rb, :] = qb
            return (i1 + a1f * qf - a2f * qfs, i1s + a1f * qfs + a2f * qf,
                    i2 + a1b * qb - a2b * qbs, i2s + a1b * qbs + a2b * qb,
                    d1f + qf * h1, d2f + qfs * h1, d1b + qb * h2, d2b + qbs * h2)

        zero = jnp.zeros((gb, p2), F32)
        fin = lax.fori_loop(0, nc, step, (zero,) * 8)
        dl_ref[0] += fin[4]
        dl_ref[1] += pltpu.roll(fin[5], ph, 1)
        dl_ref[2] += fin[6]
        dl_ref[3] += pltpu.roll(fin[7], ph, 1)
        for gi in range(gb):
            rows = pl.ds(gi * nc, nc)
            dyg = dy_ref[0, rows, :]
            ug = u_ref[0, rows, :]
            dsf = jf[pl.ds(gi, nc, stride=gb), :]
            dsb = jb[pl.ds(gi, nc, stride=gb), :]
            du_ref[0, rows, :] = (_dot(dyg, tzt_ref[gi]) + _dot(dsf, wft_ref[gi]) + _dot(dsb, wbt_ref[gi])).astype(du_ref.dtype)
            dtz_ref[gi] += _dot_tn(ug, dyg)
            dwf_ref[gi] += _dot_tn(ug, dsf)
            dwb_ref[gi] += _dot_tn(ug, dsb)
            dvf_ref[gi] += _dot_tn(hf_ref[0, rows, :], dyg)
            dvb_ref[gi] += _dot_tn(hb_ref[0, rows, :], dyg)

    act, opspecs, lspec = _s5_specs(nc, ops_t, lm)
    gshape = lambda o: jax.ShapeDtypeStruct(o.shape[:1] + o.shape[1:][::-1], F32)
    gspec = lambda o: pl.BlockSpec((gb,) + o.shape[1:][::-1], lambda i, b: (i, 0, 0))
    res = pl.pallas_call(
        body, out_shape=tuple([jax.ShapeDtypeStruct(u.shape, ACT_DT)] + [gshape(o) for o in ops_t]
                              + [jax.ShapeDtypeStruct(lm.shape, F32)]),
        grid=(g // gb, bsz),
        in_specs=[act(nc, th), act(nc, th), act(nc, p2), act(nc, p2)] + opspecs + [lspec],
        out_specs=tuple([act(nc, th)] + [gspec(o) for o in ops_t] + [lspec]),
        scratch_shapes=[pltpu.VMEM((gb * nc, p2), F32) for _ in range(6)],
        compiler_params=_params(("parallel", "arbitrary")), name=name)(dy, u, hf, hb, *ops_t, lm)
    return res[0], tuple(res[1:])


def _to_chunks(s, ctx_len, g):
    bsz, tlen, d = s.shape
    h = d // g
    seq = tlen - ctx_len
    rows = seq // GRID_W
    cpart = s[:, :ctx_len].reshape(bsz, ctx_len // S5_T, S5_T, g, h).transpose(0, 3, 1, 2, 4)
    lpart = s[:, ctx_len:].reshape(bsz, rows, GRID_W, g, h).transpose(0, 3, 2, 1, 4)
    cpart = cpart.reshape(bsz, g, ctx_len // S5_T, S5_T * h)
    lpart = lpart.reshape(bsz, g, seq // S5_T, S5_T * h)
    return jnp.concatenate([cpart, lpart], axis=2).reshape(bsz, g * (tlen // S5_T), S5_T * h)


def _from_chunks(y, ctx_len, g):
    bsz, gn, th = y.shape
    nc = gn // g
    h = th // S5_T
    ncc = ctx_len // S5_T
    seq = (nc - ncc) * S5_T
    rows = seq // GRID_W
    y = y.reshape(bsz, g, nc, th)
    cpart = y[:, :, :ncc].reshape(bsz, g, ncc, S5_T, h).transpose(0, 2, 3, 1, 4).reshape(bsz, ctx_len, g * h)
    lpart = y[:, :, ncc:].reshape(bsz, g, GRID_W, rows, h).transpose(0, 3, 2, 1, 4).reshape(bsz, seq, g * h)
    return jnp.concatenate([cpart, lpart], axis=1)


def _me():
    return lax.axis_index("x"), lax.axis_index("y"), lax.axis_index("c")


def _peer(k):
    x, y, c = _me()
    px = (1 - x) if (k & 4) else x
    py = (1 - y) if (k & 2) else y
    pc = (1 - c) if (k & 1) else c
    return (px, py, pc), 4 * px + 2 * py + pc


def _exchange(arrs, gather, name):
    n = len(arrs)

    def body(*refs):
        _exchange_start(refs[:n], refs[n:2 * n], refs[2 * n:], gather)
        _exchange_finish(refs[:n], refs[n:2 * n], refs[2 * n:], gather)

    return pl.pallas_call(
        body, out_shape=_exchange_out_shapes(arrs), in_specs=[_ANY] * n, out_specs=tuple([_ANY] * n),
        scratch_shapes=_exchange_sems(n), name=name)(*arrs)


_ANY = pl.BlockSpec(memory_space=pl.ANY)


def _exchange_out_shapes(arrs):
    return tuple(jax.ShapeDtypeStruct((N_DEV,) + a.shape[-2:], a.dtype) for a in arrs)


def _exchange_sems(n):
    return [pltpu.SemaphoreType.DMA((n * (N_DEV - 1),)), pltpu.SemaphoreType.DMA((n * (N_DEV - 1),)),
            pltpu.SemaphoreType.DMA((n,))]


def _exchange_copies(x_refs, o_refs, sems, gather, with_recvs):
    send_sems, recv_sems, local_sems = sems
    n = len(x_refs)
    npeer = N_DEV - 1
    xi, yi, ci = _me()
    me = 4 * xi + 2 * yi + ci
    mine = [x if gather else x.at[me] for x in x_refs]
    local = [pltpu.make_async_copy(mine[i], o_refs[i].at[me], local_sems.at[i]) for i in range(n)]
    sends, recvs = [], []
    for k in range(1, N_DEV):
        dev, pid = _peer(k)
        for i in range(n):
            slot = i * npeer + k - 1
            sends.append(pltpu.make_async_remote_copy(
                src_ref=x_refs[i] if gather else x_refs[i].at[pid], dst_ref=o_refs[i].at[me],
                send_sem=send_sems.at[slot], recv_sem=recv_sems.at[slot], device_id=dev, device_id_type=MESH))
            if with_recvs:
                recvs.append(pltpu.make_async_remote_copy(
                    src_ref=mine[i], dst_ref=o_refs[i].at[pid], send_sem=send_sems.at[slot],
                    recv_sem=recv_sems.at[slot], device_id=dev, device_id_type=MESH))
    return local, sends, recvs


def _exchange_start(x_refs, o_refs, sems, gather):
    local, sends, _ = _exchange_copies(x_refs, o_refs, sems, gather, False)
    for cp in local + sends:
        cp.start()


def _exchange_finish(x_refs, o_refs, sems, gather):
    local, sends, recvs = _exchange_copies(x_refs, o_refs, sems, gather, True)
    for cp in recvs:
        cp.wait_recv()
    for cp in sends:
        cp.wait_send()
    for cp in local:
        cp.wait()


def _sum_slots(x3, name):
    _, r, cdim = x3.shape
    tr = _tile(r, 256, 16)

    def body(x_ref, o_ref):
        acc = x_ref[0].astype(F32)
        for s in range(1, N_DEV):
            acc = acc + x_ref[s].astype(F32)
        o_ref[...] = acc

    return pl.pallas_call(
        body, out_shape=jax.ShapeDtypeStruct((r, cdim), F32), grid=(r // tr,),
        in_specs=[pl.BlockSpec((N_DEV, tr, cdim), lambda i: (0, i, 0))], out_specs=pl.BlockSpec((tr, cdim), lambda i: (i, 0)),
        compiler_params=_params(("parallel",)), name=name)(x3)


def _pack(arrs, dtype, lead=0):
    flat = jnp.concatenate([a.reshape(a.shape[:lead] + (-1,)).astype(dtype) for a in arrs], axis=-1)
    n = flat.shape[-1]
    pad = -n % (PACK_W * 16)
    flat = jnp.pad(flat, [(0, 0)] * lead + [(0, pad)])
    return flat.reshape(flat.shape[:lead] + (-1, PACK_W))


def _unpack(buf, shapes, lead=0):
    flat = buf.reshape(buf.shape[:lead] + (-1,))
    out, off = [], 0
    for shp in shapes:
        n = math.prod(shp)
        out.append(flat[..., off:off + n].reshape(buf.shape[:lead] + tuple(shp)))
        off += n
    return out


def _adamw_math(wv, gv, m0, v0):
    m1 = ADAM_B1 * m0 + (1.0 - ADAM_B1) * gv
    v1 = ADAM_B2 * v0 + (1.0 - ADAM_B2) * (gv * gv)
    m_hat = m1 / (1.0 - ADAM_B1 ** ADAM_STEP)
    v_hat = v1 / (1.0 - ADAM_B2 ** ADAM_STEP)
    delta = -ADAM_LR * (m_hat / (jnp.sqrt(v_hat) + ADAM_EPS) + ADAM_WD * wv)
    return delta, m1, v1


def _adamw(w, g, m, v, name):
    def fn(tv, rv, mv, seg):
        return _adamw_math(*tv), ()

    outs, _ = _ew(fn, [_full(a[None]) for a in (w, g, m, v)], outs=[(w.shape[-1], F32)] * 3, name=name)
    return [o[0] for o in outs]


def _sum_adamw(slots, w, m, v, name):
    _, r, cdim = slots.shape
    tr = _tile(r, 128, 16)

    def body(s_ref, w_ref, m_ref, v_ref, g_ref, d_ref, mo_ref, vo_ref):
        gv = s_ref[0].astype(F32)
        for s in range(1, N_DEV):
            gv = gv + s_ref[s].astype(F32)
        g_ref[...] = gv
        d_ref[...], mo_ref[...], vo_ref[...] = _adamw_math(w_ref[...], gv, m_ref[...], v_ref[...])

    flat = pl.BlockSpec((tr, cdim), lambda i: (i, 0))
    return pl.pallas_call(
        body, out_shape=tuple(jax.ShapeDtypeStruct((r, cdim), F32) for _ in range(4)), grid=(r // tr,),
        in_specs=[pl.BlockSpec((N_DEV, tr, cdim), lambda i: (0, i, 0)), flat, flat, flat], out_specs=(flat,) * 4,
        compiler_params=_params(("parallel",)), name=name)(slots, w, m, v)


def _gathered(n, p):
    return p if n in COL_SHARDED else p.reshape(-1, p.shape[-1])


def _layer_fwd(l, xin, modt, wts, sm, cfg, u_in):
    bsz, tlen, d = xin.shape
    bt = bsz * tlen
    nct, tr, ctx_len, g = cfg['nct'], cfg['tr'], cfg['ctx_len'], cfg['g']
    ncc, nc = ctx_len // S5_T, tlen // S5_T
    alpha = cfg['alpha']
    nm = lambda s: f"l{l}_{s}"

    def modulate(xv, i_shift, i_scale, name):
        def fn(tv, rv, mv, seg):
            mo = mv[0]
            return (tv[0] * (1.0 + mo[i_scale:i_scale + 1]) + mo[i_shift:i_shift + 1],), ()
        o, _, ot = _ew(fn, [_full(xv)], mods=[modt], outs=[(d, ACT_DT)], nctx_tiles=nct, tr=tr, name=name, transposed=True)
        return o[0], ot

    def ln_fwd(xv, mv_, i_gate, gam, bet, name, nxt=None):
        def fn(tv, rv, mv, seg):
            z = alpha * tv[0] + mv[0][i_gate:i_gate + 1] * tv[1]
            mu = jnp.mean(z, axis=-1, keepdims=True)
            zc = z - mu
            var = jnp.mean(zc * zc, axis=-1, keepdims=True)
            y = zc * lax.rsqrt(var + LN_EPS) * rv[0] + rv[1]
            if nxt is None:
                return (y,), ()
            return (y * (1.0 + mv[1][nxt[2]:nxt[2] + 1]) + mv[1][nxt[1]:nxt[1] + 1], y), ()
        tiles, rws = [_full(xv), _full(mv_)], [gam.reshape(1, d), bet.reshape(1, d)]
        if nxt is None:
            return _ew(fn, tiles, rows=rws, mods=[modt], outs=[(d, F32)], nctx_tiles=nct, tr=tr, name=name)[0][0], None
        o, _, ot = _ew(fn, tiles, rows=rws, mods=[modt, nxt[0]], outs=[(d, ACT_DT), (d, F32)], nctx_tiles=nct, tr=tr,
                       name=name, transposed=True)
        return o[1], (o[0], ot)

    sv = {'x': xin}
    u, sv['uT'] = u_in if u_in is not None else modulate(xin, 0, 1, nm("mod1"))
    z3 = _mm(u.reshape(bt, d), wts['w_in'][l], b_blocked=True, name=nm("w_in")).reshape(bsz, tlen, 3 * d)
    sv['z3'] = z3
    xc = _conv_fwd(z3, sm['conv_w'][l], sm['conv_b'][l], d, nct, tr, nm("conv"))
    sv['xc'] = xc
    nxt = l + 1 < cfg['depth']
    riders = [[('mlp_w1', l)], [('mlp_w2', l)] + ([('s5_glu_w', l + 1), ('w_out', l + 1)] if nxt else []),
              [('w_in', l + 1)] if nxt else []]

    def ride_of(rs):
        return ([cfg['wloc'][n][lay] for n, lay in rs], True) if rs else None

    def landed(rs, got):
        for (n, lay), p in zip(rs, got):
            wts[n][lay] = _gathered(n, p)

    hs = []
    for dd in range(2):
        h, got = _rg_fwd(xc, sm['rg_lambda'][l, dd], sm['rg_wa'][l, dd], sm['rg_ba'][l, dd], sm['rg_wi'][l, dd],
                         sm['rg_bi'][l, dd], bool(dd), nct, tr, nm(f"rg_fwd{dd}"), ride=ride_of(riders[dd]))
        hs.append(h)
        landed(riders[dd], got)
    sv['hf'], sv['hb'] = hs

    def copy_fn(tv, rv, mv, seg):
        return (tv[0],), ()
    s5u = _to_chunks(_ew(copy_fn, [(z3, 2, d, 0)], outs=[(d, ACT_DT)], tr=tr, name=nm("s5_in"))[0][0], ctx_len, g)
    sv['s5u'] = s5u
    y, hf5, hb5, got = _s5_fwd(s5u, cfg['s5_ops'][l], cfg['s5_lm'][l], ncc, nm("s5_fwd"), ride=ride_of(riders[2]))
    landed(riders[2], got)
    sv['hf5'], sv['hb5'] = hf5, hb5
    ytok = _from_chunks(y, ctx_len, g)
    sv['ytok'] = ytok

    def gelu_fn(tv, rv, mv, seg):
        return (_gelu(tv[0].astype(F32)),), ()
    o, _, sv['gactT'] = _ew(gelu_fn, [_full(ytok)], outs=[(d, ACT_DT)], tr=tr, name=nm("s5_gelu"), transposed=True)
    gact = o[0]
    sv['gact'] = gact
    gpre = _mm(gact.reshape(bt, d), wts['s5_glu_w'][l], bias=sm['s5_glu_b'][l], name=nm("glu")).reshape(bsz, tlen, d)
    sv['gpre'] = gpre

    def mix_fn(tv, rv, mv, seg):
        rg = (tv[0] + tv[1]) * _gelu(tv[2])
        s5o = tv[3].astype(F32) * _sigmoid(tv[4])
        return (jnp.concatenate([rg, s5o], axis=1),), ()
    o, _, sv['mixinT'] = _ew(mix_fn, [_full(hs[0]), _full(hs[1]), (z3, 1, d, 0), _full(gact), _full(gpre)],
                             outs=[(2 * d, ACT_DT)], tr=tr, name=nm("mix"), transposed=True)
    mixin = o[0].reshape(bt, 2 * d)
    mo = _mm(mixin, wts['w_out'][l], bias=sm['b_out'][l], name=nm("w_out")).reshape(bsz, tlen, d)
    sv['mo'] = mo
    x1, (u2, sv['u2T']) = ln_fwd(xin, mo, 2, sm['ln1_g'][l], sm['ln1_b'][l], nm("ln1"), nxt=(modt, 3, 4))
    sv['x1'] = x1
    rl, sv['rlT'] = _mm(u2.reshape(bt, d), wts['mlp_w1'][l], b_blocked=True, bias=sm['mlp_b1'][l],
                        epi=lambda r: jnp.maximum(r, 0.0), out_t=True, out_dtype=ACT_DT, tm=2176, name=nm("mlp1"))
    sv['rl'] = rl
    fo = _mm(rl, wts['mlp_w2'][l], bias=sm['mlp_b2'][l], a_fn=lambda v: v * v, name=nm("mlp2")).reshape(bsz, tlen, d)
    sv['fo'] = fo
    x2, u_next = ln_fwd(x1, fo, 5, sm['ln2_g'][l], sm['ln2_b'][l], nm("ln2"),
                        nxt=(cfg['modts'][l + 1], 0, 1) if nxt else None)
    return x2, sv, u_next


def _layer_bwd(l, dx2, sv, modt, wts, sm, cfg, pending):
    bsz, tlen, d = dx2.shape
    bt = bsz * tlen
    nct, tr, ctx_len, g = cfg['nct'], cfg['tr'], cfg['ctx_len'], cfg['g']
    ncc, nc = ctx_len // S5_T, tlen // S5_T
    alpha = cfg['alpha']
    nm = lambda s: f"l{l}_{s}"
    gr = {}

    def ln_bwd(xv, mv_, i_gate, gam, dy, name, through=None):
        def fn(tv, rv, mv, seg):
            xx, mm_, dyy = tv[:3]
            gate = mv[0][i_gate:i_gate + 1]
            z = alpha * xx + gate * mm_
            mu = jnp.mean(z, axis=-1, keepdims=True)
            zc = z - mu
            var = jnp.mean(zc * zc, axis=-1, keepdims=True)
            rstd = lax.rsqrt(var + LN_EPS)
            xhat = zc * rstd
            more = []
            if through is not None:
                duu = tv[3]
                dyy = dyy + duu * (1.0 + mv[0][through[1]:through[1] + 1])
                more = [_colsum(duu), _colsum(duu * (xhat * rv[0] + rv[1]))]
            dxh = dyy * rv[0]
            dz = rstd * (dxh - jnp.mean(dxh, axis=-1, keepdims=True) - xhat * jnp.mean(dxh * xhat, axis=-1, keepdims=True))
            dm = gate * dz
            acc = jnp.concatenate([_colsum(dz * mm_), _colsum(dyy * xhat), _colsum(dyy), _colsum(dm)] + more, axis=0)
            return (alpha * dz, dm), (acc,)
        tiles, rws = [_full(xv), _full(mv_), _full(dy)], [gam.reshape(1, d)]
        if through is not None:
            tiles.append(_full(through[0]))
            rws.append(through[2].reshape(1, d))
        o, a = _ew(fn, tiles, rows=rws, mods=[modt], outs=[(d, F32), (d, ACT_DT)],
                   accs=[(4 if through is None else 6, d)], nctx_tiles=nct, tr=tr, name=name)
        return o[0], o[1], a[0]

    def mod_bwd(du, xv, i_scale, addend, name):
        def fn(tv, rv, mv, seg):
            duu, xx, add = tv
            acc = jnp.concatenate([_colsum(duu), _colsum(duu * xx)], axis=0)
            return (add + duu * (1.0 + mv[0][i_scale:i_scale + 1]),), (acc,)
        o, a = _ew(fn, [_full(du), _full(xv), _full(addend)], mods=[modt], outs=[(d, F32)], accs=[(2, d)],
                   nctx_tiles=nct, tr=tr, name=name)
        return o[0], a[0]

    def row_blocks(gw):
        return gw.reshape(N_DEV, -1, gw.shape[-1])

    dx1a, dfo, acc2 = ln_bwd(sv['x1'], sv['fo'], 5, sm['ln2_g'][l], dx2, nm("ln2_bwd"))
    dfo2 = dfo.reshape(bt, d)
    dhp, db1 = _mm(dfo2, wts['mlp_w2'][l], tb=True, epi=lambda r, rl: r * (2.0 * rl.astype(F32)), extras=[sv['rl']],
                   colsum=True, out_dtype=ACT_DT, name=nm("mlp2_dx"))
    gr['mlp_b1'] = db1[0]
    gr['mlp_w2'] = row_blocks(_mm(sv['rlT'], dfo2, a_fn=lambda v: v * v, out_dtype=WIRE_DT, tm=512, tk=WGRAD_TK,
                                  name=nm("mlp2_dw")))
    du2 = _mm(dhp, wts['mlp_w1'][l], tb=True, b_blocked=True, name=nm("mlp1_dx")).reshape(bsz, tlen, d)
    gr['mlp_w1'] = _mm(sv['u2T'], dhp, out_blocks=N_DEV, out_dtype=WIRE_DT, tk=WGRAD_TK, name=nm("mlp1_dw"))
    gr['ln2_g'] = acc2[:, :, 1].sum((0, 1))
    gr['ln2_b'] = acc2[:, :, 2].sum((0, 1))
    gr['mlp_b2'] = acc2[:, :, 3].sum((0, 1))

    dxa, dmo, acc1 = ln_bwd(sv['x'], sv['mo'], 2, sm['ln1_g'][l], dx1a, nm("ln1_bwd"), through=(du2, 4, sm['ln1_b'][l]))
    gr['ln1_g'] = acc1[:, :, 1].sum((0, 1))
    gr['ln1_b'] = acc1[:, :, 2].sum((0, 1))
    gr['b_out'] = acc1[:, :, 3].sum((0, 1))
    dmo2 = dmo.reshape(bt, d)
    dmix = _mm(dmo2, wts['w_out'][l], tb=True, name=nm("w_out_dx")).reshape(bsz, tlen, 2 * d)
    gr['w_out'] = row_blocks(_mm(sv['mixinT'], dmo2, out_dtype=WIRE_DT, tk=WGRAD_TK, name=nm("w_out_dw")))

    def glu_bwd(tv, rv, mv, seg):
        ds, ga, gp = tv
        ga = ga.astype(F32)
        sg = _sigmoid(gp)
        dg = ds * ga * sg * (1.0 - sg)
        return (dg, ds * sg), (_colsum(dg),)
    o, a = _ew(glu_bwd, [(dmix, 1, d, 0), _full(sv['gact']), _full(sv['gpre'])], outs=[(d, ACT_DT), (d, F32)],
               accs=[(1, d)], tr=tr, name=nm("glu_bwd"))
    dgp, t1 = o
    gr['s5_glu_b'] = a[0][:, 1, 0].sum(0)
    dgp2 = dgp.reshape(bt, d)
    dyt = _mm(dgp2, wts['s5_glu_w'][l], tb=True, epi=lambda r, t1v, yv: (r + t1v) * _gelu_grad(yv.astype(F32)),
              extras=[t1.reshape(bt, d), sv['ytok'].reshape(bt, d)], out_dtype=ACT_DT, tn=512,
              name=nm("glu_dx")).reshape(bsz, tlen, d)
    gr['s5_glu_w'] = row_blocks(_mm(sv['gactT'], dgp2, out_dtype=WIRE_DT, tk=WGRAD_TK, name=nm("glu_dw")))
    du5, dops = _s5_bwd(_to_chunks(dyt, ctx_len, g), sv['s5u'], sv['hf5'], sv['hb5'], cfg['s5_ops_t'][l],
                        cfg['s5_lm'][l], ncc, nm("s5_bwd"))
    ds5u = _from_chunks(du5, ctx_len, g)
    gr['s5_dops'] = dops

    z3 = sv['z3']
    dxc = None
    for dd in range(2):
        names = ('mlp_w2',) if dd == 0 else ('mlp_w1', 's5_glu_w', 'w_out')
        riders = (pending if dd == 0 else []) + [(n, l, gr.pop(n)) for n in names]
        dxc, dwa, dwi, dv, got = _rg_bwd(sv['xc'], sv['hf'] if dd == 0 else sv['hb'], dmix, z3, sm['rg_lambda'][l, dd],
                                         sm['rg_wa'][l, dd], sm['rg_ba'][l, dd], sm['rg_wi'][l, dd],
                                         sm['rg_bi'][l, dd], dxc, bool(dd), nct, tr, nm(f"rg_bwd{dd}"),
                                         ride=([r[2] for r in riders], False))
        for (n, lay, _), p in zip(riders, got):
            cfg['slots'][n][lay] = p
        gr[f'rg_wa{dd}'], gr[f'rg_wi{dd}'] = dwa, dwi
        gr[f'rg_lambda{dd}'] = dv[0] * (-_sigmoid(-sm['rg_lambda'][l, dd]))
        gr[f'rg_ba{dd}'], gr[f'rg_bi{dd}'] = dv[1], dv[2]
    drgx, accc = _conv_bwd(dxc, z3, sm['conv_w'][l], d, nct, tr, nm("conv_bwd"))
    gr['conv_w'] = accc[:, 0:4].sum(0)
    gr['conv_b'] = accc[:, 4].sum(0)

    def dz_fn(tv, rv, mv, seg):
        dgate = tv[0] * (tv[1] + tv[2]) * _gelu_grad(tv[3])
        return (jnp.concatenate([tv[4], dgate, tv[5].astype(F32)], axis=1),), ()
    dz = _ew(dz_fn, [(dmix, 0, d, 0), _full(sv['hf']), _full(sv['hb']), (z3, 1, d, 0), _full(drgx), _full(ds5u)],
             outs=[(3 * d, ACT_DT)], tr=tr, name=nm("dz"))[0][0].reshape(bt, 3 * d)
    du = _mm(dz, wts['w_in'][l], tb=True, b_blocked=True, name=nm("w_in_dx")).reshape(bsz, tlen, d)
    still = [('w_in', l, _mm(sv['uT'], dz, out_blocks=N_DEV, out_dtype=WIRE_DT, tk=WGRAD_TK, name=nm("w_in_dw")))]
    dxin, accm1 = mod_bwd(du, sv['x'], 1, dxa, nm("mod1_bwd"))
    dmod = jnp.stack([accm1[:, :, 0], accm1[:, :, 1], acc1[:, :, 0], acc1[:, :, 4], acc1[:, :, 5], acc2[:, :, 0]], axis=2)
    return dxin, dmod, gr, still


def kernel(x, c, ctx, c_ctx, ada_w, ada_b, ln1_g, ln1_b, w_in, conv_w, conv_b, rg_lambda, rg_wa, rg_ba, rg_wi, rg_bi, s5_a_re, s5_a_im, s5_log_dt, s5_b_re, s5_b_im, s5_c_re, s5_c_im, s5_d, s5_glu_w, s5_glu_b, w_out, b_out, ln2_g, ln2_b, mlp_w1, mlp_b1, mlp_w2, mlp_b2, loss_target, m_c_ctx, m_ada_w, m_ada_b, m_ln1_g, m_ln1_b, m_w_in, m_conv_w, m_conv_b, m_rg_lambda, m_rg_wa, m_rg_ba, m_rg_wi, m_rg_bi, m_s5_a_re, m_s5_a_im, m_s5_log_dt, m_s5_b_re, m_s5_b_im, m_s5_c_re, m_s5_c_im, m_s5_d, m_s5_glu_w, m_s5_glu_b, m_w_out, m_b_out, m_ln2_g, m_ln2_b, m_mlp_w1, m_mlp_b1, m_mlp_w2, m_mlp_b2, v_c_ctx, v_ada_w, v_ada_b, v_ln1_g, v_ln1_b, v_w_in, v_conv_w, v_conv_b, v_rg_lambda, v_rg_wa, v_rg_ba, v_rg_wi, v_rg_bi, v_s5_a_re, v_s5_a_im, v_s5_log_dt, v_s5_b_re, v_s5_b_im, v_s5_c_re, v_s5_c_im, v_s5_d, v_s5_glu_w, v_s5_glu_b, v_w_out, v_b_out, v_ln2_g, v_ln2_b, v_mlp_w1, v_mlp_b1, v_mlp_w2, v_mlp_b2):
    loc = dict(locals())
    w = {n: loc[n] for n in WEIGHTS}
    mom = {n: loc["m_" + n] for n in WEIGHTS}
    vel = {n: loc["v_" + n] for n in WEIGHTS}
    bsz, seq, d = x.shape
    ctx_len = ctx.shape[1]
    depth = ada_w.shape[0]
    g = s5_a_re.shape[2]
    nmod = ada_b.shape[1] // d
    modc = ada_w.shape[2]
    tr = _tile(ctx_len, ROW_TILE, 8)
    cfg = dict(nct=ctx_len // tr, tr=tr, ctx_len=ctx_len, g=g, alpha=(2.0 * depth) ** 0.25)
    xi, yi, ci = _me()
    me = 4 * xi + 2 * yi + ci

    small_shapes = [c.shape] + [w[n].shape for n in CHAN_SHARDED]
    got = _exchange([_pack([c] + [w[n] for n in CHAN_SHARDED], F32)], True, "gather_small")[0]
    parts = _unpack(got, small_shapes, lead=1)
    c_all = parts[0].reshape(N_DEV * bsz, d)
    sm = {n: w[n] for n in WEIGHTS if n not in BIG and n not in CHAN_SHARDED and n != 'ada_w'}
    for n, p in zip(CHAN_SHARDED, parts[1:]):
        sm[n] = jnp.moveaxis(p, 0, -2).reshape(p.shape[1:-1] + (-1,))

    a_ext = jnp.concatenate([c_all, jnp.broadcast_to(c_ctx[None], (N_DEV, d))], axis=0)
    nrow = a_ext.shape[0]
    my_ada_b = lax.dynamic_slice_in_dim(ada_b, me * modc, modc, axis=1)
    mod_cols = jnp.stack([_mm(a_ext, ada_w[l], bias=my_ada_b[l], a_fn=_silu, name=f"l{l}_ada") for l in range(depth)])
    mod_all = _exchange([mod_cols.reshape(depth * nrow, modc)], True, "gather_mod")[0]
    mod_all = mod_all.reshape(N_DEV, depth, nrow, modc).transpose(1, 2, 0, 3).reshape(depth, nrow, nmod, d)
    mod_mine = lax.dynamic_slice_in_dim(mod_all, me * bsz, bsz, axis=1)
    mod_ctx = jnp.broadcast_to(mod_all[:, N_DEV * bsz][:, None], mod_mine.shape)
    modts = jnp.stack([mod_ctx, mod_mine], axis=2)

    cfg['depth'] = depth
    cfg['wloc'] = {n: [w[n][l].astype(WIRE_DT) for l in range(depth)] for n in BIG}
    wts = {n: [None] * depth for n in BIG}
    for n, p in zip(MIX_W, _exchange([cfg['wloc'][n][0] for n in MIX_W], True, "gather_w0")):
        wts[n][0] = _gathered(n, p)

    fold = lambda t: jnp.moveaxis(t, 0, 1).reshape((2, depth * g) + t.shape[3:])
    s5_in = [fold(w[n]) for n in S5_NAMES[:7]] + [s5_d.reshape(depth * g, -1)]
    ops_all, s5_vjp = jax.vjp(_s5_operators, *s5_in)
    lay = lambda t, l, ax=0: lax.slice_in_dim(t, l * g, (l + 1) * g, axis=ax)
    cfg['s5_ops'] = [tuple(lay(o, l).astype(MXU_DT) for o in ops_all[:5]) for l in range(depth)]
    cfg['s5_ops_t'] = [tuple(jnp.swapaxes(lay(o, l), 1, 2).astype(MXU_DT) for o in ops_all[:5]) for l in range(depth)]
    cfg['s5_lm'] = [lay(ops_all[5], l, 1) for l in range(depth)]

    act = jnp.concatenate([ctx, x], axis=1)
    cfg['modts'] = modts
    u_in = None
    saved = []
    for l in range(depth):
        act, sv, u_in = _layer_fwd(l, act, modts[l], wts, sm, cfg, u_in)
        saved.append(sv)

    def loss_fn(tv, rv, mv, seg):
        err = tv[0] - tv[1]
        keep = jnp.where(seg == 1, 1.0, 0.0)
        return (err * (keep / d),), (_colsum(err * err) * keep,)
    o, a = _ew(loss_fn, [_full(act), (loss_target, 0, d, cfg['nct'])], outs=[(d, F32)], accs=[(1, d)],
               nctx_tiles=cfg['nct'], tr=tr, name="loss")
    dact = o[0]
    loss = lax.psum(0.5 * jnp.sum(a[0][:, 1]) / d, ("x", "y", "c"))

    grads = [None] * depth
    dmods = [None] * depth
    slots = cfg['slots'] = {n: [None] * depth for n in BIG}
    pending = []
    for l in reversed(range(depth)):
        dact, dmods[l], grads[l], pending = _layer_bwd(l, dact, saved[l], modts[l], wts, sm, cfg, pending)
    for (n, lay, _), p in zip(pending, _exchange([r[2] for r in pending], False, "scatter_last")):
        slots[n][lay] = p
    grad_x = dact[:, ctx_len:]

    dmod = jnp.stack(dmods)
    mine = jnp.concatenate([dmod[:, :, 1].reshape(depth, bsz, nmod * d),
                            dmod[:, :, 0].sum(1).reshape(depth, 1, nmod * d)], axis=1)
    got = _exchange([mine.reshape(depth * (bsz + 1), nmod * d)], True, "gather_dmod")[0]
    got = got.reshape(N_DEV, depth, bsz + 1, nmod * d)
    dmod_rows = jnp.concatenate([got[:, :, :bsz].transpose(1, 0, 2, 3).reshape(depth, N_DEV * bsz, nmod * d),
                                 got[:, :, bsz].transpose(1, 0, 2)], axis=1)
    dmod_cols = lax.dynamic_slice_in_dim(dmod_rows, me * modc, modc, axis=2)
    g_ada_w = jnp.stack([_mm(a_ext, dmod_cols[l], ta=True, a_fn=_silu, name=f"l{l}_ada_dw") for l in range(depth)])

    def rowsum_fn(tv, rv, mv, seg):
        return (), (_colsum(tv[0]),)
    g_ada_b = _ew(rowsum_fn, [_full(dmod_rows)], accs=[(1, nmod * d)], name="ada_db")[1][0][:, 1, 0]
    dsilu = 0.0
    for l in range(depth):
        dsilu = dsilu + _mm(dmod_cols[l, N_DEV * bsz:], ada_w[l], tb=True, name=f"l{l}_ada_dc").sum(0)
    sig = _sigmoid(c_ctx)
    g_c_ctx_part = dsilu * (sig * (1.0 + c_ctx * (1.0 - sig)))

    def both(name, l):
        return jnp.stack([grads[l][f'{name}{dd}'] for dd in range(2)])
    rep = {n: jnp.stack([grads[l][n] for l in range(depth)]) for n in
           ('ln1_g', 'ln1_b', 'conv_w', 'conv_b', 's5_glu_b', 'b_out', 'ln2_g', 'ln2_b', 'mlp_b1', 'mlp_b2')}
    for n in ('rg_lambda', 'rg_wa', 'rg_ba', 'rg_wi', 'rg_bi'):
        rep[n] = jnp.stack([both(n, l) for l in range(depth)])
    rep['c_ctx'] = g_c_ctx_part
    dops_all = tuple(jnp.concatenate([grads[l]['s5_dops'][i] for l in range(depth)], axis=1 if i == 5 else 0)
                     for i in range(6))
    s5g = s5_vjp(dops_all)
    for n, val in zip(S5_NAMES[:7], s5g):
        rep[n] = jnp.moveaxis(val.reshape((2, depth, g) + val.shape[2:]), 0, 1)
    rep['s5_d'] = s5g[7].reshape(depth, -1)
    small_names = [n for n in WEIGHTS if n in rep and math.prod(rep[n].shape) <= SMALL_PARAM]
    large_names = [n for n in WEIGHTS if n in rep and math.prod(rep[n].shape) > SMALL_PARAM]
    g_rep = {}
    bufs = []
    for names, dt in ((small_names, F32), (large_names, WIRE_DT)):
        buf = _pack([rep[n] for n in names], dt)
        bufs.append(jnp.pad(buf, ((0, -buf.shape[0] % (16 * N_DEV)), (0, 0))).reshape(N_DEV, -1, PACK_W))
    parts = [_sum_slots(p, f"sum_rep{i}").astype(p.dtype) for i, p in enumerate(_exchange(bufs, False, "scatter_rep"))]
    for names, full in zip((small_names, large_names), _exchange(parts, True, "gather_rep")):
        g_rep.update(zip(names, _unpack(full.reshape(-1, PACK_W).astype(F32), [rep[n].shape for n in names])))

    grad, delta, new_m, new_v = {}, {}, {}, {}
    for n in BIG:
        res = [_sum_adamw(slots[n][l], w[n][l], mom[n][l], vel[n][l], f"l{l}_adamw_{n}") for l in range(depth)]
        grad[n], delta[n], new_m[n], new_v[n] = [jnp.stack([r[i] for r in res]) for i in range(4)]
    grad['ada_w'] = g_ada_w
    rest = [n for n in WEIGHTS if n not in BIG and n != 'ada_w']
    for n in rest:
        if n == 'ada_b':
            grad[n] = g_ada_b
        elif n in CHAN_SHARDED:
            grad[n] = lax.dynamic_slice_in_dim(g_rep[n], me * w[n].shape[-1], w[n].shape[-1], axis=g_rep[n].ndim - 1)
        else:
            grad[n] = g_rep[n].reshape(w[n].shape)
    flat2 = lambda t: t.reshape(-1, t.shape[-1])
    for n in ('ada_w', 'rg_wa', 'rg_wi'):
        res = _adamw(flat2(w[n]), flat2(grad[n]), flat2(mom[n]), flat2(vel[n]), f"adamw_{n}")
        delta[n], new_m[n], new_v[n] = [r.reshape(w[n].shape) for r in res]
    for tag, names in (("small", [n for n in rest if math.prod(w[n].shape) <= SMALL_PARAM]),
                       ("s5", [n for n in rest if math.prod(w[n].shape) > SMALL_PARAM and n not in ('rg_wa', 'rg_wi')])):
        if not names:
            continue
        shapes = [w[n].shape for n in names]
        packed = [_pack([src[n] for n in names], F32) for src in (w, grad, mom, vel)]
        for dst, o in zip((delta, new_m, new_v), _adamw(*packed, name=f"adamw_{tag}")):
            dst.update(zip(names, _unpack(o, shapes)))
    return (loss, grad_x, *[grad[n] for n in WEIGHTS], *[delta[n] for n in WEIGHTS], *[new_m[n] for n in WEIGHTS],
            *[new_v[n] for n in WEIGHTS])
```

```python
import functools
import math

import jax
import jax.numpy as jnp
from jax import lax
from jax.experimental import pallas as pl
from jax.experimental.pallas import tpu as pltpu

F32 = jnp.float32
MXU_DT = jnp.bfloat16
ACT_DT = jnp.bfloat16
WIRE_DT = jnp.bfloat16

N_DEV = 8
GRID_W = 64
RG_C = 8.0
LN_EPS = 1e-5
S5_T = 16
S5_GB = 8
ROW_TILE = 256
DGRAD_KBLOCKS = 4
VMEM_LIMIT = 56 * 1024 * 1024
PACK_W = 1024
SMALL_PARAM = 65536
WGRAD_TK = 2176

ADAM_LR = 0.001
ADAM_B1 = 0.9
ADAM_B2 = 0.999
ADAM_EPS = 1e-08
ADAM_WD = 0.01
ADAM_STEP = 10

WEIGHTS = ['c_ctx', 'ada_w', 'ada_b', 'ln1_g', 'ln1_b', 'w_in', 'conv_w', 'conv_b', 'rg_lambda', 'rg_wa', 'rg_ba',
           'rg_wi', 'rg_bi', 's5_a_re', 's5_a_im', 's5_log_dt', 's5_b_re', 's5_b_im', 's5_c_re', 's5_c_im', 's5_d',
           's5_glu_w', 's5_glu_b', 'w_out', 'b_out', 'ln2_g', 'ln2_b', 'mlp_w1', 'mlp_b1', 'mlp_w2', 'mlp_b2']
COL_SHARDED = ('w_in', 'mlp_w1')
MIX_W = ('w_in', 's5_glu_w', 'w_out')
MLP_W = ('mlp_w1', 'mlp_w2')
BIG = MIX_W + MLP_W
CHAN_SHARDED = ('conv_w', 'rg_lambda', 'rg_ba', 'rg_bi')
S5_NAMES = ('s5_a_re', 's5_a_im', 's5_log_dt', 's5_b_re', 's5_b_im', 's5_c_re', 's5_c_im', 's5_d')
MESH = pl.DeviceIdType.MESH


def _tile(n, pref, align):
    t = (min(pref, n) // align) * align
    while t >= align:
        if n % t == 0:
            return t
        t -= align
    return n


def _params(sem):
    return pltpu.CompilerParams(dimension_semantics=sem, vmem_limit_bytes=VMEM_LIMIT)


def _sigmoid(v):
    return 0.5 * jnp.tanh(0.5 * v) + 0.5


def _silu(v):
    return v * _sigmoid(v)


_GELU_K = math.sqrt(2.0 / math.pi)


def _gelu(v):
    return 0.5 * v * (1.0 + jnp.tanh(_GELU_K * (v + 0.044715 * v * v * v)))


def _gelu_grad(v):
    th = jnp.tanh(_GELU_K * (v + 0.044715 * v * v * v))
    return 0.5 * (1.0 + th) + 0.5 * v * (1.0 - th * th) * _GELU_K * (1.0 + 3.0 * 0.044715 * v * v)


def _one_minus_sq(la, a):
    v = 2.0 * la
    series = -v * (1.0 + v * (0.5 + v * (1.0 / 6.0 + v * (1.0 / 24.0 + v * (1.0 / 120.0)))))
    return jnp.where(v > -0.1, series, 1.0 - a * a)


def _softplus(v):
    return jnp.maximum(v, 0.0) + jnp.log(1.0 + jnp.exp(-jnp.abs(v)))


def _dot(a, b):
    return jnp.dot(a.astype(MXU_DT), b.astype(MXU_DT), preferred_element_type=F32)


def _dot_tn(a, b):
    return lax.dot_general(a.astype(MXU_DT), b.astype(MXU_DT), (((0,), (0,)), ((), ())), preferred_element_type=F32)


def _dot_nt(a, b):
    return lax.dot_general(a.astype(MXU_DT), b.astype(MXU_DT), (((1,), (1,)), ((), ())), preferred_element_type=F32)


def _mm(a, b, *, ta=False, tb=False, b_blocked=False, out_blocks=0, bias=None, a_fn=None, epi=None, extras=(),
        colsum=False, out_t=False, kblocks=1, out_dtype=F32, name, tm=1088, tn=1024, tk=1024):
    if ta:
        kdim, m = a.shape
    else:
        m, kdim = a.shape
    bcol = b.shape[2] if b_blocked else None
    blog = (b.shape[1], b.shape[0] * b.shape[2]) if b_blocked else b.shape
    if tb:
        n, kb = blog
    else:
        kb, n = blog
    assert kdim == kb, (a.shape, b.shape, ta, tb)
    assert not (ta and tb)
    tm = _tile(m, tm, 128 if ta else 16)
    n_lim = bcol if (b_blocked and not tb) else (n // out_blocks if out_blocks else n)
    tn = _tile(n_lim, tn, 128)
    tk = _tile(bcol if (b_blocked and tb) else kdim, tk, 16 if ta else 128)
    nk = kdim // tk
    a_spec = pl.BlockSpec((tk, tm), lambda i, j, k: (k, i)) if ta else pl.BlockSpec((tm, tk), lambda i, j, k: (i, k))
    if kblocks > 1:
        assert b_blocked and tb and b.shape[0] % kblocks == 0
        tk = bcol
        nk = b.shape[0] // kblocks
        a_spec = pl.BlockSpec((tm, kblocks * bcol), lambda i, j, k: (i, k))
        b_spec = pl.BlockSpec((kblocks, tn, bcol), lambda i, j, k: (k, j, 0))
    elif not b_blocked:
        b_spec = pl.BlockSpec((tn, tk), lambda i, j, k: (j, k)) if tb else pl.BlockSpec((tk, tn), lambda i, j, k: (k, j))
    elif tb:
        qk = bcol // tk
        b_spec = pl.BlockSpec((None, tn, tk), lambda i, j, k: (k // qk, j, k % qk))
    else:
        qn = bcol // tn
        b_spec = pl.BlockSpec((None, tk, tn), lambda i, j, k: (j // qn, k, j % qn))
    in_specs = [a_spec, b_spec]
    args = [a, b]
    has_bias = bias is not None
    if has_bias:
        in_specs.append(pl.BlockSpec((1, tn), lambda i, j, k: (0, j)))
        args.append(bias.reshape(1, n).astype(F32))
    for e in extras:
        assert e.shape == (m, n), (e.shape, m, n)
        in_specs.append(pl.BlockSpec((tm, tn), lambda i, j, k: (i, j)))
        args.append(e)
    nex = len(extras)
    dn = (((0 if ta else 1,), (1 if tb else 0,)), ((), ()))
    if out_blocks:
        qo = (n // out_blocks) // tn
        out_shape = jax.ShapeDtypeStruct((out_blocks, m, n // out_blocks), out_dtype)
        out_spec = pl.BlockSpec((None, tm, tn), lambda i, j, k: (j // qo, i, j % qo))
    else:
        out_shape = jax.ShapeDtypeStruct((m, n), out_dtype)
        out_spec = pl.BlockSpec((tm, tn), lambda i, j, k: (i, j))
    out_shapes, out_specs = [out_shape], [out_spec]
    grid = (m // tm, n // tn, nk)
    assert not (colsum and out_t)
    if out_t:
        out_shapes.append(jax.ShapeDtypeStruct((n, m), out_dtype))
        out_specs.append(pl.BlockSpec((tn, tm), lambda i, j, k: (j, i)))
    if colsum:
        out_shapes.append(jax.ShapeDtypeStruct((1, n), F32))
        out_specs.append(pl.BlockSpec((1, tn), lambda i, j, k: (0, j)))
        swap = lambda sp: pl.BlockSpec(sp.block_shape, lambda j, i, k, f=sp.index_map: f(i, j, k))
        in_specs = [swap(sp) for sp in in_specs]
        out_specs = [swap(sp) for sp in out_specs]
        grid = (n // tn, m // tm, nk)
    nout = len(out_shapes)

    def body(*refs):
        a_ref, b_ref = refs[0], refs[1]
        row_tile = pl.program_id(1)
        pos = 2
        bias_ref = refs[pos] if has_bias else None
        pos += int(has_bias)
        ex_refs = refs[pos:pos + nex]
        o_ref = refs[pos + nex]
        av = a_ref[...]
        if a_fn is not None:
            av = a_fn(av.astype(F32))
        av = av.astype(MXU_DT)
        if kblocks > 1:
            part = lax.dot_general(av[:, :tk], b_ref[0].astype(MXU_DT), dn, preferred_element_type=F32)
            for q in range(1, kblocks):
                part = part + lax.dot_general(av[:, q * tk:(q + 1) * tk], b_ref[q].astype(MXU_DT), dn,
                                              preferred_element_type=F32)
        else:
            part = lax.dot_general(av, b_ref[...].astype(MXU_DT), dn, preferred_element_type=F32)

        def finish(r):
            if has_bias:
                r = r + bias_ref[...]
            if epi is not None:
                r = epi(r, *[e[...] for e in ex_refs])
            o_ref[...] = r.astype(out_dtype)
            if out_t:
                refs[pos + nex + 1][...] = r.T.astype(out_dtype)
            if colsum:
                cs_ref = refs[pos + nex + 1]

                @pl.when(row_tile == 0)
                def _():
                    cs_ref[...] = _colsum(r)

                @pl.when(row_tile > 0)
                def _():
                    cs_ref[...] += _colsum(r)

        if nk == 1:
            finish(part)
            return
        acc_ref = refs[pos + nex + nout]
        k = pl.program_id(2)

        @pl.when(k == 0)
        def _():
            acc_ref[...] = part

        @pl.when(k > 0)
        def _():
            acc_ref[...] += part

        @pl.when(k == nk - 1)
        def _():
            finish(acc_ref[...])

    res = pl.pallas_call(
        body, out_shape=tuple(out_shapes), grid=grid, in_specs=in_specs, out_specs=tuple(out_specs),
        scratch_shapes=[pltpu.VMEM((tm, tn), F32)] if nk > 1 else [],
        compiler_params=_params(("parallel", "arbitrary" if colsum else "parallel", "arbitrary")), name=name)(*args)
    return res if (colsum or out_t) else res[0]


def _ew(fn, tiles, rows=(), mods=(), outs=(), accs=(), *, name, nctx_tiles=0, tr=None, transposed=False):
    bsz, tlen = tiles[0][0].shape[0], tiles[0][0].shape[1]
    if tr is None:
        tr = _tile(tlen, ROW_TILE, 8)
    nt = tlen // tr
    in_specs, args = [], []
    for arr, cb, width, toff in tiles:
        in_specs.append(pl.BlockSpec((1, tr, width), functools.partial(
            lambda b, t, cb, toff: (b, jnp.maximum(t - toff, 0), cb), cb=cb, toff=toff)))
        args.append(arr)
    for r in rows:
        in_specs.append(pl.BlockSpec(r.shape, lambda b, t: (0, 0)))
        args.append(r)

    def seg_of(t):
        return jnp.where(t >= nctx_tiles, 1, 0)

    for mo in mods:
        in_specs.append(pl.BlockSpec((1, 1) + mo.shape[2:], lambda b, t: (b, seg_of(t), 0, 0)))
        args.append(mo)
    out_shape, out_specs = [], []
    for width, dt in outs:
        out_shape.append(jax.ShapeDtypeStruct((bsz, tlen, width), dt))
        out_specs.append(pl.BlockSpec((1, tr, width), lambda b, t: (b, t, 0)))
    for kk, cc in accs:
        out_shape.append(jax.ShapeDtypeStruct((bsz, 2, kk, cc), F32))
        out_specs.append(pl.BlockSpec((1, 1, kk, cc), lambda b, t: (b, seg_of(t), 0, 0)))
    nti, nr, nm, no, na = len(tiles), len(rows), len(mods), len(outs), len(accs)
    if transposed:
        out_shape.append(jax.ShapeDtypeStruct((outs[0][0], bsz * tlen), outs[0][1]))
        out_specs.append(pl.BlockSpec((outs[0][0], tr), lambda b, t: (0, b * nt + t)))

    def body(*refs):
        t = pl.program_id(1)
        tv = [r[0] for r in refs[:nti]]
        rv = [r[...] for r in refs[nti:nti + nr]]
        mv = [r[0, 0] for r in refs[nti + nr:nti + nr + nm]]
        o_refs = refs[nti + nr + nm:nti + nr + nm + no]
        a_refs = refs[nti + nr + nm + no:nti + nr + nm + no + na]
        seg = seg_of(t)
        ov, av = fn(tv, rv, mv, seg)
        for r, v in zip(o_refs, ov):
            r[0] = v.astype(r.dtype)
        if transposed:
            refs[-1][...] = ov[0].astype(F32).T.astype(refs[-1].dtype)
        if na:
            @pl.when((t == 0) | (t == nctx_tiles))
            def _():
                for r in a_refs:
                    r[...] = jnp.zeros_like(r)

            for r, v in zip(a_refs, av):
                r[0, 0] += v

    res = pl.pallas_call(
        body, out_shape=tuple(out_shape), grid=(bsz, nt), in_specs=in_specs, out_specs=tuple(out_specs),
        compiler_params=_params(("arbitrary", "arbitrary")), name=name)(*args)
    return (res[:no], res[no:no + na], res[-1]) if transposed else (res[:no], res[no:])


def _full(arr):
    return (arr, 0, arr.shape[-1], 0)


def _colsum(v):
    return jnp.sum(v, axis=0, keepdims=True)


def _shifted(x, prev8, next8, first, last, k):
    tr = x.shape[0]
    rid = lax.broadcasted_iota(jnp.int32, x.shape, 0)
    keep_prev = jnp.where(first, 0.0, 1.0)
    keep_next = jnp.where(last, 0.0, 1.0)
    if k == -1:
        return jnp.where(rid == 0, prev8[7:8] * keep_prev, pltpu.roll(x, 1, 0))
    if k == -2:
        r = pltpu.roll(x, 2, 0)
        r = jnp.where(rid == 1, prev8[7:8] * keep_prev, r)
        return jnp.where(rid == 0, prev8[6:7] * keep_prev, r)
    if k == 1:
        return jnp.where(rid == tr - 1, next8[0:1] * keep_next, pltpu.roll(x, tr - 1, 0))
    if k == 2:
        r = pltpu.roll(x, tr - 2, 0)
        r = jnp.where(rid == tr - 2, next8[0:1] * keep_next, r)
        return jnp.where(rid == tr - 1, next8[1:2] * keep_next, r)
    raise ValueError(k)


def _halo_specs(tr, width, cb, n8):
    cur = pl.BlockSpec((1, tr, width), lambda b, t: (b, t, cb))
    prev = pl.BlockSpec((1, 8, width), lambda b, t: (b, jnp.maximum(t * (tr // 8) - 1, 0), cb))
    nxt = pl.BlockSpec((1, 8, width), lambda b, t: (b, jnp.minimum((t + 1) * (tr // 8), n8 - 1), cb))
    return [cur, prev, nxt]


def _conv_fwd(z3, conv_w, conv_b, d, nctx_tiles, tr, name):
    bsz, tlen, _ = z3.shape
    nt = tlen // tr

    def body(x_ref, xp_ref, xn_ref, w_ref, b_ref, o_ref):
        t = pl.program_id(1)
        first = (t == 0) | (t == nctx_tiles)
        last = (t == nctx_tiles - 1) | (t == nt - 1)
        x, p8, n8 = x_ref[0], xp_ref[0], xn_ref[0]
        w = w_ref[...]
        o_ref[0] = (b_ref[...] + w[0:1] * _shifted(x, p8, n8, first, last, -1) + w[1:2] * x
                    + w[2:3] * _shifted(x, p8, n8, first, last, 1) + w[3:4] * _shifted(x, p8, n8, first, last, 2))

    return pl.pallas_call(
        body, out_shape=jax.ShapeDtypeStruct((bsz, tlen, d), F32), grid=(bsz, nt),
        in_specs=_halo_specs(tr, d, 0, tlen // 8) + [pl.BlockSpec((4, d), lambda b, t: (0, 0)),
                                                      pl.BlockSpec((1, d), lambda b, t: (0, 0))],
        out_specs=pl.BlockSpec((1, tr, d), lambda b, t: (b, t, 0)),
        compiler_params=_params(("parallel", "parallel")), name=name)(z3, z3, z3, conv_w, conv_b.reshape(1, d))


def _conv_bwd(dxc, z3, conv_w, d, nctx_tiles, tr, name):
    bsz, tlen, _ = dxc.shape
    nt = tlen // tr

    def body(g_ref, gp_ref, gn_ref, x_ref, xp_ref, xn_ref, w_ref, o_ref, acc_ref):
        t = pl.program_id(1)
        first = (t == 0) | (t == nctx_tiles)
        last = (t == nctx_tiles - 1) | (t == nt - 1)
        g, gp, gn = g_ref[0], gp_ref[0], gn_ref[0]
        x, xp, xn = x_ref[0], xp_ref[0], xn_ref[0]
        w = w_ref[...]
        o_ref[0] = (w[0:1] * _shifted(g, gp, gn, first, last, 1) + w[1:2] * g
                    + w[2:3] * _shifted(g, gp, gn, first, last, -1) + w[3:4] * _shifted(g, gp, gn, first, last, -2))

        @pl.when(t == 0)
        def _():
            acc_ref[...] = jnp.zeros_like(acc_ref)

        acc_ref[0, 0:1] += _colsum(g * _shifted(x, xp, xn, first, last, -1))
        acc_ref[0, 1:2] += _colsum(g * x)
        acc_ref[0, 2:3] += _colsum(g * _shifted(x, xp, xn, first, last, 1))
        acc_ref[0, 3:4] += _colsum(g * _shifted(x, xp, xn, first, last, 2))
        acc_ref[0, 4:5] += _colsum(g)

    return pl.pallas_call(
        body, out_shape=(jax.ShapeDtypeStruct((bsz, tlen, d), F32), jax.ShapeDtypeStruct((bsz, 8, d), F32)),
        grid=(bsz, nt),
        in_specs=_halo_specs(tr, d, 0, tlen // 8) + _halo_specs(tr, d, 0, tlen // 8)
        + [pl.BlockSpec((4, d), lambda b, t: (0, 0))],
        out_specs=(pl.BlockSpec((1, tr, d), lambda b, t: (b, t, 0)), pl.BlockSpec((1, 8, d), lambda b, t: (b, 0, 0))),
        compiler_params=_params(("arbitrary", "arbitrary")), name=name)(dxc, dxc, dxc, z3, z3, z3, conv_w)


def _rg_gates(x, lam, wa_ref, ba, wi_ref, bi, nh, hd):
    sp = _softplus(-lam)
    prs, pis = [], []
    for h in range(nh):
        xh = x[:, h * hd:(h + 1) * hd]
        prs.append(_dot(xh, wa_ref[h]))
        pis.append(_dot(xh, wi_ref[h]))
    r = 1.0 / (1.0 + jnp.exp(-(jnp.concatenate(prs, axis=1) + ba)))
    i = _sigmoid(jnp.concatenate(pis, axis=1) + bi)
    la = -RG_C * sp * r
    a = jnp.exp(la)
    m2 = _one_minus_sq(la, a)
    return sp, r, i, a, jnp.sqrt(m2), lax.rsqrt(m2)


def _scan_tile(t, nctx_tiles, nt, reverse):
    if not reverse:
        return t
    return jnp.where(t < nctx_tiles, nctx_tiles - 1 - t, nt - 1 - (t - nctx_tiles))


def _unscan_tile(t, nctx_tiles, nt, reverse):
    if not reverse:
        return nt - 1 - t
    return jnp.where(t < nt - nctx_tiles, nctx_tiles + t, t - (nt - nctx_tiles))


def _rg_param_specs(d, nh, hd):
    vec = pl.BlockSpec((1, d), lambda b, t: (0, 0))
    mat = pl.BlockSpec((nh, hd, hd), lambda b, t: (0, 0, 0))
    return [vec, mat, vec, mat, vec]


def _ride_split(refs, n_in, n_ride, n_out, n_scratch):
    pos = [n_in, n_in + n_ride, n_in + n_ride + n_out, n_in + 2 * n_ride + n_out, n_in + 2 * n_ride + n_out + n_scratch]
    return (refs[:pos[0]], refs[pos[0]:pos[1]], refs[pos[1]:pos[2]], refs[pos[2]:pos[3]], refs[pos[3]:pos[4]],
            refs[pos[4]:])


def _rg_fwd(xc, lam, wa, ba, wi, bi, reverse, nctx_tiles, tr, name, ride=None):
    bsz, tlen, d = xc.shape
    nh, hd = wa.shape[0], wa.shape[1]
    nt = tlen // tr
    tmap = lambda b, t: (b, _scan_tile(t, nctx_tiles, nt, reverse), 0)
    rarrs, rgather = ride if ride else ([], True)
    nr = len(rarrs)

    def body(*refs):
        (x_ref, lam_ref, wa_ref, ba_ref, wi_ref, bi_ref), rin, (h_ref,), rout, (a_scr, b_scr, carry), sems = _ride_split(
            refs, 6, nr, 1, 3)
        first = (pl.program_id(0) == 0) & (pl.program_id(1) == 0)
        last = (pl.program_id(0) == bsz - 1) & (pl.program_id(1) == nt - 1)
        if nr:
            @pl.when(first)
            def _():
                _exchange_start(rin, rout, sems, rgather)

        @pl.when(pl.program_id(1) == 0)
        def _():
            carry[...] = jnp.zeros_like(carry)

        x = x_ref[0]
        _, _, i, a, mult, _ = _rg_gates(x, lam_ref[...], wa_ref, ba_ref[...], wi_ref, bi_ref[...], nh, hd)
        a_scr[...] = a
        b_scr[...] = mult * (i * x)

        def blk(j, h):
            for r in range(8):
                row = (tr - 1 - (j * 8 + r)) if reverse else (j * 8 + r)
                h = a_scr[pl.ds(row, 1), :] * h + b_scr[pl.ds(row, 1), :]
                h_ref[0, pl.ds(row, 1), :] = h
            return h

        carry[0:1, :] = lax.fori_loop(0, tr // 8, blk, carry[0:1, :])
        if nr:
            @pl.when(last)
            def _():
                _exchange_finish(rin, rout, sems, rgather)

    res = pl.pallas_call(
        body, out_shape=(jax.ShapeDtypeStruct((bsz, tlen, d), F32),) + _exchange_out_shapes(rarrs), grid=(bsz, nt),
        in_specs=[pl.BlockSpec((1, tr, d), tmap)] + _rg_param_specs(d, nh, hd) + [_ANY] * nr,
        out_specs=(pl.BlockSpec((1, tr, d), tmap),) + (_ANY,) * nr,
        scratch_shapes=[pltpu.VMEM((tr, d), F32), pltpu.VMEM((tr, d), F32), pltpu.VMEM((8, d), F32)]
        + (_exchange_sems(nr) if nr else []),
        compiler_params=_params(("arbitrary", "arbitrary")), name=name)(
            xc, lam.reshape(1, d), wa, ba.reshape(1, d), wi, bi.reshape(1, d), *rarrs)
    return res[0], list(res[1:])


def _rg_bwd(xc, h, dmix, z3, lam, wa, ba, wi, bi, addend, reverse, nctx_tiles, tr, name, ride=None):
    bsz, tlen, d = xc.shape
    nh, hd = wa.shape[0], wa.shape[1]
    nt = tlen // tr
    r8 = tr // 8

    def tile_of(t):
        return _unscan_tile(t, nctx_tiles, nt, reverse)

    tmap = lambda b, t: (b, tile_of(t), 0)
    gmap = lambda b, t: (b, tile_of(t), 1)

    def halo_map(b, t):
        tt = tile_of(t)
        if not reverse:
            return (b, jnp.maximum(tt * r8 - 1, 0), 0)
        return (b, jnp.where(tt == nt - 1, 0, (tt + 1) * r8), 0)

    has_add = addend is not None
    rarrs, rgather = ride if ride else ([], True)
    nr = len(rarrs)

    def body(*refs):
        ins, rin, (dx_ref, dwa_ref, dwi_ref, dv_ref), rout, (a_scr, g_scr, dh_scr, carry), sems = _ride_split(
            refs, 10 + int(has_add), nr, 4, 4)
        x_ref, h_ref, halo_ref, dr_ref, z_ref, lam_ref, wa_ref, ba_ref, wi_ref, bi_ref = ins[:10]
        add_ref = ins[10] if has_add else None
        b = pl.program_id(0)
        t = pl.program_id(1)
        tt = tile_of(t)
        if nr:
            @pl.when((b == 0) & (t == 0))
            def _():
                _exchange_start(rin, rout, sems, rgather)

        @pl.when((b == 0) & (t == 0))
        def _():
            dwa_ref[...] = jnp.zeros_like(dwa_ref)
            dwi_ref[...] = jnp.zeros_like(dwi_ref)
            dv_ref[...] = jnp.zeros_like(dv_ref)

        @pl.when(t == 0)
        def _():
            carry[...] = jnp.zeros_like(carry)

        x = x_ref[0]
        sp, r, i, a, mult, inv_mult = _rg_gates(x, lam_ref[...], wa_ref, ba_ref[...], wi_ref, bi_ref[...], nh, hd)
        a_scr[...] = a
        dh_scr[...] = dr_ref[0] * _gelu(z_ref[0])

        def blk(j, cc):
            for rr in range(8):
                row = (j * 8 + rr) if reverse else (tr - 1 - (j * 8 + rr))
                g = dh_scr[pl.ds(row, 1), :] + cc
                g_scr[pl.ds(row, 1), :] = g
                cc = a_scr[pl.ds(row, 1), :] * g
            return cc

        carry[0:1, :] = lax.fori_loop(0, r8, blk, carry[0:1, :])
        g = g_scr[...]
        hv = h_ref[0]
        rid = lax.broadcasted_iota(jnp.int32, hv.shape, 0)
        if not reverse:
            valid = jnp.where(tt > 0, 1.0, 0.0)
            hprev = jnp.where(rid == 0, halo_ref[0][7:8] * valid, pltpu.roll(hv, 1, 0))
        else:
            valid = jnp.where(tt == nctx_tiles - 1, 0.0, 1.0)
            hprev = jnp.where(rid == tr - 1, halo_ref[0][0:1] * valid, pltpu.roll(hv, tr - 1, 0))
        dla = g * hprev * a - g * (i * x) * (a * a) * inv_mult
        dpr = dla * (-RG_C * sp) * r * (1.0 - r)
        dpi = g * mult * x * i * (1.0 - i)
        dx = g * mult * i
        dxs = []
        for hh in range(nh):
            sl = slice(hh * hd, (hh + 1) * hd)
            dxs.append(_dot_nt(dpr[:, sl], wa_ref[hh]) + _dot_nt(dpi[:, sl], wi_ref[hh]))
            dwa_ref[hh] += _dot_tn(x[:, sl], dpr[:, sl])
            dwi_ref[hh] += _dot_tn(x[:, sl], dpi[:, sl])
        dx = dx + jnp.concatenate(dxs, axis=1)
        if has_add:
            dx = dx + add_ref[0]
        dx_ref[0] = dx
        dv_ref[0:1] += _colsum(dla * (-RG_C * r))
        dv_ref[1:2] += _colsum(dpr)
        dv_ref[2:3] += _colsum(dpi)
        if nr:
            @pl.when((b == bsz - 1) & (t == nt - 1))
            def _():
                _exchange_finish(rin, rout, sems, rgather)

    in_specs = [pl.BlockSpec((1, tr, d), tmap), pl.BlockSpec((1, tr, d), tmap), pl.BlockSpec((1, 8, d), halo_map),
                pl.BlockSpec((1, tr, d), tmap), pl.BlockSpec((1, tr, d), gmap)] + _rg_param_specs(d, nh, hd)
    args = [xc, h, h, dmix, z3, lam.reshape(1, d), wa, ba.reshape(1, d), wi, bi.reshape(1, d)]
    if has_add:
        in_specs.append(pl.BlockSpec((1, tr, d), tmap))
        args.append(addend)
    mat = pl.BlockSpec((nh, hd, hd), lambda b, t: (0, 0, 0))
    res = pl.pallas_call(
        body, out_shape=(jax.ShapeDtypeStruct((bsz, tlen, d), F32), jax.ShapeDtypeStruct((nh, hd, hd), F32),
                         jax.ShapeDtypeStruct((nh, hd, hd), F32), jax.ShapeDtypeStruct((8, d), F32))
        + _exchange_out_shapes(rarrs),
        grid=(bsz, nt), in_specs=in_specs + [_ANY] * nr,
        out_specs=(pl.BlockSpec((1, tr, d), tmap), mat, mat, pl.BlockSpec((8, d), lambda b, t: (0, 0))) + (_ANY,) * nr,
        scratch_shapes=[pltpu.VMEM((tr, d), F32), pltpu.VMEM((tr, d), F32), pltpu.VMEM((tr, d), F32),
                        pltpu.VMEM((8, d), F32)] + (_exchange_sems(nr) if nr else []),
        compiler_params=_params(("arbitrary", "arbitrary")), name=name)(*args, *rarrs)
    return res[0], res[1], res[2], res[3], list(res[4:])


def _s5_operators(a_re, a_im, log_dt, b_re, b_im, c_re, c_im, d_skip):
    tt = S5_T
    g, h = d_skip.shape
    th = tt * h
    dt = jnp.exp(log_dt)[..., None]
    xr, xi = a_re * dt, a_im * dt
    taus = jnp.arange(tt + 1, dtype=F32)[None, None, :, None]
    mag = jnp.exp(xr[:, :, None, :] * taus)
    pw_re, pw_im = mag * jnp.cos(xi[:, :, None, :] * taus), mag * jnp.sin(xi[:, :, None, :] * taus)
    nr, ni = pw_re[:, :, 1] - 1.0, pw_im[:, :, 1]
    den = a_re * a_re + a_im * a_im
    cf_re, cf_im = (nr * a_re + ni * a_im) / den, (ni * a_re - nr * a_im) / den
    bt_re, bt_im = jnp.swapaxes(b_re, 2, 3), jnp.swapaxes(b_im, 2, 3)
    bb_re = cf_re[:, :, None] * bt_re - cf_im[:, :, None] * bt_im
    bb_im = cf_re[:, :, None] * bt_im + cf_im[:, :, None] * bt_re

    def outer(p_re, p_im, q_re, q_im):
        pr, pi = p_re[:, :, None, :], p_im[:, :, None, :]
        qr, qi = q_re[:, None], q_im[:, None]
        return (pr * qr - pi * qi).reshape(g, -1, pr.shape[-1]), (pr * qi + pi * qr).reshape(g, -1, pr.shape[-1])

    ops, kerns = [], []
    for dd in range(2):
        e_re, e_im = outer(pw_re[dd], pw_im[dd], c_re[dd], c_im[dd])
        kerns.append(jnp.einsum('gic,gmc->gim', jnp.concatenate([bb_re[dd], -bb_im[dd]], -1),
                                jnp.concatenate([e_re[:, :th], e_im[:, :th]], -1), precision=lax.Precision.HIGHEST))
        v_re, v_im = e_re[:, h:].reshape(g, tt, h, -1), e_im[:, h:].reshape(g, tt, h, -1)
        if dd == 1:
            v_re, v_im = v_re[:, ::-1], v_im[:, ::-1]
        vt = jnp.concatenate([v_re, -v_im], -1).reshape(g, th, -1)
        pr, pi = pw_re[dd][:, :tt], pw_im[dd][:, :tt]
        if dd == 0:
            pr, pi = pr[:, ::-1], pi[:, ::-1]
        w_re, w_im = outer(pr, pi, bb_re[dd], bb_im[dd])
        ops.append((jnp.concatenate([w_re, w_im], -1), jnp.swapaxes(vt, 1, 2)))
    kb_rev = kerns[1].reshape(g, h, tt, h)[:, :, ::-1].reshape(g, h, th)
    blocks = []
    for s in range(tt):
        fwd = jnp.pad(kerns[0][:, :, :th - s * h], ((0, 0), (0, 0), (s * h, 0)))
        bwd = jnp.pad(kb_rev[:, :, (tt - 1 - s) * h:], ((0, 0), (0, 0), (0, (tt - 1 - s) * h)))
        blocks.append(fwd + bwd)
    tz = jnp.stack(blocks, axis=1).reshape(g, th, th)
    tz = tz + jnp.eye(th, dtype=F32)[None] * jnp.tile(d_skip, (1, tt))[:, None, :]
    rows = []
    for dd in range(2):
        re, im = pw_re[dd][:, tt], pw_im[dd][:, tt]
        rows += [jnp.concatenate([re, re], -1), jnp.concatenate([-im, im], -1)]
    return tz, ops[0][0], ops[1][0], ops[0][1], ops[1][1], jnp.stack(rows)


def _chunk_order(j, ncc, nc):
    return jnp.where(j < ncc, ncc - 1 - j, nc - 1 - (j - ncc))


def _s5_specs(nc, ops, lm):
    gb = S5_GB
    act = lambda n_chunks, width: pl.BlockSpec((1, gb * n_chunks, width), lambda i, b: (b, i, 0))
    opspecs = [pl.BlockSpec((gb,) + o.shape[1:], lambda i, b: (i, 0, 0)) for o in ops]
    lspec = pl.BlockSpec((4, gb, lm.shape[-1]), lambda i, b: (0, i, 0))
    return act, opspecs, lspec


def _s5_fwd(u, ops, lm, ncc, name, ride=None):
    bsz, gn, th = u.shape
    g = ops[0].shape[0]
    nc = gn // g
    gb = S5_GB
    p2 = ops[1].shape[-1]
    ph = p2 // 2
    rarrs, rgather = ride if ride else ([], True)
    nr = len(rarrs)

    def body(*refs):
        ((u_ref, tz_ref, wf_ref, wb_ref, vf_ref, vb_ref, l_ref), rin, (y_ref, hf_ref, hb_ref), rout,
         (sf, sfs, sb, sbs), sems) = _ride_split(refs, 7, nr, 3, 4)
        if nr:
            @pl.when((pl.program_id(0) == 0) & (pl.program_id(1) == 0))
            def _():
                _exchange_start(rin, rout, sems, rgather)

        a1f, a2f, a1b, a2b = l_ref[0], l_ref[1], l_ref[2], l_ref[3]
        for gi in range(gb):
            ug = u_ref[0, pl.ds(gi * nc, nc), :]
            s1 = _dot(ug, wf_ref[gi])
            s2 = _dot(ug, wb_ref[gi])
            sf[pl.ds(gi, nc, stride=gb), :] = s1
            sfs[pl.ds(gi, nc, stride=gb), :] = pltpu.roll(s1, ph, 1)
            sb[pl.ds(gi, nc, stride=gb), :] = s2
            sbs[pl.ds(gi, nc, stride=gb), :] = pltpu.roll(s2, ph, 1)

        def step(j, hs):
            hf, hfs, hb, hbs = hs
            rf = pl.ds(pl.multiple_of(j * gb, gb), gb)
            rb = pl.ds(pl.multiple_of(_chunk_order(j, ncc, nc) * gb, gb), gb)
            s1, s1s, s2, s2s = sf[rf, :], sfs[rf, :], sb[rb, :], sbs[rb, :]
            sf[rf, :] = hf
            sb[rb, :] = hb
            return (a1f * hf + a2f * hfs + s1, a1f * hfs - a2f * hf + s1s,
                    a1b * hb + a2b * hbs + s2, a1b * hbs - a2b * hb + s2s)

        zero = jnp.zeros((gb, p2), F32)
        lax.fori_loop(0, nc, step, (zero, zero, zero, zero))
        for gi in range(gb):
            rows = pl.ds(gi * nc, nc)
            hfg = sf[pl.ds(gi, nc, stride=gb), :]
            hbg = sb[pl.ds(gi, nc, stride=gb), :]
            hf_ref[0, rows, :] = hfg
            hb_ref[0, rows, :] = hbg
            y_ref[0, rows, :] = (_dot(u_ref[0, rows, :], tz_ref[gi]) + _dot(hfg, vf_ref[gi])
                                 + _dot(hbg, vb_ref[gi])).astype(y_ref.dtype)
        if nr:
            @pl.when((pl.program_id(0) == g // gb - 1) & (pl.program_id(1) == bsz - 1))
            def _():
                _exchange_finish(rin, rout, sems, rgather)

    act, opspecs, lspec = _s5_specs(nc, ops, lm)
    res = pl.pallas_call(
        body, out_shape=(jax.ShapeDtypeStruct(u.shape, ACT_DT), jax.ShapeDtypeStruct((bsz, gn, p2), F32),
                         jax.ShapeDtypeStruct((bsz, gn, p2), F32)) + _exchange_out_shapes(rarrs),
        grid=(g // gb, bsz), in_specs=[act(nc, th)] + opspecs + [lspec] + [_ANY] * nr,
        out_specs=(act(nc, th), act(nc, p2), act(nc, p2)) + (_ANY,) * nr,
        scratch_shapes=[pltpu.VMEM((gb * nc, p2), F32) for _ in range(4)] + (_exchange_sems(nr) if nr else []),
        compiler_params=_params(("arbitrary", "arbitrary")), name=name)(u, *ops, lm, *rarrs)
    return res[0], res[1], res[2], list(res[3:])


def _s5_bwd(dy, u, hf, hb, ops_t, lm, ncc, name):
    bsz, gn, th = u.shape
    g = ops_t[0].shape[0]
    nc = gn // g
    gb = S5_GB
    p2 = lm.shape[-1]
    ph = p2 // 2

    def body(dy_ref, u_ref, hf_ref, hb_ref, tzt_ref, wft_ref, wbt_ref, vft_ref, vbt_ref, l_ref,
             du_ref, dtz_ref, dwf_ref, dwb_ref, dvf_ref, dvb_ref, dl_ref, jf, jfs, jb, jbs, hfk, hbk):
        a1f, a2f, a1b, a2b = l_ref[0], l_ref[1], l_ref[2], l_ref[3]

        @pl.when(pl.program_id(1) == 0)
        def _():
            for r in (dtz_ref, dwf_ref, dwb_ref, dvf_ref, dvb_ref, dl_ref):
                r[...] = jnp.zeros_like(r)

        for gi in range(gb):
            rows = pl.ds(gi * nc, nc)
            dyg = dy_ref[0, rows, :]
            i1 = _dot(dyg, vft_ref[gi])
            i2 = _dot(dyg, vbt_ref[gi])
            jf[pl.ds(gi, nc, stride=gb), :] = i1
            jfs[pl.ds(gi, nc, stride=gb), :] = pltpu.roll(i1, ph, 1)
            jb[pl.ds(gi, nc, stride=gb), :] = i2
            jbs[pl.ds(gi, nc, stride=gb), :] = pltpu.roll(i2, ph, 1)
            hfk[pl.ds(gi, nc, stride=gb), :] = hf_ref[0, rows, :]
            hbk[pl.ds(gi, nc, stride=gb), :] = hb_ref[0, rows, :]

        def step(j, carry):
            qf, qfs, qb, qbs, d1f, d2f, d1b, d2b = carry
            rf = pl.ds(pl.multiple_of((nc - 1 - j) * gb, gb), gb)
            rb = pl.ds(pl.multiple_of(_chunk_order(nc - 1 - j, ncc, nc) * gb, gb), gb)
            i1, i1s, i2, i2s = jf[rf, :], jfs[rf, :], jb[rb, :], jbs[rb, :]
            h1, h2 = hfk[rf, :], hbk[rb, :]
            jf[rf, :] = qf
            jb[rb, :] = qb
            return (i1 + a1f * qf - a2f * qfs, i1s + a1f * qfs + a2f * qf,
                    i2 + a1b * qb - a2b * qbs, i2s + a1b * qbs + a2b * qb,
                    d1f + qf * h1, d2f + qfs * h1, d1b + qb * h2, d2b + qbs * h2)

        zero = jnp.zeros((gb, p2), F32)
        fin = lax.fori_loop(0, nc, step, (zero,) * 8)
        dl_ref[0] += fin[4]
        dl_ref[1] += pltpu.roll(fin[5], ph, 1)
        dl_ref[2] += fin[6]
        dl_ref[3] += pltpu.roll(fin[7], ph, 1)
        for gi in range(gb):
            rows = pl.ds(gi * nc, nc)
            dyg = dy_ref[0, rows, :]
            ug = u_ref[0, rows, :]
            dsf = jf[pl.ds(gi, nc, stride=gb), :]
            dsb = jb[pl.ds(gi, nc, stride=gb), :]
            du_ref[0, rows, :] = (_dot(dyg, tzt_ref[gi]) + _dot(dsf, wft_ref[gi]) + _dot(dsb, wbt_ref[gi])).astype(du_ref.dtype)
            dtz_ref[gi] += _dot_tn(ug, dyg)
            dwf_ref[gi] += _dot_tn(ug, dsf)
            dwb_ref[gi] += _dot_tn(ug, dsb)
            dvf_ref[gi] += _dot_tn(hf_ref[0, rows, :], dyg)
            dvb_ref[gi] += _dot_tn(hb_ref[0, rows, :], dyg)

    act, opspecs, lspec = _s5_specs(nc, ops_t, lm)
    gshape = lambda o: jax.ShapeDtypeStruct(o.shape[:1] + o.shape[1:][::-1], F32)
    gspec = lambda o: pl.BlockSpec((gb,) + o.shape[1:][::-1], lambda i, b: (i, 0, 0))
    res = pl.pallas_call(
        body, out_shape=tuple([jax.ShapeDtypeStruct(u.shape, ACT_DT)] + [gshape(o) for o in ops_t]
                              + [jax.ShapeDtypeStruct(lm.shape, F32)]),
        grid=(g // gb, bsz),
        in_specs=[act(nc, th), act(nc, th), act(nc, p2), act(nc, p2)] + opspecs + [lspec],
        out_specs=tuple([act(nc, th)] + [gspec(o) for o in ops_t] + [lspec]),
        scratch_shapes=[pltpu.VMEM((gb * nc, p2), F32) for _ in range(6)],
        compiler_params=_params(("parallel", "arbitrary")), name=name)(dy, u, hf, hb, *ops_t, lm)
    return res[0], tuple(res[1:])


def _to_chunks(s, ctx_len, g):
    bsz, tlen, d = s.shape
    h = d // g
    seq = tlen - ctx_len
    rows = seq // GRID_W
    cpart = s[:, :ctx_len].reshape(bsz, ctx_len // S5_T, S5_T, g, h).transpose(0, 3, 1, 2, 4)
    lpart = s[:, ctx_len:].reshape(bsz, rows, GRID_W, g, h).transpose(0, 3, 2, 1, 4)
    cpart = cpart.reshape(bsz, g, ctx_len // S5_T, S5_T * h)
    lpart = lpart.reshape(bsz, g, seq // S5_T, S5_T * h)
    return jnp.concatenate([cpart, lpart], axis=2).reshape(bsz, g * (tlen // S5_T), S5_T * h)


def _from_chunks(y, ctx_len, g):
    bsz, gn, th = y.shape
    nc = gn // g
    h = th // S5_T
    ncc = ctx_len // S5_T
    seq = (nc - ncc) * S5_T
    rows = seq // GRID_W
    y = y.reshape(bsz, g, nc, th)
    cpart = y[:, :, :ncc].reshape(bsz, g, ncc, S5_T, h).transpose(0, 2, 3, 1, 4).reshape(bsz, ctx_len, g * h)
    lpart = y[:, :, ncc:].reshape(bsz, g, GRID_W, rows, h).transpose(0, 3, 2, 1, 4).reshape(bsz, seq, g * h)
    return jnp.concatenate([cpart, lpart], axis=1)


def _me():
    return lax.axis_index("x"), lax.axis_index("y"), lax.axis_index("c")


def _peer(k):
    x, y, c = _me()
    px = (1 - x) if (k & 4) else x
    py = (1 - y) if (k & 2) else y
    pc = (1 - c) if (k & 1) else c
    return (px, py, pc), 4 * px + 2 * py + pc


def _exchange(arrs, gather, name):
    n = len(arrs)

    def body(*refs):
        _exchange_start(refs[:n], refs[n:2 * n], refs[2 * n:], gather)
        _exchange_finish(refs[:n], refs[n:2 * n], refs[2 * n:], gather)

    return pl.pallas_call(
        body, out_shape=_exchange_out_shapes(arrs), in_specs=[_ANY] * n, out_specs=tuple([_ANY] * n),
        scratch_shapes=_exchange_sems(n), name=name)(*arrs)


_ANY = pl.BlockSpec(memory_space=pl.ANY)


def _exchange_out_shapes(arrs):
    return tuple(jax.ShapeDtypeStruct((N_DEV,) + a.shape[-2:], a.dtype) for a in arrs)


def _exchange_sems(n):
    return [pltpu.SemaphoreType.DMA((n * (N_DEV - 1),)), pltpu.SemaphoreType.DMA((n * (N_DEV - 1),)),
            pltpu.SemaphoreType.DMA((n,))]


def _exchange_copies(x_refs, o_refs, sems, gather, with_recvs):
    send_sems, recv_sems, local_sems = sems
    n = len(x_refs)
    npeer = N_DEV - 1
    xi, yi, ci = _me()
    me = 4 * xi + 2 * yi + ci
    mine = [x if gather else x.at[me] for x in x_refs]
    local = [pltpu.make_async_copy(mine[i], o_refs[i].at[me], local_sems.at[i]) for i in range(n)]
    sends, recvs = [], []
    for k in range(1, N_DEV):
        dev, pid = _peer(k)
        for i in range(n):
            slot = i * npeer + k - 1
            sends.append(pltpu.make_async_remote_copy(
                src_ref=x_refs[i] if gather else x_refs[i].at[pid], dst_ref=o_refs[i].at[me],
                send_sem=send_sems.at[slot], recv_sem=recv_sems.at[slot], device_id=dev, device_id_type=MESH))
            if with_recvs:
                recvs.append(pltpu.make_async_remote_copy(
                    src_ref=mine[i], dst_ref=o_refs[i].at[pid], send_sem=send_sems.at[slot],
                    recv_sem=recv_sems.at[slot], device_id=dev, device_id_type=MESH))
    return local, sends, recvs


def _exchange_start(x_refs, o_refs, sems, gather):
    local, sends, _ = _exchange_copies(x_refs, o_refs, sems, gather, False)
    for cp in local + sends:
        cp.start()


def _exchange_finish(x_refs, o_refs, sems, gather):
    local, sends, recvs = _exchange_copies(x_refs, o_refs, sems, gather, True)
    for cp in recvs:
        cp.wait_recv()
    for cp in sends:
        cp.wait_send()
    for cp in local:
        cp.wait()


def _sum_slots(x3, name):
    _, r, cdim = x3.shape
    tr = _tile(r, 256, 16)

    def body(x_ref, o_ref):
        acc = x_ref[0].astype(F32)
        for s in range(1, N_DEV):
            acc = acc + x_ref[s].astype(F32)
        o_ref[...] = acc

    return pl.pallas_call(
        body, out_shape=jax.ShapeDtypeStruct((r, cdim), F32), grid=(r // tr,),
        in_specs=[pl.BlockSpec((N_DEV, tr, cdim), lambda i: (0, i, 0))], out_specs=pl.BlockSpec((tr, cdim), lambda i: (i, 0)),
        compiler_params=_params(("parallel",)), name=name)(x3)


def _pack(arrs, dtype, lead=0):
    flat = jnp.concatenate([a.reshape(a.shape[:lead] + (-1,)).astype(dtype) for a in arrs], axis=-1)
    n = flat.shape[-1]
    pad = -n % (PACK_W * 16)
    flat = jnp.pad(flat, [(0, 0)] * lead + [(0, pad)])
    return flat.reshape(flat.shape[:lead] + (-1, PACK_W))


def _unpack(buf, shapes, lead=0):
    flat = buf.reshape(buf.shape[:lead] + (-1,))
    out, off = [], 0
    for shp in shapes:
        n = math.prod(shp)
        out.append(flat[..., off:off + n].reshape(buf.shape[:lead] + tuple(shp)))
        off += n
    return out


def _adamw_math(wv, gv, m0, v0):
    m1 = ADAM_B1 * m0 + (1.0 - ADAM_B1) * gv
    v1 = ADAM_B2 * v0 + (1.0 - ADAM_B2) * (gv * gv)
    m_hat = m1 / (1.0 - ADAM_B1 ** ADAM_STEP)
    v_hat = v1 / (1.0 - ADAM_B2 ** ADAM_STEP)
    delta = -ADAM_LR * (m_hat / (jnp.sqrt(v_hat) + ADAM_EPS) + ADAM_WD * wv)
    return delta, m1, v1


def _adamw(w, g, m, v, name):
    def fn(tv, rv, mv, seg):
        return _adamw_math(*tv), ()

    outs, _ = _ew(fn, [_full(a[None]) for a in (w, g, m, v)], outs=[(w.shape[-1], F32)] * 3, name=name)
    return [o[0] for o in outs]


def _sum_adamw(slots, w, m, v, name):
    _, r, cdim = slots.shape
    tr = _tile(r, 128, 16)

    def body(s_ref, w_ref, m_ref, v_ref, g_ref, d_ref, mo_ref, vo_ref):
        gv = s_ref[0].astype(F32)
        for s in range(1, N_DEV):
            gv = gv + s_ref[s].astype(F32)
        g_ref[...] = gv
        d_ref[...], mo_ref[...], vo_ref[...] = _adamw_math(w_ref[...], gv, m_ref[...], v_ref[...])

    flat = pl.BlockSpec((tr, cdim), lambda i: (i, 0))
    return pl.pallas_call(
        body, out_shape=tuple(jax.ShapeDtypeStruct((r, cdim), F32) for _ in range(4)), grid=(r // tr,),
        in_specs=[pl.BlockSpec((N_DEV, tr, cdim), lambda i: (0, i, 0)), flat, flat, flat], out_specs=(flat,) * 4,
        compiler_params=_params(("parallel",)), name=name)(slots, w, m, v)


def _gathered(n, p):
    return p if n in COL_SHARDED else p.reshape(-1, p.shape[-1])


def _layer_fwd(l, xin, modt, wts, sm, cfg, u_in):
    bsz, tlen, d = xin.shape
    bt = bsz * tlen
    nct, tr, ctx_len, g = cfg['nct'], cfg['tr'], cfg['ctx_len'], cfg['g']
    ncc, nc = ctx_len // S5_T, tlen // S5_T
    alpha = cfg['alpha']
    nm = lambda s: f"l{l}_{s}"

    def modulate(xv, i_shift, i_scale, name):
        def fn(tv, rv, mv, seg):
            mo = mv[0]
            return (tv[0] * (1.0 + mo[i_scale:i_scale + 1]) + mo[i_shift:i_shift + 1],), ()
        o, _, ot = _ew(fn, [_full(xv)], mods=[modt], outs=[(d, ACT_DT)], nctx_tiles=nct, tr=tr, name=name, transposed=True)
        return o[0], ot

    def ln_fwd(xv, mv_, i_gate, gam, bet, name, nxt=None):
        def fn(tv, rv, mv, seg):
            z = alpha * tv[0] + mv[0][i_gate:i_gate + 1] * tv[1]
            mu = jnp.mean(z, axis=-1, keepdims=True)
            zc = z - mu
            var = jnp.mean(zc * zc, axis=-1, keepdims=True)
            y = zc * lax.rsqrt(var + LN_EPS) * rv[0] + rv[1]
            if nxt is None:
                return (y,), ()
            return (y * (1.0 + mv[1][nxt[2]:nxt[2] + 1]) + mv[1][nxt[1]:nxt[1] + 1], y), ()
        tiles, rws = [_full(xv), _full(mv_)], [gam.reshape(1, d), bet.reshape(1, d)]
        if nxt is None:
            return _ew(fn, tiles, rows=rws, mods=[modt], outs=[(d, F32)], nctx_tiles=nct, tr=tr, name=name)[0][0], None
        o, _, ot = _ew(fn, tiles, rows=rws, mods=[modt, nxt[0]], outs=[(d, ACT_DT), (d, F32)], nctx_tiles=nct, tr=tr,
                       name=name, transposed=True)
        return o[1], (o[0], ot)

    sv = {'x': xin}
    u, sv['uT'] = u_in if u_in is not None else modulate(xin, 0, 1, nm("mod1"))
    z3 = _mm(u.reshape(bt, d), wts['w_in'][l], b_blocked=True, name=nm("w_in")).reshape(bsz, tlen, 3 * d)
    sv['z3'] = z3
    xc = _conv_fwd(z3, sm['conv_w'][l], sm['conv_b'][l], d, nct, tr, nm("conv"))
    sv['xc'] = xc
    nxt = l + 1 < cfg['depth']
    riders = [[('mlp_w1', l)], [('mlp_w2', l)] + ([('s5_glu_w', l + 1), ('w_out', l + 1)] if nxt else []),
              [('w_in', l + 1)] if nxt else []]

    def ride_of(rs):
        return ([cfg['wloc'][n][lay] for n, lay in rs], True) if rs else None

    def landed(rs, got):
        for (n, lay), p in zip(rs, got):
            wts[n][lay] = _gathered(n, p)

    hs = []
    for dd in range(2):
        h, got = _rg_fwd(xc, sm['rg_lambda'][l, dd], sm['rg_wa'][l, dd], sm['rg_ba'][l, dd], sm['rg_wi'][l, dd],
                         sm['rg_bi'][l, dd], bool(dd), nct, tr, nm(f"rg_fwd{dd}"), ride=ride_of(riders[dd]))
        hs.append(h)
        landed(riders[dd], got)
    sv['hf'], sv['hb'] = hs

    def copy_fn(tv, rv, mv, seg):
        return (tv[0],), ()
    s5u = _to_chunks(_ew(copy_fn, [(z3, 2, d, 0)], outs=[(d, ACT_DT)], tr=tr, name=nm("s5_in"))[0][0], ctx_len, g)
    sv['s5u'] = s5u
    y, hf5, hb5, got = _s5_fwd(s5u, cfg['s5_ops'][l], cfg['s5_lm'][l], ncc, nm("s5_fwd"), ride=ride_of(riders[2]))
    landed(riders[2], got)
    sv['hf5'], sv['hb5'] = hf5, hb5
    ytok = _from_chunks(y, ctx_len, g)
    sv['ytok'] = ytok

    def gelu_fn(tv, rv, mv, seg):
        return (_gelu(tv[0].astype(F32)),), ()
    o, _, sv['gactT'] = _ew(gelu_fn, [_full(ytok)], outs=[(d, ACT_DT)], tr=tr, name=nm("s5_gelu"), transposed=True)
    gact = o[0]
    sv['gact'] = gact
    gpre = _mm(gact.reshape(bt, d), wts['s5_glu_w'][l], bias=sm['s5_glu_b'][l], name=nm("glu")).reshape(bsz, tlen, d)
    sv['gpre'] = gpre

    def mix_fn(tv, rv, mv, seg):
        rg = (tv[0] + tv[1]) * _gelu(tv[2])
        s5o = tv[3].astype(F32) * _sigmoid(tv[4])
        return (jnp.concatenate([rg, s5o], axis=1),), ()
    o, _, sv['mixinT'] = _ew(mix_fn, [_full(hs[0]), _full(hs[1]), (z3, 1, d, 0), _full(gact), _full(gpre)],
                             outs=[(2 * d, ACT_DT)], tr=tr, name=nm("mix"), transposed=True)
    mixin = o[0].reshape(bt, 2 * d)
    mo = _mm(mixin, wts['w_out'][l], bias=sm['b_out'][l], name=nm("w_out")).reshape(bsz, tlen, d)
    sv['mo'] = mo
    x1, (u2, sv['u2T']) = ln_fwd(xin, mo, 2, sm['ln1_g'][l], sm['ln1_b'][l], nm("ln1"), nxt=(modt, 3, 4))
    sv['x1'] = x1
    rl, sv['rlT'] = _mm(u2.reshape(bt, d), wts['mlp_w1'][l], b_blocked=True, bias=sm['mlp_b1'][l],
                        epi=lambda r: jnp.maximum(r, 0.0), out_t=True, out_dtype=ACT_DT, tm=2176, name=nm("mlp1"))
    sv['rl'] = rl
    fo = _mm(rl, wts['mlp_w2'][l], bias=sm['mlp_b2'][l], a_fn=lambda v: v * v, name=nm("mlp2")).reshape(bsz, tlen, d)
    sv['fo'] = fo
    x2, u_next = ln_fwd(x1, fo, 5, sm['ln2_g'][l], sm['ln2_b'][l], nm("ln2"),
                        nxt=(cfg['modts'][l + 1], 0, 1) if nxt else None)
    return x2, sv, u_next


def _layer_bwd(l, dx2, sv, modt, wts, sm, cfg, pending):
    bsz, tlen, d = dx2.shape
    bt = bsz * tlen
    nct, tr, ctx_len, g = cfg['nct'], cfg['tr'], cfg['ctx_len'], cfg['g']
    ncc, nc = ctx_len // S5_T, tlen // S5_T
    alpha = cfg['alpha']
    nm = lambda s: f"l{l}_{s}"
    gr = {}

    def ln_bwd(xv, mv_, i_gate, gam, dy, name, through=None):
        def fn(tv, rv, mv, seg):
            xx, mm_, dyy = tv[:3]
            gate = mv[0][i_gate:i_gate + 1]
            z = alpha * xx + gate * mm_
            mu = jnp.mean(z, axis=-1, keepdims=True)
            zc = z - mu
            var = jnp.mean(zc * zc, axis=-1, keepdims=True)
            rstd = lax.rsqrt(var + LN_EPS)
            xhat = zc * rstd
            more = []
            if through is not None:
                duu = tv[3]
                dyy = dyy + duu * (1.0 + mv[0][through[1]:through[1] + 1])
                more = [_colsum(duu), _colsum(duu * (xhat * rv[0] + rv[1]))]
            dxh = dyy * rv[0]
            dz = rstd * (dxh - jnp.mean(dxh, axis=-1, keepdims=True) - xhat * jnp.mean(dxh * xhat, axis=-1, keepdims=True))
            dm = gate * dz
            acc = jnp.concatenate([_colsum(dz * mm_), _colsum(dyy * xhat), _colsum(dyy), _colsum(dm)] + more, axis=0)
            return (alpha * dz, dm), (acc,)
        tiles, rws = [_full(xv), _full(mv_), _full(dy)], [gam.reshape(1, d)]
        if through is not None:
            tiles.append(_full(through[0]))
            rws.append(through[2].reshape(1, d))
        o, a = _ew(fn, tiles, rows=rws, mods=[modt], outs=[(d, F32), (d, ACT_DT)],
                   accs=[(4 if through is None else 6, d)], nctx_tiles=nct, tr=tr, name=name)
        return o[0], o[1], a[0]

    def mod_bwd(du, xv, i_scale, addend, name):
        def fn(tv, rv, mv, seg):
            duu, xx, add = tv
            acc = jnp.concatenate([_colsum(duu), _colsum(duu * xx)], axis=0)
            return (add + duu * (1.0 + mv[0][i_scale:i_scale + 1]),), (acc,)
        o, a = _ew(fn, [_full(du), _full(xv), _full(addend)], mods=[modt], outs=[(d, F32)], accs=[(2, d)],
                   nctx_tiles=nct, tr=tr, name=name)
        return o[0], a[0]

    def row_blocks(gw):
        return gw.reshape(N_DEV, -1, gw.shape[-1])

    dx1a, dfo, acc2 = ln_bwd(sv['x1'], sv['fo'], 5, sm['ln2_g'][l], dx2, nm("ln2_bwd"))
    dfo2 = dfo.reshape(bt, d)
    dhp, db1 = _mm(dfo2, wts['mlp_w2'][l], tb=True, epi=lambda r, rl: r * (2.0 * rl.astype(F32)), extras=[sv['rl']],
                   colsum=True, out_dtype=ACT_DT, name=nm("mlp2_dx"))
    gr['mlp_b1'] = db1[0]
    gr['mlp_w2'] = row_blocks(_mm(sv['rlT'], dfo2, a_fn=lambda v: v * v, out_dtype=WIRE_DT, tm=512, tk=WGRAD_TK,
                                  name=nm("mlp2_dw")))
    du2 = _mm(dhp, wts['mlp_w1'][l], tb=True, b_blocked=True, kblocks=DGRAD_KBLOCKS,
              name=nm("mlp1_dx")).reshape(bsz, tlen, d)
    gr['mlp_w1'] = _mm(sv['u2T'], dhp, out_blocks=N_DEV, out_dtype=WIRE_DT, tk=WGRAD_TK, name=nm("mlp1_dw"))
    gr['ln2_g'] = acc2[:, :, 1].sum((0, 1))
    gr['ln2_b'] = acc2[:, :, 2].sum((0, 1))
    gr['mlp_b2'] = acc2[:, :, 3].sum((0, 1))

    dxa, dmo, acc1 = ln_bwd(sv['x'], sv['mo'], 2, sm['ln1_g'][l], dx1a, nm("ln1_bwd"), through=(du2, 4, sm['ln1_b'][l]))
    gr['ln1_g'] = acc1[:, :, 1].sum((0, 1))
    gr['ln1_b'] = acc1[:, :, 2].sum((0, 1))
    gr['b_out'] = acc1[:, :, 3].sum((0, 1))
    dmo2 = dmo.reshape(bt, d)
    dmix = _mm(dmo2, wts['w_out'][l], tb=True, name=nm("w_out_dx")).reshape(bsz, tlen, 2 * d)
    gr['w_out'] = row_blocks(_mm(sv['mixinT'], dmo2, out_dtype=WIRE_DT, tk=WGRAD_TK, name=nm("w_out_dw")))

    def glu_bwd(tv, rv, mv, seg):
        ds, ga, gp = tv
        ga = ga.astype(F32)
        sg = _sigmoid(gp)
        dg = ds * ga * sg * (1.0 - sg)
        return (dg, ds * sg), (_colsum(dg),)
    o, a = _ew(glu_bwd, [(dmix, 1, d, 0), _full(sv['gact']), _full(sv['gpre'])], outs=[(d, ACT_DT), (d, F32)],
               accs=[(1, d)], tr=tr, name=nm("glu_bwd"))
    dgp, t1 = o
    gr['s5_glu_b'] = a[0][:, 1, 0].sum(0)
    dgp2 = dgp.reshape(bt, d)
    dyt = _mm(dgp2, wts['s5_glu_w'][l], tb=True, epi=lambda r, t1v, yv: (r + t1v) * _gelu_grad(yv.astype(F32)),
              extras=[t1.reshape(bt, d), sv['ytok'].reshape(bt, d)], out_dtype=ACT_DT, tn=512,
              name=nm("glu_dx")).reshape(bsz, tlen, d)
    gr['s5_glu_w'] = row_blocks(_mm(sv['gactT'], dgp2, out_dtype=WIRE_DT, tk=WGRAD_TK, name=nm("glu_dw")))
    du5, dops = _s5_bwd(_to_chunks(dyt, ctx_len, g), sv['s5u'], sv['hf5'], sv['hb5'], cfg['s5_ops_t'][l],
                        cfg['s5_lm'][l], ncc, nm("s5_bwd"))
    ds5u = _from_chunks(du5, ctx_len, g)
    gr['s5_dops'] = dops

    z3 = sv['z3']
    dxc = None
    for dd in range(2):
        names = ('mlp_w2',) if dd == 0 else ('mlp_w1', 's5_glu_w', 'w_out')
        riders = (pending if dd == 0 else []) + [(n, l, gr.pop(n)) for n in names]
        dxc, dwa, dwi, dv, got = _rg_bwd(sv['xc'], sv['hf'] if dd == 0 else sv['hb'], dmix, z3, sm['rg_lambda'][l, dd],
                                         sm['rg_wa'][l, dd], sm['rg_ba'][l, dd], sm['rg_wi'][l, dd],
                                         sm['rg_bi'][l, dd], dxc, bool(dd), nct, tr, nm(f"rg_bwd{dd}"),
                                         ride=([r[2] for r in riders], False))
        for (n, lay, _), p in zip(riders, got):
            cfg['slots'][n][lay] = p
        gr[f'rg_wa{dd}'], gr[f'rg_wi{dd}'] = dwa, dwi
        gr[f'rg_lambda{dd}'] = dv[0] * (-_sigmoid(-sm['rg_lambda'][l, dd]))
        gr[f'rg_ba{dd}'], gr[f'rg_bi{dd}'] = dv[1], dv[2]
    drgx, accc = _conv_bwd(dxc, z3, sm['conv_w'][l], d, nct, tr, nm("conv_bwd"))
    gr['conv_w'] = accc[:, 0:4].sum(0)
    gr['conv_b'] = accc[:, 4].sum(0)

    def dz_fn(tv, rv, mv, seg):
        dgate = tv[0] * (tv[1] + tv[2]) * _gelu_grad(tv[3])
        return (jnp.concatenate([tv[4], dgate, tv[5].astype(F32)], axis=1),), ()
    dz = _ew(dz_fn, [(dmix, 0, d, 0), _full(sv['hf']), _full(sv['hb']), (z3, 1, d, 0), _full(drgx), _full(ds5u)],
             outs=[(3 * d, ACT_DT)], tr=tr, name=nm("dz"))[0][0].reshape(bt, 3 * d)
    du = _mm(dz, wts['w_in'][l], tb=True, b_blocked=True, kblocks=DGRAD_KBLOCKS,
             name=nm("w_in_dx")).reshape(bsz, tlen, d)
    still = [('w_in', l, _mm(sv['uT'], dz, out_blocks=N_DEV, out_dtype=WIRE_DT, tk=WGRAD_TK, name=nm("w_in_dw")))]
    dxin, accm1 = mod_bwd(du, sv['x'], 1, dxa, nm("mod1_bwd"))
    dmod = jnp.stack([accm1[:, :, 0], accm1[:, :, 1], acc1[:, :, 0], acc1[:, :, 4], acc1[:, :, 5], acc2[:, :, 0]], axis=2)
    return dxin, dmod, gr, still


def kernel(x, c, ctx, c_ctx, ada_w, ada_b, ln1_g, ln1_b, w_in, conv_w, conv_b, rg_lambda, rg_wa, rg_ba, rg_wi, rg_bi, s5_a_re, s5_a_im, s5_log_dt, s5_b_re, s5_b_im, s5_c_re, s5_c_im, s5_d, s5_glu_w, s5_glu_b, w_out, b_out, ln2_g, ln2_b, mlp_w1, mlp_b1, mlp_w2, mlp_b2, loss_target, m_c_ctx, m_ada_w, m_ada_b, m_ln1_g, m_ln1_b, m_w_in, m_conv_w, m_conv_b, m_rg_lambda, m_rg_wa, m_rg_ba, m_rg_wi, m_rg_bi, m_s5_a_re, m_s5_a_im, m_s5_log_dt, m_s5_b_re, m_s5_b_im, m_s5_c_re, m_s5_c_im, m_s5_d, m_s5_glu_w, m_s5_glu_b, m_w_out, m_b_out, m_ln2_g, m_ln2_b, m_mlp_w1, m_mlp_b1, m_mlp_w2, m_mlp_b2, v_c_ctx, v_ada_w, v_ada_b, v_ln1_g, v_ln1_b, v_w_in, v_conv_w, v_conv_b, v_rg_lambda, v_rg_wa, v_rg_ba, v_rg_wi, v_rg_bi, v_s5_a_re, v_s5_a_im, v_s5_log_dt, v_s5_b_re, v_s5_b_im, v_s5_c_re, v_s5_c_im, v_s5_d, v_s5_glu_w, v_s5_glu_b, v_w_out, v_b_out, v_ln2_g, v_ln2_b, v_mlp_w1, v_mlp_b1, v_mlp_w2, v_mlp_b2):
    loc = dict(locals())
    w = {n: loc[n] for n in WEIGHTS}
    mom = {n: loc["m_" + n] for n in WEIGHTS}
    vel = {n: loc["v_" + n] for n in WEIGHTS}
    bsz, seq, d = x.shape
    ctx_len = ctx.shape[1]
    depth = ada_w.shape[0]
    g = s5_a_re.shape[2]
    nmod = ada_b.shape[1] // d
    modc = ada_w.shape[2]
    tr = _tile(ctx_len, ROW_TILE, 8)
    cfg = dict(nct=ctx_len // tr, tr=tr, ctx_len=ctx_len, g=g, alpha=(2.0 * depth) ** 0.25)
    xi, yi, ci = _me()
    me = 4 * xi + 2 * yi + ci

    small_shapes = [c.shape] + [w[n].shape for n in CHAN_SHARDED]
    got = _exchange([_pack([c] + [w[n] for n in CHAN_SHARDED], F32)], True, "gather_small")[0]
    parts = _unpack(got, small_shapes, lead=1)
    c_all = parts[0].reshape(N_DEV * bsz, d)
    sm = {n: w[n] for n in WEIGHTS if n not in BIG and n not in CHAN_SHARDED and n != 'ada_w'}
    for n, p in zip(CHAN_SHARDED, parts[1:]):
        sm[n] = jnp.moveaxis(p, 0, -2).reshape(p.shape[1:-1] + (-1,))

    a_ext = jnp.concatenate([c_all, jnp.broadcast_to(c_ctx[None], (N_DEV, d))], axis=0)
    nrow = a_ext.shape[0]
    my_ada_b = lax.dynamic_slice_in_dim(ada_b, me * modc, modc, axis=1)
    mod_cols = jnp.stack([_mm(a_ext, ada_w[l], bias=my_ada_b[l], a_fn=_silu, name=f"l{l}_ada") for l in range(depth)])
    mod_all = _exchange([mod_cols.reshape(depth * nrow, modc)], True, "gather_mod")[0]
    mod_all = mod_all.reshape(N_DEV, depth, nrow, modc).transpose(1, 2, 0, 3).reshape(depth, nrow, nmod, d)
    mod_mine = lax.dynamic_slice_in_dim(mod_all, me * bsz, bsz, axis=1)
    mod_ctx = jnp.broadcast_to(mod_all[:, N_DEV * bsz][:, None], mod_mine.shape)
    modts = jnp.stack([mod_ctx, mod_mine], axis=2)

    cfg['depth'] = depth
    cfg['wloc'] = {n: [w[n][l].astype(WIRE_DT) for l in range(depth)] for n in BIG}
    wts = {n: [None] * depth for n in BIG}
    for n, p in zip(MIX_W, _exchange([cfg['wloc'][n][0] for n in MIX_W], True, "gather_w0")):
        wts[n][0] = _gathered(n, p)

    fold = lambda t: jnp.moveaxis(t, 0, 1).reshape((2, depth * g) + t.shape[3:])
    s5_in = [fold(w[n]) for n in S5_NAMES[:7]] + [s5_d.reshape(depth * g, -1)]
    ops_all, s5_vjp = jax.vjp(_s5_operators, *s5_in)
    lay = lambda t, l, ax=0: lax.slice_in_dim(t, l * g, (l + 1) * g, axis=ax)
    cfg['s5_ops'] = [tuple(lay(o, l).astype(MXU_DT) for o in ops_all[:5]) for l in range(depth)]
    cfg['s5_ops_t'] = [tuple(jnp.swapaxes(lay(o, l), 1, 2).astype(MXU_DT) for o in ops_all[:5]) for l in range(depth)]
    cfg['s5_lm'] = [lay(ops_all[5], l, 1) for l in range(depth)]

    act = jnp.concatenate([ctx, x], axis=1)
    cfg['modts'] = modts
    u_in = None
    saved = []
    for l in range(depth):
        act, sv, u_in = _layer_fwd(l, act, modts[l], wts, sm, cfg, u_in)
        saved.append(sv)

    def loss_fn(tv, rv, mv, seg):
        err = tv[0] - tv[1]
        keep = jnp.where(seg == 1, 1.0, 0.0)
        return (err * (keep / d),), (_colsum(err * err) * keep,)
    o, a = _ew(loss_fn, [_full(act), (loss_target, 0, d, cfg['nct'])], outs=[(d, F32)], accs=[(1, d)],
               nctx_tiles=cfg['nct'], tr=tr, name="loss")
    dact = o[0]
    loss = lax.psum(0.5 * jnp.sum(a[0][:, 1]) / d, ("x", "y", "c"))

    grads = [None] * depth
    dmods = [None] * depth
    slots = cfg['slots'] = {n: [None] * depth for n in BIG}
    pending = []
    for l in reversed(range(depth)):
        dact, dmods[l], grads[l], pending = _layer_bwd(l, dact, saved[l], modts[l], wts, sm, cfg, pending)
    for (n, lay, _), p in zip(pending, _exchange([r[2] for r in pending], False, "scatter_last")):
        slots[n][lay] = p
    grad_x = dact[:, ctx_len:]

    dmod = jnp.stack(dmods)
    mine = jnp.concatenate([dmod[:, :, 1].reshape(depth, bsz, nmod * d),
                            dmod[:, :, 0].sum(1).reshape(depth, 1, nmod * d)], axis=1)
    got = _exchange([mine.reshape(depth * (bsz + 1), nmod * d)], True, "gather_dmod")[0]
    got = got.reshape(N_DEV, depth, bsz + 1, nmod * d)
    dmod_rows = jnp.concatenate([got[:, :, :bsz].transpose(1, 0, 2, 3).reshape(depth, N_DEV * bsz, nmod * d),
                                 got[:, :, bsz].transpose(1, 0, 2)], axis=1)
    dmod_cols = lax.dynamic_slice_in_dim(dmod_rows, me * modc, modc, axis=2)
    g_ada_w = jnp.stack([_mm(a_ext, dmod_cols[l], ta=True, a_fn=_silu, name=f"l{l}_ada_dw") for l in range(depth)])

    def rowsum_fn(tv, rv, mv, seg):
        return (), (_colsum(tv[0]),)
    g_ada_b = _ew(rowsum_fn, [_full(dmod_rows)], accs=[(1, nmod * d)], name="ada_db")[1][0][:, 1, 0]
    dsilu = 0.0
    for l in range(depth):
        dsilu = dsilu + _mm(dmod_cols[l, N_DEV * bsz:], ada_w[l], tb=True, name=f"l{l}_ada_dc").sum(0)
    sig = _sigmoid(c_ctx)
    g_c_ctx_part = dsilu * (sig * (1.0 + c_ctx * (1.0 - sig)))

    def both(name, l):
        return jnp.stack([grads[l][f'{name}{dd}'] for dd in range(2)])
    rep = {n: jnp.stack([grads[l][n] for l in range(depth)]) for n in
           ('ln1_g', 'ln1_b', 'conv_w', 'conv_b', 's5_glu_b', 'b_out', 'ln2_g', 'ln2_b', 'mlp_b1', 'mlp_b2')}
    for n in ('rg_lambda', 'rg_wa', 'rg_ba', 'rg_wi', 'rg_bi'):
        rep[n] = jnp.stack([both(n, l) for l in range(depth)])
    rep['c_ctx'] = g_c_ctx_part
    dops_all = tuple(jnp.concatenate([grads[l]['s5_dops'][i] for l in range(depth)], axis=1 if i == 5 else 0)
                     for i in range(6))
    s5g = s5_vjp(dops_all)
    for n, val in zip(S5_NAMES[:7], s5g):
        rep[n] = jnp.moveaxis(val.reshape((2, depth, g) + val.shape[2:]), 0, 1)
    rep['s5_d'] = s5g[7].reshape(depth, -1)
    small_names = [n for n in WEIGHTS if n in rep and math.prod(rep[n].shape) <= SMALL_PARAM]
    large_names = [n for n in WEIGHTS if n in rep and math.prod(rep[n].shape) > SMALL_PARAM]
    g_rep = {}
    bufs = []
    for names, dt in ((small_names, F32), (large_names, WIRE_DT)):
        buf = _pack([rep[n] for n in names], dt)
        bufs.append(jnp.pad(buf, ((0, -buf.shape[0] % (16 * N_DEV)), (0, 0))).reshape(N_DEV, -1, PACK_W))
    parts = [_sum_slots(p, f"sum_rep{i}").astype(p.dtype) for i, p in enumerate(_exchange(bufs, False, "scatter_rep"))]
    for names, full in zip((small_names, large_names), _exchange(parts, True, "gather_rep")):
        g_rep.update(zip(names, _unpack(full.reshape(-1, PACK_W).astype(F32), [rep[n].shape for n in names])))

    grad, delta, new_m, new_v = {}, {}, {}, {}
    for n in BIG:
        res = [_sum_adamw(slots[n][l], w[n][l], mom[n][l], vel[n][l], f"l{l}_adamw_{n}") for l in range(depth)]
        grad[n], delta[n], new_m[n], new_v[n] = [jnp.stack([r[i] for r in res]) for i in range(4)]
    grad['ada_w'] = g_ada_w
    rest = [n for n in WEIGHTS if n not in BIG and n != 'ada_w']
    for n in rest:
        if n == 'ada_b':
            grad[n] = g_ada_b
        elif n in CHAN_SHARDED:
            grad[n] = lax.dynamic_slice_in_dim(g_rep[n], me * w[n].shape[-1], w[n].shape[-1], axis=g_rep[n].ndim - 1)
        else:
            grad[n] = g_rep[n].reshape(w[n].shape)
    flat2 = lambda t: t.reshape(-1, t.shape[-1])
    for n in ('ada_w', 'rg_wa', 'rg_wi'):
        res = _adamw(flat2(w[n]), flat2(grad[n]), flat2(mom[n]), flat2(vel[n]), f"adamw_{n}")
        delta[n], new_m[n], new_v[n] = [r.reshape(w[n].shape) for r in res]
    for tag, names in (("small", [n for n in rest if math.prod(w[n].shape) <= SMALL_PARAM]),
                       ("s5", [n for n in rest if math.prod(w[n].shape) > SMALL_PARAM and n not in ('rg_wa', 'rg_wi')])):
        if not names:
            continue
        shapes = [w[n].shape for n in names]
        packed = [_pack([src[n] for n in names], F32) for src in (w, grad, mom, vel)]
        for dst, o in zip((delta, new_m, new_v), _adamw(*packed, name=f"adamw_{tag}")):
            dst.update(zip(names, _unpack(o, shapes)))
    return (loss, grad_x, *[grad[n] for n in WEIGHTS], *[delta[n] for n in WEIGHTS], *[new_m[n] for n in WEIGHTS],
            *[new_v[n] for n in WEIGHTS])
```

```python
import functools
import math

import jax
import jax.numpy as jnp
from jax import lax
from jax.experimental import pallas as pl
from jax.experimental.pallas import tpu as pltpu

F32 = jnp.float32
MXU_DT = jnp.bfloat16
ACT_DT = jnp.bfloat16
WIRE_DT = jnp.bfloat16

N_DEV = 8
GRID_W = 64
RG_C = 8.0
LN_EPS = 1e-5
S5_T = 16
S5_GB = 8
ROW_TILE = 256
DGRAD_KBLOCKS = 4
FWD_TK = 2048
VMEM_LIMIT = 56 * 1024 * 1024
PACK_W = 1024
SMALL_PARAM = 65536
WGRAD_TK = 2176

ADAM_LR = 0.001
ADAM_B1 = 0.9
ADAM_B2 = 0.999
ADAM_EPS = 1e-08
ADAM_WD = 0.01
ADAM_STEP = 10

WEIGHTS = ['c_ctx', 'ada_w', 'ada_b', 'ln1_g', 'ln1_b', 'w_in', 'conv_w', 'conv_b', 'rg_lambda', 'rg_wa', 'rg_ba',
           'rg_wi', 'rg_bi', 's5_a_re', 's5_a_im', 's5_log_dt', 's5_b_re', 's5_b_im', 's5_c_re', 's5_c_im', 's5_d',
           's5_glu_w', 's5_glu_b', 'w_out', 'b_out', 'ln2_g', 'ln2_b', 'mlp_w1', 'mlp_b1', 'mlp_w2', 'mlp_b2']
COL_SHARDED = ('w_in', 'mlp_w1')
MIX_W = ('w_in', 's5_glu_w', 'w_out')
MLP_W = ('mlp_w1', 'mlp_w2')
BIG = MIX_W + MLP_W
CHAN_SHARDED = ('conv_w', 'rg_lambda', 'rg_ba', 'rg_bi')
S5_NAMES = ('s5_a_re', 's5_a_im', 's5_log_dt', 's5_b_re', 's5_b_im', 's5_c_re', 's5_c_im', 's5_d')
MESH = pl.DeviceIdType.MESH


def _tile(n, pref, align):
    t = (min(pref, n) // align) * align
    while t >= align:
        if n % t == 0:
            return t
        t -= align
    return n


def _params(sem):
    return pltpu.CompilerParams(dimension_semantics=sem, vmem_limit_bytes=VMEM_LIMIT)


def _sigmoid(v):
    return 0.5 * jnp.tanh(0.5 * v) + 0.5


def _silu(v):
    return v * _sigmoid(v)


_GELU_K = math.sqrt(2.0 / math.pi)


def _gelu(v):
    return 0.5 * v * (1.0 + jnp.tanh(_GELU_K * (v + 0.044715 * v * v * v)))


def _gelu_grad(v):
    th = jnp.tanh(_GELU_K * (v + 0.044715 * v * v * v))
    return 0.5 * (1.0 + th) + 0.5 * v * (1.0 - th * th) * _GELU_K * (1.0 + 3.0 * 0.044715 * v * v)


def _one_minus_sq(la, a):
    v = 2.0 * la
    series = -v * (1.0 + v * (0.5 + v * (1.0 / 6.0 + v * (1.0 / 24.0 + v * (1.0 / 120.0)))))
    return jnp.where(v > -0.1, series, 1.0 - a * a)


def _softplus(v):
    return jnp.maximum(v, 0.0) + jnp.log(1.0 + jnp.exp(-jnp.abs(v)))


def _dot(a, b):
    return jnp.dot(a.astype(MXU_DT), b.astype(MXU_DT), preferred_element_type=F32)


def _dot_tn(a, b):
    return lax.dot_general(a.astype(MXU_DT), b.astype(MXU_DT), (((0,), (0,)), ((), ())), preferred_element_type=F32)


def _dot_nt(a, b):
    return lax.dot_general(a.astype(MXU_DT), b.astype(MXU_DT), (((1,), (1,)), ((), ())), preferred_element_type=F32)


def _mm(a, b, *, ta=False, tb=False, b_blocked=False, out_blocks=0, bias=None, a_fn=None, epi=None, extras=(),
        colsum=False, out_t=False, kblocks=1, out_dtype=F32, name, tm=1088, tn=1024, tk=1024):
    if ta:
        kdim, m = a.shape
    else:
        m, kdim = a.shape
    bcol = b.shape[2] if b_blocked else None
    blog = (b.shape[1], b.shape[0] * b.shape[2]) if b_blocked else b.shape
    if tb:
        n, kb = blog
    else:
        kb, n = blog
    assert kdim == kb, (a.shape, b.shape, ta, tb)
    assert not (ta and tb)
    tm = _tile(m, tm, 128 if ta else 16)
    n_lim = bcol if (b_blocked and not tb) else (n // out_blocks if out_blocks else n)
    tn = _tile(n_lim, tn, 128)
    tk = _tile(bcol if (b_blocked and tb) else kdim, tk, 16 if ta else 128)
    nk = kdim // tk
    a_spec = pl.BlockSpec((tk, tm), lambda i, j, k: (k, i)) if ta else pl.BlockSpec((tm, tk), lambda i, j, k: (i, k))
    if kblocks > 1:
        assert b_blocked and tb and b.shape[0] % kblocks == 0
        tk = bcol
        nk = b.shape[0] // kblocks
        a_spec = pl.BlockSpec((tm, kblocks * bcol), lambda i, j, k: (i, k))
        b_spec = pl.BlockSpec((kblocks, tn, bcol), lambda i, j, k: (k, j, 0))
    elif not b_blocked:
        b_spec = pl.BlockSpec((tn, tk), lambda i, j, k: (j, k)) if tb else pl.BlockSpec((tk, tn), lambda i, j, k: (k, j))
    elif tb:
        qk = bcol // tk
        b_spec = pl.BlockSpec((None, tn, tk), lambda i, j, k: (k // qk, j, k % qk))
    else:
        qn = bcol // tn
        b_spec = pl.BlockSpec((None, tk, tn), lambda i, j, k: (j // qn, k, j % qn))
    in_specs = [a_spec, b_spec]
    args = [a, b]
    has_bias = bias is not None
    if has_bias:
        in_specs.append(pl.BlockSpec((1, tn), lambda i, j, k: (0, j)))
        args.append(bias.reshape(1, n).astype(F32))
    for e in extras:
        assert e.shape == (m, n), (e.shape, m, n)
        in_specs.append(pl.BlockSpec((tm, tn), lambda i, j, k: (i, j)))
        args.append(e)
    nex = len(extras)
    dn = (((0 if ta else 1,), (1 if tb else 0,)), ((), ()))
    if out_blocks:
        qo = (n // out_blocks) // tn
        out_shape = jax.ShapeDtypeStruct((out_blocks, m, n // out_blocks), out_dtype)
        out_spec = pl.BlockSpec((None, tm, tn), lambda i, j, k: (j // qo, i, j % qo))
    else:
        out_shape = jax.ShapeDtypeStruct((m, n), out_dtype)
        out_spec = pl.BlockSpec((tm, tn), lambda i, j, k: (i, j))
    out_shapes, out_specs = [out_shape], [out_spec]
    grid = (m // tm, n // tn, nk)
    assert not (colsum and out_t)
    if out_t:
        out_shapes.append(jax.ShapeDtypeStruct((n, m), out_dtype))
        out_specs.append(pl.BlockSpec((tn, tm), lambda i, j, k: (j, i)))
    if colsum:
        out_shapes.append(jax.ShapeDtypeStruct((1, n), F32))
        out_specs.append(pl.BlockSpec((1, tn), lambda i, j, k: (0, j)))
        swap = lambda sp: pl.BlockSpec(sp.block_shape, lambda j, i, k, f=sp.index_map: f(i, j, k))
        in_specs = [swap(sp) for sp in in_specs]
        out_specs = [swap(sp) for sp in out_specs]
        grid = (n // tn, m // tm, nk)
    nout = len(out_shapes)

    def body(*refs):
        a_ref, b_ref = refs[0], refs[1]
        row_tile = pl.program_id(1)
        pos = 2
        bias_ref = refs[pos] if has_bias else None
        pos += int(has_bias)
        ex_refs = refs[pos:pos + nex]
        o_ref = refs[pos + nex]
        av = a_ref[...]
        if a_fn is not None:
            av = a_fn(av.astype(F32))
        av = av.astype(MXU_DT)
        if kblocks > 1:
            part = lax.dot_general(av[:, :tk], b_ref[0].astype(MXU_DT), dn, preferred_element_type=F32)
            for q in range(1, kblocks):
                part = part + lax.dot_general(av[:, q * tk:(q + 1) * tk], b_ref[q].astype(MXU_DT), dn,
                                              preferred_element_type=F32)
        else:
            part = lax.dot_general(av, b_ref[...].astype(MXU_DT), dn, preferred_element_type=F32)

        def finish(r):
            if has_bias:
                r = r + bias_ref[...]
            if epi is not None:
                r = epi(r, *[e[...] for e in ex_refs])
            o_ref[...] = r.astype(out_dtype)
            if out_t:
                refs[pos + nex + 1][...] = r.T.astype(out_dtype)
            if colsum:
                cs_ref = refs[pos + nex + 1]

                @pl.when(row_tile == 0)
                def _():
                    cs_ref[...] = _colsum(r)

                @pl.when(row_tile > 0)
                def _():
                    cs_ref[...] += _colsum(r)

        if nk == 1:
            finish(part)
            return
        acc_ref = refs[pos + nex + nout]
        k = pl.program_id(2)

        @pl.when(k == 0)
        def _():
            acc_ref[...] = part

        @pl.when(k > 0)
        def _():
            acc_ref[...] += part

        @pl.when(k == nk - 1)
        def _():
            finish(acc_ref[...])

    res = pl.pallas_call(
        body, out_shape=tuple(out_shapes), grid=grid, in_specs=in_specs, out_specs=tuple(out_specs),
        scratch_shapes=[pltpu.VMEM((tm, tn), F32)] if nk > 1 else [],
        compiler_params=_params(("parallel", "arbitrary" if colsum else "parallel", "arbitrary")), name=name)(*args)
    return res if (colsum or out_t) else res[0]


def _ew(fn, tiles, rows=(), mods=(), outs=(), accs=(), *, name, nctx_tiles=0, tr=None, transposed=False):
    bsz, tlen = tiles[0][0].shape[0], tiles[0][0].shape[1]
    if tr is None:
        tr = _tile(tlen, ROW_TILE, 8)
    nt = tlen // tr
    in_specs, args = [], []
    for arr, cb, width, toff in tiles:
        in_specs.append(pl.BlockSpec((1, tr, width), functools.partial(
            lambda b, t, cb, toff: (b, jnp.maximum(t - toff, 0), cb), cb=cb, toff=toff)))
        args.append(arr)
    for r in rows:
        in_specs.append(pl.BlockSpec(r.shape, lambda b, t: (0, 0)))
        args.append(r)

    def seg_of(t):
        return jnp.where(t >= nctx_tiles, 1, 0)

    for mo in mods:
        in_specs.append(pl.BlockSpec((1, 1) + mo.shape[2:], lambda b, t: (b, seg_of(t), 0, 0)))
        args.append(mo)
    out_shape, out_specs = [], []
    for width, dt in outs:
        out_shape.append(jax.ShapeDtypeStruct((bsz, tlen, width), dt))
        out_specs.append(pl.BlockSpec((1, tr, width), lambda b, t: (b, t, 0)))
    for kk, cc in accs:
        out_shape.append(jax.ShapeDtypeStruct((bsz, 2, kk, cc), F32))
        out_specs.append(pl.BlockSpec((1, 1, kk, cc), lambda b, t: (b, seg_of(t), 0, 0)))
    nti, nr, nm, no, na = len(tiles), len(rows), len(mods), len(outs), len(accs)
    if transposed:
        out_shape.append(jax.ShapeDtypeStruct((outs[0][0], bsz * tlen), outs[0][1]))
        out_specs.append(pl.BlockSpec((outs[0][0], tr), lambda b, t: (0, b * nt + t)))

    def body(*refs):
        t = pl.program_id(1)
        tv = [r[0] for r in refs[:nti]]
        rv = [r[...] for r in refs[nti:nti + nr]]
        mv = [r[0, 0] for r in refs[nti + nr:nti + nr + nm]]
        o_refs = refs[nti + nr + nm:nti + nr + nm + no]
        a_refs = refs[nti + nr + nm + no:nti + nr + nm + no + na]
        seg = seg_of(t)
        ov, av = fn(tv, rv, mv, seg)
        for r, v in zip(o_refs, ov):
            r[0] = v.astype(r.dtype)
        if transposed:
            refs[-1][...] = ov[0].astype(F32).T.astype(refs[-1].dtype)
        if na:
            @pl.when((t == 0) | (t == nctx_tiles))
            def _():
                for r in a_refs:
                    r[...] = jnp.zeros_like(r)

            for r, v in zip(a_refs, av):
                r[0, 0] += v

    res = pl.pallas_call(
        body, out_shape=tuple(out_shape), grid=(bsz, nt), in_specs=in_specs, out_specs=tuple(out_specs),
        compiler_params=_params(("arbitrary", "arbitrary")), name=name)(*args)
    return (res[:no], res[no:no + na], res[-1]) if transposed else (res[:no], res[no:])


def _full(arr):
    return (arr, 0, arr.shape[-1], 0)


def _colsum(v):
    return jnp.sum(v, axis=0, keepdims=True)


def _shifted(x, prev8, next8, first, last, k):
    tr = x.shape[0]
    rid = lax.broadcasted_iota(jnp.int32, x.shape, 0)
    keep_prev = jnp.where(first, 0.0, 1.0)
    keep_next = jnp.where(last, 0.0, 1.0)
    if k == -1:
        return jnp.where(rid == 0, prev8[7:8] * keep_prev, pltpu.roll(x, 1, 0))
    if k == -2:
        r = pltpu.roll(x, 2, 0)
        r = jnp.where(rid == 1, prev8[7:8] * keep_prev, r)
        return jnp.where(rid == 0, prev8[6:7] * keep_prev, r)
    if k == 1:
        return jnp.where(rid == tr - 1, next8[0:1] * keep_next, pltpu.roll(x, tr - 1, 0))
    if k == 2:
        r = pltpu.roll(x, tr - 2, 0)
        r = jnp.where(rid == tr - 2, next8[0:1] * keep_next, r)
        return jnp.where(rid == tr - 1, next8[1:2] * keep_next, r)
    raise ValueError(k)


def _halo_specs(tr, width, cb, n8):
    cur = pl.BlockSpec((1, tr, width), lambda b, t: (b, t, cb))
    prev = pl.BlockSpec((1, 8, width), lambda b, t: (b, jnp.maximum(t * (tr // 8) - 1, 0), cb))
    nxt = pl.BlockSpec((1, 8, width), lambda b, t: (b, jnp.minimum((t + 1) * (tr // 8), n8 - 1), cb))
    return [cur, prev, nxt]


def _conv_fwd(z3, conv_w, conv_b, d, nctx_tiles, tr, name):
    bsz, tlen, _ = z3.shape
    nt = tlen // tr

    def body(x_ref, xp_ref, xn_ref, w_ref, b_ref, o_ref):
        t = pl.program_id(1)
        first = (t == 0) | (t == nctx_tiles)
        last = (t == nctx_tiles - 1) | (t == nt - 1)
        x, p8, n8 = x_ref[0], xp_ref[0], xn_ref[0]
        w = w_ref[...]
        o_ref[0] = (b_ref[...] + w[0:1] * _shifted(x, p8, n8, first, last, -1) + w[1:2] * x
                    + w[2:3] * _shifted(x, p8, n8, first, last, 1) + w[3:4] * _shifted(x, p8, n8, first, last, 2))

    return pl.pallas_call(
        body, out_shape=jax.ShapeDtypeStruct((bsz, tlen, d), F32), grid=(bsz, nt),
        in_specs=_halo_specs(tr, d, 0, tlen // 8) + [pl.BlockSpec((4, d), lambda b, t: (0, 0)),
                                                      pl.BlockSpec((1, d), lambda b, t: (0, 0))],
        out_specs=pl.BlockSpec((1, tr, d), lambda b, t: (b, t, 0)),
        compiler_params=_params(("parallel", "parallel")), name=name)(z3, z3, z3, conv_w, conv_b.reshape(1, d))


def _conv_bwd(dxc, z3, conv_w, d, nctx_tiles, tr, name):
    bsz, tlen, _ = dxc.shape
    nt = tlen // tr

    def body(g_ref, gp_ref, gn_ref, x_ref, xp_ref, xn_ref, w_ref, o_ref, acc_ref):
        t = pl.program_id(1)
        first = (t == 0) | (t == nctx_tiles)
        last = (t == nctx_tiles - 1) | (t == nt - 1)
        g, gp, gn = g_ref[0], gp_ref[0], gn_ref[0]
        x, xp, xn = x_ref[0], xp_ref[0], xn_ref[0]
        w = w_ref[...]
        o_ref[0] = (w[0:1] * _shifted(g, gp, gn, first, last, 1) + w[1:2] * g
                    + w[2:3] * _shifted(g, gp, gn, first, last, -1) + w[3:4] * _shifted(g, gp, gn, first, last, -2))

        @pl.when(t == 0)
        def _():
            acc_ref[...] = jnp.zeros_like(acc_ref)

        acc_ref[0, 0:1] += _colsum(g * _shifted(x, xp, xn, first, last, -1))
        acc_ref[0, 1:2] += _colsum(g * x)
        acc_ref[0, 2:3] += _colsum(g * _shifted(x, xp, xn, first, last, 1))
        acc_ref[0, 3:4] += _colsum(g * _shifted(x, xp, xn, first, last, 2))
        acc_ref[0, 4:5] += _colsum(g)

    return pl.pallas_call(
        body, out_shape=(jax.ShapeDtypeStruct((bsz, tlen, d), F32), jax.ShapeDtypeStruct((bsz, 8, d), F32)),
        grid=(bsz, nt),
        in_specs=_halo_specs(tr, d, 0, tlen // 8) + _halo_specs(tr, d, 0, tlen // 8)
        + [pl.BlockSpec((4, d), lambda b, t: (0, 0))],
        out_specs=(pl.BlockSpec((1, tr, d), lambda b, t: (b, t, 0)), pl.BlockSpec((1, 8, d), lambda b, t: (b, 0, 0))),
        compiler_params=_params(("arbitrary", "arbitrary")), name=name)(dxc, dxc, dxc, z3, z3, z3, conv_w)


def _rg_gates(x, lam, wa_ref, ba, wi_ref, bi, nh, hd):
    sp = _softplus(-lam)
    prs, pis = [], []
    for h in range(nh):
        xh = x[:, h * hd:(h + 1) * hd]
        prs.append(_dot(xh, wa_ref[h]))
        pis.append(_dot(xh, wi_ref[h]))
    r = 1.0 / (1.0 + jnp.exp(-(jnp.concatenate(prs, axis=1) + ba)))
    i = _sigmoid(jnp.concatenate(pis, axis=1) + bi)
    la = -RG_C * sp * r
    a = jnp.exp(la)
    m2 = _one_minus_sq(la, a)
    return sp, r, i, a, jnp.sqrt(m2), lax.rsqrt(m2)


def _scan_tile(t, nctx_tiles, nt, reverse):
    if not reverse:
        return t
    return jnp.where(t < nctx_tiles, nctx_tiles - 1 - t, nt - 1 - (t - nctx_tiles))


def _unscan_tile(t, nctx_tiles, nt, reverse):
    if not reverse:
        return nt - 1 - t
    return jnp.where(t < nt - nctx_tiles, nctx_tiles + t, t - (nt - nctx_tiles))


def _rg_param_specs(d, nh, hd):
    vec = pl.BlockSpec((1, d), lambda b, t: (0, 0))
    mat = pl.BlockSpec((nh, hd, hd), lambda b, t: (0, 0, 0))
    return [vec, mat, vec, mat, vec]


def _ride_split(refs, n_in, n_ride, n_out, n_scratch):
    pos = [n_in, n_in + n_ride, n_in + n_ride + n_out, n_in + 2 * n_ride + n_out, n_in + 2 * n_ride + n_out + n_scratch]
    return (refs[:pos[0]], refs[pos[0]:pos[1]], refs[pos[1]:pos[2]], refs[pos[2]:pos[3]], refs[pos[3]:pos[4]],
            refs[pos[4]:])


def _rg_fwd(xc, lam, wa, ba, wi, bi, reverse, nctx_tiles, tr, name, ride=None):
    bsz, tlen, d = xc.shape
    nh, hd = wa.shape[0], wa.shape[1]
    nt = tlen // tr
    tmap = lambda b, t: (b, _scan_tile(t, nctx_tiles, nt, reverse), 0)
    rarrs, rgather = ride if ride else ([], True)
    nr = len(rarrs)

    def body(*refs):
        (x_ref, lam_ref, wa_ref, ba_ref, wi_ref, bi_ref), rin, (h_ref,), rout, (a_scr, b_scr, carry), sems = _ride_split(
            refs, 6, nr, 1, 3)
        first = (pl.program_id(0) == 0) & (pl.program_id(1) == 0)
        last = (pl.program_id(0) == bsz - 1) & (pl.program_id(1) == nt - 1)
        if nr:
            @pl.when(first)
            def _():
                _exchange_start(rin, rout, sems, rgather)

        @pl.when(pl.program_id(1) == 0)
        def _():
            carry[...] = jnp.zeros_like(carry)

        x = x_ref[0]
        _, _, i, a, mult, _ = _rg_gates(x, lam_ref[...], wa_ref, ba_ref[...], wi_ref, bi_ref[...], nh, hd)
        a_scr[...] = a
        b_scr[...] = mult * (i * x)

        def blk(j, h):
            for r in range(8):
                row = (tr - 1 - (j * 8 + r)) if reverse else (j * 8 + r)
                h = a_scr[pl.ds(row, 1), :] * h + b_scr[pl.ds(row, 1), :]
                h_ref[0, pl.ds(row, 1), :] = h
            return h

        carry[0:1, :] = lax.fori_loop(0, tr // 8, blk, carry[0:1, :])
        if nr:
            @pl.when(last)
            def _():
                _exchange_finish(rin, rout, sems, rgather)

    res = pl.pallas_call(
        body, out_shape=(jax.ShapeDtypeStruct((bsz, tlen, d), F32),) + _exchange_out_shapes(rarrs), grid=(bsz, nt),
        in_specs=[pl.BlockSpec((1, tr, d), tmap)] + _rg_param_specs(d, nh, hd) + [_ANY] * nr,
        out_specs=(pl.BlockSpec((1, tr, d), tmap),) + (_ANY,) * nr,
        scratch_shapes=[pltpu.VMEM((tr, d), F32), pltpu.VMEM((tr, d), F32), pltpu.VMEM((8, d), F32)]
        + (_exchange_sems(nr) if nr else []),
        compiler_params=_params(("arbitrary", "arbitrary")), name=name)(
            xc, lam.reshape(1, d), wa, ba.reshape(1, d), wi, bi.reshape(1, d), *rarrs)
    return res[0], list(res[1:])


def _rg_bwd(xc, h, dmix, z3, lam, wa, ba, wi, bi, addend, reverse, nctx_tiles, tr, name, ride=None):
    bsz, tlen, d = xc.shape
    nh, hd = wa.shape[0], wa.shape[1]
    nt = tlen // tr
    r8 = tr // 8

    def tile_of(t):
        return _unscan_tile(t, nctx_tiles, nt, reverse)

    tmap = lambda b, t: (b, tile_of(t), 0)
    gmap = lambda b, t: (b, tile_of(t), 1)

    def halo_map(b, t):
        tt = tile_of(t)
        if not reverse:
            return (b, jnp.maximum(tt * r8 - 1, 0), 0)
        return (b, jnp.where(tt == nt - 1, 0, (tt + 1) * r8), 0)

    has_add = addend is not None
    rarrs, rgather = ride if ride else ([], True)
    nr = len(rarrs)

    def body(*refs):
        ins, rin, (dx_ref, dwa_ref, dwi_ref, dv_ref), rout, (a_scr, g_scr, dh_scr, carry), sems = _ride_split(
            refs, 10 + int(has_add), nr, 4, 4)
        x_ref, h_ref, halo_ref, dr_ref, z_ref, lam_ref, wa_ref, ba_ref, wi_ref, bi_ref = ins[:10]
        add_ref = ins[10] if has_add else None
        b = pl.program_id(0)
        t = pl.program_id(1)
        tt = tile_of(t)
        if nr:
            @pl.when((b == 0) & (t == 0))
            def _():
                _exchange_start(rin, rout, sems, rgather)

        @pl.when((b == 0) & (t == 0))
        def _():
            dwa_ref[...] = jnp.zeros_like(dwa_ref)
            dwi_ref[...] = jnp.zeros_like(dwi_ref)
            dv_ref[...] = jnp.zeros_like(dv_ref)

        @pl.when(t == 0)
        def _():
            carry[...] = jnp.zeros_like(carry)

        x = x_ref[0]
        sp, r, i, a, mult, inv_mult = _rg_gates(x, lam_ref[...], wa_ref, ba_ref[...], wi_ref, bi_ref[...], nh, hd)
        a_scr[...] = a
        dh_scr[...] = dr_ref[0] * _gelu(z_ref[0])

        def blk(j, cc):
            for rr in range(8):
                row = (j * 8 + rr) if reverse else (tr - 1 - (j * 8 + rr))
                g = dh_scr[pl.ds(row, 1), :] + cc
                g_scr[pl.ds(row, 1), :] = g
                cc = a_scr[pl.ds(row, 1), :] * g
            return cc

        carry[0:1, :] = lax.fori_loop(0, r8, blk, carry[0:1, :])
        g = g_scr[...]
        hv = h_ref[0]
        rid = lax.broadcasted_iota(jnp.int32, hv.shape, 0)
        if not reverse:
            valid = jnp.where(tt > 0, 1.0, 0.0)
            hprev = jnp.where(rid == 0, halo_ref[0][7:8] * valid, pltpu.roll(hv, 1, 0))
        else:
            valid = jnp.where(tt == nctx_tiles - 1, 0.0, 1.0)
            hprev = jnp.where(rid == tr - 1, halo_ref[0][0:1] * valid, pltpu.roll(hv, tr - 1, 0))
        dla = g * hprev * a - g * (i * x) * (a * a) * inv_mult
        dpr = dla * (-RG_C * sp) * r * (1.0 - r)
        dpi = g * mult * x * i * (1.0 - i)
        dx = g * mult * i
        dxs = []
        for hh in range(nh):
            sl = slice(hh * hd, (hh + 1) * hd)
            dxs.append(_dot_nt(dpr[:, sl], wa_ref[hh]) + _dot_nt(dpi[:, sl], wi_ref[hh]))
            dwa_ref[hh] += _dot_tn(x[:, sl], dpr[:, sl])
            dwi_ref[hh] += _dot_tn(x[:, sl], dpi[:, sl])
        dx = dx + jnp.concatenate(dxs, axis=1)
        if has_add:
            dx = dx + add_ref[0]
        dx_ref[0] = dx
        dv_ref[0:1] += _colsum(dla * (-RG_C * r))
        dv_ref[1:2] += _colsum(dpr)
        dv_ref[2:3] += _colsum(dpi)
        if nr:
            @pl.when((b == bsz - 1) & (t == nt - 1))
            def _():
                _exchange_finish(rin, rout, sems, rgather)

    in_specs = [pl.BlockSpec((1, tr, d), tmap), pl.BlockSpec((1, tr, d), tmap), pl.BlockSpec((1, 8, d), halo_map),
                pl.BlockSpec((1, tr, d), tmap), pl.BlockSpec((1, tr, d), gmap)] + _rg_param_specs(d, nh, hd)
    args = [xc, h, h, dmix, z3, lam.reshape(1, d), wa, ba.reshape(1, d), wi, bi.reshape(1, d)]
    if has_add:
        in_specs.append(pl.BlockSpec((1, tr, d), tmap))
        args.append(addend)
    mat = pl.BlockSpec((nh, hd, hd), lambda b, t: (0, 0, 0))
    res = pl.pallas_call(
        body, out_shape=(jax.ShapeDtypeStruct((bsz, tlen, d), F32), jax.ShapeDtypeStruct((nh, hd, hd), F32),
                         jax.ShapeDtypeStruct((nh, hd, hd), F32), jax.ShapeDtypeStruct((8, d), F32))
        + _exchange_out_shapes(rarrs),
        grid=(bsz, nt), in_specs=in_specs + [_ANY] * nr,
        out_specs=(pl.BlockSpec((1, tr, d), tmap), mat, mat, pl.BlockSpec((8, d), lambda b, t: (0, 0))) + (_ANY,) * nr,
        scratch_shapes=[pltpu.VMEM((tr, d), F32), pltpu.VMEM((tr, d), F32), pltpu.VMEM((tr, d), F32),
                        pltpu.VMEM((8, d), F32)] + (_exchange_sems(nr) if nr else []),
        compiler_params=_params(("arbitrary", "arbitrary")), name=name)(*args, *rarrs)
    return res[0], res[1], res[2], res[3], list(res[4:])


def _s5_operators(a_re, a_im, log_dt, b_re, b_im, c_re, c_im, d_skip):
    tt = S5_T
    g, h = d_skip.shape
    th = tt * h
    dt = jnp.exp(log_dt)[..., None]
    xr, xi = a_re * dt, a_im * dt
    taus = jnp.arange(tt + 1, dtype=F32)[None, None, :, None]
    mag = jnp.exp(xr[:, :, None, :] * taus)
    pw_re, pw_im = mag * jnp.cos(xi[:, :, None, :] * taus), mag * jnp.sin(xi[:, :, None, :] * taus)
    nr, ni = pw_re[:, :, 1] - 1.0, pw_im[:, :, 1]
    den = a_re * a_re + a_im * a_im
    cf_re, cf_im = (nr * a_re + ni * a_im) / den, (ni * a_re - nr * a_im) / den
    bt_re, bt_im = jnp.swapaxes(b_re, 2, 3), jnp.swapaxes(b_im, 2, 3)
    bb_re = cf_re[:, :, None] * bt_re - cf_im[:, :, None] * bt_im
    bb_im = cf_re[:, :, None] * bt_im + cf_im[:, :, None] * bt_re

    def outer(p_re, p_im, q_re, q_im):
        pr, pi = p_re[:, :, None, :], p_im[:, :, None, :]
        qr, qi = q_re[:, None], q_im[:, None]
        return (pr * qr - pi * qi).reshape(g, -1, pr.shape[-1]), (pr * qi + pi * qr).reshape(g, -1, pr.shape[-1])

    ops, kerns = [], []
    for dd in range(2):
        e_re, e_im = outer(pw_re[dd], pw_im[dd], c_re[dd], c_im[dd])
        kerns.append(jnp.einsum('gic,gmc->gim', jnp.concatenate([bb_re[dd], -bb_im[dd]], -1),
                                jnp.concatenate([e_re[:, :th], e_im[:, :th]], -1), precision=lax.Precision.HIGHEST))
        v_re, v_im = e_re[:, h:].reshape(g, tt, h, -1), e_im[:, h:].reshape(g, tt, h, -1)
        if dd == 1:
            v_re, v_im = v_re[:, ::-1], v_im[:, ::-1]
        vt = jnp.concatenate([v_re, -v_im], -1).reshape(g, th, -1)
        pr, pi = pw_re[dd][:, :tt], pw_im[dd][:, :tt]
        if dd == 0:
            pr, pi = pr[:, ::-1], pi[:, ::-1]
        w_re, w_im = outer(pr, pi, bb_re[dd], bb_im[dd])
        ops.append((jnp.concatenate([w_re, w_im], -1), jnp.swapaxes(vt, 1, 2)))
    kb_rev = kerns[1].reshape(g, h, tt, h)[:, :, ::-1].reshape(g, h, th)
    blocks = []
    for s in range(tt):
        fwd = jnp.pad(kerns[0][:, :, :th - s * h], ((0, 0), (0, 0), (s * h, 0)))
        bwd = jnp.pad(kb_rev[:, :, (tt - 1 - s) * h:], ((0, 0), (0, 0), (0, (tt - 1 - s) * h)))
        blocks.append(fwd + bwd)
    tz = jnp.stack(blocks, axis=1).reshape(g, th, th)
    tz = tz + jnp.eye(th, dtype=F32)[None] * jnp.tile(d_skip, (1, tt))[:, None, :]
    rows = []
    for dd in range(2):
        re, im = pw_re[dd][:, tt], pw_im[dd][:, tt]
        rows += [jnp.concatenate([re, re], -1), jnp.concatenate([-im, im], -1)]
    return tz, ops[0][0], ops[1][0], ops[0][1], ops[1][1], jnp.stack(rows)


def _chunk_order(j, ncc, nc):
    return jnp.where(j < ncc, ncc - 1 - j, nc - 1 - (j - ncc))


def _s5_specs(nc, ops, lm):
    gb = S5_GB
    act = lambda n_chunks, width: pl.BlockSpec((1, gb * n_chunks, width), lambda i, b: (b, i, 0))
    opspecs = [pl.BlockSpec((gb,) + o.shape[1:], lambda i, b: (i, 0, 0)) for o in ops]
    lspec = pl.BlockSpec((4, gb, lm.shape[-1]), lambda i, b: (0, i, 0))
    return act, opspecs, lspec


def _s5_fwd(u, ops, lm, ncc, name, ride=None):
    bsz, gn, th = u.shape
    g = ops[0].shape[0]
    nc = gn // g
    gb = S5_GB
    p2 = ops[1].shape[-1]
    ph = p2 // 2
    rarrs, rgather = ride if ride else ([], True)
    nr = len(rarrs)

    def body(*refs):
        ((u_ref, tz_ref, wf_ref, wb_ref, vf_ref, vb_ref, l_ref), rin, (y_ref, hf_ref, hb_ref), rout,
         (sf, sfs, sb, sbs), sems) = _ride_split(refs, 7, nr, 3, 4)
        if nr:
            @pl.when((pl.program_id(0) == 0) & (pl.program_id(1) == 0))
            def _():
                _exchange_start(rin, rout, sems, rgather)

        a1f, a2f, a1b, a2b = l_ref[0], l_ref[1], l_ref[2], l_ref[3]
        for gi in range(gb):
            ug = u_ref[0, pl.ds(gi * nc, nc), :]
            s1 = _dot(ug, wf_ref[gi])
            s2 = _dot(ug, wb_ref[gi])
            sf[pl.ds(gi, nc, stride=gb), :] = s1
            sfs[pl.ds(gi, nc, stride=gb), :] = pltpu.roll(s1, ph, 1)
            sb[pl.ds(gi, nc, stride=gb), :] = s2
            sbs[pl.ds(gi, nc, stride=gb), :] = pltpu.roll(s2, ph, 1)

        def step(j, hs):
            hf, hfs, hb, hbs = hs
            rf = pl.ds(pl.multiple_of(j * gb, gb), gb)
            rb = pl.ds(pl.multiple_of(_chunk_order(j, ncc, nc) * gb, gb), gb)
            s1, s1s, s2, s2s = sf[rf, :], sfs[rf, :], sb[rb, :], sbs[rb, :]
            sf[rf, :] = hf
            sb[rb, :] = hb
            return (a1f * hf + a2f * hfs + s1, a1f * hfs - a2f * hf + s1s,
                    a1b * hb + a2b * hbs + s2, a1b * hbs - a2b * hb + s2s)

        zero = jnp.zeros((gb, p2), F32)
        lax.fori_loop(0, nc, step, (zero, zero, zero, zero))
        for gi in range(gb):
            rows = pl.ds(gi * nc, nc)
            hfg = sf[pl.ds(gi, nc, stride=gb), :]
            hbg = sb[pl.ds(gi, nc, stride=gb), :]
            hf_ref[0, rows, :] = hfg
            hb_ref[0, rows, :] = hbg
            y_ref[0, rows, :] = (_dot(u_ref[0, rows, :], tz_ref[gi]) + _dot(hfg, vf_ref[gi])
                                 + _dot(hbg, vb_ref[gi])).astype(y_ref.dtype)
        if nr:
            @pl.when((pl.program_id(0) == g // gb - 1) & (pl.program_id(1) == bsz - 1))
            def _():
                _exchange_finish(rin, rout, sems, rgather)

    act, opspecs, lspec = _s5_specs(nc, ops, lm)
    res = pl.pallas_call(
        body, out_shape=(jax.ShapeDtypeStruct(u.shape, ACT_DT), jax.ShapeDtypeStruct((bsz, gn, p2), F32),
                         jax.ShapeDtypeStruct((bsz, gn, p2), F32)) + _exchange_out_shapes(rarrs),
        grid=(g // gb, bsz), in_specs=[act(nc, th)] + opspecs + [lspec] + [_ANY] * nr,
        out_specs=(act(nc, th), act(nc, p2), act(nc, p2)) + (_ANY,) * nr,
        scratch_shapes=[pltpu.VMEM((gb * nc, p2), F32) for _ in range(4)] + (_exchange_sems(nr) if nr else []),
        compiler_params=_params(("arbitrary", "arbitrary")), name=name)(u, *ops, lm, *rarrs)
    return res[0], res[1], res[2], list(res[3:])


def _s5_bwd(dy, u, hf, hb, ops_t, lm, ncc, name):
    bsz, gn, th = u.shape
    g = ops_t[0].shape[0]
    nc = gn // g
    gb = S5_GB
    p2 = lm.shape[-1]
    ph = p2 // 2

    def body(dy_ref, u_ref, hf_ref, hb_ref, tzt_ref, wft_ref, wbt_ref, vft_ref, vbt_ref, l_ref,
             du_ref, dtz_ref, dwf_ref, dwb_ref, dvf_ref, dvb_ref, dl_ref, jf, jfs, jb, jbs, hfk, hbk):
        a1f, a2f, a1b, a2b = l_ref[0], l_ref[1], l_ref[2], l_ref[3]

        @pl.when(pl.program_id(1) == 0)
        def _():
            for r in (dtz_ref, dwf_ref, dwb_ref, dvf_ref, dvb_ref, dl_ref):
                r[...] = jnp.zeros_like(r)

        for gi in range(gb):
            rows = pl.ds(gi * nc, nc)
            dyg = dy_ref[0, rows, :]
            i1 = _dot(dyg, vft_ref[gi])
            i2 = _dot(dyg, vbt_ref[gi])
            jf[pl.ds(gi, nc, stride=gb), :] = i1
            jfs[pl.ds(gi, nc, stride=gb), :] = pltpu.roll(i1, ph, 1)
            jb[pl.ds(gi, nc, stride=gb), :] = i2
            jbs[pl.ds(gi, nc, stride=gb), :] = pltpu.roll(i2, ph, 1)
            hfk[pl.ds(gi, nc, stride=gb), :] = hf_ref[0, rows, :]
            hbk[pl.ds(gi, nc, stride=gb), :] = hb_ref[0, rows, :]

        def step(j, carry):
            qf, qfs, qb, qbs, d1f, d2f, d1b, d2b = carry
            rf = pl.ds(pl.multiple_of((nc - 1 - j) * gb, gb), gb)
            rb = pl.ds(pl.multiple_of(_chunk_order(nc - 1 - j, ncc, nc) * gb, gb), gb)
            i1, i1s, i2, i2s = jf[rf, :], jfs[rf, :], jb[rb, :], jbs[rb, :]
            h1, h2 = hfk[rf, :], hbk[rb, :]
            jf[rf, :] = qf
            jb[rb, :] = qb
            return (i1 + a1f * qf - a2f * qfs, i1s + a1f * qfs + a2f * qf,
                    i2 + a1b * qb - a2b * qbs, i2s + a1b * qbs + a2b * qb,
                    d1f + qf * h1, d2f + qfs * h1, d1b + qb * h2, d2b + qbs * h2)

        zero = jnp.zeros((gb, p2), F32)
        fin = lax.fori_loop(0, nc, step, (zero,) * 8)
        dl_ref[0] += fin[4]
        dl_ref[1] += pltpu.roll(fin[5], ph, 1)
        dl_ref[2] += fin[6]
        dl_ref[3] += pltpu.roll(fin[7], ph, 1)
        for gi in range(gb):
            rows = pl.ds(gi * nc, nc)
            dyg = dy_ref[0, rows, :]
            ug = u_ref[0, rows, :]
            dsf = jf[pl.ds(gi, nc, stride=gb), :]
            dsb = jb[pl.ds(gi, nc, stride=gb), :]
            du_ref[0, rows, :] = (_dot(dyg, tzt_ref[gi]) + _dot(dsf, wft_ref[gi]) + _dot(dsb, wbt_ref[gi])).astype(du_ref.dtype)
            dtz_ref[gi] += _dot_tn(ug, dyg)
            dwf_ref[gi] += _dot_tn(ug, dsf)
            dwb_ref[gi] += _dot_tn(ug, dsb)
            dvf_ref[gi] += _dot_tn(hf_ref[0, rows, :], dyg)
            dvb_ref[gi] += _dot_tn(hb_ref[0, rows, :], dyg)

    act, opspecs, lspec = _s5_specs(nc, ops_t, lm)
    gshape = lambda o: jax.ShapeDtypeStruct(o.shape[:1] + o.shape[1:][::-1], F32)
    gspec = lambda o: pl.BlockSpec((gb,) + o.shape[1:][::-1], lambda i, b: (i, 0, 0))
    res = pl.pallas_call(
        body, out_shape=tuple([jax.ShapeDtypeStruct(u.shape, ACT_DT)] + [gshape(o) for o in ops_t]
                              + [jax.ShapeDtypeStruct(lm.shape, F32)]),
        grid=(g // gb, bsz),
        in_specs=[act(nc, th), act(nc, th), act(nc, p2), act(nc, p2)] + opspecs + [lspec],
        out_specs=tuple([act(nc, th)] + [gspec(o) for o in ops_t] + [lspec]),
        scratch_shapes=[pltpu.VMEM((gb * nc, p2), F32) for _ in range(6)],
        compiler_params=_params(("parallel", "arbitrary")), name=name)(dy, u, hf, hb, *ops_t, lm)
    return res[0], tuple(res[1:])


def _to_chunks(s, ctx_len, g):
    bsz, tlen, d = s.shape
    h = d // g
    seq = tlen - ctx_len
    rows = seq // GRID_W
    cpart = s[:, :ctx_len].reshape(bsz, ctx_len // S5_T, S5_T, g, h).transpose(0, 3, 1, 2, 4)
    lpart = s[:, ctx_len:].reshape(bsz, rows, GRID_W, g, h).transpose(0, 3, 2, 1, 4)
    cpart = cpart.reshape(bsz, g, ctx_len // S5_T, S5_T * h)
    lpart = lpart.reshape(bsz, g, seq // S5_T, S5_T * h)
    return jnp.concatenate([cpart, lpart], axis=2).reshape(bsz, g * (tlen // S5_T), S5_T * h)


def _from_chunks(y, ctx_len, g):
    bsz, gn, th = y.shape
    nc = gn // g
    h = th // S5_T
    ncc = ctx_len // S5_T
    seq = (nc - ncc) * S5_T
    rows = seq // GRID_W
    y = y.reshape(bsz, g, nc, th)
    cpart = y[:, :, :ncc].reshape(bsz, g, ncc, S5_T, h).transpose(0, 2, 3, 1, 4).reshape(bsz, ctx_len, g * h)
    lpart = y[:, :, ncc:].reshape(bsz, g, GRID_W, rows, h).transpose(0, 3, 2, 1, 4).reshape(bsz, seq, g * h)
    return jnp.concatenate([cpart, lpart], axis=1)


def _me():
    return lax.axis_index("x"), lax.axis_index("y"), lax.axis_index("c")


def _peer(k):
    x, y, c = _me()
    px = (1 - x) if (k & 4) else x
    py = (1 - y) if (k & 2) else y
    pc = (1 - c) if (k & 1) else c
    return (px, py, pc), 4 * px + 2 * py + pc


def _exchange(arrs, gather, name):
    n = len(arrs)

    def body(*refs):
        _exchange_start(refs[:n], refs[n:2 * n], refs[2 * n:], gather)
        _exchange_finish(refs[:n], refs[n:2 * n], refs[2 * n:], gather)

    return pl.pallas_call(
        body, out_shape=_exchange_out_shapes(arrs), in_specs=[_ANY] * n, out_specs=tuple([_ANY] * n),
        scratch_shapes=_exchange_sems(n), name=name)(*arrs)


_ANY = pl.BlockSpec(memory_space=pl.ANY)


def _exchange_out_shapes(arrs):
    return tuple(jax.ShapeDtypeStruct((N_DEV,) + a.shape[-2:], a.dtype) for a in arrs)


def _exchange_sems(n):
    return [pltpu.SemaphoreType.DMA((n * (N_DEV - 1),)), pltpu.SemaphoreType.DMA((n * (N_DEV - 1),)),
            pltpu.SemaphoreType.DMA((n,))]


def _exchange_copies(x_refs, o_refs, sems, gather, with_recvs):
    send_sems, recv_sems, local_sems = sems
    n = len(x_refs)
    npeer = N_DEV - 1
    xi, yi, ci = _me()
    me = 4 * xi + 2 * yi + ci
    mine = [x if gather else x.at[me] for x in x_refs]
    local = [pltpu.make_async_copy(mine[i], o_refs[i].at[me], local_sems.at[i]) for i in range(n)]
    sends, recvs = [], []
    for k in range(1, N_DEV):
        dev, pid = _peer(k)
        for i in range(n):
            slot = i * npeer + k - 1
            sends.append(pltpu.make_async_remote_copy(
                src_ref=x_refs[i] if gather else x_refs[i].at[pid], dst_ref=o_refs[i].at[me],
                send_sem=send_sems.at[slot], recv_sem=recv_sems.at[slot], device_id=dev, device_id_type=MESH))
            if with_recvs:
                recvs.append(pltpu.make_async_remote_copy(
                    src_ref=mine[i], dst_ref=o_refs[i].at[pid], send_sem=send_sems.at[slot],
                    recv_sem=recv_sems.at[slot], device_id=dev, device_id_type=MESH))
    return local, sends, recvs


def _exchange_start(x_refs, o_refs, sems, gather):
    local, sends, _ = _exchange_copies(x_refs, o_refs, sems, gather, False)
    for cp in local + sends:
        cp.start()


def _exchange_finish(x_refs, o_refs, sems, gather):
    local, sends, recvs = _exchange_copies(x_refs, o_refs, sems, gather, True)
    for cp in recvs:
        cp.wait_recv()
    for cp in sends:
        cp.wait_send()
    for cp in local:
        cp.wait()


def _sum_slots(x3, name):
    _, r, cdim = x3.shape
    tr = _tile(r, 256, 16)

    def body(x_ref, o_ref):
        acc = x_ref[0].astype(F32)
        for s in range(1, N_DEV):
            acc = acc + x_ref[s].astype(F32)
        o_ref[...] = acc

    return pl.pallas_call(
        body, out_shape=jax.ShapeDtypeStruct((r, cdim), F32), grid=(r // tr,),
        in_specs=[pl.BlockSpec((N_DEV, tr, cdim), lambda i: (0, i, 0))], out_specs=pl.BlockSpec((tr, cdim), lambda i: (i, 0)),
        compiler_params=_params(("parallel",)), name=name)(x3)


def _pack(arrs, dtype, lead=0):
    flat = jnp.concatenate([a.reshape(a.shape[:lead] + (-1,)).astype(dtype) for a in arrs], axis=-1)
    n = flat.shape[-1]
    pad = -n % (PACK_W * 16)
    flat = jnp.pad(flat, [(0, 0)] * lead + [(0, pad)])
    return flat.reshape(flat.shape[:lead] + (-1, PACK_W))


def _unpack(buf, shapes, lead=0):
    flat = buf.reshape(buf.shape[:lead] + (-1,))
    out, off = [], 0
    for shp in shapes:
        n = math.prod(shp)
        out.append(flat[..., off:off + n].reshape(buf.shape[:lead] + tuple(shp)))
        off += n
    return out


def _adamw_math(wv, gv, m0, v0):
    m1 = ADAM_B1 * m0 + (1.0 - ADAM_B1) * gv
    v1 = ADAM_B2 * v0 + (1.0 - ADAM_B2) * (gv * gv)
    m_hat = m1 / (1.0 - ADAM_B1 ** ADAM_STEP)
    v_hat = v1 / (1.0 - ADAM_B2 ** ADAM_STEP)
    delta = -ADAM_LR * (m_hat / (jnp.sqrt(v_hat) + ADAM_EPS) + ADAM_WD * wv)
    return delta, m1, v1


def _adamw(w, g, m, v, name):
    def fn(tv, rv, mv, seg):
        return _adamw_math(*tv), ()

    outs, _ = _ew(fn, [_full(a[None]) for a in (w, g, m, v)], outs=[(w.shape[-1], F32)] * 3, name=name)
    return [o[0] for o in outs]


def _sum_adamw(slots, w, m, v, name):
    _, r, cdim = slots.shape
    tr = _tile(r, 128, 16)

    def body(s_ref, w_ref, m_ref, v_ref, g_ref, d_ref, mo_ref, vo_ref):
        gv = s_ref[0].astype(F32)
        for s in range(1, N_DEV):
            gv = gv + s_ref[s].astype(F32)
        g_ref[...] = gv
        d_ref[...], mo_ref[...], vo_ref[...] = _adamw_math(w_ref[...], gv, m_ref[...], v_ref[...])

    flat = pl.BlockSpec((tr, cdim), lambda i: (i, 0))
    return pl.pallas_call(
        body, out_shape=tuple(jax.ShapeDtypeStruct((r, cdim), F32) for _ in range(4)), grid=(r // tr,),
        in_specs=[pl.BlockSpec((N_DEV, tr, cdim), lambda i: (0, i, 0)), flat, flat, flat], out_specs=(flat,) * 4,
        compiler_params=_params(("parallel",)), name=name)(slots, w, m, v)


def _gathered(n, p):
    return p if n in COL_SHARDED else p.reshape(-1, p.shape[-1])


def _layer_fwd(l, xin, modt, wts, sm, cfg, u_in):
    bsz, tlen, d = xin.shape
    bt = bsz * tlen
    nct, tr, ctx_len, g = cfg['nct'], cfg['tr'], cfg['ctx_len'], cfg['g']
    ncc, nc = ctx_len // S5_T, tlen // S5_T
    alpha = cfg['alpha']
    nm = lambda s: f"l{l}_{s}"

    def modulate(xv, i_shift, i_scale, name):
        def fn(tv, rv, mv, seg):
            mo = mv[0]
            return (tv[0] * (1.0 + mo[i_scale:i_scale + 1]) + mo[i_shift:i_shift + 1],), ()
        o, _, ot = _ew(fn, [_full(xv)], mods=[modt], outs=[(d, ACT_DT)], nctx_tiles=nct, tr=tr, name=name, transposed=True)
        return o[0], ot

    def ln_fwd(xv, mv_, i_gate, gam, bet, name, nxt=None):
        def fn(tv, rv, mv, seg):
            z = alpha * tv[0] + mv[0][i_gate:i_gate + 1] * tv[1]
            mu = jnp.mean(z, axis=-1, keepdims=True)
            zc = z - mu
            var = jnp.mean(zc * zc, axis=-1, keepdims=True)
            y = zc * lax.rsqrt(var + LN_EPS) * rv[0] + rv[1]
            if nxt is None:
                return (y,), ()
            return (y * (1.0 + mv[1][nxt[2]:nxt[2] + 1]) + mv[1][nxt[1]:nxt[1] + 1], y), ()
        tiles, rws = [_full(xv), _full(mv_)], [gam.reshape(1, d), bet.reshape(1, d)]
        if nxt is None:
            return _ew(fn, tiles, rows=rws, mods=[modt], outs=[(d, F32)], nctx_tiles=nct, tr=tr, name=name)[0][0], None
        o, _, ot = _ew(fn, tiles, rows=rws, mods=[modt, nxt[0]], outs=[(d, ACT_DT), (d, F32)], nctx_tiles=nct, tr=tr,
                       name=name, transposed=True)
        return o[1], (o[0], ot)

    sv = {'x': xin}
    u, sv['uT'] = u_in if u_in is not None else modulate(xin, 0, 1, nm("mod1"))
    z3 = _mm(u.reshape(bt, d), wts['w_in'][l], b_blocked=True, name=nm("w_in")).reshape(bsz, tlen, 3 * d)
    sv['z3'] = z3
    xc = _conv_fwd(z3, sm['conv_w'][l], sm['conv_b'][l], d, nct, tr, nm("conv"))
    sv['xc'] = xc
    nxt = l + 1 < cfg['depth']
    riders = [[('mlp_w1', l)], [('mlp_w2', l)] + ([('s5_glu_w', l + 1), ('w_out', l + 1)] if nxt else []),
              [('w_in', l + 1)] if nxt else []]

    def ride_of(rs):
        return ([cfg['wloc'][n][lay] for n, lay in rs], True) if rs else None

    def landed(rs, got):
        for (n, lay), p in zip(rs, got):
            wts[n][lay] = _gathered(n, p)

    hs = []
    for dd in range(2):
        h, got = _rg_fwd(xc, sm['rg_lambda'][l, dd], sm['rg_wa'][l, dd], sm['rg_ba'][l, dd], sm['rg_wi'][l, dd],
                         sm['rg_bi'][l, dd], bool(dd), nct, tr, nm(f"rg_fwd{dd}"), ride=ride_of(riders[dd]))
        hs.append(h)
        landed(riders[dd], got)
    sv['hf'], sv['hb'] = hs

    def copy_fn(tv, rv, mv, seg):
        return (tv[0],), ()
    s5u = _to_chunks(_ew(copy_fn, [(z3, 2, d, 0)], outs=[(d, ACT_DT)], tr=tr, name=nm("s5_in"))[0][0], ctx_len, g)
    sv['s5u'] = s5u
    y, hf5, hb5, got = _s5_fwd(s5u, cfg['s5_ops'][l], cfg['s5_lm'][l], ncc, nm("s5_fwd"), ride=ride_of(riders[2]))
    landed(riders[2], got)
    sv['hf5'], sv['hb5'] = hf5, hb5
    ytok = _from_chunks(y, ctx_len, g)
    sv['ytok'] = ytok

    def gelu_fn(tv, rv, mv, seg):
        return (_gelu(tv[0].astype(F32)),), ()
    o, _, sv['gactT'] = _ew(gelu_fn, [_full(ytok)], outs=[(d, ACT_DT)], tr=tr, name=nm("s5_gelu"), transposed=True)
    gact = o[0]
    sv['gact'] = gact
    gpre = _mm(gact.reshape(bt, d), wts['s5_glu_w'][l], bias=sm['s5_glu_b'][l], name=nm("glu")).reshape(bsz, tlen, d)
    sv['gpre'] = gpre

    def mix_fn(tv, rv, mv, seg):
        rg = (tv[0] + tv[1]) * _gelu(tv[2])
        s5o = tv[3].astype(F32) * _sigmoid(tv[4])
        return (jnp.concatenate([rg, s5o], axis=1),), ()
    o, _, sv['mixinT'] = _ew(mix_fn, [_full(hs[0]), _full(hs[1]), (z3, 1, d, 0), _full(gact), _full(gpre)],
                             outs=[(2 * d, ACT_DT)], tr=tr, name=nm("mix"), transposed=True)
    mixin = o[0].reshape(bt, 2 * d)
    mo = _mm(mixin, wts['w_out'][l], bias=sm['b_out'][l], tk=FWD_TK, name=nm("w_out")).reshape(bsz, tlen, d)
    sv['mo'] = mo
    x1, (u2, sv['u2T']) = ln_fwd(xin, mo, 2, sm['ln1_g'][l], sm['ln1_b'][l], nm("ln1"), nxt=(modt, 3, 4))
    sv['x1'] = x1
    rl, sv['rlT'] = _mm(u2.reshape(bt, d), wts['mlp_w1'][l], b_blocked=True, bias=sm['mlp_b1'][l],
                        epi=lambda r: jnp.maximum(r, 0.0), out_t=True, out_dtype=ACT_DT, tm=2176, name=nm("mlp1"))
    sv['rl'] = rl
    fo = _mm(rl, wts['mlp_w2'][l], bias=sm['mlp_b2'][l], a_fn=lambda v: v * v, tm=544, tk=FWD_TK,
             name=nm("mlp2")).reshape(bsz, tlen, d)
    sv['fo'] = fo
    x2, u_next = ln_fwd(x1, fo, 5, sm['ln2_g'][l], sm['ln2_b'][l], nm("ln2"),
                        nxt=(cfg['modts'][l + 1], 0, 1) if nxt else None)
    return x2, sv, u_next


def _layer_bwd(l, dx2, sv, modt, wts, sm, cfg, pending):
    bsz, tlen, d = dx2.shape
    bt = bsz * tlen
    nct, tr, ctx_len, g = cfg['nct'], cfg['tr'], cfg['ctx_len'], cfg['g']
    ncc, nc = ctx_len // S5_T, tlen // S5_T
    alpha = cfg['alpha']
    nm = lambda s: f"l{l}_{s}"
    gr = {}

    def ln_bwd(xv, mv_, i_gate, gam, dy, name, through=None):
        def fn(tv, rv, mv, seg):
            xx, mm_, dyy = tv[:3]
            gate = mv[0][i_gate:i_gate + 1]
            z = alpha * xx + gate * mm_
            mu = jnp.mean(z, axis=-1, keepdims=True)
            zc = z - mu
            var = jnp.mean(zc * zc, axis=-1, keepdims=True)
            rstd = lax.rsqrt(var + LN_EPS)
            xhat = zc * rstd
            more = []
            if through is not None:
                duu = tv[3]
                dyy = dyy + duu * (1.0 + mv[0][through[1]:through[1] + 1])
                more = [_colsum(duu), _colsum(duu * (xhat * rv[0] + rv[1]))]
            dxh = dyy * rv[0]
            dz = rstd * (dxh - jnp.mean(dxh, axis=-1, keepdims=True) - xhat * jnp.mean(dxh * xhat, axis=-1, keepdims=True))
            dm = gate * dz
            acc = jnp.concatenate([_colsum(dz * mm_), _colsum(dyy * xhat), _colsum(dyy), _colsum(dm)] + more, axis=0)
            return (alpha * dz, dm), (acc,)
        tiles, rws = [_full(xv), _full(mv_), _full(dy)], [gam.reshape(1, d)]
        if through is not None:
            tiles.append(_full(through[0]))
            rws.append(through[2].reshape(1, d))
        o, a = _ew(fn, tiles, rows=rws, mods=[modt], outs=[(d, F32), (d, ACT_DT)],
                   accs=[(4 if through is None else 6, d)], nctx_tiles=nct, tr=tr, name=name)
        return o[0], o[1], a[0]

    def mod_bwd(du, xv, i_scale, addend, name):
        def fn(tv, rv, mv, seg):
            duu, xx, add = tv
            acc = jnp.concatenate([_colsum(duu), _colsum(duu * xx)], axis=0)
            return (add + duu * (1.0 + mv[0][i_scale:i_scale + 1]),), (acc,)
        o, a = _ew(fn, [_full(du), _full(xv), _full(addend)], mods=[modt], outs=[(d, F32)], accs=[(2, d)],
                   nctx_tiles=nct, tr=tr, name=name)
        return o[0], a[0]

    def row_blocks(gw):
        return gw.reshape(N_DEV, -1, gw.shape[-1])

    dx1a, dfo, acc2 = ln_bwd(sv['x1'], sv['fo'], 5, sm['ln2_g'][l], dx2, nm("ln2_bwd"))
    dfo2 = dfo.reshape(bt, d)
    dhp, db1 = _mm(dfo2, wts['mlp_w2'][l], tb=True, epi=lambda r, rl: r * (2.0 * rl.astype(F32)), extras=[sv['rl']],
                   colsum=True, out_dtype=ACT_DT, name=nm("mlp2_dx"))
    gr['mlp_b1'] = db1[0]
    gr['mlp_w2'] = row_blocks(_mm(sv['rlT'], dfo2, a_fn=lambda v: v * v, out_dtype=WIRE_DT, tm=512, tk=WGRAD_TK,
                                  name=nm("mlp2_dw")))
    du2 = _mm(dhp, wts['mlp_w1'][l], tb=True, b_blocked=True, kblocks=DGRAD_KBLOCKS,
              name=nm("mlp1_dx")).reshape(bsz, tlen, d)
    gr['mlp_w1'] = _mm(sv['u2T'], dhp, out_blocks=N_DEV, out_dtype=WIRE_DT, tk=WGRAD_TK, name=nm("mlp1_dw"))
    gr['ln2_g'] = acc2[:, :, 1].sum((0, 1))
    gr['ln2_b'] = acc2[:, :, 2].sum((0, 1))
    gr['mlp_b2'] = acc2[:, :, 3].sum((0, 1))

    dxa, dmo, acc1 = ln_bwd(sv['x'], sv['mo'], 2, sm['ln1_g'][l], dx1a, nm("ln1_bwd"), through=(du2, 4, sm['ln1_b'][l]))
    gr['ln1_g'] = acc1[:, :, 1].sum((0, 1))
    gr['ln1_b'] = acc1[:, :, 2].sum((0, 1))
    gr['b_out'] = acc1[:, :, 3].sum((0, 1))
    dmo2 = dmo.reshape(bt, d)
    dmix = _mm(dmo2, wts['w_out'][l], tb=True, name=nm("w_out_dx")).reshape(bsz, tlen, 2 * d)
    gr['w_out'] = row_blocks(_mm(sv['mixinT'], dmo2, out_dtype=WIRE_DT, tk=WGRAD_TK, name=nm("w_out_dw")))

    def glu_bwd(tv, rv, mv, seg):
        ds, ga, gp = tv
        ga = ga.astype(F32)
        sg = _sigmoid(gp)
        dg = ds * ga * sg * (1.0 - sg)
        return (dg, ds * sg), (_colsum(dg),)
    o, a = _ew(glu_bwd, [(dmix, 1, d, 0), _full(sv['gact']), _full(sv['gpre'])], outs=[(d, ACT_DT), (d, F32)],
               accs=[(1, d)], tr=tr, name=nm("glu_bwd"))
    dgp, t1 = o
    gr['s5_glu_b'] = a[0][:, 1, 0].sum(0)
    dgp2 = dgp.reshape(bt, d)
    dyt = _mm(dgp2, wts['s5_glu_w'][l], tb=True, epi=lambda r, t1v, yv: (r + t1v) * _gelu_grad(yv.astype(F32)),
              extras=[t1.reshape(bt, d), sv['ytok'].reshape(bt, d)], out_dtype=ACT_DT, tn=512,
              name=nm("glu_dx")).reshape(bsz, tlen, d)
    gr['s5_glu_w'] = row_blocks(_mm(sv['gactT'], dgp2, out_dtype=WIRE_DT, tk=WGRAD_TK, name=nm("glu_dw")))
    du5, dops = _s5_bwd(_to_chunks(dyt, ctx_len, g), sv['s5u'], sv['hf5'], sv['hb5'], cfg['s5_ops_t'][l],
                        cfg['s5_lm'][l], ncc, nm("s5_bwd"))
    ds5u = _from_chunks(du5, ctx_len, g)
    gr['s5_dops'] = dops

    z3 = sv['z3']
    dxc = None
    for dd in range(2):
        names = ('mlp_w2',) if dd == 0 else ('mlp_w1', 's5_glu_w', 'w_out')
        riders = (pending if dd == 0 else []) + [(n, l, gr.pop(n)) for n in names]
        dxc, dwa, dwi, dv, got = _rg_bwd(sv['xc'], sv['hf'] if dd == 0 else sv['hb'], dmix, z3, sm['rg_lambda'][l, dd],
                                         sm['rg_wa'][l, dd], sm['rg_ba'][l, dd], sm['rg_wi'][l, dd],
                                         sm['rg_bi'][l, dd], dxc, bool(dd), nct, tr, nm(f"rg_bwd{dd}"),
                                         ride=([r[2] for r in riders], False))
        for (n, lay, _), p in zip(riders, got):
            cfg['slots'][n][lay] = p
        gr[f'rg_wa{dd}'], gr[f'rg_wi{dd}'] = dwa, dwi
        gr[f'rg_lambda{dd}'] = dv[0] * (-_sigmoid(-sm['rg_lambda'][l, dd]))
        gr[f'rg_ba{dd}'], gr[f'rg_bi{dd}'] = dv[1], dv[2]
    drgx, accc = _conv_bwd(dxc, z3, sm['conv_w'][l], d, nct, tr, nm("conv_bwd"))
    gr['conv_w'] = accc[:, 0:4].sum(0)
    gr['conv_b'] = accc[:, 4].sum(0)

    def dz_fn(tv, rv, mv, seg):
        dgate = tv[0] * (tv[1] + tv[2]) * _gelu_grad(tv[3])
        return (jnp.concatenate([tv[4], dgate, tv[5].astype(F32)], axis=1),), ()
    dz = _ew(dz_fn, [(dmix, 0, d, 0), _full(sv['hf']), _full(sv['hb']), (z3, 1, d, 0), _full(drgx), _full(ds5u)],
             outs=[(3 * d, ACT_DT)], tr=tr, name=nm("dz"))[0][0].reshape(bt, 3 * d)
    du = _mm(dz, wts['w_in'][l], tb=True, b_blocked=True, kblocks=DGRAD_KBLOCKS,
             name=nm("w_in_dx")).reshape(bsz, tlen, d)
    still = [('w_in', l, _mm(sv['uT'], dz, out_blocks=N_DEV, out_dtype=WIRE_DT, tk=WGRAD_TK, name=nm("w_in_dw")))]
    dxin, accm1 = mod_bwd(du, sv['x'], 1, dxa, nm("mod1_bwd"))
    dmod = jnp.stack([accm1[:, :, 0], accm1[:, :, 1], acc1[:, :, 0], acc1[:, :, 4], acc1[:, :, 5], acc2[:, :, 0]], axis=2)
    return dxin, dmod, gr, still


def kernel(x, c, ctx, c_ctx, ada_w, ada_b, ln1_g, ln1_b, w_in, conv_w, conv_b, rg_lambda, rg_wa, rg_ba, rg_wi, rg_bi, s5_a_re, s5_a_im, s5_log_dt, s5_b_re, s5_b_im, s5_c_re, s5_c_im, s5_d, s5_glu_w, s5_glu_b, w_out, b_out, ln2_g, ln2_b, mlp_w1, mlp_b1, mlp_w2, mlp_b2, loss_target, m_c_ctx, m_ada_w, m_ada_b, m_ln1_g, m_ln1_b, m_w_in, m_conv_w, m_conv_b, m_rg_lambda, m_rg_wa, m_rg_ba, m_rg_wi, m_rg_bi, m_s5_a_re, m_s5_a_im, m_s5_log_dt, m_s5_b_re, m_s5_b_im, m_s5_c_re, m_s5_c_im, m_s5_d, m_s5_glu_w, m_s5_glu_b, m_w_out, m_b_out, m_ln2_g, m_ln2_b, m_mlp_w1, m_mlp_b1, m_mlp_w2, m_mlp_b2, v_c_ctx, v_ada_w, v_ada_b, v_ln1_g, v_ln1_b, v_w_in, v_conv_w, v_conv_b, v_rg_lambda, v_rg_wa, v_rg_ba, v_rg_wi, v_rg_bi, v_s5_a_re, v_s5_a_im, v_s5_log_dt, v_s5_b_re, v_s5_b_im, v_s5_c_re, v_s5_c_im, v_s5_d, v_s5_glu_w, v_s5_glu_b, v_w_out, v_b_out, v_ln2_g, v_ln2_b, v_mlp_w1, v_mlp_b1, v_mlp_w2, v_mlp_b2):
    loc = dict(locals())
    w = {n: loc[n] for n in WEIGHTS}
    mom = {n: loc["m_" + n] for n in WEIGHTS}
    vel = {n: loc["v_" + n] for n in WEIGHTS}
    bsz, seq, d = x.shape
    ctx_len = ctx.shape[1]
    depth = ada_w.shape[0]
    g = s5_a_re.shape[2]
    nmod = ada_b.shape[1] // d
    modc = ada_w.shape[2]
    tr = _tile(ctx_len, ROW_TILE, 8)
    cfg = dict(nct=ctx_len // tr, tr=tr, ctx_len=ctx_len, g=g, alpha=(2.0 * depth) ** 0.25)
    xi, yi, ci = _me()
    me = 4 * xi + 2 * yi + ci

    small_shapes = [c.shape] + [w[n].shape for n in CHAN_SHARDED]
    got = _exchange([_pack([c] + [w[n] for n in CHAN_SHARDED], F32)], True, "gather_small")[0]
    parts = _unpack(got, small_shapes, lead=1)
    c_all = parts[0].reshape(N_DEV * bsz, d)
    sm = {n: w[n] for n in WEIGHTS if n not in BIG and n not in CHAN_SHARDED and n != 'ada_w'}
    for n, p in zip(CHAN_SHARDED, parts[1:]):
        sm[n] = jnp.moveaxis(p, 0, -2).reshape(p.shape[1:-1] + (-1,))

    a_ext = jnp.concatenate([c_all, jnp.broadcast_to(c_ctx[None], (N_DEV, d))], axis=0)
    nrow = a_ext.shape[0]
    my_ada_b = lax.dynamic_slice_in_dim(ada_b, me * modc, modc, axis=1)
    mod_cols = jnp.stack([_mm(a_ext, ada_w[l], bias=my_ada_b[l], a_fn=_silu, name=f"l{l}_ada") for l in range(depth)])
    mod_all = _exchange([mod_cols.reshape(depth * nrow, modc)], True, "gather_mod")[0]
    mod_all = mod_all.reshape(N_DEV, depth, nrow, modc).transpose(1, 2, 0, 3).reshape(depth, nrow, nmod, d)
    mod_mine = lax.dynamic_slice_in_dim(mod_all, me * bsz, bsz, axis=1)
    mod_ctx = jnp.broadcast_to(mod_all[:, N_DEV * bsz][:, None], mod_mine.shape)
    modts = jnp.stack([mod_ctx, mod_mine], axis=2)

    cfg['depth'] = depth
    cfg['wloc'] = {n: [w[n][l].astype(WIRE_DT) for l in range(depth)] for n in BIG}
    wts = {n: [None] * depth for n in BIG}
    for n, p in zip(MIX_W, _exchange([cfg['wloc'][n][0] for n in MIX_W], True, "gather_w0")):
        wts[n][0] = _gathered(n, p)

    fold = lambda t: jnp.moveaxis(t, 0, 1).reshape((2, depth * g) + t.shape[3:])
    s5_in = [fold(w[n]) for n in S5_NAMES[:7]] + [s5_d.reshape(depth * g, -1)]
    ops_all, s5_vjp = jax.vjp(_s5_operators, *s5_in)
    lay = lambda t, l, ax=0: lax.slice_in_dim(t, l * g, (l + 1) * g, axis=ax)
    cfg['s5_ops'] = [tuple(lay(o, l).astype(MXU_DT) for o in ops_all[:5]) for l in range(depth)]
    cfg['s5_ops_t'] = [tuple(jnp.swapaxes(lay(o, l), 1, 2).astype(MXU_DT) for o in ops_all[:5]) for l in range(depth)]
    cfg['s5_lm'] = [lay(ops_all[5], l, 1) for l in range(depth)]

    act = jnp.concatenate([ctx, x], axis=1)
    cfg['modts'] = modts
    u_in = None
    saved = []
    for l in range(depth):
        act, sv, u_in = _layer_fwd(l, act, modts[l], wts, sm, cfg, u_in)
        saved.append(sv)

    def loss_fn(tv, rv, mv, seg):
        err = tv[0] - tv[1]
        keep = jnp.where(seg == 1, 1.0, 0.0)
        return (err * (keep / d),), (_colsum(err * err) * keep,)
    o, a = _ew(loss_fn, [_full(act), (loss_target, 0, d, cfg['nct'])], outs=[(d, F32)], accs=[(1, d)],
               nctx_tiles=cfg['nct'], tr=tr, name="loss")
    dact = o[0]
    loss = lax.psum(0.5 * jnp.sum(a[0][:, 1]) / d, ("x", "y", "c"))

    grads = [None] * depth
    dmods = [None] * depth
    slots = cfg['slots'] = {n: [None] * depth for n in BIG}
    pending = []
    for l in reversed(range(depth)):
        dact, dmods[l], grads[l], pending = _layer_bwd(l, dact, saved[l], modts[l], wts, sm, cfg, pending)
    for (n, lay, _), p in zip(pending, _exchange([r[2] for r in pending], False, "scatter_last")):
        slots[n][lay] = p
    grad_x = dact[:, ctx_len:]

    dmod = jnp.stack(dmods)
    mine = jnp.concatenate([dmod[:, :, 1].reshape(depth, bsz, nmod * d),
                            dmod[:, :, 0].sum(1).reshape(depth, 1, nmod * d)], axis=1)
    got = _exchange([mine.reshape(depth * (bsz + 1), nmod * d)], True, "gather_dmod")[0]
    got = got.reshape(N_DEV, depth, bsz + 1, nmod * d)
    dmod_rows = jnp.concatenate([got[:, :, :bsz].transpose(1, 0, 2, 3).reshape(depth, N_DEV * bsz, nmod * d),
                                 got[:, :, bsz].transpose(1, 0, 2)], axis=1)
    dmod_cols = lax.dynamic_slice_in_dim(dmod_rows, me * modc, modc, axis=2)
    g_ada_w = jnp.stack([_mm(a_ext, dmod_cols[l], ta=True, a_fn=_silu, name=f"l{l}_ada_dw") for l in range(depth)])

    def rowsum_fn(tv, rv, mv, seg):
        return (), (_colsum(tv[0]),)
    g_ada_b = _ew(rowsum_fn, [_full(dmod_rows)], accs=[(1, nmod * d)], name="ada_db")[1][0][:, 1, 0]
    dsilu = 0.0
    for l in range(depth):
        dsilu = dsilu + _mm(dmod_cols[l, N_DEV * bsz:], ada_w[l], tb=True, name=f"l{l}_ada_dc").sum(0)
    sig = _sigmoid(c_ctx)
    g_c_ctx_part = dsilu * (sig * (1.0 + c_ctx * (1.0 - sig)))

    def both(name, l):
        return jnp.stack([grads[l][f'{name}{dd}'] for dd in range(2)])
    rep = {n: jnp.stack([grads[l][n] for l in range(depth)]) for n in
           ('ln1_g', 'ln1_b', 'conv_w', 'conv_b', 's5_glu_b', 'b_out', 'ln2_g', 'ln2_b', 'mlp_b1', 'mlp_b2')}
    for n in ('rg_lambda', 'rg_wa', 'rg_ba', 'rg_wi', 'rg_bi'):
        rep[n] = jnp.stack([both(n, l) for l in range(depth)])
    rep['c_ctx'] = g_c_ctx_part
    dops_all = tuple(jnp.concatenate([grads[l]['s5_dops'][i] for l in range(depth)], axis=1 if i == 5 else 0)
                     for i in range(6))
    s5g = s5_vjp(dops_all)
    for n, val in zip(S5_NAMES[:7], s5g):
        rep[n] = jnp.moveaxis(val.reshape((2, depth, g) + val.shape[2:]), 0, 1)
    rep['s5_d'] = s5g[7].reshape(depth, -1)
    small_names = [n for n in WEIGHTS if n in rep and math.prod(rep[n].shape) <= SMALL_PARAM]
    large_names = [n for n in WEIGHTS if n in rep and math.prod(rep[n].shape) > SMALL_PARAM]
    g_rep = {}
    bufs = []
    for names, dt in ((small_names, F32), (large_names, WIRE_DT)):
        buf = _pack([rep[n] for n in names], dt)
        bufs.append(jnp.pad(buf, ((0, -buf.shape[0] % (16 * N_DEV)), (0, 0))).reshape(N_DEV, -1, PACK_W))
    parts = [_sum_slots(p, f"sum_rep{i}").astype(p.dtype) for i, p in enumerate(_exchange(bufs, False, "scatter_rep"))]
    for names, full in zip((small_names, large_names), _exchange(parts, True, "gather_rep")):
        g_rep.update(zip(names, _unpack(full.reshape(-1, PACK_W).astype(F32), [rep[n].shape for n in names])))

    grad, delta, new_m, new_v = {}, {}, {}, {}
    for n in BIG:
        res = [_sum_adamw(slots[n][l], w[n][l], mom[n][l], vel[n][l], f"l{l}_adamw_{n}") for l in range(depth)]
        grad[n], delta[n], new_m[n], new_v[n] = [jnp.stack([r[i] for r in res]) for i in range(4)]
    grad['ada_w'] = g_ada_w
    rest = [n for n in WEIGHTS if n not in BIG and n != 'ada_w']
    for n in rest:
        if n == 'ada_b':
            grad[n] = g_ada_b
        elif n in CHAN_SHARDED:
            grad[n] = lax.dynamic_slice_in_dim(g_rep[n], me * w[n].shape[-1], w[n].shape[-1], axis=g_rep[n].ndim - 1)
        else:
            grad[n] = g_rep[n].reshape(w[n].shape)
    flat2 = lambda t: t.reshape(-1, t.shape[-1])
    for n in ('ada_w', 'rg_wa', 'rg_wi'):
        res = _adamw(flat2(w[n]), flat2(grad[n]), flat2(mom[n]), flat2(vel[n]), f"adamw_{n}")
        delta[n], new_m[n], new_v[n] = [r.reshape(w[n].shape) for r in res]
    for tag, names in (("small", [n for n in rest if math.prod(w[n].shape) <= SMALL_PARAM]),
                       ("s5", [n for n in rest if math.prod(w[n].shape) > SMALL_PARAM and n not in ('rg_wa', 'rg_wi')])):
        if not names:
            continue
        shapes = [w[n].shape for n in names]
        packed = [_pack([src[n] for n in names], F32) for src in (w, grad, mom, vel)]
        for dst, o in zip((delta, new_m, new_v), _adamw(*packed, name=f"adamw_{tag}")):
            dst.update(zip(names, _unpack(o, shapes)))
    return (loss, grad_x, *[grad[n] for n in WEIGHTS], *[delta[n] for n in WEIGHTS], *[new_m[n] for n in WEIGHTS],
            *[new_v[n] for n in WEIGHTS])
```
